```python
import jax, jax.numpy as jnp
from jax import lax
import numpy as np

D_MODEL = 2048
BATCH = 8
SEQ = 2048
DEPTH = 4

N_MIXERS = 4
HEAD_DIM = 128
N_HEADS = D_MODEL // HEAD_DIM
Q_BLOCK = 128
EPS = 1e-6

FOX_FGATE_BIAS = 2.0

MLA_Q_RANK = 512
MLA_KV_RANK = 512
MLA_NOPE = 128
MLA_ROPE = 64
MLA_V = 128
ROPE_THETA = 10000.0

SGU_CHUNK = 128
SGU_WIDTH = D_MODEL
SGU_GROUP_DIM = 128
SGU_GROUPS = SGU_WIDTH // SGU_GROUP_DIM

D_FF = 5632
CONV_WIDTH = 3

kernel_name = 'hybrid_interleaved_fox_mla_stickbreak_sgu'


def _n_layers_of(m):
    return len(range(m, DEPTH, N_MIXERS))


def rms_norm(x, gain):
    x32 = x.astype(jnp.float32)
    y = x32 * lax.rsqrt(jnp.mean(x32 * x32, axis=-1, keepdims=True) + EPS)
    return (y * gain.astype(jnp.float32)).astype(x.dtype)


def _sweep_query_blocks(block_fn, seq):
    out = lax.map(block_fn, jnp.arange(seq // Q_BLOCK))
    nb, b, qb, h, dv = out.shape
    return out.transpose(1, 0, 2, 3, 4).reshape(b, seq, h * dv)


def _causal_softmax(s, start, v):
    seq = s.shape[-1]
    q_pos = start + jnp.arange(Q_BLOCK)
    allowed = jnp.arange(seq)[None, :] <= q_pos[:, None]
    p = jax.nn.softmax(jnp.where(allowed, s, -jnp.inf), axis=-1)
    return jnp.einsum('bhqs,bshd->bqhd', p.astype(v.dtype), v)


def fox_mixer(h, w_in, b_f, q_gain, k_gain, w_out):
    bsz, seq, _ = h.shape
    hd = N_HEADS * HEAD_DIM
    q, k, v, f_logit = jnp.split(h @ w_in, [hd, 2 * hd, 3 * hd], axis=-1)
    q = rms_norm(q.reshape(bsz, seq, N_HEADS, HEAD_DIM), q_gain)
    k = rms_norm(k.reshape(bsz, seq, N_HEADS, HEAD_DIM), k_gain)
    v = v.reshape(bsz, seq, N_HEADS, HEAD_DIM)
    log_f = jax.nn.log_sigmoid(f_logit.astype(jnp.float32) + b_f.astype(jnp.float32))
    cum = jnp.cumsum(log_f, axis=1).transpose(0, 2, 1)
    scale = HEAD_DIM ** -0.5

    def block(i):
        start = i * Q_BLOCK
        qb = lax.dynamic_slice_in_dim(q, start, Q_BLOCK, axis=1)
        cq = lax.dynamic_slice_in_dim(cum, start, Q_BLOCK, axis=2)
        s = jnp.einsum('bqhd,bshd->bhqs', qb, k, preferred_element_type=jnp.float32) * scale
        s = s + (cq[..., :, None] - cum[:, :, None, :])
        return _causal_softmax(s, start, v)

    return _sweep_query_blocks(block, seq) @ w_out


def _rope_tables(positions):
    inv_freq = ROPE_THETA ** (-jnp.arange(0, MLA_ROPE, 2, dtype=jnp.float32) / MLA_ROPE)
    ang = positions.astype(jnp.float32)[..., None] * inv_freq
    return jnp.cos(ang)[:, :, None, :], jnp.sin(ang)[:, :, None, :]


def _apply_rope(x, cos, sin):
    x1, x2 = jnp.split(x.astype(jnp.float32), 2, axis=-1)
    return jnp.concatenate([x1 * cos - x2 * sin, x1 * sin + x2 * cos], axis=-1).astype(x.dtype)


def mla_mixer(h, positions, w_in, q_a_gain, kv_a_gain, w_q_b, w_kv_b, q_gain, k_gain, w_out):
    bsz, seq, _ = h.shape
    c_q, c_kv, k_rope = jnp.split(h @ w_in, [MLA_Q_RANK, MLA_Q_RANK + MLA_KV_RANK], axis=-1)
    q = (rms_norm(c_q, q_a_gain) @ w_q_b).reshape(bsz, seq, N_HEADS, MLA_NOPE + MLA_ROPE)
    kv = (rms_norm(c_kv, kv_a_gain) @ w_kv_b).reshape(bsz, seq, N_HEADS, MLA_NOPE + MLA_V)
    q_nope, q_rope = jnp.split(q, [MLA_NOPE], axis=-1)
    k_nope, v = jnp.split(kv, [MLA_NOPE], axis=-1)
    cos, sin = _rope_tables(positions)
    q_nope = rms_norm(q_nope, q_gain[:MLA_NOPE])
    k_nope = rms_norm(k_nope, k_gain[:MLA_NOPE])
    q_rope = _apply_rope(rms_norm(q_rope, q_gain[MLA_NOPE:]), cos, sin)
    k_rope = _apply_rope(rms_norm(k_rope[:, :, None, :], k_gain[MLA_NOPE:]), cos, sin)
    q = jnp.concatenate([q_nope, q_rope], axis=-1)
    k = jnp.concatenate([k_nope, jnp.broadcast_to(k_rope, (bsz, seq, N_HEADS, MLA_ROPE))], axis=-1)
    scale = (MLA_NOPE + MLA_ROPE) ** -0.5

    def block(i):
        start = i * Q_BLOCK
        qb = lax.dynamic_slice_in_dim(q, start, Q_BLOCK, axis=1)
        s = jnp.einsum('bqhd,bshd->bhqs', qb, k, preferred_element_type=jnp.float32) * scale
        return _causal_softmax(s, start, v)

    return _sweep_query_blocks(block, seq) @ w_out


def stick_breaking_mixer(h, w_in, q_gain, k_gain, w_out):
    bsz, seq, _ = h.shape
    hd = N_HEADS * HEAD_DIM
    q, k, v = jnp.split(h @ w_in, [hd, 2 * hd], axis=-1)
    q = rms_norm(q.reshape(bsz, seq, N_HEADS, HEAD_DIM), q_gain)
    k = rms_norm(k.reshape(bsz, seq, N_HEADS, HEAD_DIM), k_gain)
    v = v.reshape(bsz, seq, N_HEADS, HEAD_DIM)
    scale = HEAD_DIM ** -0.5

    def block(i):
        start = i * Q_BLOCK
        qb = lax.dynamic_slice_in_dim(q, start, Q_BLOCK, axis=1)
        z = jnp.einsum('bqhd,bshd->bhqs', qb, k, preferred_element_type=jnp.float32) * scale
        q_pos = start + jnp.arange(Q_BLOCK)
        strict = jnp.arange(seq)[None, :] < q_pos[:, None]
        log_keep = jnp.where(strict, -jax.nn.softplus(z), 0.0)
        after = lax.cumsum(log_keep, axis=3, reverse=True) - log_keep
        a = jnp.where(strict, jnp.exp(jax.nn.log_sigmoid(z) + after), 0.0)
        return jnp.einsum('bhqs,bshd->bqhd', a.astype(v.dtype), v)

    return _sweep_query_blocks(block, seq) @ w_out


def sgu_mixer(h, w_in, v_gain, w_s, b_s, w_out):
    bsz, seq, _ = h.shape
    u, vv = jnp.split(jax.nn.gelu(h @ w_in), 2, axis=-1)
    vv = rms_norm(vv, v_gain)
    n_chunks = seq // SGU_CHUNK
    vv = vv.reshape(bsz, n_chunks, SGU_CHUNK, SGU_GROUPS, SGU_GROUP_DIM)
    causal = jnp.tril(jnp.ones((SGU_CHUNK, SGU_CHUNK), dtype=bool))
    ws = jnp.where(causal[None], w_s, 0.0).astype(vv.dtype)
    mixed = jnp.einsum('gts,bnsgc->bntgc', ws, vv) + b_s.T[:, :, None]
    return (u * mixed.reshape(bsz, seq, SGU_WIDTH)) @ w_out


def conv_ffn(h, w_up, conv_w, conv_b, w_down):
    seq = h.shape[1]
    up = h @ w_up
    padded = jnp.pad(up, ((0, 0), (CONV_WIDTH - 1, 0), (0, 0)))
    y = conv_b + conv_w[0] * padded[:, 0:seq]
    for tap in range(1, CONV_WIDTH):
        y = y + conv_w[tap] * padded[:, tap:tap + seq]
    gate, val = jnp.split(y, 2, axis=-1)
    return (jax.nn.silu(gate) * val) @ w_down


def _fwd_setup_inputs(seed: int = 0) -> dict:
    key = jax.random.key(seed)
    ks = iter(list(jax.random.split(key, 32)))

    def w(shape, fan_in):
        return jax.random.normal(next(ks), shape, jnp.float32) * fan_in ** -0.5

    def gain(shape):
        return 1.0 + 0.1 * jax.random.normal(next(ks), shape, jnp.float32)

    n_a, n_b, n_c, n_d = (_n_layers_of(m) for m in range(N_MIXERS))
    hd = N_HEADS * HEAD_DIM
    x = jax.random.normal(next(ks), (BATCH, SEQ, D_MODEL), jnp.float32)
    offset = jax.random.randint(next(ks), (BATCH, 1), 0, 4096, dtype=jnp.int32)
    positions = offset + jnp.arange(SEQ, dtype=jnp.int32)[None, :]
    return {
        'x': x,
        'positions': positions,
        'mix_norm': gain((DEPTH, D_MODEL)),
        'ffn_norm': gain((DEPTH, D_MODEL)),
        'fox_w_in': w((n_a, D_MODEL, 3 * hd + N_HEADS), D_MODEL),
        'fox_b_f': FOX_FGATE_BIAS + 0.5 * jax.random.normal(next(ks), (n_a, N_HEADS), jnp.float32),
        'fox_q_gain': gain((n_a, HEAD_DIM)),
        'fox_k_gain': gain((n_a, HEAD_DIM)),
        'fox_w_out': w((n_a, hd, D_MODEL), hd),
        'mla_w_in': w((n_b, D_MODEL, MLA_Q_RANK + MLA_KV_RANK + MLA_ROPE), D_MODEL),
        'mla_q_a_gain': gain((n_b, MLA_Q_RANK)),
        'mla_kv_a_gain': gain((n_b, MLA_KV_RANK)),
        'mla_w_q_b': w((n_b, MLA_Q_RANK, N_HEADS * (MLA_NOPE + MLA_ROPE)), MLA_Q_RANK),
        'mla_w_kv_b': w((n_b, MLA_KV_RANK, N_HEADS * (MLA_NOPE + MLA_V)), MLA_KV_RANK),
        'mla_q_gain': gain((n_b, MLA_NOPE + MLA_ROPE)),
        'mla_k_gain': gain((n_b, MLA_NOPE + MLA_ROPE)),
        'mla_w_out': w((n_b, N_HEADS * MLA_V, D_MODEL), N_HEADS * MLA_V),
        'sb_w_in': w((n_c, D_MODEL, 3 * hd), D_MODEL),
        'sb_q_gain': gain((n_c, HEAD_DIM)),
        'sb_k_gain': gain((n_c, HEAD_DIM)),
        'sb_w_out': w((n_c, hd, D_MODEL), hd),
        'sgu_w_in': w((n_d, D_MODEL, 2 * SGU_WIDTH), D_MODEL),
        'sgu_v_gain': gain((n_d, SGU_WIDTH)),
        'sgu_w_s': w((n_d, SGU_GROUPS, SGU_CHUNK, SGU_CHUNK), SGU_CHUNK),
        'sgu_b_s': gain((n_d, SGU_GROUPS, SGU_CHUNK)),
        'sgu_w_out': w((n_d, SGU_WIDTH, D_MODEL), SGU_WIDTH),
        'ffn_w_up': w((DEPTH, D_MODEL, 2 * D_FF), D_MODEL),
        'ffn_conv_w': w((DEPTH, CONV_WIDTH, 2 * D_FF), CONV_WIDTH),
        'ffn_conv_b': 0.01 * jax.random.normal(next(ks), (DEPTH, 2 * D_FF), jnp.float32),
        'ffn_w_down': w((DEPTH, D_FF, D_MODEL), D_FF),
    }


def _fwd_reference(x, positions, mix_norm, ffn_norm,
              fox_w_in, fox_b_f, fox_q_gain, fox_k_gain, fox_w_out,
              mla_w_in, mla_q_a_gain, mla_kv_a_gain, mla_w_q_b, mla_w_kv_b,
              mla_q_gain, mla_k_gain, mla_w_out,
              sb_w_in, sb_q_gain, sb_k_gain, sb_w_out,
              sgu_w_in, sgu_v_gain, sgu_w_s, sgu_b_s, sgu_w_out,
              ffn_w_up, ffn_conv_w, ffn_conv_b, ffn_w_down):
    h = x
    for i in range(DEPTH):
        m, j = i % N_MIXERS, i // N_MIXERS
        a = rms_norm(h, mix_norm[i])
        if m == 0:
            mixed = fox_mixer(a, fox_w_in[j], fox_b_f[j], fox_q_gain[j], fox_k_gain[j], fox_w_out[j])
        elif m == 1:
            mixed = mla_mixer(a, positions, mla_w_in[j], mla_q_a_gain[j], mla_kv_a_gain[j],
                              mla_w_q_b[j], mla_w_kv_b[j], mla_q_gain[j], mla_k_gain[j], mla_w_out[j])
        elif m == 2:
            mixed = stick_breaking_mixer(a, sb_w_in[j], sb_q_gain[j], sb_k_gain[j], sb_w_out[j])
        else:
            mixed = sgu_mixer(a, sgu_w_in[j], sgu_v_gain[j], sgu_w_s[j], sgu_b_s[j], sgu_w_out[j])
        h = h + mixed
        h = h + conv_ffn(rms_norm(h, ffn_norm[i]), ffn_w_up[i], ffn_conv_w[i], ffn_conv_b[i], ffn_w_down[i])
    return h


import jax as _jax
import jax.numpy as _jnp

TWIN_FORMAT = 'train_step'
FWD_PARAMS = ['x', 'positions', 'mix_norm', 'ffn_norm', 'fox_w_in', 'fox_b_f', 'fox_q_gain', 'fox_k_gain', 'fox_w_out', 'mla_w_in', 'mla_q_a_gain', 'mla_kv_a_gain', 'mla_w_q_b', 'mla_w_kv_b', 'mla_q_gain', 'mla_k_gain', 'mla_w_out', 'sb_w_in', 'sb_q_gain', 'sb_k_gain', 'sb_w_out', 'sgu_w_in', 'sgu_v_gain', 'sgu_w_s', 'sgu_b_s', 'sgu_w_out', 'ffn_w_up', 'ffn_conv_w', 'ffn_conv_b', 'ffn_w_down']
TWIN_WEIGHTS = ['mix_norm', 'ffn_norm', 'fox_w_in', 'fox_b_f', 'fox_q_gain', 'fox_k_gain', 'fox_w_out', 'mla_w_in', 'mla_q_a_gain', 'mla_kv_a_gain', 'mla_w_q_b', 'mla_w_kv_b', 'mla_q_gain', 'mla_k_gain', 'mla_w_out', 'sb_w_in', 'sb_q_gain', 'sb_k_gain', 'sb_w_out', 'sgu_w_in', 'sgu_v_gain', 'sgu_w_s', 'sgu_b_s', 'sgu_w_out', 'ffn_w_up', 'ffn_conv_w', 'ffn_conv_b', 'ffn_w_down']
TWIN_DIFF_INPUT = 'x'
TWIN_INPUTS = ['x', 'positions', 'mix_norm', 'ffn_norm', 'fox_w_in', 'fox_b_f', 'fox_q_gain', 'fox_k_gain', 'fox_w_out', 'mla_w_in', 'mla_q_a_gain', 'mla_kv_a_gain', 'mla_w_q_b', 'mla_w_kv_b', 'mla_q_gain', 'mla_k_gain', 'mla_w_out', 'sb_w_in', 'sb_q_gain', 'sb_k_gain', 'sb_w_out', 'sgu_w_in', 'sgu_v_gain', 'sgu_w_s', 'sgu_b_s', 'sgu_w_out', 'ffn_w_up', 'ffn_conv_w', 'ffn_conv_b', 'ffn_w_down', 'loss_target', 'm_mix_norm', 'm_ffn_norm', 'm_fox_w_in', 'm_fox_b_f', 'm_fox_q_gain', 'm_fox_k_gain', 'm_fox_w_out', 'm_mla_w_in', 'm_mla_q_a_gain', 'm_mla_kv_a_gain', 'm_mla_w_q_b', 'm_mla_w_kv_b', 'm_mla_q_gain', 'm_mla_k_gain', 'm_mla_w_out', 'm_sb_w_in', 'm_sb_q_gain', 'm_sb_k_gain', 'm_sb_w_out', 'm_sgu_w_in', 'm_sgu_v_gain', 'm_sgu_w_s', 'm_sgu_b_s', 'm_sgu_w_out', 'm_ffn_w_up', 'm_ffn_conv_w', 'm_ffn_conv_b', 'm_ffn_w_down', 'v_mix_norm', 'v_ffn_norm', 'v_fox_w_in', 'v_fox_b_f', 'v_fox_q_gain', 'v_fox_k_gain', 'v_fox_w_out', 'v_mla_w_in', 'v_mla_q_a_gain', 'v_mla_kv_a_gain', 'v_mla_w_q_b', 'v_mla_w_kv_b', 'v_mla_q_gain', 'v_mla_k_gain', 'v_mla_w_out', 'v_sb_w_in', 'v_sb_q_gain', 'v_sb_k_gain', 'v_sb_w_out', 'v_sgu_w_in', 'v_sgu_v_gain', 'v_sgu_w_s', 'v_sgu_b_s', 'v_sgu_w_out', 'v_ffn_w_up', 'v_ffn_conv_w', 'v_ffn_conv_b', 'v_ffn_w_down']
TWIN_OUTPUTS = ['loss', 'grad_x', 'grad_mix_norm', 'grad_ffn_norm', 'grad_fox_w_in', 'grad_fox_b_f', 'grad_fox_q_gain', 'grad_fox_k_gain', 'grad_fox_w_out', 'grad_mla_w_in', 'grad_mla_q_a_gain', 'grad_mla_kv_a_gain', 'grad_mla_w_q_b', 'grad_mla_w_kv_b', 'grad_mla_q_gain', 'grad_mla_k_gain', 'grad_mla_w_out', 'grad_sb_w_in', 'grad_sb_q_gain', 'grad_sb_k_gain', 'grad_sb_w_out', 'grad_sgu_w_in', 'grad_sgu_v_gain', 'grad_sgu_w_s', 'grad_sgu_b_s', 'grad_sgu_w_out', 'grad_ffn_w_up', 'grad_ffn_conv_w', 'grad_ffn_conv_b', 'grad_ffn_w_down', 'delta_mix_norm', 'delta_ffn_norm', 'delta_fox_w_in', 'delta_fox_b_f', 'delta_fox_q_gain', 'delta_fox_k_gain', 'delta_fox_w_out', 'delta_mla_w_in', 'delta_mla_q_a_gain', 'delta_mla_kv_a_gain', 'delta_mla_w_q_b', 'delta_mla_w_kv_b', 'delta_mla_q_gain', 'delta_mla_k_gain', 'delta_mla_w_out', 'delta_sb_w_in', 'delta_sb_q_gain', 'delta_sb_k_gain', 'delta_sb_w_out', 'delta_sgu_w_in', 'delta_sgu_v_gain', 'delta_sgu_w_s', 'delta_sgu_b_s', 'delta_sgu_w_out', 'delta_ffn_w_up', 'delta_ffn_conv_w', 'delta_ffn_conv_b', 'delta_ffn_w_down', 'new_m_mix_norm', 'new_m_ffn_norm', 'new_m_fox_w_in', 'new_m_fox_b_f', 'new_m_fox_q_gain', 'new_m_fox_k_gain', 'new_m_fox_w_out', 'new_m_mla_w_in', 'new_m_mla_q_a_gain', 'new_m_mla_kv_a_gain', 'new_m_mla_w_q_b', 'new_m_mla_w_kv_b', 'new_m_mla_q_gain', 'new_m_mla_k_gain', 'new_m_mla_w_out', 'new_m_sb_w_in', 'new_m_sb_q_gain', 'new_m_sb_k_gain', 'new_m_sb_w_out', 'new_m_sgu_w_in', 'new_m_sgu_v_gain', 'new_m_sgu_w_s', 'new_m_sgu_b_s', 'new_m_sgu_w_out', 'new_m_ffn_w_up', 'new_m_ffn_conv_w', 'new_m_ffn_conv_b', 'new_m_ffn_w_down', 'new_v_mix_norm', 'new_v_ffn_norm', 'new_v_fox_w_in', 'new_v_fox_b_f', 'new_v_fox_q_gain', 'new_v_fox_k_gain', 'new_v_fox_w_out', 'new_v_mla_w_in', 'new_v_mla_q_a_gain', 'new_v_mla_kv_a_gain', 'new_v_mla_w_q_b', 'new_v_mla_w_kv_b', 'new_v_mla_q_gain', 'new_v_mla_k_gain', 'new_v_mla_w_out', 'new_v_sb_w_in', 'new_v_sb_q_gain', 'new_v_sb_k_gain', 'new_v_sb_w_out', 'new_v_sgu_w_in', 'new_v_sgu_v_gain', 'new_v_sgu_w_s', 'new_v_sgu_b_s', 'new_v_sgu_w_out', 'new_v_ffn_w_up', 'new_v_ffn_conv_w', 'new_v_ffn_conv_b', 'new_v_ffn_w_down']
TWIN_LEAF_KINDS = {'loss': 'loss', 'grad_x': 'grad_x', 'grad_mix_norm': 'grad_w', 'grad_ffn_norm': 'grad_w', 'grad_fox_w_in': 'grad_w', 'grad_fox_b_f': 'grad_w', 'grad_fox_q_gain': 'grad_w', 'grad_fox_k_gain': 'grad_w', 'grad_fox_w_out': 'grad_w', 'grad_mla_w_in': 'grad_w', 'grad_mla_q_a_gain': 'grad_w', 'grad_mla_kv_a_gain': 'grad_w', 'grad_mla_w_q_b': 'grad_w', 'grad_mla_w_kv_b': 'grad_w', 'grad_mla_q_gain': 'grad_w', 'grad_mla_k_gain': 'grad_w', 'grad_mla_w_out': 'grad_w', 'grad_sb_w_in': 'grad_w', 'grad_sb_q_gain': 'grad_w', 'grad_sb_k_gain': 'grad_w', 'grad_sb_w_out': 'grad_w', 'grad_sgu_w_in': 'grad_w', 'grad_sgu_v_gain': 'grad_w', 'grad_sgu_w_s': 'grad_w', 'grad_sgu_b_s': 'grad_w', 'grad_sgu_w_out': 'grad_w', 'grad_ffn_w_up': 'grad_w', 'grad_ffn_conv_w': 'grad_w', 'grad_ffn_conv_b': 'grad_w', 'grad_ffn_w_down': 'grad_w', 'delta_mix_norm': 'delta_w', 'delta_ffn_norm': 'delta_w', 'delta_fox_w_in': 'delta_w', 'delta_fox_b_f': 'delta_w', 'delta_fox_q_gain': 'delta_w', 'delta_fox_k_gain': 'delta_w', 'delta_fox_w_out': 'delta_w', 'delta_mla_w_in': 'delta_w', 'delta_mla_q_a_gain': 'delta_w', 'delta_mla_kv_a_gain': 'delta_w', 'delta_mla_w_q_b': 'delta_w', 'delta_mla_w_kv_b': 'delta_w', 'delta_mla_q_gain': 'delta_w', 'delta_mla_k_gain': 'delta_w', 'delta_mla_w_out': 'delta_w', 'delta_sb_w_in': 'delta_w', 'delta_sb_q_gain': 'delta_w', 'delta_sb_k_gain': 'delta_w', 'delta_sb_w_out': 'delta_w', 'delta_sgu_w_in': 'delta_w', 'delta_sgu_v_gain': 'delta_w', 'delta_sgu_w_s': 'delta_w', 'delta_sgu_b_s': 'delta_w', 'delta_sgu_w_out': 'delta_w', 'delta_ffn_w_up': 'delta_w', 'delta_ffn_conv_w': 'delta_w', 'delta_ffn_conv_b': 'delta_w', 'delta_ffn_w_down': 'delta_w', 'new_m_mix_norm': 'new_m', 'new_m_ffn_norm': 'new_m', 'new_m_fox_w_in': 'new_m', 'new_m_fox_b_f': 'new_m', 'new_m_fox_q_gain': 'new_m', 'new_m_fox_k_gain': 'new_m', 'new_m_fox_w_out': 'new_m', 'new_m_mla_w_in': 'new_m', 'new_m_mla_q_a_gain': 'new_m', 'new_m_mla_kv_a_gain': 'new_m', 'new_m_mla_w_q_b': 'new_m', 'new_m_mla_w_kv_b': 'new_m', 'new_m_mla_q_gain': 'new_m', 'new_m_mla_k_gain': 'new_m', 'new_m_mla_w_out': 'new_m', 'new_m_sb_w_in': 'new_m', 'new_m_sb_q_gain': 'new_m', 'new_m_sb_k_gain': 'new_m', 'new_m_sb_w_out': 'new_m', 'new_m_sgu_w_in': 'new_m', 'new_m_sgu_v_gain': 'new_m', 'new_m_sgu_w_s': 'new_m', 'new_m_sgu_b_s': 'new_m', 'new_m_sgu_w_out': 'new_m', 'new_m_ffn_w_up': 'new_m', 'new_m_ffn_conv_w': 'new_m', 'new_m_ffn_conv_b': 'new_m', 'new_m_ffn_w_down': 'new_m', 'new_v_mix_norm': 'new_v', 'new_v_ffn_norm': 'new_v', 'new_v_fox_w_in': 'new_v', 'new_v_fox_b_f': 'new_v', 'new_v_fox_q_gain': 'new_v', 'new_v_fox_k_gain': 'new_v', 'new_v_fox_w_out': 'new_v', 'new_v_mla_w_in': 'new_v', 'new_v_mla_q_a_gain': 'new_v', 'new_v_mla_kv_a_gain': 'new_v', 'new_v_mla_w_q_b': 'new_v', 'new_v_mla_w_kv_b': 'new_v', 'new_v_mla_q_gain': 'new_v', 'new_v_mla_k_gain': 'new_v', 'new_v_mla_w_out': 'new_v', 'new_v_sb_w_in': 'new_v', 'new_v_sb_q_gain': 'new_v', 'new_v_sb_k_gain': 'new_v', 'new_v_sb_w_out': 'new_v', 'new_v_sgu_w_in': 'new_v', 'new_v_sgu_v_gain': 'new_v', 'new_v_sgu_w_s': 'new_v', 'new_v_sgu_b_s': 'new_v', 'new_v_sgu_w_out': 'new_v', 'new_v_ffn_w_up': 'new_v', 'new_v_ffn_conv_w': 'new_v', 'new_v_ffn_conv_b': 'new_v', 'new_v_ffn_w_down': 'new_v'}


def _forward(args):
    return _fwd_reference(*[args[k] for k in FWD_PARAMS])


def _output_shape():
    out = _jax.eval_shape(lambda: _forward(_fwd_setup_inputs(0)))
    return out.shape, out.dtype

N_MICROBATCH = 1
ADAM_LR = 0.001
ADAM_B1 = 0.9
ADAM_B2 = 0.999
ADAM_EPS = 1e-08
ADAM_WD = 0.01
ADAM_STEP = 10
PER_EXAMPLE_BATCH_AXIS = {'x': 0, 'positions': 0, 'loss_target': 0}
SHARED_INPUTS = []
_WEIGHT_DTYPES = {'mix_norm': _jnp.float32, 'ffn_norm': _jnp.float32, 'fox_w_in': _jnp.float32, 'fox_b_f': _jnp.float32, 'fox_q_gain': _jnp.float32, 'fox_k_gain': _jnp.float32, 'fox_w_out': _jnp.float32, 'mla_w_in': _jnp.float32, 'mla_q_a_gain': _jnp.float32, 'mla_kv_a_gain': _jnp.float32, 'mla_w_q_b': _jnp.float32, 'mla_w_kv_b': _jnp.float32, 'mla_q_gain': _jnp.float32, 'mla_k_gain': _jnp.float32, 'mla_w_out': _jnp.float32, 'sb_w_in': _jnp.float32, 'sb_q_gain': _jnp.float32, 'sb_k_gain': _jnp.float32, 'sb_w_out': _jnp.float32, 'sgu_w_in': _jnp.float32, 'sgu_v_gain': _jnp.float32, 'sgu_w_s': _jnp.float32, 'sgu_b_s': _jnp.float32, 'sgu_w_out': _jnp.float32, 'ffn_w_up': _jnp.float32, 'ffn_conv_w': _jnp.float32, 'ffn_conv_b': _jnp.float32, 'ffn_w_down': _jnp.float32}
MOMENT_SCALE = {'mix_norm': 3.440950e+00, 'ffn_norm': 6.497979e+00, 'fox_w_in': 1.955814e-01, 'fox_b_f': 5.221743e+01, 'fox_q_gain': 7.384223e+00, 'fox_k_gain': 7.574256e+00, 'fox_w_out': 2.390675e-01, 'mla_w_in': 2.324565e-01, 'mla_q_a_gain': 1.346388e-01, 'mla_kv_a_gain': 6.158617e-01, 'mla_w_q_b': 5.329243e-02, 'mla_w_kv_b': 1.058758e-01, 'mla_q_gain': 6.228490e-01, 'mla_k_gain': 6.263328e-01, 'mla_w_out': 1.359289e-01, 'sb_w_in': 1.396856e-01, 'sb_q_gain': 7.606902e+00, 'sb_k_gain': 7.657663e+00, 'sb_w_out': 1.950258e-01, 'sgu_w_in': 1.842124e-01, 'sgu_v_gain': 1.693669e+00, 'sgu_w_s': 1.101338e+00, 'sgu_b_s': 3.565677e+00, 'sgu_w_out': 7.466262e-01, 'ffn_w_up': 1.357081e-01, 'ffn_conv_w': 8.917280e-01, 'ffn_conv_b': 9.312595e-01, 'ffn_w_down': 1.813205e-01}


def _to_microbatches(a, axis):
    t = _jnp.moveaxis(a, axis, 0)
    t = t.reshape((N_MICROBATCH, t.shape[0] // N_MICROBATCH) + t.shape[1:])
    return _jnp.moveaxis(t, 1, axis + 1)


def setup_inputs(seed: int = 0) -> dict:
    inp = _fwd_setup_inputs(seed)
    key = _jax.random.fold_in(_jax.random.key(seed), 7919)
    shape, _ = _output_shape()
    out = dict(inp)
    out["loss_target"] = _jax.random.normal(_jax.random.fold_in(key, 0), shape, _jnp.float32)
    for i, name in enumerate(TWIN_WEIGHTS):
        w = inp[name].astype(_jnp.float32)
        if MOMENT_SCALE is None:
            s = _jnp.sqrt(_jnp.mean(_jnp.square(w)) + 1e-30)
        else:
            s = MOMENT_SCALE[name]
        km, kv = _jax.random.split(_jax.random.fold_in(key, i + 1))
        out[name] = w
        out["m_" + name] = s * _jax.random.normal(km, w.shape, _jnp.float32)
        out["v_" + name] = (s * s) * _jax.random.uniform(kv, w.shape, _jnp.float32, 0.5, 1.5)
    if N_MICROBATCH > 1:
        for name, axis in PER_EXAMPLE_BATCH_AXIS.items():
            out[name] = _to_microbatches(out[name], axis)
    return {'x': out['x'], 'positions': out['positions'], 'mix_norm': out['mix_norm'], 'ffn_norm': out['ffn_norm'], 'fox_w_in': out['fox_w_in'], 'fox_b_f': out['fox_b_f'], 'fox_q_gain': out['fox_q_gain'], 'fox_k_gain': out['fox_k_gain'], 'fox_w_out': out['fox_w_out'], 'mla_w_in': out['mla_w_in'], 'mla_q_a_gain': out['mla_q_a_gain'], 'mla_kv_a_gain': out['mla_kv_a_gain'], 'mla_w_q_b': out['mla_w_q_b'], 'mla_w_kv_b': out['mla_w_kv_b'], 'mla_q_gain': out['mla_q_gain'], 'mla_k_gain': out['mla_k_gain'], 'mla_w_out': out['mla_w_out'], 'sb_w_in': out['sb_w_in'], 'sb_q_gain': out['sb_q_gain'], 'sb_k_gain': out['sb_k_gain'], 'sb_w_out': out['sb_w_out'], 'sgu_w_in': out['sgu_w_in'], 'sgu_v_gain': out['sgu_v_gain'], 'sgu_w_s': out['sgu_w_s'], 'sgu_b_s': out['sgu_b_s'], 'sgu_w_out': out['sgu_w_out'], 'ffn_w_up': out['ffn_w_up'], 'ffn_conv_w': out['ffn_conv_w'], 'ffn_conv_b': out['ffn_conv_b'], 'ffn_w_down': out['ffn_w_down'], 'loss_target': out['loss_target'], 'm_mix_norm': out['m_mix_norm'], 'm_ffn_norm': out['m_ffn_norm'], 'm_fox_w_in': out['m_fox_w_in'], 'm_fox_b_f': out['m_fox_b_f'], 'm_fox_q_gain': out['m_fox_q_gain'], 'm_fox_k_gain': out['m_fox_k_gain'], 'm_fox_w_out': out['m_fox_w_out'], 'm_mla_w_in': out['m_mla_w_in'], 'm_mla_q_a_gain': out['m_mla_q_a_gain'], 'm_mla_kv_a_gain': out['m_mla_kv_a_gain'], 'm_mla_w_q_b': out['m_mla_w_q_b'], 'm_mla_w_kv_b': out['m_mla_w_kv_b'], 'm_mla_q_gain': out['m_mla_q_gain'], 'm_mla_k_gain': out['m_mla_k_gain'], 'm_mla_w_out': out['m_mla_w_out'], 'm_sb_w_in': out['m_sb_w_in'], 'm_sb_q_gain': out['m_sb_q_gain'], 'm_sb_k_gain': out['m_sb_k_gain'], 'm_sb_w_out': out['m_sb_w_out'], 'm_sgu_w_in': out['m_sgu_w_in'], 'm_sgu_v_gain': out['m_sgu_v_gain'], 'm_sgu_w_s': out['m_sgu_w_s'], 'm_sgu_b_s': out['m_sgu_b_s'], 'm_sgu_w_out': out['m_sgu_w_out'], 'm_ffn_w_up': out['m_ffn_w_up'], 'm_ffn_conv_w': out['m_ffn_conv_w'], 'm_ffn_conv_b': out['m_ffn_conv_b'], 'm_ffn_w_down': out['m_ffn_w_down'], 'v_mix_norm': out['v_mix_norm'], 'v_ffn_norm': out['v_ffn_norm'], 'v_fox_w_in': out['v_fox_w_in'], 'v_fox_b_f': out['v_fox_b_f'], 'v_fox_q_gain': out['v_fox_q_gain'], 'v_fox_k_gain': out['v_fox_k_gain'], 'v_fox_w_out': out['v_fox_w_out'], 'v_mla_w_in': out['v_mla_w_in'], 'v_mla_q_a_gain': out['v_mla_q_a_gain'], 'v_mla_kv_a_gain': out['v_mla_kv_a_gain'], 'v_mla_w_q_b': out['v_mla_w_q_b'], 'v_mla_w_kv_b': out['v_mla_w_kv_b'], 'v_mla_q_gain': out['v_mla_q_gain'], 'v_mla_k_gain': out['v_mla_k_gain'], 'v_mla_w_out': out['v_mla_w_out'], 'v_sb_w_in': out['v_sb_w_in'], 'v_sb_q_gain': out['v_sb_q_gain'], 'v_sb_k_gain': out['v_sb_k_gain'], 'v_sb_w_out': out['v_sb_w_out'], 'v_sgu_w_in': out['v_sgu_w_in'], 'v_sgu_v_gain': out['v_sgu_v_gain'], 'v_sgu_w_s': out['v_sgu_w_s'], 'v_sgu_b_s': out['v_sgu_b_s'], 'v_sgu_w_out': out['v_sgu_w_out'], 'v_ffn_w_up': out['v_ffn_w_up'], 'v_ffn_conv_w': out['v_ffn_conv_w'], 'v_ffn_conv_b': out['v_ffn_conv_b'], 'v_ffn_w_down': out['v_ffn_w_down']}


def _loss(weights, diff, rest, loss_target):
    with _jax.named_scope("forward"):
        args = {**rest, TWIN_DIFF_INPUT: diff, **{k: w.astype(_WEIGHT_DTYPES[k]) for k, w in weights.items()}}
        y = _forward(args)
    with _jax.named_scope("loss_head"):
        err = _jnp.square(y.astype(_jnp.float32) - loss_target)
        return 0.5 * _jnp.sum(_jnp.mean(err, axis=-1)) if err.ndim else 0.5 * err


def _adamw(w, g, m, v):
    m = ADAM_B1 * m + (1.0 - ADAM_B1) * g
    v = ADAM_B2 * v + (1.0 - ADAM_B2) * _jnp.square(g)
    m_hat = m / (1.0 - ADAM_B1 ** ADAM_STEP)
    v_hat = v / (1.0 - ADAM_B2 ** ADAM_STEP)
    delta = -ADAM_LR * (m_hat / (_jnp.sqrt(v_hat) + ADAM_EPS) + ADAM_WD * w)
    return delta, m, v


def reference(x, positions, mix_norm, ffn_norm, fox_w_in, fox_b_f, fox_q_gain, fox_k_gain, fox_w_out, mla_w_in, mla_q_a_gain, mla_kv_a_gain, mla_w_q_b, mla_w_kv_b, mla_q_gain, mla_k_gain, mla_w_out, sb_w_in, sb_q_gain, sb_k_gain, sb_w_out, sgu_w_in, sgu_v_gain, sgu_w_s, sgu_b_s, sgu_w_out, ffn_w_up, ffn_conv_w, ffn_conv_b, ffn_w_down, loss_target, m_mix_norm, m_ffn_norm, m_fox_w_in, m_fox_b_f, m_fox_q_gain, m_fox_k_gain, m_fox_w_out, m_mla_w_in, m_mla_q_a_gain, m_mla_kv_a_gain, m_mla_w_q_b, m_mla_w_kv_b, m_mla_q_gain, m_mla_k_gain, m_mla_w_out, m_sb_w_in, m_sb_q_gain, m_sb_k_gain, m_sb_w_out, m_sgu_w_in, m_sgu_v_gain, m_sgu_w_s, m_sgu_b_s, m_sgu_w_out, m_ffn_w_up, m_ffn_conv_w, m_ffn_conv_b, m_ffn_w_down, v_mix_norm, v_ffn_norm, v_fox_w_in, v_fox_b_f, v_fox_q_gain, v_fox_k_gain, v_fox_w_out, v_mla_w_in, v_mla_q_a_gain, v_mla_kv_a_gain, v_mla_w_q_b, v_mla_w_kv_b, v_mla_q_gain, v_mla_k_gain, v_mla_w_out, v_sb_w_in, v_sb_q_gain, v_sb_k_gain, v_sb_w_out, v_sgu_w_in, v_sgu_v_gain, v_sgu_w_s, v_sgu_b_s, v_sgu_w_out, v_ffn_w_up, v_ffn_conv_w, v_ffn_conv_b, v_ffn_w_down):
    given = dict(x=x, positions=positions, mix_norm=mix_norm, ffn_norm=ffn_norm, fox_w_in=fox_w_in, fox_b_f=fox_b_f, fox_q_gain=fox_q_gain, fox_k_gain=fox_k_gain, fox_w_out=fox_w_out, mla_w_in=mla_w_in, mla_q_a_gain=mla_q_a_gain, mla_kv_a_gain=mla_kv_a_gain, mla_w_q_b=mla_w_q_b, mla_w_kv_b=mla_w_kv_b, mla_q_gain=mla_q_gain, mla_k_gain=mla_k_gain, mla_w_out=mla_w_out, sb_w_in=sb_w_in, sb_q_gain=sb_q_gain, sb_k_gain=sb_k_gain, sb_w_out=sb_w_out, sgu_w_in=sgu_w_in, sgu_v_gain=sgu_v_gain, sgu_w_s=sgu_w_s, sgu_b_s=sgu_b_s, sgu_w_out=sgu_w_out, ffn_w_up=ffn_w_up, ffn_conv_w=ffn_conv_w, ffn_conv_b=ffn_conv_b, ffn_w_down=ffn_w_down, loss_target=loss_target, m_mix_norm=m_mix_norm, m_ffn_norm=m_ffn_norm, m_fox_w_in=m_fox_w_in, m_fox_b_f=m_fox_b_f, m_fox_q_gain=m_fox_q_gain, m_fox_k_gain=m_fox_k_gain, m_fox_w_out=m_fox_w_out, m_mla_w_in=m_mla_w_in, m_mla_q_a_gain=m_mla_q_a_gain, m_mla_kv_a_gain=m_mla_kv_a_gain, m_mla_w_q_b=m_mla_w_q_b, m_mla_w_kv_b=m_mla_w_kv_b, m_mla_q_gain=m_mla_q_gain, m_mla_k_gain=m_mla_k_gain, m_mla_w_out=m_mla_w_out, m_sb_w_in=m_sb_w_in, m_sb_q_gain=m_sb_q_gain, m_sb_k_gain=m_sb_k_gain, m_sb_w_out=m_sb_w_out, m_sgu_w_in=m_sgu_w_in, m_sgu_v_gain=m_sgu_v_gain, m_sgu_w_s=m_sgu_w_s, m_sgu_b_s=m_sgu_b_s, m_sgu_w_out=m_sgu_w_out, m_ffn_w_up=m_ffn_w_up, m_ffn_conv_w=m_ffn_conv_w, m_ffn_conv_b=m_ffn_conv_b, m_ffn_w_down=m_ffn_w_down, v_mix_norm=v_mix_norm, v_ffn_norm=v_ffn_norm, v_fox_w_in=v_fox_w_in, v_fox_b_f=v_fox_b_f, v_fox_q_gain=v_fox_q_gain, v_fox_k_gain=v_fox_k_gain, v_fox_w_out=v_fox_w_out, v_mla_w_in=v_mla_w_in, v_mla_q_a_gain=v_mla_q_a_gain, v_mla_kv_a_gain=v_mla_kv_a_gain, v_mla_w_q_b=v_mla_w_q_b, v_mla_w_kv_b=v_mla_w_kv_b, v_mla_q_gain=v_mla_q_gain, v_mla_k_gain=v_mla_k_gain, v_mla_w_out=v_mla_w_out, v_sb_w_in=v_sb_w_in, v_sb_q_gain=v_sb_q_gain, v_sb_k_gain=v_sb_k_gain, v_sb_w_out=v_sb_w_out, v_sgu_w_in=v_sgu_w_in, v_sgu_v_gain=v_sgu_v_gain, v_sgu_w_s=v_sgu_w_s, v_sgu_b_s=v_sgu_b_s, v_sgu_w_out=v_sgu_w_out, v_ffn_w_up=v_ffn_w_up, v_ffn_conv_w=v_ffn_conv_w, v_ffn_conv_b=v_ffn_conv_b, v_ffn_w_down=v_ffn_w_down)
    weights = {n: given[n] for n in TWIN_WEIGHTS}
    shared = {n: given[n] for n in SHARED_INPUTS}
    per_example = {n: given[n] for n in ['x', 'positions']}
    grad_fn = _jax.value_and_grad(_loss, argnums=(0, 1))

    def one_microbatch(ex, loss_target):
        ex = dict(ex)
        diff = ex.pop(TWIN_DIFF_INPUT)
        return grad_fn(weights, diff, {**shared, **ex}, loss_target)

    if N_MICROBATCH == 1:
        loss, (grad_w, grad_x) = one_microbatch(per_example, given["loss_target"])
    else:
        def body(carry, xs):
            loss_sum, grad_sum = carry
            l_k, (gw_k, gx_k) = one_microbatch(xs[0], xs[1])
            with _jax.named_scope("update"):
                return (loss_sum + l_k, _jax.tree.map(_jnp.add, grad_sum, gw_k)), gx_k

        init = (_jnp.zeros((), _jnp.float32), _jax.tree.map(_jnp.zeros_like, weights))
        (loss, grad_w), grad_x = _jax.lax.scan(body, init, (per_example, given["loss_target"]))
    with _jax.named_scope("update"):
        delta_w, new_m, new_v = {}, {}, {}
        for n in TWIN_WEIGHTS:
            delta_w[n], new_m[n], new_v[n] = _adamw(weights[n], grad_w[n], given["m_" + n], given["v_" + n])
    return (loss, grad_x, *[grad_w[n] for n in TWIN_WEIGHTS], *[delta_w[n] for n in TWIN_WEIGHTS],
            *[new_m[n] for n in TWIN_WEIGHTS], *[new_v[n] for n in TWIN_WEIGHTS])
```

```python
import functools
import math

import jax
import jax.numpy as jnp
from jax import lax
from jax.experimental import pallas as pl
from jax.experimental.pallas import tpu as pltpu

F32 = jnp.float32
BF16 = jnp.bfloat16
MESH = pl.DeviceIdType.MESH

N_DEV = 8
N_HEADS = 16
HEAD_DIM = 128
EPS = 1e-6
DEPTH = 4
D_FF = 5632
MLA_ROPE = 64
ROPE_THETA = 10000.0
VMEM_LIMIT = 48 * 1024 * 1024

ADAM_LR, ADAM_B1, ADAM_B2, ADAM_EPS, ADAM_WD, ADAM_STEP = 0.001, 0.9, 0.999, 1e-08, 0.01, 10

WEIGHTS = ['mix_norm', 'ffn_norm', 'fox_w_in', 'fox_b_f', 'fox_q_gain', 'fox_k_gain', 'fox_w_out', 'mla_w_in',
           'mla_q_a_gain', 'mla_kv_a_gain', 'mla_w_q_b', 'mla_w_kv_b', 'mla_q_gain', 'mla_k_gain', 'mla_w_out',
           'sb_w_in', 'sb_q_gain', 'sb_k_gain', 'sb_w_out', 'sgu_w_in', 'sgu_v_gain', 'sgu_w_s', 'sgu_b_s',
           'sgu_w_out', 'ffn_w_up', 'ffn_conv_w', 'ffn_conv_b', 'ffn_w_down']
BIG = ['fox_w_in', 'fox_w_out', 'mla_w_in', 'mla_w_q_b', 'mla_w_kv_b', 'mla_w_out', 'sb_w_in', 'sb_w_out',
       'sgu_w_in', 'sgu_w_out', 'ffn_w_up', 'ffn_w_down']
SMALL = [w for w in WEIGHTS if w not in BIG]
SMALL_SHARDED = {'mla_q_a_gain': 1, 'mla_kv_a_gain': 1, 'sgu_v_gain': 1, 'ffn_conv_w': 2}


def _cparams(n_grid):
    return pltpu.CompilerParams(dimension_semantics=("arbitrary",) * n_grid, vmem_limit_bytes=VMEM_LIMIT)


def _pick(n, cap):
    best = None
    t = 128
    while t <= min(n, cap):
        if n % t == 0:
            best = t
        t += 128
    return best if best is not None else n


def _mm_call(name, a, b, out_shape, a_spec, b_spec, o_spec, grid, dims, acc_shape, add=None):
    nk = grid[2]

    def body(*refs):
        if add is None:
            a_ref, b_ref, o_ref, acc = refs
            add_ref = None
        else:
            a_ref, b_ref, add_ref, o_ref, acc = refs
        k = pl.program_id(2)

        @pl.when(k == 0)
        def _():
            acc[...] = jnp.zeros_like(acc)

        acc[...] += lax.dot_general(a_ref[...].astype(BF16), b_ref[...].astype(BF16), (dims, ((), ())),
                                    preferred_element_type=F32)

        @pl.when(k == nk - 1)
        def _():
            r = acc[...]
            if add_ref is not None:
                r = r + add_ref[...].astype(F32)
            o_ref[...] = r.astype(o_ref.dtype)

    ins = [a, b] + ([] if add is None else [add])
    in_specs = [a_spec, b_spec] + ([] if add is None else [o_spec])
    return pl.pallas_call(body, grid=grid, in_specs=in_specs, out_specs=o_spec, out_shape=out_shape,
                          scratch_shapes=[pltpu.VMEM(acc_shape, F32)], name=name,
                          compiler_params=_cparams(3))(*ins)


def mm_nn(a, b3, out_dtype, name, add=None, joff=0, nj=None):
    m, kk = a.shape
    _, kb, n = b3.shape
    assert kb == kk
    nj = b3.shape[0] - joff if nj is None else nj
    tm, tn, tk = _pick(m, 1024), _pick(n, 1536), _pick(kk, 512)
    nb = n // tn
    return _mm_call(
        name, a, b3, jax.ShapeDtypeStruct((m, nj * n), out_dtype),
        pl.BlockSpec((tm, tk), lambda i, c, k: (i, k)),
        pl.BlockSpec((None, tk, tn), lambda i, c, k: (joff + c // nb, k, c % nb)),
        pl.BlockSpec((tm, tn), lambda i, c, k: (i, c)),
        (m // tm, nj * nb, kk // tk), ((1,), (0,)), (tm, tn), add=add)


def mm_nt(a, b3, out_dtype, name, add=None, joff=0, nj=None):
    m, na = a.shape
    _, ko, n = b3.shape
    nj = b3.shape[0] - joff if nj is None else nj
    assert na == nj * n
    tm, to, tn = _pick(m, 1024), _pick(ko, 1024), _pick(n, 1536)
    nb = n // tn
    return _mm_call(
        name, a, b3, jax.ShapeDtypeStruct((m, ko), out_dtype),
        pl.BlockSpec((tm, tn), lambda i, o, c: (i, c)),
        pl.BlockSpec((None, to, tn), lambda i, o, c: (joff + c // nb, o, c % nb)),
        pl.BlockSpec((tm, to), lambda i, o, c: (i, o)),
        (m // tm, ko // to, nj * nb), ((1,), (1,)), (tm, to), add=add)


def mm_tn(a, b, nj, out_dtype, name):
    s, ko = a.shape
    sb, nb_tot = b.shape
    assert sb == s and nb_tot % nj == 0
    n = nb_tot // nj
    to, tn, ts = _pick(ko, 1024), _pick(n, 1536), _pick(s, 512)
    nb = n // tn
    return _mm_call(
        name, a, b, jax.ShapeDtypeStruct((nj, ko, n), out_dtype),
        pl.BlockSpec((ts, to), lambda o, c, k: (k, o)),
        pl.BlockSpec((ts, tn), lambda o, c, k: (k, c)),
        pl.BlockSpec((None, to, tn), lambda o, c, k: (c // nb, o, c % nb)),
        (ko // to, nj * nb, s // ts), ((0,), (0,)), (to, tn))


def blockk(name, fn, grid, ins, outs):
    n_in = len(ins)
    accs = [o[2] for o in outs]

    def body(*refs):
        vals = fn(*[r[...] for r in refs[:n_in]])
        if not isinstance(vals, (tuple, list)):
            vals = (vals,)
        i, j = pl.program_id(0), pl.program_id(1)
        for r, v, acc in zip(refs[n_in:], vals, accs):
            if acc is None:
                r[...] = v.astype(r.dtype)
            else:
                first = (j == 0) if acc == 'inner' else jnp.logical_and(i == 0, j == 0)

                @pl.when(first)
                def _(r=r, v=v):
                    r[...] = v.astype(r.dtype)

                @pl.when(jnp.logical_not(first))
                def _(r=r, v=v):
                    r[...] += v.astype(r.dtype)

    res = pl.pallas_call(body, grid=grid, in_specs=[s for _, s in ins], out_specs=[o[1] for o in outs],
                         out_shape=[o[0] for o in outs], name=name, compiler_params=_cparams(2))(*[a for a, _ in ins])
    return res


def _sds(shape, dtype):
    return jax.ShapeDtypeStruct(tuple(shape), dtype)


def _rows(tr, w, col=0):
    if col == 'j':
        return pl.BlockSpec((tr, w), lambda i, j: (i, j))
    if callable(col):
        return pl.BlockSpec((tr, w), lambda i, j: (i, col(j)))
    return pl.BlockSpec((tr, w), lambda i, j: (i, col))


def _whole(shape):
    nd = len(shape)
    return pl.BlockSpec(tuple(shape), lambda i, j: (0,) * nd)


def _rms(x, g, n):
    ms = jnp.sum(x * x, axis=-1, keepdims=True) * (1.0 / n)
    return x * lax.rsqrt(ms + EPS) * g


def _sig(x):
    return 1.0 / (1.0 + jnp.exp(-x))


def _gelu(x):
    return 0.5 * x * (1.0 + jnp.tanh(math.sqrt(2.0 / math.pi) * (x + 0.044715 * (x * x * x))))


def _lane_iota(shape):
    return lax.broadcasted_iota(jnp.int32, shape, len(shape) - 1)


def _rope(x, cos, sin):
    lane = _lane_iota(x.shape)
    half = MLA_ROPE // 2
    swapped = jnp.where(lane < half, pltpu.roll(x, HEAD_DIM - half, 1), pltpu.roll(x, half, 1))
    sign = jnp.where(lane < half, -1.0, 1.0)
    return x * cos + swapped * (sin * sign)


def _rope_t(dy, cos, sin):
    lane = _lane_iota(dy.shape)
    half = MLA_ROPE // 2
    t = dy * sin
    swapped = jnp.where(lane < half, pltpu.roll(t, HEAD_DIM - half, 1), pltpu.roll(t, half, 1))
    sign = jnp.where(lane < half, 1.0, -1.0)
    return dy * cos + swapped * sign


def rms_fwd(name, x, gain, n, tr, width, xcol=0, nh=1, out_dtype=BF16, out_cols=None):
    r = x.shape[0]
    out_cols = width * nh if out_cols is None else out_cols
    xspec = _rows(tr, width, (lambda j: xcol + j) if nh > 1 else xcol)
    ospec = _rows(tr, width, 'j' if nh > 1 else 0)
    return blockk(name, lambda xb, g: _rms(xb.astype(F32), g, n), (r // tr, nh),
                  [(x, xspec), (gain, _whole(gain.shape))], [(_sds((r, out_cols), out_dtype), ospec, None)])[0]


def rms_bwd(name, x, gain, dy, n, tr, width, xcol=0, nh=1, dycol=0, resid=None, out_dtype=F32):
    r = x.shape[0]
    xspec = _rows(tr, width, (lambda j: xcol + j) if nh > 1 else xcol)
    dyspec = _rows(tr, width, (lambda j: dycol + j) if nh > 1 else dycol)
    ospec = _rows(tr, width, 'j' if nh > 1 else 0)

    def fn(xb, g, dyb, *rest):
        _, vjp = jax.vjp(lambda a, b: _rms(a, b, n), xb.astype(F32), g)
        dx, dg = vjp(dyb.astype(F32))
        if rest:
            dx = dx + rest[0].astype(F32)
        return dx, dg

    ins = [(x, xspec), (gain, _whole(gain.shape)), (dy, dyspec)]
    if resid is not None:
        ins.append((resid, ospec))
    return blockk(name, fn, (r // tr, nh), ins,
                  [(_sds((r, width * nh), out_dtype), ospec, None), (_sds(gain.shape, F32), _whole(gain.shape), 'all')])


def _nt(a, b):
    return lax.dot_general(a, b, (((1,), (1,)), ((), ())), preferred_element_type=F32)


def _tn(a, b):
    return lax.dot_general(a, b, (((0,), (0,)), ((), ())), preferred_element_type=F32)


def _nn(a, b):
    return lax.dot_general(a, b, (((1,), (0,)), ((), ())), preferred_element_type=F32)


def _attn_specs(tq, qoff, koff, voff, extra, bias, q2off):
    d = HEAD_DIM
    specs = [pl.BlockSpec((tq, d), lambda h, i, j: (i, qoff + h)),
             pl.BlockSpec((tq, d), lambda h, i, j: (jnp.minimum(i, j), koff + h)),
             pl.BlockSpec((tq, d), lambda h, i, j: (jnp.minimum(i, j), voff + h))]
    if extra:
        specs += [pl.BlockSpec((tq, d), lambda h, i, j: (i, q2off + h)),
                  pl.BlockSpec((tq, d), lambda h, i, j: (jnp.minimum(i, j), 0))]
    if bias:
        specs += [pl.BlockSpec((None, tq, 1), lambda h, i, j: (h, i, 0)),
                  pl.BlockSpec((None, 1, tq), lambda h, i, j: (h, 0, jnp.minimum(i, j)))]
    return specs


def _scores(q_ref, k_ref, q2_ref, k2_ref, cc_ref, cr_ref, scale, qi, kj, tq):
    s = _nt(q_ref[...].astype(BF16), k_ref[...].astype(BF16))
    if q2_ref is not None:
        s = s + _nt(q2_ref[...].astype(BF16), k2_ref[...].astype(BF16))
    s = s * scale
    if cc_ref is not None:
        s = s + (cc_ref[...] - cr_ref[...])
    row = qi * tq + lax.broadcasted_iota(jnp.int32, (tq, tq), 0)
    col = kj * tq + lax.broadcasted_iota(jnp.int32, (tq, tq), 1)
    return s, col <= row


def attn_fwd(name, q, k, v, scale, *, qoff=0, koff=0, voff=0, q2=None, k2=None, q2off=0, cum=None, tq=256,
             exact_o=False):
    s_len = q.shape[0]
    nq = s_len // tq
    extra, bias = q2 is not None, cum is not None
    n_in = 3 + 2 * extra + 2 * bias

    def body(*refs):
        q_ref, k_ref, v_ref = refs[:3]
        p = 3
        q2_ref = k2_ref = cc_ref = cr_ref = None
        if extra:
            q2_ref, k2_ref = refs[p:p + 2]
            p += 2
        if bias:
            cc_ref, cr_ref = refs[p:p + 2]
            p += 2
        o_ref, lse_ref, m_s, l_s, acc_s = refs[p:]
        qi, kj = pl.program_id(1), pl.program_id(2)

        @pl.when(kj == 0)
        def _():
            m_s[...] = jnp.full_like(m_s, -jnp.inf)
            l_s[...] = jnp.zeros_like(l_s)
            acc_s[...] = jnp.zeros_like(acc_s)

        @pl.when(kj <= qi)
        def _():
            s, allowed = _scores(q_ref, k_ref, q2_ref, k2_ref, cc_ref, cr_ref, scale, qi, kj, tq)
            s = jnp.where(allowed, s, -jnp.inf)
            m_old = m_s[...]
            m_new = jnp.maximum(m_old, jnp.max(s, axis=-1, keepdims=True))
            alpha = jnp.exp(m_old - m_new)
            pr = jnp.exp(s - m_new)
            l_s[...] = alpha * l_s[...] + jnp.sum(pr, axis=-1, keepdims=True)
            vb = v_ref[...].astype(BF16)
            pv = _nn(pr.astype(BF16), vb)
            if exact_o:
                pv = pv + _nn((pr - pr.astype(BF16).astype(F32)).astype(BF16), vb)
            acc_s[...] = alpha * acc_s[...] + pv
            m_s[...] = m_new

        @pl.when(kj == qi)
        def _():
            o_ref[...] = (acc_s[...] / l_s[...]).astype(o_ref.dtype)
            lse_ref[...] = m_s[...] + jnp.log(l_s[...])

    ins = [q, k, v] + ([q2, k2] if extra else []) + (list(cum) if bias else [])
    d = HEAD_DIM
    return pl.pallas_call(
        body, grid=(N_HEADS, nq, nq), in_specs=_attn_specs(tq, qoff, koff, voff, extra, bias, q2off),
        out_specs=[pl.BlockSpec((tq, d), lambda h, i, j: (i, h)), pl.BlockSpec((None, tq, 1), lambda h, i, j: (h, i, 0))],
        out_shape=[_sds((s_len, N_HEADS * d), F32 if exact_o else BF16), _sds((N_HEADS, s_len, 1), F32)],
        scratch_shapes=[pltpu.VMEM((tq, 1), F32), pltpu.VMEM((tq, 1), F32), pltpu.VMEM((tq, d), F32)],
        name=name, compiler_params=_cparams(3))(*ins)


def attn_bwd(name, q, k, v, o, do, lse, scale, *, qoff=0, koff=0, voff=0, q2=None, k2=None, q2off=0, cum=None, tq=256):
    s_len = q.shape[0]
    nq = s_len // tq
    extra, bias = q2 is not None, cum is not None
    d = HEAD_DIM
    n_in = 6 + 2 * extra + 2 * bias

    def body(*refs):
        q_ref, k_ref, v_ref = refs[:3]
        p = 3
        q2_ref = k2_ref = cc_ref = cr_ref = None
        if extra:
            q2_ref, k2_ref = refs[p:p + 2]
            p += 2
        if bias:
            cc_ref, cr_ref = refs[p:p + 2]
            p += 2
        o_ref, do_ref, lse_ref = refs[p:p + 3]
        p += 3
        dq_ref, dk_ref, dv_ref = refs[p:p + 3]
        p += 3
        dq2_ref = dk2_ref = dcs_ref = None
        if extra:
            dq2_ref, dk2_ref = refs[p:p + 2]
            p += 2
        if bias:
            dcs_ref = refs[p]
            p += 1
        dq_s, delta_s = refs[p:p + 2]
        dq2_s = refs[p + 2] if extra else None
        h, qi, kj = pl.program_id(0), pl.program_id(1), pl.program_id(2)

        @pl.when(jnp.logical_and(qi == 0, kj == 0))
        def _():
            dk_ref[...] = jnp.zeros_like(dk_ref)
            dv_ref[...] = jnp.zeros_like(dv_ref)

        if extra:
            @pl.when(jnp.logical_and(h == 0, jnp.logical_and(qi == 0, kj == 0)))
            def _():
                dk2_ref[...] = jnp.zeros_like(dk2_ref)

        @pl.when(kj == 0)
        def _():
            dq_s[...] = jnp.zeros_like(dq_s)
            if extra:
                dq2_s[...] = jnp.zeros_like(dq2_s)
            delta_s[...] = jnp.sum(do_ref[...].astype(F32) * o_ref[...].astype(F32), axis=-1, keepdims=True)

        if bias:
            @pl.when(kj > qi)
            def _():
                dcs_ref[...] = jnp.zeros_like(dcs_ref)

        @pl.when(kj <= qi)
        def _():
            s, allowed = _scores(q_ref, k_ref, q2_ref, k2_ref, cc_ref, cr_ref, scale, qi, kj, tq)
            pr = jnp.where(allowed, jnp.exp(s - lse_ref[...]), 0.0)
            dob = do_ref[...].astype(BF16)
            dp = _nt(dob, v_ref[...].astype(BF16))
            ds = pr * (dp - delta_s[...])
            dsb = (ds * scale).astype(BF16)
            ks = pl.ds(pl.multiple_of(kj * tq, tq), tq)
            dq_s[...] += _nn(dsb, k_ref[...].astype(BF16))
            dk_ref[ks, :] += _tn(dsb, q_ref[...].astype(BF16))
            dv_ref[ks, :] += _tn(pr.astype(BF16), dob)
            if extra:
                dq2_s[...] += _nn(dsb, k2_ref[...].astype(BF16))
                dk2_ref[ks, :] += _tn(dsb, q2_ref[...].astype(BF16))
            if bias:
                dcs_ref[...] = jnp.sum(ds, axis=0, keepdims=True)

        @pl.when(kj == qi)
        def _():
            dq_ref[...] = dq_s[...]
            if extra:
                dq2_ref[...] = dq2_s[...]

    ins = [q, k, v] + ([q2, k2] if extra else []) + (list(cum) if bias else []) + [o, do, lse]
    in_specs = _attn_specs(tq, qoff, koff, voff, extra, bias, q2off) + [
        pl.BlockSpec((tq, d), lambda h, i, j: (i, h)), pl.BlockSpec((tq, d), lambda h, i, j: (i, h)),
        pl.BlockSpec((None, tq, 1), lambda h, i, j: (h, i, 0))]
    full = _sds((s_len, N_HEADS * d), F32)
    out_shape = [full, full, full]
    out_specs = [pl.BlockSpec((tq, d), lambda h, i, j: (i, h)), pl.BlockSpec((s_len, d), lambda h, i, j: (0, h)),
                 pl.BlockSpec((s_len, d), lambda h, i, j: (0, h))]
    scratch = [pltpu.VMEM((tq, d), F32), pltpu.VMEM((tq, 1), F32)]
    if extra:
        out_shape += [full, _sds((s_len, d), F32)]
        out_specs += [pl.BlockSpec((tq, d), lambda h, i, j: (i, h)), pl.BlockSpec((s_len, d), lambda h, i, j: (0, 0))]
        scratch.append(pltpu.VMEM((tq, d), F32))
    if bias:
        out_shape.append(_sds((N_HEADS, nq, 1, s_len), F32))
        out_specs.append(pl.BlockSpec((None, None, 1, tq), lambda h, i, j: (h, i, 0, j)))
    return pl.pallas_call(body, grid=(N_HEADS, nq, nq), in_specs=in_specs, out_specs=out_specs, out_shape=out_shape,
                          scratch_shapes=scratch, name=name, compiler_params=_cparams(3))(*ins)


def _dot3(x, m01):
    x1 = x.astype(BF16)
    r1 = x - x1.astype(F32)
    x2 = r1.astype(BF16)
    x3 = (r1 - x2.astype(F32)).astype(BF16)
    return _nn(x1, m01) + _nn(x2, m01) + _nn(x3, m01)


def _sb_terms(q_ref, k_ref, scale, qi, kb, tq):
    z = _nt(q_ref[...].astype(BF16), k_ref[...].astype(BF16)) * scale
    row = qi * tq + lax.broadcasted_iota(jnp.int32, (tq, tq), 0)
    col = kb * tq + lax.broadcasted_iota(jnp.int32, (tq, tq), 1)
    strict = col < row
    lg = jnp.log(1.0 + jnp.exp(-jnp.abs(z)))
    log_keep = jnp.where(strict, -(jnp.maximum(z, 0.0) + lg), 0.0)
    log_beta = jnp.minimum(z, 0.0) - lg
    return z, strict, log_keep, log_beta


def _tri(tq, pred):
    a = lax.broadcasted_iota(jnp.int32, (tq, tq), 0)
    b = lax.broadcasted_iota(jnp.int32, (tq, tq), 1)
    return jnp.where(pred(a, b), 1.0, 0.0).astype(BF16)


def sb_fwd(name, qn, kn, qkv, scale, voff, tq=256):
    s_len = qn.shape[0]
    nq = s_len // tq
    d = HEAD_DIM

    def body(q_ref, k_ref, v_ref, o_ref, car_ref, ca_s, acc_s):
        qi, kj = pl.program_id(1), pl.program_id(2)
        kb = qi - kj

        @pl.when(kj == 0)
        def _():
            ca_s[...] = jnp.zeros_like(ca_s)
            acc_s[...] = jnp.zeros_like(acc_s)

        @pl.when(kj <= qi)
        def _():
            z, strict, log_keep, log_beta = _sb_terms(q_ref, k_ref, scale, qi, kb, tq)
            ca = ca_s[...]
            car_ref[...] = ca
            after = ca + _dot3(log_keep, _tri(tq, lambda m, j: m > j))
            a = jnp.where(strict, jnp.exp(log_beta + after), 0.0)
            acc_s[...] += _nn(a.astype(BF16), v_ref[...].astype(BF16))
            ca_s[...] = ca + jnp.sum(log_keep, axis=-1, keepdims=True)

        @pl.when(kj == qi)
        def _():
            o_ref[...] = acc_s[...].astype(o_ref.dtype)

    kblk = lambda i, j: jnp.maximum(i - j, 0)
    return pl.pallas_call(
        body, grid=(N_HEADS, nq, nq),
        in_specs=[pl.BlockSpec((tq, d), lambda h, i, j: (i, h)), pl.BlockSpec((tq, d), lambda h, i, j: (kblk(i, j), h)),
                  pl.BlockSpec((tq, d), lambda h, i, j: (kblk(i, j), voff + h))],
        out_specs=[pl.BlockSpec((tq, d), lambda h, i, j: (i, h)),
                   pl.BlockSpec((None, None, tq, 1), lambda h, i, j: (h, kblk(i, j), i, 0))],
        out_shape=[_sds((s_len, N_HEADS * d), BF16), _sds((N_HEADS, nq, s_len, 1), F32)],
        scratch_shapes=[pltpu.VMEM((tq, 1), F32), pltpu.VMEM((tq, d), F32)],
        name=name, compiler_params=_cparams(3))(qn, kn, qkv)


def sb_bwd(name, qn, kn, qkv, do, carries, scale, voff, tq=256):
    s_len = qn.shape[0]
    nq = s_len // tq
    d = HEAD_DIM

    def body(q_ref, k_ref, v_ref, do_ref, car_ref, dq_ref, dk_ref, dv_ref, dq_s, cg_s):
        qi, kj = pl.program_id(1), pl.program_id(2)

        @pl.when(jnp.logical_and(qi == 0, kj == 0))
        def _():
            dk_ref[...] = jnp.zeros_like(dk_ref)
            dv_ref[...] = jnp.zeros_like(dv_ref)

        @pl.when(kj == 0)
        def _():
            dq_s[...] = jnp.zeros_like(dq_s)
            cg_s[...] = jnp.zeros_like(cg_s)

        @pl.when(kj <= qi)
        def _():
            z, strict, log_keep, log_beta = _sb_terms(q_ref, k_ref, scale, qi, kj, tq)
            after = car_ref[...] + _dot3(log_keep, _tri(tq, lambda m, j: m > j))
            a = jnp.where(strict, jnp.exp(log_beta + after), 0.0)
            dob = do_ref[...].astype(BF16)
            g = a * _nt(dob, v_ref[...].astype(BF16))
            cg = cg_s[...]
            big_g = cg + _dot3(g, _tri(tq, lambda m, j: m < j))
            cg_s[...] = cg + jnp.sum(g, axis=-1, keepdims=True)
            beta = jnp.exp(log_beta)
            dz = jnp.where(strict, g * (1.0 - beta) - big_g * beta, 0.0)
            dzb = (dz * scale).astype(BF16)
            ks = pl.ds(pl.multiple_of(kj * tq, tq), tq)
            dq_s[...] += _nn(dzb, k_ref[...].astype(BF16))
            dk_ref[ks, :] += _tn(dzb, q_ref[...].astype(BF16))
            dv_ref[ks, :] += _tn(a.astype(BF16), dob)

        @pl.when(kj == qi)
        def _():
            dq_ref[...] = dq_s[...]

    kblk = lambda i, j: jnp.minimum(i, j)
    full = _sds((s_len, N_HEADS * d), F32)
    return pl.pallas_call(
        body, grid=(N_HEADS, nq, nq),
        in_specs=[pl.BlockSpec((tq, d), lambda h, i, j: (i, h)), pl.BlockSpec((tq, d), lambda h, i, j: (kblk(i, j), h)),
                  pl.BlockSpec((tq, d), lambda h, i, j: (kblk(i, j), voff + h)),
                  pl.BlockSpec((tq, d), lambda h, i, j: (i, h)),
                  pl.BlockSpec((None, None, tq, 1), lambda h, i, j: (h, kblk(i, j), i, 0))],
        out_specs=[pl.BlockSpec((tq, d), lambda h, i, j: (i, h)), pl.BlockSpec((s_len, d), lambda h, i, j: (0, h)),
                   pl.BlockSpec((s_len, d), lambda h, i, j: (0, h))],
        out_shape=[full, full, full],
        scratch_shapes=[pltpu.VMEM((tq, d), F32), pltpu.VMEM((tq, 1), F32)],
        name=name, compiler_params=_cparams(3))(qn, kn, qkv, do, carries)


def _cumsum_rows(x, reverse):
    n = x.shape[0] // HEAD_DIM
    tri = _tri(HEAD_DIM, (lambda a, b: b >= a) if reverse else (lambda a, b: b <= a))
    pieces = [None] * n
    carry = jnp.zeros((1, HEAD_DIM), F32)
    order = range(n - 1, -1, -1) if reverse else range(n)
    for blk in order:
        xb = x[blk * HEAD_DIM:(blk + 1) * HEAD_DIM, :]
        x1 = xb.astype(BF16)
        r1 = xb - x1.astype(F32)
        x2 = r1.astype(BF16)
        x3 = (r1 - x2.astype(F32)).astype(BF16)
        c = _nn(tri, x1) + _nn(tri, x2) + _nn(tri, x3) + carry
        pieces[blk] = c
        carry = c[0:1, :] if reverse else c[HEAD_DIM - 1:HEAD_DIM, :]
    return jnp.concatenate(pieces, axis=0)


def _log_sigmoid(x):
    return jnp.minimum(x, 0.0) - jnp.log(1.0 + jnp.exp(-jnp.abs(x)))


def fox_gate_fwd(name, fl, bf):
    return blockk(name, lambda f, b: _cumsum_rows(_log_sigmoid(f + b), False), (1, 1),
                  [(fl, _whole(fl.shape)), (bf, _whole(bf.shape))], [(_sds(fl.shape, F32), _whole(fl.shape), None)])[0]


def fox_gate_bwd(name, fl, bf, dcum):
    def fn(f, b, dc):
        dlogf = _cumsum_rows(dc, True)
        dfl = dlogf * _sig(-(f + b))
        return dfl, jnp.sum(dfl, axis=0, keepdims=True)

    return blockk(name, fn, (1, 1), [(fl, _whole(fl.shape)), (bf, _whole(bf.shape)), (dcum, _whole(dcum.shape))],
                  [(_sds(fl.shape, F32), _whole(fl.shape), None), (_sds(bf.shape, F32), _whole(bf.shape), None)])


def rope_fwd(name, x, gain, cos, sin, tr, xcol, nh):
    r = x.shape[0]
    xspec = _rows(tr, HEAD_DIM, lambda j: xcol + j)
    tspec = _rows(tr, HEAD_DIM, 0)
    return blockk(name, lambda xb, g, c, s: _rope(_rms(xb.astype(F32), g, MLA_ROPE), c, s), (r // tr, nh),
                  [(x, xspec), (gain, _whole(gain.shape)), (cos, tspec), (sin, tspec)],
                  [(_sds((r, HEAD_DIM * nh), BF16), _rows(tr, HEAD_DIM, 'j'), None)])[0]


def rope_bwd(name, x, gain, cos, sin, dy, tr, xcol, nh):
    r = x.shape[0]
    xspec = _rows(tr, HEAD_DIM, lambda j: xcol + j)
    tspec = _rows(tr, HEAD_DIM, 0)
    ospec = _rows(tr, HEAD_DIM, 'j')

    def fn(xb, g, c, s, dyb):
        _, vjp = jax.vjp(lambda a, b: _rms(a, b, MLA_ROPE), xb.astype(F32), g)
        return vjp(_rope_t(dyb.astype(F32), c, s))

    return blockk(name, fn, (r // tr, nh),
                  [(x, xspec), (gain, _whole(gain.shape)), (cos, tspec), (sin, tspec), (dy, ospec)],
                  [(_sds((r, HEAD_DIM * nh), F32), ospec, None), (_sds(gain.shape, F32), _whole(gain.shape), 'all')])


def sgu_pre_fwd(name, pre, gain, tr):
    s_len, w2 = pre.shape
    w = w2 // 2
    return blockk(name, lambda pu, pv, g: (_gelu(pu.astype(F32)), _rms(_gelu(pv.astype(F32)), g, w)), (s_len // tr, 1),
                  [(pre, _rows(tr, w, 0)), (pre, _rows(tr, w, 1)), (gain, _whole(gain.shape))],
                  [(_sds((s_len, w), BF16), _rows(tr, w, 0), None), (_sds((s_len, w), BF16), _rows(tr, w, 0), None)])


def sgu_pre_bwd(name, pre, gain, du, dvn, tr):
    s_len, w2 = pre.shape
    w = w2 // 2

    def fn(pu, pv, g, dub, dvb):
        _, vjp_u = jax.vjp(_gelu, pu.astype(F32))
        _, vjp_v = jax.vjp(lambda a, b: _rms(_gelu(a), b, w), pv.astype(F32), g)
        dpv, dg = vjp_v(dvb.astype(F32))
        return vjp_u(dub.astype(F32))[0], dpv, dg

    spec = _rows(tr, w, 0)
    return blockk(name, fn, (s_len // tr, 1),
                  [(pre, spec), (pre, _rows(tr, w, 1)), (gain, _whole(gain.shape)), (du, spec), (dvn, spec)],
                  [(_sds((s_len, w), BF16), spec, None), (_sds((s_len, w), BF16), spec, None),
                   (_sds(gain.shape, F32), _whole(gain.shape), 'all')])


def _ws_masked(ws):
    t = ws.shape[0]
    a = lax.broadcasted_iota(jnp.int32, (t, t), 0)
    b = lax.broadcasted_iota(jnp.int32, (t, t), 1)
    return jnp.where(b <= a, ws, 0.0)


def sgu_mix_fwd(name, vn, u, ws, bs3):
    s_len, w = vn.shape
    t = ws.shape[1]

    def fn(vb, ub, wsb, bsb):
        mixed = _nn(_ws_masked(wsb).astype(BF16), vb.astype(BF16)) + bsb
        return ub.astype(F32) * mixed

    blk = pl.BlockSpec((t, t), lambda i, j: (i, j))
    return blockk(name, fn, (s_len // t, w // t),
                  [(vn, blk), (u, blk), (ws, pl.BlockSpec((None, t, t), lambda i, j: (j, 0, 0))),
                   (bs3, pl.BlockSpec((None, t, 1), lambda i, j: (j, 0, 0)))],
                  [(_sds((s_len, w), BF16), blk, None)])[0]


def sgu_mix_bwd(name, vn, u, ws, bs3, dprod):
    s_len, w = vn.shape
    t = ws.shape[1]

    def fn(vb, ub, wsb, bsb, dpb):
        wm = _ws_masked(wsb).astype(BF16)
        vb16 = vb.astype(BF16)
        mixed = _nn(wm, vb16) + bsb
        dp = dpb.astype(F32)
        du = dp * mixed
        dm = dp * ub.astype(F32)
        dmb = dm.astype(BF16)
        dvn = _tn(wm, dmb)
        dws = _ws_masked(_nt(dmb, vb16))
        return du, dvn, dws, jnp.sum(dm, axis=-1, keepdims=True)

    blk = pl.BlockSpec((t, t), lambda g, n: (n, g))
    wspec = pl.BlockSpec((None, t, t), lambda g, n: (g, 0, 0))
    bspec = pl.BlockSpec((None, t, 1), lambda g, n: (g, 0, 0))
    return blockk(name, fn, (w // t, s_len // t), [(vn, blk), (u, blk), (ws, wspec), (bs3, bspec), (dprod, blk)],
                  [(_sds((s_len, w), BF16), blk, None), (_sds((s_len, w), BF16), blk, None),
                   (_sds(ws.shape, F32), wspec, 'inner'), (_sds(bs3.shape, F32), bspec, 'inner')])


def _shift_down(x, k):
    row = lax.broadcasted_iota(jnp.int32, x.shape, 0)
    return jnp.where(row >= k, pltpu.roll(x, k, 0), 0.0)


def _shift_up(x, k):
    n = x.shape[0]
    row = lax.broadcasted_iota(jnp.int32, x.shape, 0)
    return jnp.where(row < n - k, pltpu.roll(x, n - k, 0), 0.0)


def _conv(up, cw, cb):
    return cb + cw[0:1, :] * _shift_down(up, 2) + cw[1:2, :] * _shift_down(up, 1) + cw[2:3, :] * up


def _conv_specs(s_len, tc, nf):
    g = pl.BlockSpec((s_len, tc), lambda i, j: (0, j))
    v = pl.BlockSpec((s_len, tc), lambda i, j: (0, j + nf))
    wg = pl.BlockSpec((3, tc), lambda i, j: (0, j))
    wv = pl.BlockSpec((3, tc), lambda i, j: (0, j + nf))
    bg = pl.BlockSpec((1, tc), lambda i, j: (0, j))
    bv = pl.BlockSpec((1, tc), lambda i, j: (0, j + nf))
    return g, v, wg, wv, bg, bv


def conv_fwd(name, up, cw, cb, tc=256):
    s_len, f2 = up.shape
    f = f2 // 2
    nf = f // tc
    g, v, wg, wv, bg, bv = _conv_specs(s_len, tc, nf)

    def fn(ug, uv, cwg, cwv, cbg, cbv):
        yg = _conv(ug.astype(F32), cwg, cbg)
        yv = _conv(uv.astype(F32), cwv, cbv)
        return yg * _sig(yg) * yv

    return blockk(name, fn, (1, nf), [(up, g), (up, v), (cw, wg), (cw, wv), (cb, bg), (cb, bv)],
                  [(_sds((s_len, f), BF16), g, None)])[0]


def conv_bwd(name, up, cw, cb, dact, tc=128):
    s_len, f2 = up.shape
    f = f2 // 2
    nf = f // tc
    g, v, wg, wv, bg, bv = _conv_specs(s_len, tc, nf)

    def half(dy, u, cwh):
        u1, u2 = _shift_down(u, 1), _shift_down(u, 2)
        dup = cwh[2:3, :] * dy + cwh[1:2, :] * _shift_up(dy, 1) + cwh[0:1, :] * _shift_up(dy, 2)
        dcw = jnp.concatenate([jnp.sum(dy * u2, axis=0, keepdims=True), jnp.sum(dy * u1, axis=0, keepdims=True),
                               jnp.sum(dy * u, axis=0, keepdims=True)], axis=0)
        return dup, dcw, jnp.sum(dy, axis=0, keepdims=True)

    def fn(ug, uv, cwg, cwv, cbg, cbv, da):
        ug, uv, da = ug.astype(F32), uv.astype(F32), da.astype(F32)
        yg = _conv(ug, cwg, cbg)
        yv = _conv(uv, cwv, cbv)
        sg = _sig(yg)
        dyv = da * (yg * sg)
        dyg = da * yv * (sg * (1.0 + yg * (1.0 - sg)))
        dug, dcwg, dcbg = half(dyg, ug, cwg)
        duv, dcwv, dcbv = half(dyv, uv, cwv)
        return dug, duv, dcwg, dcwv, dcbg, dcbv

    return blockk(name, fn, (1, nf), [(up, g), (up, v), (cw, wg), (cw, wv), (cb, bg), (cb, bv), (dact, g)],
                  [(_sds((s_len, f), BF16), g, None), (_sds((s_len, f), BF16), g, None),
                   (_sds((3, f), F32), wg, None), (_sds((3, f), F32), wg, None),
                   (_sds((1, f), F32), bg, None), (_sds((1, f), F32), bg, None)])


def loss_head(name, y, target, tr=256):
    s_len, d = y.shape

    def fn(yb, tb):
        e = yb - tb
        return e * (1.0 / d), jnp.sum(e * e, axis=0, keepdims=True)

    spec = _rows(tr, d, 0)
    return blockk(name, fn, (s_len // tr, 1), [(y, spec), (target, spec)],
                  [(_sds((s_len, d), F32), spec, None), (_sds((1, d), F32), _whole((1, d)), 'all')])


def _adam(w, g, m, v):
    m = ADAM_B1 * m + (1.0 - ADAM_B1) * g
    v = ADAM_B2 * v + (1.0 - ADAM_B2) * (g * g)
    m_hat = m / (1.0 - ADAM_B1 ** ADAM_STEP)
    v_hat = v / (1.0 - ADAM_B2 ** ADAM_STEP)
    delta = -ADAM_LR * (m_hat / (jnp.sqrt(v_hat) + ADAM_EPS) + ADAM_WD * w)
    return delta, m, v


def adam_sum(name, parts, w, m, v):
    n_layers, r, c = w.shape
    n_parts = parts[0].shape[0]
    assert len(parts) == n_layers
    tr = r
    for cand in (512, 256, 128, 64, 32, 16):
        if r % cand == 0 and cand * c * 4 <= 1024 * 1024:
            tr = cand
            break

    def body(*refs):
        part_refs = refs[:n_layers]
        w_ref, m_ref, v_ref, g_out, d_out, m_out, v_out = refs[n_layers:]
        layer = pl.program_id(0)
        for ll in range(n_layers):
            @pl.when(layer == ll)
            def _(ll=ll):
                g = part_refs[ll][0].astype(F32)
                for k in range(1, n_parts):
                    g = g + part_refs[ll][k].astype(F32)
                delta, m_new, v_new = _adam(w_ref[...], g, m_ref[...], v_ref[...])
                g_out[...] = g
                d_out[...] = delta
                m_out[...] = m_new
                v_out[...] = v_new

    spec = pl.BlockSpec((None, tr, c), lambda l, i: (l, i, 0))
    part_specs = [pl.BlockSpec((n_parts, tr, c), lambda l, i, ll=ll: (0, jnp.where(l == ll, i, 0), 0))
                  for ll in range(n_layers)]
    out = _sds((n_layers, r, c), F32)
    return pl.pallas_call(body, grid=(n_layers, r // tr), in_specs=part_specs + [spec] * 3, out_specs=[spec] * 4,
                          out_shape=[out] * 4, name=name, compiler_params=_cparams(2))(*parts, w, m, v)


def sum_parts(name, parts, tr=256):
    n_parts, r, c = parts.shape

    def fn(pb):
        g = pb[0]
        for k in range(1, n_parts):
            g = g + pb[k]
        return g

    return blockk(name, fn, (r // tr, 1), [(parts, pl.BlockSpec((n_parts, tr, c), lambda i, j: (0, i, 0)))],
                  [(_sds((r, c), F32), _rows(tr, c, 0), None)])[0]


def adam_flat(name, w, g, m, v, tr=256):
    r, c = w.shape
    spec = _rows(tr, c, 0)
    return blockk(name, lambda wb, gb, mb, vb: _adam(wb, gb, mb, vb), (r // tr, 1),
                  [(w, spec), (g, spec), (m, spec), (v, spec)], [(_sds((r, c), F32), spec, None)] * 3)


_ANY = pl.BlockSpec(memory_space=pl.ANY)


def _place():
    return lax.axis_index("x"), lax.axis_index("y"), lax.axis_index("c")


def _slot(px, py, pc):
    return 4 * px + 2 * py + pc


def all_gather(name, items):
    n = len(items)

    def body(*refs):
        xs, outs = refs[:n], refs[n:2 * n]
        send_sems, recv_sems, local_sems = refs[2 * n:]
        x, y, c = _place()
        me, sibling = (x, y, c), (x, y, 1 - c)
        chips = [(1 - x, y), (x, 1 - y), (1 - x, 1 - y)]

        def copy(t, k, block, to, src=None):
            dst = outs[t].at[_slot(*block)]
            return pltpu.make_async_remote_copy(src_ref=dst if src is None else src, dst_ref=dst,
                                                send_sem=send_sems.at[7 * t + k], recv_sem=recv_sems.at[7 * t + k],
                                                device_id=to, device_id_type=MESH)

        mine = [pltpu.make_async_copy(xs[t], outs[t].at[_slot(*me)], local_sems.at[t]) for t in range(n)]
        for cp in mine:
            cp.start()
        started = []
        for t in range(n):
            started.append(copy(t, 0, me, sibling, src=xs[t]))
            started += [copy(t, 1 + j, me, (*chip, c), src=xs[t]) for j, chip in enumerate(chips)]
        for cp in started:
            cp.start()
        for j, chip in enumerate(chips):
            for t in range(n):
                copy(t, 1 + j, (*chip, c), me).wait_recv()
                passed = copy(t, 4 + j, (*chip, c), sibling)
                passed.start()
                started.append(passed)
        for t in range(n):
            copy(t, 0, sibling, me).wait_recv()
            for j, chip in enumerate(chips):
                copy(t, 4 + j, (*chip, 1 - c), me).wait_recv()
        for cp in started:
            cp.wait_send()
        for cp in mine:
            cp.wait()

    return pl.pallas_call(
        body, in_specs=[_ANY] * n, out_specs=[_ANY] * n,
        out_shape=[_sds((N_DEV,) + a.shape, a.dtype) for a in items],
        scratch_shapes=[pltpu.SemaphoreType.DMA((7 * n,)), pltpu.SemaphoreType.DMA((7 * n,)), pltpu.SemaphoreType.DMA((n,))],
        name=name)(*items)


def scatter_parts(name, items):
    n = len(items)

    def body(*refs):
        gs, outs = refs[:n], refs[n:2 * n]
        send_sems, recv_sems, local_sems = refs[2 * n:]
        x, y, c = _place()
        my_slot = _slot(x, y, c)
        own = [pltpu.make_async_copy(gs[t].at[my_slot], outs[t].at[my_slot], local_sems.at[t]) for t in range(n)]
        for cp in own:
            cp.start()
        pairs = []
        for t in range(n):
            for k in range(1, N_DEV):
                peer = ((1 - x) if k & 4 else x, (1 - y) if k & 2 else y, (1 - c) if k & 1 else c)
                sems = dict(send_sem=send_sems.at[7 * t + k - 1], recv_sem=recv_sems.at[7 * t + k - 1],
                            device_id=peer, device_id_type=MESH)
                send = pltpu.make_async_remote_copy(src_ref=gs[t].at[_slot(*peer)], dst_ref=outs[t].at[my_slot], **sems)
                land = pltpu.make_async_remote_copy(src_ref=gs[t].at[my_slot], dst_ref=outs[t].at[_slot(*peer)], **sems)
                send.start()
                pairs.append((send, land))
        for send, land in pairs:
            land.wait_recv()
            send.wait_send()
        for cp in own:
            cp.wait()

    return pl.pallas_call(
        body, in_specs=[_ANY] * n, out_specs=[_ANY] * n, out_shape=[_sds(g.shape, g.dtype) for g in items],
        scratch_shapes=[pltpu.SemaphoreType.DMA((7 * n,)), pltpu.SemaphoreType.DMA((7 * n,)), pltpu.SemaphoreType.DMA((n,))],
        name=name)(*items)


def _pad_lanes(a, width=HEAD_DIM):
    return jnp.pad(a, [(0, 0)] * (a.ndim - 1) + [(0, width - a.shape[-1])])


def _slabs(full, n):
    k = full.shape[0]
    return full.reshape(k, N_DEV, n).transpose(1, 0, 2)


def fox_fwd(a, w, p):
    qkv = mm_nn(a, w['qkv'], BF16, "fox_qkv")
    fl = mm_nn(a, w['f'], F32, "fox_flogit")
    bf = _pad_lanes(p['fox_b_f'])
    cum = fox_gate_fwd("fox_gate_fwd", fl, bf)
    cum_t = cum[:, :N_HEADS].T
    cums = (cum_t[:, :, None], cum_t[:, None, :])
    qn = rms_fwd("fox_qnorm", qkv, p['fox_q_gain'], HEAD_DIM, 512, HEAD_DIM, xcol=0, nh=N_HEADS)
    kn = rms_fwd("fox_knorm", qkv, p['fox_k_gain'], HEAD_DIM, 512, HEAD_DIM, xcol=N_HEADS, nh=N_HEADS)
    o, lse = attn_fwd("fox_attn_fwd", qn, kn, qkv, HEAD_DIM ** -0.5, voff=2 * N_HEADS, cum=cums, exact_o=True)
    return o, dict(a=a, qkv=qkv, fl=fl, bf=bf, cums=cums, qn=qn, kn=kn, o=o, lse=lse)


def fox_bwd(do, s, w, p):
    dqn, dkn, dv, dcs = attn_bwd("fox_attn_bwd", s['qn'], s['kn'], s['qkv'], s['o'], do, s['lse'], HEAD_DIM ** -0.5,
                                 voff=2 * N_HEADS, cum=s['cums'])
    dq, dgq = rms_bwd("fox_qnorm_bwd", s['qkv'], p['fox_q_gain'], dqn, HEAD_DIM, 512, HEAD_DIM, xcol=0, nh=N_HEADS,
                      out_dtype=BF16)
    dk, dgk = rms_bwd("fox_knorm_bwd", s['qkv'], p['fox_k_gain'], dkn, HEAD_DIM, 512, HEAD_DIM, xcol=N_HEADS,
                      nh=N_HEADS, out_dtype=BF16)
    dqkv = jnp.concatenate([dq, dk, dv.astype(BF16)], axis=1)
    dcum = _pad_lanes(-jnp.sum(dcs[:, :, 0, :], axis=1).T)
    dfl, dbf = fox_gate_bwd("fox_gate_bwd", s['fl'], s['bf'], dcum)
    da = mm_nt(dqkv, w['qkv'], F32, "fox_da_qkv")
    da = mm_nt(dfl, w['f'], F32, "fox_da_f", add=da)
    dw_qkv = mm_tn(s['a'], dqkv, 1, BF16, "fox_dw_qkv")[0]
    dw_f = mm_tn(s['a'], dfl, 1, BF16, "fox_dw_f")[0][:, :N_HEADS]
    dw = _slabs(jnp.concatenate([dw_qkv, dw_f], axis=1), 770)
    return da, dict(fox_w_in=dw), dict(fox_b_f=dbf[:, :N_HEADS], fox_q_gain=dgq, fox_k_gain=dgk)


def _rope_tables(positions):
    inv_freq = ROPE_THETA ** (-jnp.arange(0, MLA_ROPE, 2, dtype=F32) / MLA_ROPE)
    ang = positions.astype(F32)[:, None] * inv_freq
    cos, sin = jnp.cos(ang), jnp.sin(ang)
    return _pad_lanes(jnp.concatenate([cos, cos], axis=1)), _pad_lanes(jnp.concatenate([sin, sin], axis=1))


def mla_fwd(a, w, p):
    qg, kg = p['mla_q_gain'], p['mla_k_gain']
    gains = dict(qn=qg[:, :HEAD_DIM], qr=_pad_lanes(qg[:, HEAD_DIM:]), kn=kg[:, :HEAD_DIM], kr=_pad_lanes(kg[:, HEAD_DIM:]))
    cos, sin = _rope_tables(p['positions'])
    ccr = mm_nn(a, w['in'], F32, "mla_in")
    cqn = rms_fwd("mla_cq_norm", ccr, w['q_a_gain'], 512, 256, 512, xcol=0)
    ckvn = rms_fwd("mla_ckv_norm", ccr, w['kv_a_gain'], 512, 256, 512, xcol=1)
    qf = mm_nn(cqn, w['q_b'], F32, "mla_q_b")
    kvf = mm_nn(ckvn, w['kv_b'], F32, "mla_kv_b")
    q_nope = rms_fwd("mla_qnope_norm", qf, gains['qn'], HEAD_DIM, 512, HEAD_DIM, xcol=0, nh=N_HEADS)
    k_nope = rms_fwd("mla_knope_norm", kvf, gains['kn'], HEAD_DIM, 512, HEAD_DIM, xcol=0, nh=N_HEADS)
    q_rope = rope_fwd("mla_qrope", qf, gains['qr'], cos, sin, 512, N_HEADS, N_HEADS)
    k_rope = rope_fwd("mla_krope", ccr, gains['kr'], cos, sin, 512, 8, 1)
    scale = (HEAD_DIM + MLA_ROPE) ** -0.5
    o, lse = attn_fwd("mla_attn_fwd", q_nope, k_nope, kvf, scale, voff=N_HEADS, q2=q_rope, k2=k_rope)
    return o, dict(a=a, gains=gains, cos=cos, sin=sin, ccr=ccr, cqn=cqn, ckvn=ckvn, qf=qf, kvf=kvf, q_nope=q_nope,
                   k_nope=k_nope, q_rope=q_rope, k_rope=k_rope, o=o, lse=lse, scale=scale)


def mla_bwd(do, s, w, p):
    g = s['gains']
    dqn, dkn, dv, dq2, dk2 = attn_bwd("mla_attn_bwd", s['q_nope'], s['k_nope'], s['kvf'], s['o'], do, s['lse'], s['scale'],
                                      voff=N_HEADS, q2=s['q_rope'], k2=s['k_rope'])
    dqf_n, dg_qn = rms_bwd("mla_qnope_bwd", s['qf'], g['qn'], dqn, HEAD_DIM, 512, HEAD_DIM, xcol=0, nh=N_HEADS)
    dkf, dg_kn = rms_bwd("mla_knope_bwd", s['kvf'], g['kn'], dkn, HEAD_DIM, 512, HEAD_DIM, xcol=0, nh=N_HEADS)
    dqf_r, dg_qr = rope_bwd("mla_qrope_bwd", s['qf'], g['qr'], s['cos'], s['sin'], dq2, 512, N_HEADS, N_HEADS)
    dkr, dg_kr = rope_bwd("mla_krope_bwd", s['ccr'], g['kr'], s['cos'], s['sin'], dk2, 512, 8, 1)
    dqf = jnp.concatenate([dqf_n, dqf_r], axis=1)
    dkvf = jnp.concatenate([dkf, dv], axis=1)
    dcqn = mm_nt(dqf, w['q_b'], F32, "mla_dcq")
    dckvn = mm_nt(dkvf, w['kv_b'], F32, "mla_dckv")
    dw_qb = mm_tn(s['cqn'], dqf, 1, BF16, "mla_dw_qb")[0]
    dw_kvb = mm_tn(s['ckvn'], dkvf, 1, BF16, "mla_dw_kvb")[0]
    dcq, dg_qa = rms_bwd("mla_cq_bwd", s['ccr'], w['q_a_gain'], dcqn, 512, 256, 512, xcol=0)
    dckv, dg_kva = rms_bwd("mla_ckv_bwd", s['ccr'], w['kv_a_gain'], dckvn, 512, 256, 512, xcol=1)
    dccr = jnp.concatenate([dcq, dckv, dkr], axis=1)
    da = mm_nt(dccr, w['in'], F32, "mla_da")
    dw_in = mm_tn(s['a'], dccr, 1, BF16, "mla_dw_in")[0][:, :1088].reshape(N_DEV, 256, 1088)
    hp = 2
    nope = dw_qb[:, :2048].reshape(512, N_DEV, hp, HEAD_DIM)
    rope = dw_qb[:, 2048:].reshape(512, N_DEV, hp, HEAD_DIM)[..., :MLA_ROPE]
    dw_qb_s = jnp.concatenate([nope, rope], axis=-1).transpose(1, 0, 2, 3).reshape(N_DEV, 512, hp * 192)
    kk = dw_kvb[:, :2048].reshape(512, N_DEV, hp, HEAD_DIM)
    vv = dw_kvb[:, 2048:].reshape(512, N_DEV, hp, HEAD_DIM)
    dw_kvb_s = jnp.concatenate([kk, vv], axis=-1).transpose(1, 0, 2, 3).reshape(N_DEV, 512, hp * 256)
    small = dict(mla_q_a_gain=dg_qa, mla_kv_a_gain=dg_kva,
                 mla_q_gain=jnp.concatenate([dg_qn, dg_qr[:, :MLA_ROPE]], axis=1),
                 mla_k_gain=jnp.concatenate([dg_kn, dg_kr[:, :MLA_ROPE]], axis=1))
    return da, dict(mla_w_in=dw_in, mla_w_q_b=dw_qb_s, mla_w_kv_b=dw_kvb_s), small


def sb_fwd_layer(a, w, p):
    qkv = mm_nn(a, w['in'], BF16, "sb_qkv")
    qn = rms_fwd("sb_qnorm", qkv, p['sb_q_gain'], HEAD_DIM, 512, HEAD_DIM, xcol=0, nh=N_HEADS)
    kn = rms_fwd("sb_knorm", qkv, p['sb_k_gain'], HEAD_DIM, 512, HEAD_DIM, xcol=N_HEADS, nh=N_HEADS)
    o, carries = sb_fwd("sb_attn_fwd", qn, kn, qkv, HEAD_DIM ** -0.5, 2 * N_HEADS)
    return o, dict(a=a, qkv=qkv, qn=qn, kn=kn, carries=carries)


def sb_bwd_layer(do, s, w, p):
    dqn, dkn, dv = sb_bwd("sb_attn_bwd", s['qn'], s['kn'], s['qkv'], do, s['carries'], HEAD_DIM ** -0.5, 2 * N_HEADS)
    dq, dgq = rms_bwd("sb_qnorm_bwd", s['qkv'], p['sb_q_gain'], dqn, HEAD_DIM, 512, HEAD_DIM, xcol=0, nh=N_HEADS,
                      out_dtype=BF16)
    dk, dgk = rms_bwd("sb_knorm_bwd", s['qkv'], p['sb_k_gain'], dkn, HEAD_DIM, 512, HEAD_DIM, xcol=N_HEADS, nh=N_HEADS,
                      out_dtype=BF16)
    dqkv = jnp.concatenate([dq, dk, dv.astype(BF16)], axis=1)
    da = mm_nt(dqkv, w['in'], F32, "sb_da")
    dw = mm_tn(s['a'], dqkv, N_DEV, BF16, "sb_dw_in")
    return da, dict(sb_w_in=dw), dict(sb_q_gain=dgq, sb_k_gain=dgk)


def sgu_fwd(a, w, p):
    pre = mm_nn(a, w['in'], BF16, "sgu_in")
    u, vn = sgu_pre_fwd("sgu_pre_fwd", pre, w['v_gain'], 256)
    ws = p['sgu_w_s'][0]
    bs3 = p['sgu_b_s'][0][:, :, None]
    prod = sgu_mix_fwd("sgu_mix_fwd", vn, u, ws, bs3)
    return prod, dict(a=a, pre=pre, u=u, vn=vn, ws=ws, bs3=bs3)


def sgu_bwd(dprod, s, w, p):
    du, dvn, dws, dbs3 = sgu_mix_bwd("sgu_mix_bwd", s['vn'], s['u'], s['ws'], s['bs3'], dprod)
    dpu, dpv, dgv = sgu_pre_bwd("sgu_pre_bwd", s['pre'], w['v_gain'], du, dvn, 128)
    dpre = jnp.concatenate([dpu, dpv], axis=1)
    da = mm_nt(dpre, w['in'], F32, "sgu_da")
    dw = mm_tn(s['a'], dpre, N_DEV, BF16, "sgu_dw_in")
    return da, dict(sgu_w_in=dw), dict(sgu_v_gain=dgv, sgu_w_s=dws[None], sgu_b_s=dbs3[None, :, :, 0])


MIXERS = [("fox", fox_fwd, fox_bwd), ("mla", mla_fwd, mla_bwd), ("sb", sb_fwd_layer, sb_bwd_layer),
          ("sgu", sgu_fwd, sgu_bwd)]


def _pack_rows(arrays, row_mult=256):
    flat = jnp.concatenate([a.reshape(-1).astype(F32) for a in arrays])
    per = row_mult * HEAD_DIM
    total = -(-flat.shape[0] // per) * per
    return jnp.pad(flat, (0, total - flat.shape[0])).reshape(total // HEAD_DIM, HEAD_DIM)


def _unpack_rows(packed, shapes):
    flat = packed.reshape(-1)
    out, off = [], 0
    for shp in shapes:
        n = math.prod(shp)
        out.append(flat[off:off + n].reshape(shp))
        off += n
    return out


def _gather_layer_weights(i, p):
    name = MIXERS[i][0]
    bf = lambda a: a.astype(BF16)
    items = {'up': bf(p['ffn_w_up'][i]), 'down': bf(p['ffn_w_down'][i]), 'out': bf(p[name + '_w_out'][0])}
    if name == "fox":
        items['in'] = bf(p['fox_w_in'][0])
        small_shapes = [p[n].shape for n in SMALL_SHARDED]
        items['small'] = _pack_rows([p[n] for n in SMALL_SHARDED], 8)
    elif name == "mla":
        items.update({'in': bf(p['mla_w_in'][0]), 'q_b': bf(p['mla_w_q_b'][0]), 'kv_b': bf(p['mla_w_kv_b'][0])})
    else:
        items['in'] = bf(p[name + '_w_in'][0])
    keys = list(items)
    got = dict(zip(keys, all_gather("gather_" + name, [items[k] for k in keys])))
    w = {'up': got['up'], 'down': got['down'].reshape(1, D_FF, -1), 'out': got['out'].reshape(1, -1, got['out'].shape[-1])}
    small = None
    if name == "fox":
        full = got['in'].transpose(1, 0, 2).reshape(got['in'].shape[1], -1)
        w['qkv'] = full[None, :, :3 * N_HEADS * HEAD_DIM]
        w['f'] = _pad_lanes(full[:, 3 * N_HEADS * HEAD_DIM:])[None]
        parts = [_unpack_rows(got['small'][d], small_shapes) for d in range(N_DEV)]
        small = {n: jnp.concatenate([parts[d][k] for d in range(N_DEV)], axis=ax)
                 for k, (n, ax) in enumerate(SMALL_SHARDED.items())}
    elif name == "mla":
        w['in'] = _pad_lanes(got['in'].reshape(-1, 1088), 1152)[None]
        hp = 2
        qb = got['q_b'].reshape(N_DEV, 512, hp, 192).transpose(1, 0, 2, 3)
        nope = qb[..., :HEAD_DIM].reshape(512, -1)
        rope = _pad_lanes(qb[..., HEAD_DIM:]).reshape(512, -1)
        w['q_b'] = jnp.concatenate([nope, rope], axis=1)[None]
        kvb = got['kv_b'].reshape(N_DEV, 512, hp, 256).transpose(1, 0, 2, 3)
        w['kv_b'] = jnp.concatenate([kvb[..., :HEAD_DIM].reshape(512, -1), kvb[..., HEAD_DIM:].reshape(512, -1)], axis=1)[None]
    else:
        w['in'] = got['in']
    return w, small


def _train_step(p):
    x, target = p['x'][0], p['loss_target'][0]
    p = dict(p, positions=p['positions'][0])
    xi, yi, ci = _place()
    my_slot = _slot(xi, yi, ci)

    weights = []
    small_full = None
    for i in range(DEPTH):
        w, small = _gather_layer_weights(i, p)
        if small is not None:
            small_full = small
        weights.append(w)
    weights[1]['q_a_gain'] = small_full['mla_q_a_gain']
    weights[1]['kv_a_gain'] = small_full['mla_kv_a_gain']
    weights[3]['v_gain'] = small_full['sgu_v_gain']
    conv_w = small_full['ffn_conv_w']

    h = x
    saved = []
    for i in range(DEPTH):
        w = weights[i]
        a = rms_fwd(f"mix_norm_{i}", h, p['mix_norm'][i:i + 1], h.shape[1], 256, h.shape[1])
        mixed, s_mix = MIXERS[i][1](a, w, p)
        h1 = mm_nn(mixed, w['out'], F32, f"mix_out_{i}", add=h)
        b = rms_fwd(f"ffn_norm_{i}", h1, p['ffn_norm'][i:i + 1], h.shape[1], 256, h.shape[1])
        up = mm_nn(b, w['up'], BF16, f"ffn_up_{i}")
        act = conv_fwd(f"ffn_conv_{i}", up, conv_w[i], p['ffn_conv_b'][i:i + 1])
        h2 = mm_nn(act, w['down'], F32, f"ffn_down_{i}", add=h1)
        saved.append(dict(h=h, mixed=mixed, s_mix=s_mix, h1=h1, b=b, up=up, act=act))
        h = h2

    dh, sq = loss_head("loss_head", h, target)
    loss = lax.psum(0.5 * jnp.sum(sq) / h.shape[1], ("x", "y", "c"))

    big = {}
    small_g = {n: [None] * p[n].shape[0] for n in ('mix_norm', 'ffn_norm', 'ffn_conv_w', 'ffn_conv_b')}
    for i in reversed(range(DEPTH)):
        w, s = weights[i], saved[i]
        name = MIXERS[i][0]
        dact = mm_nt(dh, w['down'], BF16, f"ffn_dact_{i}")
        dw_down = mm_tn(s['act'], dh, 1, BF16, f"ffn_dw_down_{i}").reshape(N_DEV, D_FF // N_DEV, -1)
        dug, duv, dcwg, dcwv, dcbg, dcbv = conv_bwd(f"ffn_conv_bwd_{i}", s['up'], conv_w[i], p['ffn_conv_b'][i:i + 1], dact)
        dup = jnp.concatenate([dug, duv], axis=1)
        dw_up = mm_tn(s['b'], dup, N_DEV, BF16, f"ffn_dw_up_{i}")
        db = mm_nt(dup, w['up'], F32, f"ffn_db_{i}")
        dh1, dg_ffn = rms_bwd(f"ffn_norm_bwd_{i}", s['h1'], p['ffn_norm'][i:i + 1], db, h.shape[1], 128, h.shape[1], resid=dh)
        dmix = mm_nt(dh1, w['out'], BF16, f"mix_dout_{i}")
        dw_out = mm_tn(s['mixed'], dh1, 1, BF16, f"mix_dw_out_{i}").reshape(N_DEV, -1, h.shape[1])
        da, big_i, small_i = MIXERS[i][2](dmix, s['s_mix'], w, p)
        dh, dg_mix = rms_bwd(f"mix_norm_bwd_{i}", s['h'], p['mix_norm'][i:i + 1], da, h.shape[1], 128, h.shape[1], resid=dh1)
        big_i[name + '_w_out'] = dw_out
        keys = list(big_i) + ['ffn_w_up', 'ffn_w_down']
        landed = scatter_parts("scatter_" + name, [big_i[k] for k in big_i] + [dw_up, dw_down])
        for k, arr in zip(keys, landed):
            big.setdefault(k, []).insert(0, arr)
        for k, v in small_i.items():
            small_g[k] = v
        small_g['mix_norm'][i], small_g['ffn_norm'][i] = dg_mix, dg_ffn
        small_g['ffn_conv_w'][i] = jnp.concatenate([dcwg, dcwv], axis=1)[None]
        small_g['ffn_conv_b'][i] = jnp.concatenate([dcbg, dcbv], axis=1)
    for n in ('mix_norm', 'ffn_norm', 'ffn_conv_w', 'ffn_conv_b'):
        small_g[n] = jnp.concatenate(small_g[n], axis=0)
    grad_x = dh[None]

    grads, deltas, new_m, new_v = {}, {}, {}, {}
    for n in BIG:
        grads[n], deltas[n], new_m[n], new_v[n] = adam_sum("adam_" + n, big[n], p[n], p['m_' + n], p['v_' + n])

    full_shapes = [tuple(small_g[n].shape) for n in SMALL]
    partials = all_gather("gather_small_grads", [_pack_rows([small_g[n] for n in SMALL])])[0]
    summed = _unpack_rows(sum_parts("sum_small_grads", partials), full_shapes)
    mine = []
    for n, g in zip(SMALL, summed):
        if n in SMALL_SHARDED:
            ax = SMALL_SHARDED[n]
            g = lax.dynamic_slice_in_dim(g, my_slot * p[n].shape[ax], p[n].shape[ax], axis=ax)
        mine.append(g)
    shapes = [p[n].shape for n in SMALL]
    packed = [_pack_rows(arrs) for arrs in ([p[n] for n in SMALL], mine, [p['m_' + n] for n in SMALL], [p['v_' + n] for n in SMALL])]
    d_s, m_s, v_s = adam_flat("adam_small", *packed)
    for n, g, d, m, v in zip(SMALL, mine, _unpack_rows(d_s, shapes), _unpack_rows(m_s, shapes), _unpack_rows(v_s, shapes)):
        grads[n], deltas[n], new_m[n], new_v[n] = g, d, m, v

    return (loss, grad_x, *[grads[n] for n in WEIGHTS], *[deltas[n] for n in WEIGHTS], *[new_m[n] for n in WEIGHTS],
            *[new_v[n] for n in WEIGHTS])


def kernel(x, positions, mix_norm, ffn_norm, fox_w_in, fox_b_f, fox_q_gain, fox_k_gain, fox_w_out, mla_w_in, mla_q_a_gain, mla_kv_a_gain, mla_w_q_b, mla_w_kv_b, mla_q_gain, mla_k_gain, mla_w_out, sb_w_in, sb_q_gain, sb_k_gain, sb_w_out, sgu_w_in, sgu_v_gain, sgu_w_s, sgu_b_s, sgu_w_out, ffn_w_up, ffn_conv_w, ffn_conv_b, ffn_w_down, loss_target, m_mix_norm, m_ffn_norm, m_fox_w_in, m_fox_b_f, m_fox_q_gain, m_fox_k_gain, m_fox_w_out, m_mla_w_in, m_mla_q_a_gain, m_mla_kv_a_gain, m_mla_w_q_b, m_mla_w_kv_b, m_mla_q_gain, m_mla_k_gain, m_mla_w_out, m_sb_w_in, m_sb_q_gain, m_sb_k_gain, m_sb_w_out, m_sgu_w_in, m_sgu_v_gain, m_sgu_w_s, m_sgu_b_s, m_sgu_w_out, m_ffn_w_up, m_ffn_conv_w, m_ffn_conv_b, m_ffn_w_down, v_mix_norm, v_ffn_norm, v_fox_w_in, v_fox_b_f, v_fox_q_gain, v_fox_k_gain, v_fox_w_out, v_mla_w_in, v_mla_q_a_gain, v_mla_kv_a_gain, v_mla_w_q_b, v_mla_w_kv_b, v_mla_q_gain, v_mla_k_gain, v_mla_w_out, v_sb_w_in, v_sb_q_gain, v_sb_k_gain, v_sb_w_out, v_sgu_w_in, v_sgu_v_gain, v_sgu_w_s, v_sgu_b_s, v_sgu_w_out, v_ffn_w_up, v_ffn_conv_w, v_ffn_conv_b, v_ffn_w_down):
    args = locals()
    names = ['x', 'positions'] + WEIGHTS + ['loss_target'] + ['m_' + n for n in WEIGHTS] + ['v_' + n for n in WEIGHTS]
    return _train_step({n: args[n] for n in names})
```

```python
import functools
import math

import jax
import jax.numpy as jnp
from jax import lax
from jax.experimental import pallas as pl
from jax.experimental.pallas import tpu as pltpu

F32 = jnp.float32
BF16 = jnp.bfloat16
MESH = pl.DeviceIdType.MESH

N_DEV = 8
N_HEADS = 16
HEAD_DIM = 128
EPS = 1e-6
DEPTH = 4
D_FF = 5632
MLA_ROPE = 64
ROPE_THETA = 10000.0
VMEM_LIMIT = 48 * 1024 * 1024

ADAM_LR, ADAM_B1, ADAM_B2, ADAM_EPS, ADAM_WD, ADAM_STEP = 0.001, 0.9, 0.999, 1e-08, 0.01, 10

WEIGHTS = ['mix_norm', 'ffn_norm', 'fox_w_in', 'fox_b_f', 'fox_q_gain', 'fox_k_gain', 'fox_w_out', 'mla_w_in',
           'mla_q_a_gain', 'mla_kv_a_gain', 'mla_w_q_b', 'mla_w_kv_b', 'mla_q_gain', 'mla_k_gain', 'mla_w_out',
           'sb_w_in', 'sb_q_gain', 'sb_k_gain', 'sb_w_out', 'sgu_w_in', 'sgu_v_gain', 'sgu_w_s', 'sgu_b_s',
           'sgu_w_out', 'ffn_w_up', 'ffn_conv_w', 'ffn_conv_b', 'ffn_w_down']
BIG = ['fox_w_in', 'fox_w_out', 'mla_w_in', 'mla_w_q_b', 'mla_w_kv_b', 'mla_w_out', 'sb_w_in', 'sb_w_out',
       'sgu_w_in', 'sgu_w_out', 'ffn_w_up', 'ffn_w_down']
SMALL = [w for w in WEIGHTS if w not in BIG]
SMALL_SHARDED = {'mla_q_a_gain': 1, 'mla_kv_a_gain': 1, 'sgu_v_gain': 1, 'ffn_conv_w': 2}


def _cparams(n_grid):
    return pltpu.CompilerParams(dimension_semantics=("arbitrary",) * n_grid, vmem_limit_bytes=VMEM_LIMIT)


def _pick(n, cap):
    best = None
    t = 128
    while t <= min(n, cap):
        if n % t == 0:
            best = t
        t += 128
    return best if best is not None else n


def _mm_call(name, a, b, out_shape, a_spec, b_spec, o_spec, grid, dims, acc_shape, add=None):
    nk = grid[2]

    def body(*refs):
        if add is None:
            a_ref, b_ref, o_ref, acc = refs
            add_ref = None
        else:
            a_ref, b_ref, add_ref, o_ref, acc = refs
        k = pl.program_id(2)

        @pl.when(k == 0)
        def _():
            acc[...] = jnp.zeros_like(acc)

        acc[...] += lax.dot_general(a_ref[...].astype(BF16), b_ref[...].astype(BF16), (dims, ((), ())),
                                    preferred_element_type=F32)

        @pl.when(k == nk - 1)
        def _():
            r = acc[...]
            if add_ref is not None:
                r = r + add_ref[...].astype(F32)
            o_ref[...] = r.astype(o_ref.dtype)

    ins = [a, b] + ([] if add is None else [add])
    in_specs = [a_spec, b_spec] + ([] if add is None else [o_spec])
    return pl.pallas_call(body, grid=grid, in_specs=in_specs, out_specs=o_spec, out_shape=out_shape,
                          scratch_shapes=[pltpu.VMEM(acc_shape, F32)], name=name,
                          compiler_params=_cparams(3))(*ins)


def mm_nn(a, b3, out_dtype, name, add=None, joff=0, nj=None):
    m, kk = a.shape
    _, kb, n = b3.shape
    assert kb == kk
    nj = b3.shape[0] - joff if nj is None else nj
    tm, tn, tk = _pick(m, 1024), _pick(n, 1536), _pick(kk, 512)
    nb = n // tn
    return _mm_call(
        name, a, b3, jax.ShapeDtypeStruct((m, nj * n), out_dtype),
        pl.BlockSpec((tm, tk), lambda i, c, k: (i, k)),
        pl.BlockSpec((None, tk, tn), lambda i, c, k: (joff + c // nb, k, c % nb)),
        pl.BlockSpec((tm, tn), lambda i, c, k: (i, c)),
        (m // tm, nj * nb, kk // tk), ((1,), (0,)), (tm, tn), add=add)


def mm_nt(a, b3, out_dtype, name, add=None, joff=0, nj=None):
    m, na = a.shape
    _, ko, n = b3.shape
    nj = b3.shape[0] - joff if nj is None else nj
    assert na == nj * n
    tm, to, tn = _pick(m, 1024), _pick(ko, 1024), _pick(n, 1536)
    nb = n // tn
    return _mm_call(
        name, a, b3, jax.ShapeDtypeStruct((m, ko), out_dtype),
        pl.BlockSpec((tm, tn), lambda i, o, c: (i, c)),
        pl.BlockSpec((None, to, tn), lambda i, o, c: (joff + c // nb, o, c % nb)),
        pl.BlockSpec((tm, to), lambda i, o, c: (i, o)),
        (m // tm, ko // to, nj * nb), ((1,), (1,)), (tm, to), add=add)


def mm_tn(a, b, nj, out_dtype, name):
    s, ko = a.shape
    sb, nb_tot = b.shape
    assert sb == s and nb_tot % nj == 0
    n = nb_tot // nj
    to, tn, ts = _pick(ko, 1024), _pick(n, 1536), _pick(s, 512)
    nb = n // tn
    return _mm_call(
        name, a, b, jax.ShapeDtypeStruct((nj, ko, n), out_dtype),
        pl.BlockSpec((ts, to), lambda o, c, k: (k, o)),
        pl.BlockSpec((ts, tn), lambda o, c, k: (k, c)),
        pl.BlockSpec((None, to, tn), lambda o, c, k: (c // nb, o, c % nb)),
        (ko // to, nj * nb, s // ts), ((0,), (0,)), (to, tn))


def blockk(name, fn, grid, ins, outs):
    n_in = len(ins)
    accs = [o[2] for o in outs]

    def body(*refs):
        vals = fn(*[r[...] for r in refs[:n_in]])
        if not isinstance(vals, (tuple, list)):
            vals = (vals,)
        i, j = pl.program_id(0), pl.program_id(1)
        for r, v, acc in zip(refs[n_in:], vals, accs):
            if acc is None:
                r[...] = v.astype(r.dtype)
            else:
                first = (j == 0) if acc == 'inner' else jnp.logical_and(i == 0, j == 0)

                @pl.when(first)
                def _(r=r, v=v):
                    r[...] = v.astype(r.dtype)

                @pl.when(jnp.logical_not(first))
                def _(r=r, v=v):
                    r[...] += v.astype(r.dtype)

    res = pl.pallas_call(body, grid=grid, in_specs=[s for _, s in ins], out_specs=[o[1] for o in outs],
                         out_shape=[o[0] for o in outs], name=name, compiler_params=_cparams(2))(*[a for a, _ in ins])
    return res


def _sds(shape, dtype):
    return jax.ShapeDtypeStruct(tuple(shape), dtype)


def _rows(tr, w, col=0):
    if col == 'j':
        return pl.BlockSpec((tr, w), lambda i, j: (i, j))
    if callable(col):
        return pl.BlockSpec((tr, w), lambda i, j: (i, col(j)))
    return pl.BlockSpec((tr, w), lambda i, j: (i, col))


def _whole(shape):
    nd = len(shape)
    return pl.BlockSpec(tuple(shape), lambda i, j: (0,) * nd)


def _rms(x, g, n):
    ms = jnp.sum(x * x, axis=-1, keepdims=True) * (1.0 / n)
    return x * lax.rsqrt(ms + EPS) * g


def _sig(x):
    return 1.0 / (1.0 + jnp.exp(-x))


def _gelu(x):
    return 0.5 * x * (1.0 + jnp.tanh(math.sqrt(2.0 / math.pi) * (x + 0.044715 * (x * x * x))))


def _lane_iota(shape):
    return lax.broadcasted_iota(jnp.int32, shape, len(shape) - 1)


def _rope(x, cos, sin):
    lane = _lane_iota(x.shape)
    half = MLA_ROPE // 2
    swapped = jnp.where(lane < half, pltpu.roll(x, HEAD_DIM - half, 1), pltpu.roll(x, half, 1))
    sign = jnp.where(lane < half, -1.0, 1.0)
    return x * cos + swapped * (sin * sign)


def _rope_t(dy, cos, sin):
    lane = _lane_iota(dy.shape)
    half = MLA_ROPE // 2
    t = dy * sin
    swapped = jnp.where(lane < half, pltpu.roll(t, HEAD_DIM - half, 1), pltpu.roll(t, half, 1))
    sign = jnp.where(lane < half, 1.0, -1.0)
    return dy * cos + swapped * sign


def rms_fwd(name, x, gain, n, tr, width, xcol=0, nh=1, out_dtype=BF16, out_cols=None):
    r = x.shape[0]
    out_cols = width * nh if out_cols is None else out_cols
    xspec = _rows(tr, width, (lambda j: xcol + j) if nh > 1 else xcol)
    ospec = _rows(tr, width, 'j' if nh > 1 else 0)
    return blockk(name, lambda xb, g: _rms(xb.astype(F32), g, n), (r // tr, nh),
                  [(x, xspec), (gain, _whole(gain.shape))], [(_sds((r, out_cols), out_dtype), ospec, None)])[0]


def rms_bwd(name, x, gain, dy, n, tr, width, xcol=0, nh=1, dycol=0, resid=None, out_dtype=F32):
    r = x.shape[0]
    xspec = _rows(tr, width, (lambda j: xcol + j) if nh > 1 else xcol)
    dyspec = _rows(tr, width, (lambda j: dycol + j) if nh > 1 else dycol)
    ospec = _rows(tr, width, 'j' if nh > 1 else 0)

    def fn(xb, g, dyb, *rest):
        _, vjp = jax.vjp(lambda a, b: _rms(a, b, n), xb.astype(F32), g)
        dx, dg = vjp(dyb.astype(F32))
        if rest:
            dx = dx + rest[0].astype(F32)
        return dx, dg

    ins = [(x, xspec), (gain, _whole(gain.shape)), (dy, dyspec)]
    if resid is not None:
        ins.append((resid, ospec))
    return blockk(name, fn, (r // tr, nh), ins,
                  [(_sds((r, width * nh), out_dtype), ospec, None), (_sds(gain.shape, F32), _whole(gain.shape), 'all')])


def _nt(a, b):
    return lax.dot_general(a, b, (((1,), (1,)), ((), ())), preferred_element_type=F32)


def _tn(a, b):
    return lax.dot_general(a, b, (((0,), (0,)), ((), ())), preferred_element_type=F32)


def _nn(a, b):
    return lax.dot_general(a, b, (((1,), (0,)), ((), ())), preferred_element_type=F32)


def _attn_specs(tq, qoff, koff, voff, extra, bias, q2off):
    d = HEAD_DIM
    specs = [pl.BlockSpec((tq, d), lambda h, i, j: (i, qoff + h)),
             pl.BlockSpec((tq, d), lambda h, i, j: (jnp.minimum(i, j), koff + h)),
             pl.BlockSpec((tq, d), lambda h, i, j: (jnp.minimum(i, j), voff + h))]
    if extra:
        specs += [pl.BlockSpec((tq, d), lambda h, i, j: (i, q2off + h)),
                  pl.BlockSpec((tq, d), lambda h, i, j: (jnp.minimum(i, j), 0))]
    if bias:
        specs += [pl.BlockSpec((None, tq, 1), lambda h, i, j: (h, i, 0)),
                  pl.BlockSpec((None, 1, tq), lambda h, i, j: (h, 0, jnp.minimum(i, j)))]
    return specs


def _scores(q_ref, k_ref, q2_ref, k2_ref, cc_ref, cr_ref, scale, qi, kj, tq):
    s = _nt(q_ref[...].astype(BF16), k_ref[...].astype(BF16))
    if q2_ref is not None:
        s = s + _nt(q2_ref[...].astype(BF16), k2_ref[...].astype(BF16))
    s = s * scale
    if cc_ref is not None:
        s = s + (cc_ref[...] - cr_ref[...])
    row = qi * tq + lax.broadcasted_iota(jnp.int32, (tq, tq), 0)
    col = kj * tq + lax.broadcasted_iota(jnp.int32, (tq, tq), 1)
    return s, col <= row


def attn_fwd(name, q, k, v, scale, *, qoff=0, koff=0, voff=0, q2=None, k2=None, q2off=0, cum=None, tq=256,
             exact_o=False):
    s_len = q.shape[0]
    nq = s_len // tq
    extra, bias = q2 is not None, cum is not None
    n_in = 3 + 2 * extra + 2 * bias

    def body(*refs):
        q_ref, k_ref, v_ref = refs[:3]
        p = 3
        q2_ref = k2_ref = cc_ref = cr_ref = None
        if extra:
            q2_ref, k2_ref = refs[p:p + 2]
            p += 2
        if bias:
            cc_ref, cr_ref = refs[p:p + 2]
            p += 2
        o_ref, lse_ref, m_s, l_s, acc_s = refs[p:]
        qi, kj = pl.program_id(1), pl.program_id(2)

        @pl.when(kj == 0)
        def _():
            m_s[...] = jnp.full_like(m_s, -jnp.inf)
            l_s[...] = jnp.zeros_like(l_s)
            acc_s[...] = jnp.zeros_like(acc_s)

        @pl.when(kj <= qi)
        def _():
            s, allowed = _scores(q_ref, k_ref, q2_ref, k2_ref, cc_ref, cr_ref, scale, qi, kj, tq)
            s = jnp.where(allowed, s, -jnp.inf)
            m_old = m_s[...]
            m_new = jnp.maximum(m_old, jnp.max(s, axis=-1, keepdims=True))
            alpha = jnp.exp(m_old - m_new)
            pr = jnp.exp(s - m_new)
            l_s[...] = alpha * l_s[...] + jnp.sum(pr, axis=-1, keepdims=True)
            vb = v_ref[...].astype(BF16)
            pv = _nn(pr.astype(BF16), vb)
            if exact_o:
                pv = pv + _nn((pr - pr.astype(BF16).astype(F32)).astype(BF16), vb)
            acc_s[...] = alpha * acc_s[...] + pv
            m_s[...] = m_new

        @pl.when(kj == qi)
        def _():
            o_ref[...] = (acc_s[...] / l_s[...]).astype(o_ref.dtype)
            lse_ref[...] = m_s[...] + jnp.log(l_s[...])

    ins = [q, k, v] + ([q2, k2] if extra else []) + (list(cum) if bias else [])
    d = HEAD_DIM
    return pl.pallas_call(
        body, grid=(N_HEADS, nq, nq), in_specs=_attn_specs(tq, qoff, koff, voff, extra, bias, q2off),
        out_specs=[pl.BlockSpec((tq, d), lambda h, i, j: (i, h)), pl.BlockSpec((None, tq, 1), lambda h, i, j: (h, i, 0))],
        out_shape=[_sds((s_len, N_HEADS * d), F32 if exact_o else BF16), _sds((N_HEADS, s_len, 1), F32)],
        scratch_shapes=[pltpu.VMEM((tq, 1), F32), pltpu.VMEM((tq, 1), F32), pltpu.VMEM((tq, d), F32)],
        name=name, compiler_params=_cparams(3))(*ins)


def attn_bwd(name, q, k, v, o, do, lse, scale, *, qoff=0, koff=0, voff=0, q2=None, k2=None, q2off=0, cum=None, tq=256):
    s_len = q.shape[0]
    nq = s_len // tq
    extra, bias = q2 is not None, cum is not None
    d = HEAD_DIM
    n_in = 6 + 2 * extra + 2 * bias

    def body(*refs):
        q_ref, k_ref, v_ref = refs[:3]
        p = 3
        q2_ref = k2_ref = cc_ref = cr_ref = None
        if extra:
            q2_ref, k2_ref = refs[p:p + 2]
            p += 2
        if bias:
            cc_ref, cr_ref = refs[p:p + 2]
            p += 2
        o_ref, do_ref, lse_ref = refs[p:p + 3]
        p += 3
        dq_ref, dk_ref, dv_ref = refs[p:p + 3]
        p += 3
        dq2_ref = dk2_ref = dcs_ref = None
        if extra:
            dq2_ref, dk2_ref = refs[p:p + 2]
            p += 2
        if bias:
            dcs_ref = refs[p]
            p += 1
        dq_s, delta_s = refs[p:p + 2]
        dq2_s = refs[p + 2] if extra else None
        h, qi, kj = pl.program_id(0), pl.program_id(1), pl.program_id(2)

        @pl.when(jnp.logical_and(qi == 0, kj == 0))
        def _():
            dk_ref[...] = jnp.zeros_like(dk_ref)
            dv_ref[...] = jnp.zeros_like(dv_ref)

        if extra:
            @pl.when(jnp.logical_and(h == 0, jnp.logical_and(qi == 0, kj == 0)))
            def _():
                dk2_ref[...] = jnp.zeros_like(dk2_ref)

        @pl.when(kj == 0)
        def _():
            dq_s[...] = jnp.zeros_like(dq_s)
            if extra:
                dq2_s[...] = jnp.zeros_like(dq2_s)
            delta_s[...] = jnp.sum(do_ref[...].astype(F32) * o_ref[...].astype(F32), axis=-1, keepdims=True)

        if bias:
            @pl.when(kj > qi)
            def _():
                dcs_ref[...] = jnp.zeros_like(dcs_ref)

        @pl.when(kj <= qi)
        def _():
            s, allowed = _scores(q_ref, k_ref, q2_ref, k2_ref, cc_ref, cr_ref, scale, qi, kj, tq)
            pr = jnp.where(allowed, jnp.exp(s - lse_ref[...]), 0.0)
            dob = do_ref[...].astype(BF16)
            dp = _nt(dob, v_ref[...].astype(BF16))
            ds = pr * (dp - delta_s[...])
            dsb = (ds * scale).astype(BF16)
            ks = pl.ds(pl.multiple_of(kj * tq, tq), tq)
            dq_s[...] += _nn(dsb, k_ref[...].astype(BF16))
            dk_ref[ks, :] += _tn(dsb, q_ref[...].astype(BF16))
            dv_ref[ks, :] += _tn(pr.astype(BF16), dob)
            if extra:
                dq2_s[...] += _nn(dsb, k2_ref[...].astype(BF16))
                dk2_ref[ks, :] += _tn(dsb, q2_ref[...].astype(BF16))
            if bias:
                dcs_ref[...] = jnp.sum(ds, axis=0, keepdims=True)

        @pl.when(kj == qi)
        def _():
            dq_ref[...] = dq_s[...]
            if extra:
                dq2_ref[...] = dq2_s[...]

    ins = [q, k, v] + ([q2, k2] if extra else []) + (list(cum) if bias else []) + [o, do, lse]
    in_specs = _attn_specs(tq, qoff, koff, voff, extra, bias, q2off) + [
        pl.BlockSpec((tq, d), lambda h, i, j: (i, h)), pl.BlockSpec((tq, d), lambda h, i, j: (i, h)),
        pl.BlockSpec((None, tq, 1), lambda h, i, j: (h, i, 0))]
    full = _sds((s_len, N_HEADS * d), F32)
    out_shape = [full, full, full]
    out_specs = [pl.BlockSpec((tq, d), lambda h, i, j: (i, h)), pl.BlockSpec((s_len, d), lambda h, i, j: (0, h)),
                 pl.BlockSpec((s_len, d), lambda h, i, j: (0, h))]
    scratch = [pltpu.VMEM((tq, d), F32), pltpu.VMEM((tq, 1), F32)]
    if extra:
        out_shape += [full, _sds((s_len, d), F32)]
        out_specs += [pl.BlockSpec((tq, d), lambda h, i, j: (i, h)), pl.BlockSpec((s_len, d), lambda h, i, j: (0, 0))]
        scratch.append(pltpu.VMEM((tq, d), F32))
    if bias:
        out_shape.append(_sds((N_HEADS, nq, 1, s_len), F32))
        out_specs.append(pl.BlockSpec((None, None, 1, tq), lambda h, i, j: (h, i, 0, j)))
    return pl.pallas_call(body, grid=(N_HEADS, nq, nq), in_specs=in_specs, out_specs=out_specs, out_shape=out_shape,
                          scratch_shapes=scratch, name=name, compiler_params=_cparams(3))(*ins)


def _dot3(x, m01):
    x1 = x.astype(BF16)
    r1 = x - x1.astype(F32)
    x2 = r1.astype(BF16)
    x3 = (r1 - x2.astype(F32)).astype(BF16)
    return _nn(x1, m01) + _nn(x2, m01) + _nn(x3, m01)


def _sb_terms(q_ref, k_ref, scale, qi, kb, tq):
    z = _nt(q_ref[...].astype(BF16), k_ref[...].astype(BF16)) * scale
    row = qi * tq + lax.broadcasted_iota(jnp.int32, (tq, tq), 0)
    col = kb * tq + lax.broadcasted_iota(jnp.int32, (tq, tq), 1)
    strict = col < row
    lg = jnp.log(1.0 + jnp.exp(-jnp.abs(z)))
    log_keep = jnp.where(strict, -(jnp.maximum(z, 0.0) + lg), 0.0)
    log_beta = jnp.minimum(z, 0.0) - lg
    return z, strict, log_keep, log_beta


def _tri(tq, pred):
    a = lax.broadcasted_iota(jnp.int32, (tq, tq), 0)
    b = lax.broadcasted_iota(jnp.int32, (tq, tq), 1)
    return jnp.where(pred(a, b), 1.0, 0.0).astype(BF16)


def sb_fwd(name, qn, kn, qkv, scale, voff, tq=256):
    s_len = qn.shape[0]
    nq = s_len // tq
    d = HEAD_DIM

    def body(q_ref, k_ref, v_ref, o_ref, car_ref, ca_s, acc_s):
        qi, kj = pl.program_id(1), pl.program_id(2)
        kb = qi - kj

        @pl.when(kj == 0)
        def _():
            ca_s[...] = jnp.zeros_like(ca_s)
            acc_s[...] = jnp.zeros_like(acc_s)

        @pl.when(kj <= qi)
        def _():
            z, strict, log_keep, log_beta = _sb_terms(q_ref, k_ref, scale, qi, kb, tq)
            ca = ca_s[...]
            car_ref[...] = ca
            after = ca + _dot3(log_keep, _tri(tq, lambda m, j: m > j))
            a = jnp.where(strict, jnp.exp(log_beta + after), 0.0)
            acc_s[...] += _nn(a.astype(BF16), v_ref[...].astype(BF16))
            ca_s[...] = ca + jnp.sum(log_keep, axis=-1, keepdims=True)

        @pl.when(kj == qi)
        def _():
            o_ref[...] = acc_s[...].astype(o_ref.dtype)

    kblk = lambda i, j: jnp.maximum(i - j, 0)
    return pl.pallas_call(
        body, grid=(N_HEADS, nq, nq),
        in_specs=[pl.BlockSpec((tq, d), lambda h, i, j: (i, h)), pl.BlockSpec((tq, d), lambda h, i, j: (kblk(i, j), h)),
                  pl.BlockSpec((tq, d), lambda h, i, j: (kblk(i, j), voff + h))],
        out_specs=[pl.BlockSpec((tq, d), lambda h, i, j: (i, h)),
                   pl.BlockSpec((None, None, tq, 1), lambda h, i, j: (h, kblk(i, j), i, 0))],
        out_shape=[_sds((s_len, N_HEADS * d), BF16), _sds((N_HEADS, nq, s_len, 1), F32)],
        scratch_shapes=[pltpu.VMEM((tq, 1), F32), pltpu.VMEM((tq, d), F32)],
        name=name, compiler_params=_cparams(3))(qn, kn, qkv)


def sb_bwd(name, qn, kn, qkv, do, carries, scale, voff, tq=256):
    s_len = qn.shape[0]
    nq = s_len // tq
    d = HEAD_DIM

    def body(q_ref, k_ref, v_ref, do_ref, car_ref, dq_ref, dk_ref, dv_ref, dq_s, cg_s):
        qi, kj = pl.program_id(1), pl.program_id(2)

        @pl.when(jnp.logical_and(qi == 0, kj == 0))
        def _():
            dk_ref[...] = jnp.zeros_like(dk_ref)
            dv_ref[...] = jnp.zeros_like(dv_ref)

        @pl.when(kj == 0)
        def _():
            dq_s[...] = jnp.zeros_like(dq_s)
            cg_s[...] = jnp.zeros_like(cg_s)

        @pl.when(kj <= qi)
        def _():
            z, strict, log_keep, log_beta = _sb_terms(q_ref, k_ref, scale, qi, kj, tq)
            after = car_ref[...] + _dot3(log_keep, _tri(tq, lambda m, j: m > j))
            a = jnp.where(strict, jnp.exp(log_beta + after), 0.0)
            dob = do_ref[...].astype(BF16)
            g = a * _nt(dob, v_ref[...].astype(BF16))
            cg = cg_s[...]
            big_g = cg + _dot3(g, _tri(tq, lambda m, j: m < j))
            cg_s[...] = cg + jnp.sum(g, axis=-1, keepdims=True)
            beta = jnp.exp(log_beta)
            dz = jnp.where(strict, g * (1.0 - beta) - big_g * beta, 0.0)
            dzb = (dz * scale).astype(BF16)
            ks = pl.ds(pl.multiple_of(kj * tq, tq), tq)
            dq_s[...] += _nn(dzb, k_ref[...].astype(BF16))
            dk_ref[ks, :] += _tn(dzb, q_ref[...].astype(BF16))
            dv_ref[ks, :] += _tn(a.astype(BF16), dob)

        @pl.when(kj == qi)
        def _():
            dq_ref[...] = dq_s[...]

    kblk = lambda i, j: jnp.minimum(i, j)
    full = _sds((s_len, N_HEADS * d), F32)
    return pl.pallas_call(
        body, grid=(N_HEADS, nq, nq),
        in_specs=[pl.BlockSpec((tq, d), lambda h, i, j: (i, h)), pl.BlockSpec((tq, d), lambda h, i, j: (kblk(i, j), h)),
                  pl.BlockSpec((tq, d), lambda h, i, j: (kblk(i, j), voff + h)),
                  pl.BlockSpec((tq, d), lambda h, i, j: (i, h)),
                  pl.BlockSpec((None, None, tq, 1), lambda h, i, j: (h, kblk(i, j), i, 0))],
        out_specs=[pl.BlockSpec((tq, d), lambda h, i, j: (i, h)), pl.BlockSpec((s_len, d), lambda h, i, j: (0, h)),
                   pl.BlockSpec((s_len, d), lambda h, i, j: (0, h))],
        out_shape=[full, full, full],
        scratch_shapes=[pltpu.VMEM((tq, d), F32), pltpu.VMEM((tq, 1), F32)],
        name=name, compiler_params=_cparams(3))(qn, kn, qkv, do, carries)


def _cumsum_rows(x, reverse):
    n = x.shape[0] // HEAD_DIM
    tri = _tri(HEAD_DIM, (lambda a, b: b >= a) if reverse else (lambda a, b: b <= a))
    pieces = [None] * n
    carry = jnp.zeros((1, HEAD_DIM), F32)
    order = range(n - 1, -1, -1) if reverse else range(n)
    for blk in order:
        xb = x[blk * HEAD_DIM:(blk + 1) * HEAD_DIM, :]
        x1 = xb.astype(BF16)
        r1 = xb - x1.astype(F32)
        x2 = r1.astype(BF16)
        x3 = (r1 - x2.astype(F32)).astype(BF16)
        c = _nn(tri, x1) + _nn(tri, x2) + _nn(tri, x3) + carry
        pieces[blk] = c
        carry = c[0:1, :] if reverse else c[HEAD_DIM - 1:HEAD_DIM, :]
    return jnp.concatenate(pieces, axis=0)


def _log_sigmoid(x):
    return jnp.minimum(x, 0.0) - jnp.log(1.0 + jnp.exp(-jnp.abs(x)))


def fox_gate_fwd(name, fl, bf):
    return blockk(name, lambda f, b: _cumsum_rows(_log_sigmoid(f + b), False), (1, 1),
                  [(fl, _whole(fl.shape)), (bf, _whole(bf.shape))], [(_sds(fl.shape, F32), _whole(fl.shape), None)])[0]


def fox_gate_bwd(name, fl, bf, dcum):
    def fn(f, b, dc):
        dlogf = _cumsum_rows(dc, True)
        dfl = dlogf * _sig(-(f + b))
        return dfl, jnp.sum(dfl, axis=0, keepdims=True)

    return blockk(name, fn, (1, 1), [(fl, _whole(fl.shape)), (bf, _whole(bf.shape)), (dcum, _whole(dcum.shape))],
                  [(_sds(fl.shape, F32), _whole(fl.shape), None), (_sds(bf.shape, F32), _whole(bf.shape), None)])


def rope_fwd(name, x, gain, cos, sin, tr, xcol, nh):
    r = x.shape[0]
    xspec = _rows(tr, HEAD_DIM, lambda j: xcol + j)
    tspec = _rows(tr, HEAD_DIM, 0)
    return blockk(name, lambda xb, g, c, s: _rope(_rms(xb.astype(F32), g, MLA_ROPE), c, s), (r // tr, nh),
                  [(x, xspec), (gain, _whole(gain.shape)), (cos, tspec), (sin, tspec)],
                  [(_sds((r, HEAD_DIM * nh), BF16), _rows(tr, HEAD_DIM, 'j'), None)])[0]


def rope_bwd(name, x, gain, cos, sin, dy, tr, xcol, nh):
    r = x.shape[0]
    xspec = _rows(tr, HEAD_DIM, lambda j: xcol + j)
    tspec = _rows(tr, HEAD_DIM, 0)
    ospec = _rows(tr, HEAD_DIM, 'j')

    def fn(xb, g, c, s, dyb):
        _, vjp = jax.vjp(lambda a, b: _rms(a, b, MLA_ROPE), xb.astype(F32), g)
        return vjp(_rope_t(dyb.astype(F32), c, s))

    return blockk(name, fn, (r // tr, nh),
                  [(x, xspec), (gain, _whole(gain.shape)), (cos, tspec), (sin, tspec), (dy, ospec)],
                  [(_sds((r, HEAD_DIM * nh), F32), ospec, None), (_sds(gain.shape, F32), _whole(gain.shape), 'all')])


def sgu_pre_fwd(name, pre, gain, tr):
    s_len, w2 = pre.shape
    w = w2 // 2
    return blockk(name, lambda pu, pv, g: (_gelu(pu.astype(F32)), _rms(_gelu(pv.astype(F32)), g, w)), (s_len // tr, 1),
                  [(pre, _rows(tr, w, 0)), (pre, _rows(tr, w, 1)), (gain, _whole(gain.shape))],
                  [(_sds((s_len, w), BF16), _rows(tr, w, 0), None), (_sds((s_len, w), BF16), _rows(tr, w, 0), None)])


def sgu_pre_bwd(name, pre, gain, du, dvn, tr):
    s_len, w2 = pre.shape
    w = w2 // 2

    def fn(pu, pv, g, dub, dvb):
        _, vjp_u = jax.vjp(_gelu, pu.astype(F32))
        _, vjp_v = jax.vjp(lambda a, b: _rms(_gelu(a), b, w), pv.astype(F32), g)
        dpv, dg = vjp_v(dvb.astype(F32))
        return vjp_u(dub.astype(F32))[0], dpv, dg

    spec = _rows(tr, w, 0)
    return blockk(name, fn, (s_len // tr, 1),
                  [(pre, spec), (pre, _rows(tr, w, 1)), (gain, _whole(gain.shape)), (du, spec), (dvn, spec)],
                  [(_sds((s_len, w), BF16), spec, None), (_sds((s_len, w), BF16), spec, None),
                   (_sds(gain.shape, F32), _whole(gain.shape), 'all')])


def _ws_masked(ws):
    t = ws.shape[0]
    a = lax.broadcasted_iota(jnp.int32, (t, t), 0)
    b = lax.broadcasted_iota(jnp.int32, (t, t), 1)
    return jnp.where(b <= a, ws, 0.0)


def sgu_mix_fwd(name, vn, u, ws, bs3):
    s_len, w = vn.shape
    t = ws.shape[1]

    def fn(vb, ub, wsb, bsb):
        mixed = _nn(_ws_masked(wsb).astype(BF16), vb.astype(BF16)) + bsb
        return ub.astype(F32) * mixed

    blk = pl.BlockSpec((t, t), lambda i, j: (i, j))
    return blockk(name, fn, (s_len // t, w // t),
                  [(vn, blk), (u, blk), (ws, pl.BlockSpec((None, t, t), lambda i, j: (j, 0, 0))),
                   (bs3, pl.BlockSpec((None, t, 1), lambda i, j: (j, 0, 0)))],
                  [(_sds((s_len, w), BF16), blk, None)])[0]


def sgu_mix_bwd(name, vn, u, ws, bs3, dprod):
    s_len, w = vn.shape
    t = ws.shape[1]

    def fn(vb, ub, wsb, bsb, dpb):
        wm = _ws_masked(wsb).astype(BF16)
        vb16 = vb.astype(BF16)
        mixed = _nn(wm, vb16) + bsb
        dp = dpb.astype(F32)
        du = dp * mixed
        dm = dp * ub.astype(F32)
        dmb = dm.astype(BF16)
        dvn = _tn(wm, dmb)
        dws = _ws_masked(_nt(dmb, vb16))
        return du, dvn, dws, jnp.sum(dm, axis=-1, keepdims=True)

    blk = pl.BlockSpec((t, t), lambda g, n: (n, g))
    wspec = pl.BlockSpec((None, t, t), lambda g, n: (g, 0, 0))
    bspec = pl.BlockSpec((None, t, 1), lambda g, n: (g, 0, 0))
    return blockk(name, fn, (w // t, s_len // t), [(vn, blk), (u, blk), (ws, wspec), (bs3, bspec), (dprod, blk)],
                  [(_sds((s_len, w), BF16), blk, None), (_sds((s_len, w), BF16), blk, None),
                   (_sds(ws.shape, F32), wspec, 'inner'), (_sds(bs3.shape, F32), bspec, 'inner')])


def _shift_down(x, k):
    row = lax.broadcasted_iota(jnp.int32, x.shape, 0)
    return jnp.where(row >= k, pltpu.roll(x, k, 0), 0.0)


def _shift_up(x, k):
    n = x.shape[0]
    row = lax.broadcasted_iota(jnp.int32, x.shape, 0)
    return jnp.where(row < n - k, pltpu.roll(x, n - k, 0), 0.0)


def _conv(up, cw, cb):
    return cb + cw[0:1, :] * _shift_down(up, 2) + cw[1:2, :] * _shift_down(up, 1) + cw[2:3, :] * up


def _conv_specs(s_len, tc, nf):
    g = pl.BlockSpec((s_len, tc), lambda i, j: (0, j))
    v = pl.BlockSpec((s_len, tc), lambda i, j: (0, j + nf))
    wg = pl.BlockSpec((3, tc), lambda i, j: (0, j))
    wv = pl.BlockSpec((3, tc), lambda i, j: (0, j + nf))
    bg = pl.BlockSpec((1, tc), lambda i, j: (0, j))
    bv = pl.BlockSpec((1, tc), lambda i, j: (0, j + nf))
    return g, v, wg, wv, bg, bv


def conv_fwd(name, up, cw, cb, tc=256):
    s_len, f2 = up.shape
    f = f2 // 2
    nf = f // tc
    g, v, wg, wv, bg, bv = _conv_specs(s_len, tc, nf)

    def fn(ug, uv, cwg, cwv, cbg, cbv):
        yg = _conv(ug.astype(F32), cwg, cbg)
        yv = _conv(uv.astype(F32), cwv, cbv)
        return yg * _sig(yg) * yv

    return blockk(name, fn, (1, nf), [(up, g), (up, v), (cw, wg), (cw, wv), (cb, bg), (cb, bv)],
                  [(_sds((s_len, f), BF16), g, None)])[0]


def conv_bwd(name, up, cw, cb, dact, tc=128):
    s_len, f2 = up.shape
    f = f2 // 2
    nf = f // tc
    g, v, wg, wv, bg, bv = _conv_specs(s_len, tc, nf)

    def half(dy, u, cwh):
        u1, u2 = _shift_down(u, 1), _shift_down(u, 2)
        dup = cwh[2:3, :] * dy + cwh[1:2, :] * _shift_up(dy, 1) + cwh[0:1, :] * _shift_up(dy, 2)
        dcw = jnp.concatenate([jnp.sum(dy * u2, axis=0, keepdims=True), jnp.sum(dy * u1, axis=0, keepdims=True),
                               jnp.sum(dy * u, axis=0, keepdims=True)], axis=0)
        return dup, dcw, jnp.sum(dy, axis=0, keepdims=True)

    def fn(ug, uv, cwg, cwv, cbg, cbv, da):
        ug, uv, da = ug.astype(F32), uv.astype(F32), da.astype(F32)
        yg = _conv(ug, cwg, cbg)
        yv = _conv(uv, cwv, cbv)
        sg = _sig(yg)
        dyv = da * (yg * sg)
        dyg = da * yv * (sg * (1.0 + yg * (1.0 - sg)))
        dug, dcwg, dcbg = half(dyg, ug, cwg)
        duv, dcwv, dcbv = half(dyv, uv, cwv)
        return dug, duv, dcwg, dcwv, dcbg, dcbv

    return blockk(name, fn, (1, nf), [(up, g), (up, v), (cw, wg), (cw, wv), (cb, bg), (cb, bv), (dact, g)],
                  [(_sds((s_len, f), BF16), g, None), (_sds((s_len, f), BF16), g, None),
                   (_sds((3, f), F32), wg, None), (_sds((3, f), F32), wg, None),
                   (_sds((1, f), F32), bg, None), (_sds((1, f), F32), bg, None)])


def loss_head(name, y, target, tr=256):
    s_len, d = y.shape

    def fn(yb, tb):
        e = yb - tb
        return e * (1.0 / d), jnp.sum(e * e, axis=0, keepdims=True)

    spec = _rows(tr, d, 0)
    return blockk(name, fn, (s_len // tr, 1), [(y, spec), (target, spec)],
                  [(_sds((s_len, d), F32), spec, None), (_sds((1, d), F32), _whole((1, d)), 'all')])


def _adam(w, g, m, v):
    m = ADAM_B1 * m + (1.0 - ADAM_B1) * g
    v = ADAM_B2 * v + (1.0 - ADAM_B2) * (g * g)
    m_hat = m / (1.0 - ADAM_B1 ** ADAM_STEP)
    v_hat = v / (1.0 - ADAM_B2 ** ADAM_STEP)
    delta = -ADAM_LR * (m_hat / (jnp.sqrt(v_hat) + ADAM_EPS) + ADAM_WD * w)
    return delta, m, v


def adam_sum(name, parts, w, m, v):
    n_layers, r, c = w.shape
    n_parts = parts[0].shape[0]
    assert len(parts) == n_layers
    tr = r
    for cand in (512, 256, 128, 64, 32, 16):
        if r % cand == 0 and cand * c * 4 <= 1024 * 1024:
            tr = cand
            break

    def body(*refs):
        part_refs = refs[:n_layers]
        w_ref, m_ref, v_ref, g_out, d_out, m_out, v_out = refs[n_layers:]
        layer = pl.program_id(0)
        for ll in range(n_layers):
            @pl.when(layer == ll)
            def _(ll=ll):
                g = part_refs[ll][0].astype(F32)
                for k in range(1, n_parts):
                    g = g + part_refs[ll][k].astype(F32)
                delta, m_new, v_new = _adam(w_ref[...], g, m_ref[...], v_ref[...])
                g_out[...] = g
                d_out[...] = delta
                m_out[...] = m_new
                v_out[...] = v_new

    spec = pl.BlockSpec((None, tr, c), lambda l, i: (l, i, 0))
    part_specs = [pl.BlockSpec((n_parts, tr, c), lambda l, i, ll=ll: (0, jnp.where(l == ll, i, 0), 0))
                  for ll in range(n_layers)]
    out = _sds((n_layers, r, c), F32)
    return pl.pallas_call(body, grid=(n_layers, r // tr), in_specs=part_specs + [spec] * 3, out_specs=[spec] * 4,
                          out_shape=[out] * 4, name=name, compiler_params=_cparams(2))(*parts, w, m, v)


def sum_parts(name, parts, tr=256):
    n_parts, r, c = parts.shape

    def fn(pb):
        g = pb[0]
        for k in range(1, n_parts):
            g = g + pb[k]
        return g

    return blockk(name, fn, (r // tr, 1), [(parts, pl.BlockSpec((n_parts, tr, c), lambda i, j: (0, i, 0)))],
                  [(_sds((r, c), F32), _rows(tr, c, 0), None)])[0]


def adam_flat(name, w, g, m, v, tr=256):
    r, c = w.shape
    spec = _rows(tr, c, 0)
    return blockk(name, lambda wb, gb, mb, vb: _adam(wb, gb, mb, vb), (r // tr, 1),
                  [(w, spec), (g, spec), (m, spec), (v, spec)], [(_sds((r, c), F32), spec, None)] * 3)


_ANY = pl.BlockSpec(memory_space=pl.ANY)


def _place():
    return lax.axis_index("x"), lax.axis_index("y"), lax.axis_index("c")


def _slot(px, py, pc):
    return 4 * px + 2 * py + pc


def all_gather(name, items):
    n = len(items)

    def body(*refs):
        xs, outs = refs[:n], refs[n:2 * n]
        send_sems, recv_sems, local_sems = refs[2 * n:]
        x, y, c = _place()
        me, sibling = (x, y, c), (x, y, 1 - c)
        chips = [(1 - x, y), (x, 1 - y), (1 - x, 1 - y)]

        def copy(t, k, block, to, src=None):
            dst = outs[t].at[_slot(*block)]
            return pltpu.make_async_remote_copy(src_ref=dst if src is None else src, dst_ref=dst,
                                                send_sem=send_sems.at[7 * t + k], recv_sem=recv_sems.at[7 * t + k],
                                                device_id=to, device_id_type=MESH)

        mine = [pltpu.make_async_copy(xs[t], outs[t].at[_slot(*me)], local_sems.at[t]) for t in range(n)]
        for cp in mine:
            cp.start()
        started = []
        for t in range(n):
            started.append(copy(t, 0, me, sibling, src=xs[t]))
            started += [copy(t, 1 + j, me, (*chip, c), src=xs[t]) for j, chip in enumerate(chips)]
        for cp in started:
            cp.start()
        for j, chip in enumerate(chips):
            for t in range(n):
                copy(t, 1 + j, (*chip, c), me).wait_recv()
                passed = copy(t, 4 + j, (*chip, c), sibling)
                passed.start()
                started.append(passed)
        for t in range(n):
            copy(t, 0, sibling, me).wait_recv()
            for j, chip in enumerate(chips):
                copy(t, 4 + j, (*chip, 1 - c), me).wait_recv()
        for cp in started:
            cp.wait_send()
        for cp in mine:
            cp.wait()

    return pl.pallas_call(
        body, in_specs=[_ANY] * n, out_specs=[_ANY] * n,
        out_shape=[_sds((N_DEV,) + a.shape, a.dtype) for a in items],
        scratch_shapes=[pltpu.SemaphoreType.DMA((7 * n,)), pltpu.SemaphoreType.DMA((7 * n,)), pltpu.SemaphoreType.DMA((n,))],
        name=name)(*items)


_HBM = pl.BlockSpec(memory_space=pltpu.HBM)
_SEM = pl.BlockSpec(memory_space=pltpu.SEMAPHORE)
_EFFECT = pltpu.SideEffectType.DATAFLOW_SIDE_EFFECTING


def _peer(k, x, y, c):
    return ((1 - x) if k & 4 else x, (1 - y) if k & 2 else y, (1 - c) if k & 1 else c)


def _exchange_copies(src_refs, land_refs, send_sems, recv_sems, scatter, landing):
    x, y, c = _place()
    my_slot = _slot(x, y, c)
    copies = []
    for t, (src, land) in enumerate(zip(src_refs, land_refs)):
        for k in range(1, N_DEV):
            peer = _peer(k, x, y, c)
            copies.append(pltpu.make_async_remote_copy(
                src_ref=src.at[_slot(*peer)] if scatter else src,
                dst_ref=land.at[_slot(*peer) if landing else my_slot],
                send_sem=send_sems.at[7 * t + k - 1], recv_sem=recv_sems.at[7 * t + k - 1],
                device_id=peer, device_id_type=MESH))
    return copies


def exchange_start(name, srcs, scatter):
    n = len(srcs)
    x, y, c = _place()
    my_slot = _slot(x, y, c)
    lands = []
    for s in srcs:
        own = lax.dynamic_index_in_dim(s, my_slot, 0, keepdims=True) if scatter else s[None]
        shape = s.shape if scatter else (N_DEV,) + s.shape
        lands.append(lax.dynamic_update_slice(lax.empty(shape, s.dtype), own, (my_slot,) + (0,) * (len(shape) - 1)))

    def body(*refs):
        src_refs, land_refs = refs[:n], refs[n:2 * n]
        send_sems, recv_sems = refs[2 * n:2 * n + 2]
        token = refs[-1]
        for send in _exchange_copies(src_refs, land_refs, send_sems, recv_sems, scatter, False):
            send.start()
        token[...] = jnp.zeros_like(token)

    bufs = [pltpu.with_memory_space_constraint(a, pltpu.HBM) for a in list(srcs) + lands]
    res = pl.pallas_call(
        body, name=name, in_specs=[_HBM] * (2 * n),
        out_specs=(_SEM, _SEM) + (_HBM,) * (2 * n) + (pl.BlockSpec(memory_space=pltpu.VMEM),),
        out_shape=(pltpu.SemaphoreType.DMA((7 * n,)), pltpu.SemaphoreType.DMA((7 * n,)))
        + tuple(pltpu.HBM(a.shape, a.dtype) for a in bufs) + (_sds((8, HEAD_DIM), F32),),
        input_output_aliases={i: 2 + i for i in range(2 * n)},
        compiler_params=pltpu.CompilerParams(has_side_effects=_EFFECT))(*bufs)
    return dict(name=name, n=n, scatter=scatter, sems=res[:2], bufs=res[2:2 + 2 * n]), res[-1]


def exchange_wait(handle, after):
    n, scatter = handle['n'], handle['scatter']

    def body(*refs):
        src_refs, land_refs = refs[:n], refs[n:2 * n]
        send_sems, recv_sems = refs[2 * n:2 * n + 2]
        for landed in _exchange_copies(src_refs, land_refs, send_sems, recv_sems, scatter, True):
            landed.wait_send()
            landed.wait_recv()

    bufs = handle['bufs']
    res = pl.pallas_call(
        body, name=handle['name'].replace("start", "wait"), in_specs=[_HBM] * (2 * n) + [_SEM, _SEM, _ANY],
        out_specs=(_HBM,) * (2 * n), out_shape=tuple(pltpu.HBM(a.shape, a.dtype) for a in bufs),
        input_output_aliases={i: i for i in range(2 * n)},
        compiler_params=pltpu.CompilerParams(has_side_effects=_EFFECT))(*bufs, *handle['sems'], after)
    return list(res[n:])


def _pad_lanes(a, width=HEAD_DIM):
    return jnp.pad(a, [(0, 0)] * (a.ndim - 1) + [(0, width - a.shape[-1])])


def _slabs(full, n):
    k = full.shape[0]
    return full.reshape(k, N_DEV, n).transpose(1, 0, 2)


def fox_fwd(a, w, p):
    qkv = mm_nn(a, w['qkv'], BF16, "fox_qkv")
    fl = mm_nn(a, w['f'], F32, "fox_flogit")
    bf = _pad_lanes(p['fox_b_f'])
    cum = fox_gate_fwd("fox_gate_fwd", fl, bf)
    cum_t = cum[:, :N_HEADS].T
    cums = (cum_t[:, :, None], cum_t[:, None, :])
    qn = rms_fwd("fox_qnorm", qkv, p['fox_q_gain'], HEAD_DIM, 512, HEAD_DIM, xcol=0, nh=N_HEADS)
    kn = rms_fwd("fox_knorm", qkv, p['fox_k_gain'], HEAD_DIM, 512, HEAD_DIM, xcol=N_HEADS, nh=N_HEADS)
    o, lse = attn_fwd("fox_attn_fwd", qn, kn, qkv, HEAD_DIM ** -0.5, voff=2 * N_HEADS, cum=cums, exact_o=True)
    return o, dict(a=a, qkv=qkv, fl=fl, bf=bf, cums=cums, qn=qn, kn=kn, o=o, lse=lse)


def fox_bwd(do, s, w, p):
    dqn, dkn, dv, dcs = attn_bwd("fox_attn_bwd", s['qn'], s['kn'], s['qkv'], s['o'], do, s['lse'], HEAD_DIM ** -0.5,
                                 voff=2 * N_HEADS, cum=s['cums'])
    dq, dgq = rms_bwd("fox_qnorm_bwd", s['qkv'], p['fox_q_gain'], dqn, HEAD_DIM, 512, HEAD_DIM, xcol=0, nh=N_HEADS,
                      out_dtype=BF16)
    dk, dgk = rms_bwd("fox_knorm_bwd", s['qkv'], p['fox_k_gain'], dkn, HEAD_DIM, 512, HEAD_DIM, xcol=N_HEADS,
                      nh=N_HEADS, out_dtype=BF16)
    dqkv = jnp.concatenate([dq, dk, dv.astype(BF16)], axis=1)
    dcum = _pad_lanes(-jnp.sum(dcs[:, :, 0, :], axis=1).T)
    dfl, dbf = fox_gate_bwd("fox_gate_bwd", s['fl'], s['bf'], dcum)
    da = mm_nt(dqkv, w['qkv'], F32, "fox_da_qkv")
    da = mm_nt(dfl, w['f'], F32, "fox_da_f", add=da)
    dw_qkv = mm_tn(s['a'], dqkv, 1, BF16, "fox_dw_qkv")[0]
    dw_f = mm_tn(s['a'], dfl, 1, BF16, "fox_dw_f")[0][:, :N_HEADS]
    dw = _slabs(jnp.concatenate([dw_qkv, dw_f], axis=1), 770)
    return da, dict(fox_w_in=dw), dict(fox_b_f=dbf[:, :N_HEADS], fox_q_gain=dgq, fox_k_gain=dgk)


def _rope_tables(positions):
    inv_freq = ROPE_THETA ** (-jnp.arange(0, MLA_ROPE, 2, dtype=F32) / MLA_ROPE)
    ang = positions.astype(F32)[:, None] * inv_freq
    cos, sin = jnp.cos(ang), jnp.sin(ang)
    return _pad_lanes(jnp.concatenate([cos, cos], axis=1)), _pad_lanes(jnp.concatenate([sin, sin], axis=1))


def mla_fwd(a, w, p):
    qg, kg = p['mla_q_gain'], p['mla_k_gain']
    gains = dict(qn=qg[:, :HEAD_DIM], qr=_pad_lanes(qg[:, HEAD_DIM:]), kn=kg[:, :HEAD_DIM], kr=_pad_lanes(kg[:, HEAD_DIM:]))
    cos, sin = _rope_tables(p['positions'])
    ccr = mm_nn(a, w['in'], F32, "mla_in")
    cqn = rms_fwd("mla_cq_norm", ccr, w['q_a_gain'], 512, 256, 512, xcol=0)
    ckvn = rms_fwd("mla_ckv_norm", ccr, w['kv_a_gain'], 512, 256, 512, xcol=1)
    qf = mm_nn(cqn, w['q_b'], F32, "mla_q_b")
    kvf = mm_nn(ckvn, w['kv_b'], F32, "mla_kv_b")
    q_nope = rms_fwd("mla_qnope_norm", qf, gains['qn'], HEAD_DIM, 512, HEAD_DIM, xcol=0, nh=N_HEADS)
    k_nope = rms_fwd("mla_knope_norm", kvf, gains['kn'], HEAD_DIM, 512, HEAD_DIM, xcol=0, nh=N_HEADS)
    q_rope = rope_fwd("mla_qrope", qf, gains['qr'], cos, sin, 512, N_HEADS, N_HEADS)
    k_rope = rope_fwd("mla_krope", ccr, gains['kr'], cos, sin, 512, 8, 1)
    scale = (HEAD_DIM + MLA_ROPE) ** -0.5
    o, lse = attn_fwd("mla_attn_fwd", q_nope, k_nope, kvf, scale, voff=N_HEADS, q2=q_rope, k2=k_rope)
    return o, dict(a=a, gains=gains, cos=cos, sin=sin, ccr=ccr, cqn=cqn, ckvn=ckvn, qf=qf, kvf=kvf, q_nope=q_nope,
                   k_nope=k_nope, q_rope=q_rope, k_rope=k_rope, o=o, lse=lse, scale=scale)


def mla_bwd(do, s, w, p):
    g = s['gains']
    dqn, dkn, dv, dq2, dk2 = attn_bwd("mla_attn_bwd", s['q_nope'], s['k_nope'], s['kvf'], s['o'], do, s['lse'], s['scale'],
                                      voff=N_HEADS, q2=s['q_rope'], k2=s['k_rope'])
    dqf_n, dg_qn = rms_bwd("mla_qnope_bwd", s['qf'], g['qn'], dqn, HEAD_DIM, 512, HEAD_DIM, xcol=0, nh=N_HEADS)
    dkf, dg_kn = rms_bwd("mla_knope_bwd", s['kvf'], g['kn'], dkn, HEAD_DIM, 512, HEAD_DIM, xcol=0, nh=N_HEADS)
    dqf_r, dg_qr = rope_bwd("mla_qrope_bwd", s['qf'], g['qr'], s['cos'], s['sin'], dq2, 512, N_HEADS, N_HEADS)
    dkr, dg_kr = rope_bwd("mla_krope_bwd", s['ccr'], g['kr'], s['cos'], s['sin'], dk2, 512, 8, 1)
    dqf = jnp.concatenate([dqf_n, dqf_r], axis=1)
    dkvf = jnp.concatenate([dkf, dv], axis=1)
    dcqn = mm_nt(dqf, w['q_b'], F32, "mla_dcq")
    dckvn = mm_nt(dkvf, w['kv_b'], F32, "mla_dckv")
    dw_qb = mm_tn(s['cqn'], dqf, 1, BF16, "mla_dw_qb")[0]
    dw_kvb = mm_tn(s['ckvn'], dkvf, 1, BF16, "mla_dw_kvb")[0]
    dcq, dg_qa = rms_bwd("mla_cq_bwd", s['ccr'], w['q_a_gain'], dcqn, 512, 256, 512, xcol=0)
    dckv, dg_kva = rms_bwd("mla_ckv_bwd", s['ccr'], w['kv_a_gain'], dckvn, 512, 256, 512, xcol=1)
    dccr = jnp.concatenate([dcq, dckv, dkr], axis=1)
    da = mm_nt(dccr, w['in'], F32, "mla_da")
    dw_in = mm_tn(s['a'], dccr, 1, BF16, "mla_dw_in")[0][:, :1088].reshape(N_DEV, 256, 1088)
    hp = 2
    nope = dw_qb[:, :2048].reshape(512, N_DEV, hp, HEAD_DIM)
    rope = dw_qb[:, 2048:].reshape(512, N_DEV, hp, HEAD_DIM)[..., :MLA_ROPE]
    dw_qb_s = jnp.concatenate([nope, rope], axis=-1).transpose(1, 0, 2, 3).reshape(N_DEV, 512, hp * 192)
    kk = dw_kvb[:, :2048].reshape(512, N_DEV, hp, HEAD_DIM)
    vv = dw_kvb[:, 2048:].reshape(512, N_DEV, hp, HEAD_DIM)
    dw_kvb_s = jnp.concatenate([kk, vv], axis=-1).transpose(1, 0, 2, 3).reshape(N_DEV, 512, hp * 256)
    small = dict(mla_q_a_gain=dg_qa, mla_kv_a_gain=dg_kva,
                 mla_q_gain=jnp.concatenate([dg_qn, dg_qr[:, :MLA_ROPE]], axis=1),
                 mla_k_gain=jnp.concatenate([dg_kn, dg_kr[:, :MLA_ROPE]], axis=1))
    return da, dict(mla_w_in=dw_in, mla_w_q_b=dw_qb_s, mla_w_kv_b=dw_kvb_s), small


def sb_fwd_layer(a, w, p):
    qkv = mm_nn(a, w['in'], BF16, "sb_qkv")
    qn = rms_fwd("sb_qnorm", qkv, p['sb_q_gain'], HEAD_DIM, 512, HEAD_DIM, xcol=0, nh=N_HEADS)
    kn = rms_fwd("sb_knorm", qkv, p['sb_k_gain'], HEAD_DIM, 512, HEAD_DIM, xcol=N_HEADS, nh=N_HEADS)
    o, carries = sb_fwd("sb_attn_fwd", qn, kn, qkv, HEAD_DIM ** -0.5, 2 * N_HEADS)
    return o, dict(a=a, qkv=qkv, qn=qn, kn=kn, carries=carries)


def sb_bwd_layer(do, s, w, p):
    dqn, dkn, dv = sb_bwd("sb_attn_bwd", s['qn'], s['kn'], s['qkv'], do, s['carries'], HEAD_DIM ** -0.5, 2 * N_HEADS)
    dq, dgq = rms_bwd("sb_qnorm_bwd", s['qkv'], p['sb_q_gain'], dqn, HEAD_DIM, 512, HEAD_DIM, xcol=0, nh=N_HEADS,
                      out_dtype=BF16)
    dk, dgk = rms_bwd("sb_knorm_bwd", s['qkv'], p['sb_k_gain'], dkn, HEAD_DIM, 512, HEAD_DIM, xcol=N_HEADS, nh=N_HEADS,
                      out_dtype=BF16)
    dqkv = jnp.concatenate([dq, dk, dv.astype(BF16)], axis=1)
    da = mm_nt(dqkv, w['in'], F32, "sb_da")
    dw = mm_tn(s['a'], dqkv, N_DEV, BF16, "sb_dw_in")
    return da, dict(sb_w_in=dw), dict(sb_q_gain=dgq, sb_k_gain=dgk)


def sgu_fwd(a, w, p):
    pre = mm_nn(a, w['in'], BF16, "sgu_in")
    u, vn = sgu_pre_fwd("sgu_pre_fwd", pre, w['v_gain'], 256)
    ws = p['sgu_w_s'][0]
    bs3 = p['sgu_b_s'][0][:, :, None]
    prod = sgu_mix_fwd("sgu_mix_fwd", vn, u, ws, bs3)
    return prod, dict(a=a, pre=pre, u=u, vn=vn, ws=ws, bs3=bs3)


def sgu_bwd(dprod, s, w, p):
    du, dvn, dws, dbs3 = sgu_mix_bwd("sgu_mix_bwd", s['vn'], s['u'], s['ws'], s['bs3'], dprod)
    dpu, dpv, dgv = sgu_pre_bwd("sgu_pre_bwd", s['pre'], w['v_gain'], du, dvn, 128)
    dpre = jnp.concatenate([dpu, dpv], axis=1)
    da = mm_nt(dpre, w['in'], F32, "sgu_da")
    dw = mm_tn(s['a'], dpre, N_DEV, BF16, "sgu_dw_in")
    return da, dict(sgu_w_in=dw), dict(sgu_v_gain=dgv, sgu_w_s=dws[None], sgu_b_s=dbs3[None, :, :, 0])


MIXERS = [("fox", fox_fwd, fox_bwd), ("mla", mla_fwd, mla_bwd), ("sb", sb_fwd_layer, sb_bwd_layer),
          ("sgu", sgu_fwd, sgu_bwd)]


def _pack_rows(arrays, row_mult=256):
    flat = jnp.concatenate([a.reshape(-1).astype(F32) for a in arrays])
    per = row_mult * HEAD_DIM
    total = -(-flat.shape[0] // per) * per
    return jnp.pad(flat, (0, total - flat.shape[0])).reshape(total // HEAD_DIM, HEAD_DIM)


def _unpack_rows(packed, shapes):
    flat = packed.reshape(-1)
    out, off = [], 0
    for shp in shapes:
        n = math.prod(shp)
        out.append(flat[off:off + n].reshape(shp))
        off += n
    return out


def _mixer_shards(i, p):
    name = MIXERS[i][0]
    bf = lambda a: a.astype(BF16)
    items = {'out': bf(p[name + '_w_out'][0])}
    if name == "fox":
        items['in'] = bf(p['fox_w_in'][0])
        items['small'] = _pack_rows([p[n] for n in SMALL_SHARDED], 8)
    elif name == "mla":
        items.update({'in': bf(p['mla_w_in'][0]), 'q_b': bf(p['mla_w_q_b'][0]), 'kv_b': bf(p['mla_w_kv_b'][0])})
    else:
        items['in'] = bf(p[name + '_w_in'][0])
    return items


def _ffn_shards(i, p):
    return {'up': p['ffn_w_up'][i].astype(BF16), 'down': p['ffn_w_down'][i].astype(BF16)}


def _assemble_ffn(got):
    return {'up': got['up'], 'down': got['down'].reshape(1, D_FF, -1)}


def _assemble_mixer(i, got, p):
    name = MIXERS[i][0]
    w = {'out': got['out'].reshape(1, -1, got['out'].shape[-1])}
    small = None
    if name == "fox":
        small_shapes = [p[n].shape for n in SMALL_SHARDED]
        full = got['in'].transpose(1, 0, 2).reshape(got['in'].shape[1], -1)
        w['qkv'] = full[None, :, :3 * N_HEADS * HEAD_DIM]
        w['f'] = _pad_lanes(full[:, 3 * N_HEADS * HEAD_DIM:])[None]
        parts = [_unpack_rows(got['small'][d], small_shapes) for d in range(N_DEV)]
        small = {n: jnp.concatenate([parts[d][k] for d in range(N_DEV)], axis=ax)
                 for k, (n, ax) in enumerate(SMALL_SHARDED.items())}
    elif name == "mla":
        w['in'] = _pad_lanes(got['in'].reshape(-1, 1088), 1152)[None]
        hp = 2
        qb = got['q_b'].reshape(N_DEV, 512, hp, 192).transpose(1, 0, 2, 3)
        nope = qb[..., :HEAD_DIM].reshape(512, -1)
        rope = _pad_lanes(qb[..., HEAD_DIM:]).reshape(512, -1)
        w['q_b'] = jnp.concatenate([nope, rope], axis=1)[None]
        kvb = got['kv_b'].reshape(N_DEV, 512, hp, 256).transpose(1, 0, 2, 3)
        w['kv_b'] = jnp.concatenate([kvb[..., :HEAD_DIM].reshape(512, -1), kvb[..., HEAD_DIM:].reshape(512, -1)], axis=1)[None]
    else:
        w['in'] = got['in']
    return w, small


def _train_step(p):
    x, target = p['x'][0], p['loss_target'][0]
    p = dict(p, positions=p['positions'][0])
    xi, yi, ci = _place()
    my_slot = _slot(xi, yi, ci)

    shards0 = _mixer_shards(0, p)
    got0 = dict(zip(shards0, all_gather("gather_first", list(shards0.values()))))
    pending, order_token = {}, jnp.zeros((1, 1), F32)
    for i in range(DEPTH):
        for kind, shards in (("mix", _mixer_shards(i, p) if i else None), ("ffn", _ffn_shards(i, p))):
            if shards is not None:
                handle, token = exchange_start(f"xstart_ag_{kind}{i}", list(shards.values()), False)
                pending[kind, i] = (handle, list(shards))
                order_token = order_token + token[0:1, 0:1]

    def gathered(kind, i, after):
        handle, keys = pending[kind, i]
        return dict(zip(keys, exchange_wait(handle, after)))

    h = x
    saved, weights = [], []
    small_full = None
    for i in range(DEPTH):
        if i == 0:
            w, small_full = _assemble_mixer(0, got0, p)
            gain = p['mix_norm'][0:1] + order_token
        else:
            w, _ = _assemble_mixer(i, gathered("mix", i, h), p)
            gain = p['mix_norm'][i:i + 1]
        w.update(q_a_gain=small_full['mla_q_a_gain'], kv_a_gain=small_full['mla_kv_a_gain'],
                 v_gain=small_full['sgu_v_gain'])
        conv_w = small_full['ffn_conv_w']
        a = rms_fwd(f"mix_norm_{i}", h, gain, h.shape[1], 256, h.shape[1])
        mixed, s_mix = MIXERS[i][1](a, w, p)
        w.update(_assemble_ffn(gathered("ffn", i, mixed)))
        weights.append(w)
        h1 = mm_nn(mixed, w['out'], F32, f"mix_out_{i}", add=h)
        b = rms_fwd(f"ffn_norm_{i}", h1, p['ffn_norm'][i:i + 1], h.shape[1], 256, h.shape[1])
        up = mm_nn(b, w['up'], BF16, f"ffn_up_{i}")
        act = conv_fwd(f"ffn_conv_{i}", up, conv_w[i], p['ffn_conv_b'][i:i + 1])
        h2 = mm_nn(act, w['down'], F32, f"ffn_down_{i}", add=h1)
        saved.append(dict(h=h, mixed=mixed, s_mix=s_mix, h1=h1, b=b, up=up, act=act))
        h = h2

    dh, sq = loss_head("loss_head", h, target)
    loss = lax.psum(0.5 * jnp.sum(sq) / h.shape[1], ("x", "y", "c"))

    scatters = []
    small_g = {n: [None] * p[n].shape[0] for n in ('mix_norm', 'ffn_norm', 'ffn_conv_w', 'ffn_conv_b')}
    for i in reversed(range(DEPTH)):
        w, s = weights[i], saved[i]
        name = MIXERS[i][0]
        dact = mm_nt(dh, w['down'], BF16, f"ffn_dact_{i}")
        dw_down = mm_tn(s['act'], dh, 1, BF16, f"ffn_dw_down_{i}").reshape(N_DEV, D_FF // N_DEV, -1)
        dug, duv, dcwg, dcwv, dcbg, dcbv = conv_bwd(f"ffn_conv_bwd_{i}", s['up'], conv_w[i], p['ffn_conv_b'][i:i + 1], dact)
        dup = jnp.concatenate([dug, duv], axis=1)
        dw_up = mm_tn(s['b'], dup, N_DEV, BF16, f"ffn_dw_up_{i}")
        handle, token = exchange_start(f"xstart_rs_ffn{i}", [dw_up, dw_down], True)
        scatters.append((handle, ['ffn_w_up', 'ffn_w_down']))
        db = mm_nt(dup, w['up'], F32, f"ffn_db_{i}")
        dh1, dg_ffn = rms_bwd(f"ffn_norm_bwd_{i}", s['h1'], p['ffn_norm'][i:i + 1] + token[0:1, 0:1], db, h.shape[1], 128,
                              h.shape[1], resid=dh)
        dmix = mm_nt(dh1, w['out'], BF16, f"mix_dout_{i}")
        dw_out = mm_tn(s['mixed'], dh1, 1, BF16, f"mix_dw_out_{i}").reshape(N_DEV, -1, h.shape[1])
        da, big_i, small_i = MIXERS[i][2](dmix, s['s_mix'], w, p)
        big_i[name + '_w_out'] = dw_out
        handle, token = exchange_start(f"xstart_rs_mix{i}", list(big_i.values()), True)
        scatters.append((handle, list(big_i)))
        dh, dg_mix = rms_bwd(f"mix_norm_bwd_{i}", s['h'], p['mix_norm'][i:i + 1] + token[0:1, 0:1], da, h.shape[1], 128,
                             h.shape[1], resid=dh1)
        for k, v in small_i.items():
            small_g[k] = v
        small_g['mix_norm'][i], small_g['ffn_norm'][i] = dg_mix, dg_ffn
        small_g['ffn_conv_w'][i] = jnp.concatenate([dcwg, dcwv], axis=1)[None]
        small_g['ffn_conv_b'][i] = jnp.concatenate([dcbg, dcbv], axis=1)
    for n in ('mix_norm', 'ffn_norm', 'ffn_conv_w', 'ffn_conv_b'):
        small_g[n] = jnp.concatenate(small_g[n], axis=0)
    grad_x = dh[None]

    big = {}
    for handle, names in scatters:
        for n, landed in zip(names, exchange_wait(handle, dh)):
            big.setdefault(n, []).insert(0, landed)
    grads, deltas, new_m, new_v = {}, {}, {}, {}
    for n in BIG:
        grads[n], deltas[n], new_m[n], new_v[n] = adam_sum("adam_" + n, big[n], p[n], p['m_' + n], p['v_' + n])

    full_shapes = [tuple(small_g[n].shape) for n in SMALL]
    partials = all_gather("gather_small_grads", [_pack_rows([small_g[n] for n in SMALL])])[0]
    summed = _unpack_rows(sum_parts("sum_small_grads", partials), full_shapes)
    mine = []
    for n, g in zip(SMALL, summed):
        if n in SMALL_SHARDED:
            ax = SMALL_SHARDED[n]
            g = lax.dynamic_slice_in_dim(g, my_slot * p[n].shape[ax], p[n].shape[ax], axis=ax)
        mine.append(g)
    shapes = [p[n].shape for n in SMALL]
    packed = [_pack_rows(arrs) for arrs in ([p[n] for n in SMALL], mine, [p['m_' + n] for n in SMALL], [p['v_' + n] for n in SMALL])]
    d_s, m_s, v_s = adam_flat("adam_small", *packed)
    for n, g, d, m, v in zip(SMALL, mine, _unpack_rows(d_s, shapes), _unpack_rows(m_s, shapes), _unpack_rows(v_s, shapes)):
        grads[n], deltas[n], new_m[n], new_v[n] = g, d, m, v

    return (loss, grad_x, *[grads[n] for n in WEIGHTS], *[deltas[n] for n in WEIGHTS], *[new_m[n] for n in WEIGHTS],
            *[new_v[n] for n in WEIGHTS])


def kernel(x, positions, mix_norm, ffn_norm, fox_w_in, fox_b_f, fox_q_gain, fox_k_gain, fox_w_out, mla_w_in, mla_q_a_gain, mla_kv_a_gain, mla_w_q_b, mla_w_kv_b, mla_q_gain, mla_k_gain, mla_w_out, sb_w_in, sb_q_gain, sb_k_gain, sb_w_out, sgu_w_in, sgu_v_gain, sgu_w_s, sgu_b_s, sgu_w_out, ffn_w_up, ffn_conv_w, ffn_conv_b, ffn_w_down, loss_target, m_mix_norm, m_ffn_norm, m_fox_w_in, m_fox_b_f, m_fox_q_gain, m_fox_k_gain, m_fox_w_out, m_mla_w_in, m_mla_q_a_gain, m_mla_kv_a_gain, m_mla_w_q_b, m_mla_w_kv_b, m_mla_q_gain, m_mla_k_gain, m_mla_w_out, m_sb_w_in, m_sb_q_gain, m_sb_k_gain, m_sb_w_out, m_sgu_w_in, m_sgu_v_gain, m_sgu_w_s, m_sgu_b_s, m_sgu_w_out, m_ffn_w_up, m_ffn_conv_w, m_ffn_conv_b, m_ffn_w_down, v_mix_norm, v_ffn_norm, v_fox_w_in, v_fox_b_f, v_fox_q_gain, v_fox_k_gain, v_fox_w_out, v_mla_w_in, v_mla_q_a_gain, v_mla_kv_a_gain, v_mla_w_q_b, v_mla_w_kv_b, v_mla_q_gain, v_mla_k_gain, v_mla_w_out, v_sb_w_in, v_sb_q_gain, v_sb_k_gain, v_sb_w_out, v_sgu_w_in, v_sgu_v_gain, v_sgu_w_s, v_sgu_b_s, v_sgu_w_out, v_ffn_w_up, v_ffn_conv_w, v_ffn_conv_b, v_ffn_w_down):
    args = locals()
    names = ['x', 'positions'] + WEIGHTS + ['loss_target'] + ['m_' + n for n in WEIGHTS] + ['v_' + n for n in WEIGHTS]
    return _train_step({n: args[n] for n in names})
```

```python
import functools
import math

import jax
import jax.numpy as jnp
from jax import lax
from jax.experimental import pallas as pl
from jax.experimental.pallas import tpu as pltpu

F32 = jnp.float32
BF16 = jnp.bfloat16
MESH = pl.DeviceIdType.MESH

N_DEV = 8
N_HEADS = 16
HEAD_DIM = 128
EPS = 1e-6
DEPTH = 4
D_FF = 5632
MLA_ROPE = 64
ROPE_THETA = 10000.0
VMEM_LIMIT = 48 * 1024 * 1024

ADAM_LR, ADAM_B1, ADAM_B2, ADAM_EPS, ADAM_WD, ADAM_STEP = 0.001, 0.9, 0.999, 1e-08, 0.01, 10

WEIGHTS = ['mix_norm', 'ffn_norm', 'fox_w_in', 'fox_b_f', 'fox_q_gain', 'fox_k_gain', 'fox_w_out', 'mla_w_in',
           'mla_q_a_gain', 'mla_kv_a_gain', 'mla_w_q_b', 'mla_w_kv_b', 'mla_q_gain', 'mla_k_gain', 'mla_w_out',
           'sb_w_in', 'sb_q_gain', 'sb_k_gain', 'sb_w_out', 'sgu_w_in', 'sgu_v_gain', 'sgu_w_s', 'sgu_b_s',
           'sgu_w_out', 'ffn_w_up', 'ffn_conv_w', 'ffn_conv_b', 'ffn_w_down']
BIG = ['fox_w_in', 'fox_w_out', 'mla_w_in', 'mla_w_q_b', 'mla_w_kv_b', 'mla_w_out', 'sb_w_in', 'sb_w_out',
       'sgu_w_in', 'sgu_w_out', 'ffn_w_up', 'ffn_w_down']
SMALL = [w for w in WEIGHTS if w not in BIG]
SMALL_SHARDED = {'mla_q_a_gain': 1, 'mla_kv_a_gain': 1, 'sgu_v_gain': 1, 'ffn_conv_w': 2}


def _cparams(n_grid):
    return pltpu.CompilerParams(dimension_semantics=("arbitrary",) * n_grid, vmem_limit_bytes=VMEM_LIMIT)


def _pick(n, cap):
    best = None
    t = 128
    while t <= min(n, cap):
        if n % t == 0:
            best = t
        t += 128
    return best if best is not None else n


def _mm_call(name, a, b, out_shape, a_spec, b_spec, o_spec, grid, dims, acc_shape, add=None):
    nk = grid[2]

    def body(*refs):
        if add is None:
            a_ref, b_ref, o_ref, acc = refs
            add_ref = None
        else:
            a_ref, b_ref, add_ref, o_ref, acc = refs
        k = pl.program_id(2)

        @pl.when(k == 0)
        def _():
            acc[...] = jnp.zeros_like(acc)

        acc[...] += lax.dot_general(a_ref[...].astype(BF16), b_ref[...].astype(BF16), (dims, ((), ())),
                                    preferred_element_type=F32)

        @pl.when(k == nk - 1)
        def _():
            r = acc[...]
            if add_ref is not None:
                r = r + add_ref[...].astype(F32)
            o_ref[...] = r.astype(o_ref.dtype)

    ins = [a, b] + ([] if add is None else [add])
    in_specs = [a_spec, b_spec] + ([] if add is None else [o_spec])
    return pl.pallas_call(body, grid=grid, in_specs=in_specs, out_specs=o_spec, out_shape=out_shape,
                          scratch_shapes=[pltpu.VMEM(acc_shape, F32)], name=name,
                          compiler_params=_cparams(3))(*ins)


def mm_nn(a, b3, out_dtype, name, add=None, joff=0, nj=None):
    m, kk = a.shape
    _, kb, n = b3.shape
    assert kb == kk
    nj = b3.shape[0] - joff if nj is None else nj
    tm, tn, tk = _pick(m, 1024), _pick(n, 1536), _pick(kk, 512)
    nb = n // tn
    return _mm_call(
        name, a, b3, jax.ShapeDtypeStruct((m, nj * n), out_dtype),
        pl.BlockSpec((tm, tk), lambda i, c, k: (i, k)),
        pl.BlockSpec((None, tk, tn), lambda i, c, k: (joff + c // nb, k, c % nb)),
        pl.BlockSpec((tm, tn), lambda i, c, k: (i, c)),
        (m // tm, nj * nb, kk // tk), ((1,), (0,)), (tm, tn), add=add)


def mm_nt(a, b3, out_dtype, name, add=None, joff=0, nj=None):
    m, na = a.shape
    _, ko, n = b3.shape
    nj = b3.shape[0] - joff if nj is None else nj
    assert na == nj * n
    tm, to, tn = _pick(m, 1024), _pick(ko, 1024), _pick(n, 1536)
    nb = n // tn
    return _mm_call(
        name, a, b3, jax.ShapeDtypeStruct((m, ko), out_dtype),
        pl.BlockSpec((tm, tn), lambda i, o, c: (i, c)),
        pl.BlockSpec((None, to, tn), lambda i, o, c: (joff + c // nb, o, c % nb)),
        pl.BlockSpec((tm, to), lambda i, o, c: (i, o)),
        (m // tm, ko // to, nj * nb), ((1,), (1,)), (tm, to), add=add)


def mm_tn(a, b, nj, out_dtype, name):
    s, ko = a.shape
    sb, nb_tot = b.shape
    assert sb == s and nb_tot % nj == 0
    n = nb_tot // nj
    to, tn, ts = _pick(ko, 1024), _pick(n, 1536), _pick(s, 512)
    nb = n // tn
    return _mm_call(
        name, a, b, jax.ShapeDtypeStruct((nj, ko, n), out_dtype),
        pl.BlockSpec((ts, to), lambda o, c, k: (k, o)),
        pl.BlockSpec((ts, tn), lambda o, c, k: (k, c)),
        pl.BlockSpec((None, to, tn), lambda o, c, k: (c // nb, o, c % nb)),
        (ko // to, nj * nb, s // ts), ((0,), (0,)), (to, tn))


def blockk(name, fn, grid, ins, outs):
    n_in = len(ins)
    accs = [o[2] for o in outs]

    def body(*refs):
        vals = fn(*[r[...] for r in refs[:n_in]])
        if not isinstance(vals, (tuple, list)):
            vals = (vals,)
        i, j = pl.program_id(0), pl.program_id(1)
        for r, v, acc in zip(refs[n_in:], vals, accs):
            if acc is None:
                r[...] = v.astype(r.dtype)
            else:
                first = (j == 0) if acc == 'inner' else jnp.logical_and(i == 0, j == 0)

                @pl.when(first)
                def _(r=r, v=v):
                    r[...] = v.astype(r.dtype)

                @pl.when(jnp.logical_not(first))
                def _(r=r, v=v):
                    r[...] += v.astype(r.dtype)

    res = pl.pallas_call(body, grid=grid, in_specs=[s for _, s in ins], out_specs=[o[1] for o in outs],
                         out_shape=[o[0] for o in outs], name=name, compiler_params=_cparams(2))(*[a for a, _ in ins])
    return res


def _sds(shape, dtype):
    return jax.ShapeDtypeStruct(tuple(shape), dtype)


def _rows(tr, w, col=0):
    if col == 'j':
        return pl.BlockSpec((tr, w), lambda i, j: (i, j))
    if callable(col):
        return pl.BlockSpec((tr, w), lambda i, j: (i, col(j)))
    return pl.BlockSpec((tr, w), lambda i, j: (i, col))


def _whole(shape):
    nd = len(shape)
    return pl.BlockSpec(tuple(shape), lambda i, j: (0,) * nd)


def _rms(x, g, n):
    ms = jnp.sum(x * x, axis=-1, keepdims=True) * (1.0 / n)
    return x * lax.rsqrt(ms + EPS) * g


def _sig(x):
    return 1.0 / (1.0 + jnp.exp(-x))


def _gelu(x):
    return 0.5 * x * (1.0 + jnp.tanh(math.sqrt(2.0 / math.pi) * (x + 0.044715 * (x * x * x))))


def _lane_iota(shape):
    return lax.broadcasted_iota(jnp.int32, shape, len(shape) - 1)


def _rope(x, cos, sin):
    lane = _lane_iota(x.shape)
    half = MLA_ROPE // 2
    swapped = jnp.where(lane < half, pltpu.roll(x, HEAD_DIM - half, 1), pltpu.roll(x, half, 1))
    sign = jnp.where(lane < half, -1.0, 1.0)
    return x * cos + swapped * (sin * sign)


def _rope_t(dy, cos, sin):
    lane = _lane_iota(dy.shape)
    half = MLA_ROPE // 2
    t = dy * sin
    swapped = jnp.where(lane < half, pltpu.roll(t, HEAD_DIM - half, 1), pltpu.roll(t, half, 1))
    sign = jnp.where(lane < half, 1.0, -1.0)
    return dy * cos + swapped * sign


def rms_fwd(name, x, gain, n, tr, width, xcol=0, nh=1, out_dtype=BF16, out_cols=None):
    r = x.shape[0]
    out_cols = width * nh if out_cols is None else out_cols
    xspec = _rows(tr, width, (lambda j: xcol + j) if nh > 1 else xcol)
    ospec = _rows(tr, width, 'j' if nh > 1 else 0)
    return blockk(name, lambda xb, g: _rms(xb.astype(F32), g, n), (r // tr, nh),
                  [(x, xspec), (gain, _whole(gain.shape))], [(_sds((r, out_cols), out_dtype), ospec, None)])[0]


def rms_bwd(name, x, gain, dy, n, tr, width, xcol=0, nh=1, dycol=0, resid=None, out_dtype=F32):
    r = x.shape[0]
    xspec = _rows(tr, width, (lambda j: xcol + j) if nh > 1 else xcol)
    dyspec = _rows(tr, width, (lambda j: dycol + j) if nh > 1 else dycol)
    ospec = _rows(tr, width, 'j' if nh > 1 else 0)

    def fn(xb, g, dyb, *rest):
        _, vjp = jax.vjp(lambda a, b: _rms(a, b, n), xb.astype(F32), g)
        dx, dg = vjp(dyb.astype(F32))
        if rest:
            dx = dx + rest[0].astype(F32)
        return dx, dg

    ins = [(x, xspec), (gain, _whole(gain.shape)), (dy, dyspec)]
    if resid is not None:
        ins.append((resid, ospec))
    return blockk(name, fn, (r // tr, nh), ins,
                  [(_sds((r, width * nh), out_dtype), ospec, None), (_sds(gain.shape, F32), _whole(gain.shape), 'all')])


def _nt(a, b):
    return lax.dot_general(a, b, (((1,), (1,)), ((), ())), preferred_element_type=F32)


def _tn(a, b):
    return lax.dot_general(a, b, (((0,), (0,)), ((), ())), preferred_element_type=F32)


def _nn(a, b):
    return lax.dot_general(a, b, (((1,), (0,)), ((), ())), preferred_element_type=F32)


def _attn_specs(tq, qoff, koff, voff, extra, bias, q2off):
    d = HEAD_DIM
    specs = [pl.BlockSpec((tq, d), lambda h, i, j: (i, qoff + h)),
             pl.BlockSpec((tq, d), lambda h, i, j: (jnp.minimum(i, j), koff + h)),
             pl.BlockSpec((tq, d), lambda h, i, j: (jnp.minimum(i, j), voff + h))]
    if extra:
        specs += [pl.BlockSpec((tq, d), lambda h, i, j: (i, q2off + h)),
                  pl.BlockSpec((tq, d), lambda h, i, j: (jnp.minimum(i, j), 0))]
    if bias:
        specs += [pl.BlockSpec((None, tq, 1), lambda h, i, j: (h, i, 0)),
                  pl.BlockSpec((None, 1, tq), lambda h, i, j: (h, 0, jnp.minimum(i, j)))]
    return specs


ATTN_BLOCK = 512


def _scores(q_ref, k_ref, q2_ref, k2_ref, cc_ref, cr_ref, scale, tq, diagonal):
    s = _nt(q_ref[...].astype(BF16), k_ref[...].astype(BF16))
    if q2_ref is not None:
        s = s + _nt(q2_ref[...].astype(BF16), k2_ref[...].astype(BF16))
    s = s * scale
    if cc_ref is not None:
        s = s + (cc_ref[...] - cr_ref[...])
    if not diagonal:
        return s, None
    return s, lax.broadcasted_iota(jnp.int32, (tq, tq), 1) <= lax.broadcasted_iota(jnp.int32, (tq, tq), 0)


def _on_blocks(qi, kj, step):
    @pl.when(kj < qi)
    def _():
        step(False)

    @pl.when(kj == qi)
    def _():
        step(True)


def attn_fwd(name, q, k, v, scale, *, qoff=0, koff=0, voff=0, q2=None, k2=None, q2off=0, cum=None, tq=ATTN_BLOCK,
             exact_o=False):
    s_len = q.shape[0]
    nq = s_len // tq
    extra, bias = q2 is not None, cum is not None
    n_in = 3 + 2 * extra + 2 * bias

    def body(*refs):
        q_ref, k_ref, v_ref = refs[:3]
        p = 3
        q2_ref = k2_ref = cc_ref = cr_ref = None
        if extra:
            q2_ref, k2_ref = refs[p:p + 2]
            p += 2
        if bias:
            cc_ref, cr_ref = refs[p:p + 2]
            p += 2
        o_ref, lse_ref, m_s, l_s, acc_s = refs[p:]
        qi, kj = pl.program_id(1), pl.program_id(2)

        @pl.when(kj == 0)
        def _():
            m_s[...] = jnp.full_like(m_s, -jnp.inf)
            l_s[...] = jnp.zeros_like(l_s)
            acc_s[...] = jnp.zeros_like(acc_s)

        def step(diagonal):
            s, allowed = _scores(q_ref, k_ref, q2_ref, k2_ref, cc_ref, cr_ref, scale, tq, diagonal)
            if diagonal:
                s = jnp.where(allowed, s, -jnp.inf)
            m_old = m_s[...]
            m_new = jnp.maximum(m_old, jnp.max(s, axis=-1, keepdims=True))
            alpha = jnp.exp(m_old - m_new)
            pr = jnp.exp(s - m_new)
            l_s[...] = alpha * l_s[...] + jnp.sum(pr, axis=-1, keepdims=True)
            vb = v_ref[...].astype(BF16)
            pv = _nn(pr.astype(BF16), vb)
            if exact_o:
                pv = pv + _nn((pr - pr.astype(BF16).astype(F32)).astype(BF16), vb)
            acc_s[...] = alpha * acc_s[...] + pv
            m_s[...] = m_new

        _on_blocks(qi, kj, step)

        @pl.when(kj == qi)
        def _():
            o_ref[...] = (acc_s[...] / l_s[...]).astype(o_ref.dtype)
            lse_ref[...] = m_s[...] + jnp.log(l_s[...])

    ins = [q, k, v] + ([q2, k2] if extra else []) + (list(cum) if bias else [])
    d = HEAD_DIM
    return pl.pallas_call(
        body, grid=(N_HEADS, nq, nq), in_specs=_attn_specs(tq, qoff, koff, voff, extra, bias, q2off),
        out_specs=[pl.BlockSpec((tq, d), lambda h, i, j: (i, h)), pl.BlockSpec((None, tq, 1), lambda h, i, j: (h, i, 0))],
        out_shape=[_sds((s_len, N_HEADS * d), F32 if exact_o else BF16), _sds((N_HEADS, s_len, 1), F32)],
        scratch_shapes=[pltpu.VMEM((tq, 1), F32), pltpu.VMEM((tq, 1), F32), pltpu.VMEM((tq, d), F32)],
        name=name, compiler_params=_cparams(3))(*ins)


def attn_bwd(name, q, k, v, o, do, lse, scale, *, qoff=0, koff=0, voff=0, q2=None, k2=None, q2off=0, cum=None,
             tq=ATTN_BLOCK):
    s_len = q.shape[0]
    nq = s_len // tq
    extra, bias = q2 is not None, cum is not None
    d = HEAD_DIM
    n_in = 6 + 2 * extra + 2 * bias

    def body(*refs):
        q_ref, k_ref, v_ref = refs[:3]
        p = 3
        q2_ref = k2_ref = cc_ref = cr_ref = None
        if extra:
            q2_ref, k2_ref = refs[p:p + 2]
            p += 2
        if bias:
            cc_ref, cr_ref = refs[p:p + 2]
            p += 2
        o_ref, do_ref, lse_ref = refs[p:p + 3]
        p += 3
        dq_ref, dk_ref, dv_ref = refs[p:p + 3]
        p += 3
        dq2_ref = dk2_ref = dcs_ref = None
        if extra:
            dq2_ref, dk2_ref = refs[p:p + 2]
            p += 2
        if bias:
            dcs_ref = refs[p]
            p += 1
        dq_s, delta_s = refs[p:p + 2]
        dq2_s = refs[p + 2] if extra else None
        h, qi, kj = pl.program_id(0), pl.program_id(1), pl.program_id(2)

        @pl.when(jnp.logical_and(qi == 0, kj == 0))
        def _():
            dk_ref[...] = jnp.zeros_like(dk_ref)
            dv_ref[...] = jnp.zeros_like(dv_ref)

        if extra:
            @pl.when(jnp.logical_and(h == 0, jnp.logical_and(qi == 0, kj == 0)))
            def _():
                dk2_ref[...] = jnp.zeros_like(dk2_ref)

        @pl.when(kj == 0)
        def _():
            dq_s[...] = jnp.zeros_like(dq_s)
            if extra:
                dq2_s[...] = jnp.zeros_like(dq2_s)
            delta_s[...] = jnp.sum(do_ref[...].astype(F32) * o_ref[...].astype(F32), axis=-1, keepdims=True)

        if bias:
            @pl.when(kj > qi)
            def _():
                dcs_ref[...] = jnp.zeros_like(dcs_ref)

        def step(diagonal):
            s, allowed = _scores(q_ref, k_ref, q2_ref, k2_ref, cc_ref, cr_ref, scale, tq, diagonal)
            pr = jnp.exp(s - lse_ref[...])
            if diagonal:
                pr = jnp.where(allowed, pr, 0.0)
            dob = do_ref[...].astype(BF16)
            dp = _nt(dob, v_ref[...].astype(BF16))
            ds = pr * (dp - delta_s[...])
            dsb = (ds * scale).astype(BF16)
            ks = pl.ds(pl.multiple_of(kj * tq, tq), tq)
            dq_s[...] += _nn(dsb, k_ref[...].astype(BF16))
            dk_ref[ks, :] += _tn(dsb, q_ref[...].astype(BF16))
            dv_ref[ks, :] += _tn(pr.astype(BF16), dob)
            if extra:
                dq2_s[...] += _nn(dsb, k2_ref[...].astype(BF16))
                dk2_ref[ks, :] += _tn(dsb, q2_ref[...].astype(BF16))
            if bias:
                dcs_ref[...] = jnp.sum(ds, axis=0, keepdims=True)

        _on_blocks(qi, kj, step)

        @pl.when(kj == qi)
        def _():
            dq_ref[...] = dq_s[...]
            if extra:
                dq2_ref[...] = dq2_s[...]

    ins = [q, k, v] + ([q2, k2] if extra else []) + (list(cum) if bias else []) + [o, do, lse]
    in_specs = _attn_specs(tq, qoff, koff, voff, extra, bias, q2off) + [
        pl.BlockSpec((tq, d), lambda h, i, j: (i, h)), pl.BlockSpec((tq, d), lambda h, i, j: (i, h)),
        pl.BlockSpec((None, tq, 1), lambda h, i, j: (h, i, 0))]
    full = _sds((s_len, N_HEADS * d), F32)
    out_shape = [full, full, full]
    out_specs = [pl.BlockSpec((tq, d), lambda h, i, j: (i, h)), pl.BlockSpec((s_len, d), lambda h, i, j: (0, h)),
                 pl.BlockSpec((s_len, d), lambda h, i, j: (0, h))]
    scratch = [pltpu.VMEM((tq, d), F32), pltpu.VMEM((tq, 1), F32)]
    if extra:
        out_shape += [full, _sds((s_len, d), F32)]
        out_specs += [pl.BlockSpec((tq, d), lambda h, i, j: (i, h)), pl.BlockSpec((s_len, d), lambda h, i, j: (0, 0))]
        scratch.append(pltpu.VMEM((tq, d), F32))
    if bias:
        out_shape.append(_sds((N_HEADS, nq, 1, s_len), F32))
        out_specs.append(pl.BlockSpec((None, None, 1, tq), lambda h, i, j: (h, i, 0, j)))
    return pl.pallas_call(body, grid=(N_HEADS, nq, nq), in_specs=in_specs, out_specs=out_specs, out_shape=out_shape,
                          scratch_shapes=scratch, name=name, compiler_params=_cparams(3))(*ins)


def _sb_terms(q_ref, k_ref, scale, tq, diagonal):
    z = _nt(q_ref[...].astype(BF16), k_ref[...].astype(BF16)) * scale
    lg = jnp.log(1.0 + jnp.exp(-jnp.abs(z)))
    log_keep = -(jnp.maximum(z, 0.0) + lg)
    log_beta = jnp.minimum(z, 0.0) - lg
    if not diagonal:
        return None, log_keep, log_beta
    strict = lax.broadcasted_iota(jnp.int32, (tq, tq), 1) < lax.broadcasted_iota(jnp.int32, (tq, tq), 0)
    return strict, jnp.where(strict, log_keep, 0.0), log_beta


def _tri(tq, pred):
    a = lax.broadcasted_iota(jnp.int32, (tq, tq), 0)
    b = lax.broadcasted_iota(jnp.int32, (tq, tq), 1)
    return jnp.where(pred(a, b), 1.0, 0.0).astype(BF16)


SUM_CHUNK = 256


def _lane_sums(x, later):
    n = x.shape[1]
    chunk = min(SUM_CHUNK, n)
    tri = _tri(chunk, (lambda m, j: m > j) if later else (lambda m, j: m < j))
    order = range(n // chunk - 1, -1, -1) if later else range(n // chunk)
    pieces, carry = [None] * (n // chunk), None
    for cidx in order:
        xc = x[:, cidx * chunk:(cidx + 1) * chunk]
        hi = xc.astype(BF16)
        local = _nn(hi, tri) + _nn((xc - hi.astype(F32)).astype(BF16), tri)
        total = jnp.sum(xc, axis=-1, keepdims=True)
        pieces[cidx] = local if carry is None else local + carry
        carry = total if carry is None else carry + total
    return jnp.concatenate(pieces, axis=1), carry


def sb_fwd(name, qn, kn, qkv, scale, voff, tq=ATTN_BLOCK):
    s_len = qn.shape[0]
    nq = s_len // tq
    d = HEAD_DIM

    def body(q_ref, k_ref, v_ref, o_ref, car_ref, ca_s, acc_s):
        qi, kj = pl.program_id(1), pl.program_id(2)

        @pl.when(kj == 0)
        def _():
            ca_s[...] = jnp.zeros_like(ca_s)
            acc_s[...] = jnp.zeros_like(acc_s)

        def step(diagonal):
            strict, log_keep, log_beta = _sb_terms(q_ref, k_ref, scale, tq, diagonal)
            ca = ca_s[...]
            car_ref[...] = ca
            after, total = _lane_sums(log_keep, True)
            a = jnp.exp(log_beta + (after + ca))
            if diagonal:
                a = jnp.where(strict, a, 0.0)
            acc_s[...] += _nn(a.astype(BF16), v_ref[...].astype(BF16))
            ca_s[...] = ca + total

        @pl.when(kj == 0)
        def _():
            step(True)

        @pl.when(jnp.logical_and(kj > 0, kj <= qi))
        def _():
            step(False)

        @pl.when(kj == qi)
        def _():
            o_ref[...] = acc_s[...].astype(o_ref.dtype)

    kblk = lambda i, j: jnp.maximum(i - j, 0)
    return pl.pallas_call(
        body, grid=(N_HEADS, nq, nq),
        in_specs=[pl.BlockSpec((tq, d), lambda h, i, j: (i, h)), pl.BlockSpec((tq, d), lambda h, i, j: (kblk(i, j), h)),
                  pl.BlockSpec((tq, d), lambda h, i, j: (kblk(i, j), voff + h))],
        out_specs=[pl.BlockSpec((tq, d), lambda h, i, j: (i, h)),
                   pl.BlockSpec((None, None, tq, 1), lambda h, i, j: (h, kblk(i, j), i, 0))],
        out_shape=[_sds((s_len, N_HEADS * d), BF16), _sds((N_HEADS, nq, s_len, 1), F32)],
        scratch_shapes=[pltpu.VMEM((tq, 1), F32), pltpu.VMEM((tq, d), F32)],
        name=name, compiler_params=_cparams(3))(qn, kn, qkv)


def sb_bwd(name, qn, kn, qkv, do, carries, scale, voff, tq=ATTN_BLOCK):
    s_len = qn.shape[0]
    nq = s_len // tq
    d = HEAD_DIM

    def body(q_ref, k_ref, v_ref, do_ref, car_ref, dq_ref, dk_ref, dv_ref, dq_s, cg_s):
        qi, kj = pl.program_id(1), pl.program_id(2)

        @pl.when(jnp.logical_and(qi == 0, kj == 0))
        def _():
            dk_ref[...] = jnp.zeros_like(dk_ref)
            dv_ref[...] = jnp.zeros_like(dv_ref)

        @pl.when(kj == 0)
        def _():
            dq_s[...] = jnp.zeros_like(dq_s)
            cg_s[...] = jnp.zeros_like(cg_s)

        def step(diagonal):
            strict, log_keep, log_beta = _sb_terms(q_ref, k_ref, scale, tq, diagonal)
            after, _ = _lane_sums(log_keep, True)
            a = jnp.exp(log_beta + (after + car_ref[...]))
            if diagonal:
                a = jnp.where(strict, a, 0.0)
            dob = do_ref[...].astype(BF16)
            g = a * _nt(dob, v_ref[...].astype(BF16))
            cg = cg_s[...]
            before, total = _lane_sums(g, False)
            big_g = before + cg
            cg_s[...] = cg + total
            beta = jnp.exp(log_beta)
            dz = g * (1.0 - beta) - big_g * beta
            if diagonal:
                dz = jnp.where(strict, dz, 0.0)
            dzb = (dz * scale).astype(BF16)
            ks = pl.ds(pl.multiple_of(kj * tq, tq), tq)
            dq_s[...] += _nn(dzb, k_ref[...].astype(BF16))
            dk_ref[ks, :] += _tn(dzb, q_ref[...].astype(BF16))
            dv_ref[ks, :] += _tn(a.astype(BF16), dob)

        _on_blocks(qi, kj, step)

        @pl.when(kj == qi)
        def _():
            dq_ref[...] = dq_s[...]

    kblk = lambda i, j: jnp.minimum(i, j)
    full = _sds((s_len, N_HEADS * d), F32)
    return pl.pallas_call(
        body, grid=(N_HEADS, nq, nq),
        in_specs=[pl.BlockSpec((tq, d), lambda h, i, j: (i, h)), pl.BlockSpec((tq, d), lambda h, i, j: (kblk(i, j), h)),
                  pl.BlockSpec((tq, d), lambda h, i, j: (kblk(i, j), voff + h)),
                  pl.BlockSpec((tq, d), lambda h, i, j: (i, h)),
                  pl.BlockSpec((None, None, tq, 1), lambda h, i, j: (h, kblk(i, j), i, 0))],
        out_specs=[pl.BlockSpec((tq, d), lambda h, i, j: (i, h)), pl.BlockSpec((s_len, d), lambda h, i, j: (0, h)),
                   pl.BlockSpec((s_len, d), lambda h, i, j: (0, h))],
        out_shape=[full, full, full],
        scratch_shapes=[pltpu.VMEM((tq, d), F32), pltpu.VMEM((tq, 1), F32)],
        name=name, compiler_params=_cparams(3))(qn, kn, qkv, do, carries)


def _cumsum_rows(x, reverse):
    n = x.shape[0] // HEAD_DIM
    tri = _tri(HEAD_DIM, (lambda a, b: b >= a) if reverse else (lambda a, b: b <= a))
    pieces = [None] * n
    carry = jnp.zeros((1, HEAD_DIM), F32)
    order = range(n - 1, -1, -1) if reverse else range(n)
    for blk in order:
        xb = x[blk * HEAD_DIM:(blk + 1) * HEAD_DIM, :]
        x1 = xb.astype(BF16)
        r1 = xb - x1.astype(F32)
        x2 = r1.astype(BF16)
        x3 = (r1 - x2.astype(F32)).astype(BF16)
        c = _nn(tri, x1) + _nn(tri, x2) + _nn(tri, x3) + carry
        pieces[blk] = c
        carry = c[0:1, :] if reverse else c[HEAD_DIM - 1:HEAD_DIM, :]
    return jnp.concatenate(pieces, axis=0)


def _log_sigmoid(x):
    return jnp.minimum(x, 0.0) - jnp.log(1.0 + jnp.exp(-jnp.abs(x)))


def fox_gate_fwd(name, fl, bf):
    return blockk(name, lambda f, b: _cumsum_rows(_log_sigmoid(f + b), False), (1, 1),
                  [(fl, _whole(fl.shape)), (bf, _whole(bf.shape))], [(_sds(fl.shape, F32), _whole(fl.shape), None)])[0]


def fox_gate_bwd(name, fl, bf, dcum):
    def fn(f, b, dc):
        dlogf = _cumsum_rows(dc, True)
        dfl = dlogf * _sig(-(f + b))
        return dfl, jnp.sum(dfl, axis=0, keepdims=True)

    return blockk(name, fn, (1, 1), [(fl, _whole(fl.shape)), (bf, _whole(bf.shape)), (dcum, _whole(dcum.shape))],
                  [(_sds(fl.shape, F32), _whole(fl.shape), None), (_sds(bf.shape, F32), _whole(bf.shape), None)])


def rope_fwd(name, x, gain, cos, sin, tr, xcol, nh):
    r = x.shape[0]
    xspec = _rows(tr, HEAD_DIM, lambda j: xcol + j)
    tspec = _rows(tr, HEAD_DIM, 0)
    return blockk(name, lambda xb, g, c, s: _rope(_rms(xb.astype(F32), g, MLA_ROPE), c, s), (r // tr, nh),
                  [(x, xspec), (gain, _whole(gain.shape)), (cos, tspec), (sin, tspec)],
                  [(_sds((r, HEAD_DIM * nh), BF16), _rows(tr, HEAD_DIM, 'j'), None)])[0]


def rope_bwd(name, x, gain, cos, sin, dy, tr, xcol, nh):
    r = x.shape[0]
    xspec = _rows(tr, HEAD_DIM, lambda j: xcol + j)
    tspec = _rows(tr, HEAD_DIM, 0)
    ospec = _rows(tr, HEAD_DIM, 'j')

    def fn(xb, g, c, s, dyb):
        _, vjp = jax.vjp(lambda a, b: _rms(a, b, MLA_ROPE), xb.astype(F32), g)
        return vjp(_rope_t(dyb.astype(F32), c, s))

    return blockk(name, fn, (r // tr, nh),
                  [(x, xspec), (gain, _whole(gain.shape)), (cos, tspec), (sin, tspec), (dy, ospec)],
                  [(_sds((r, HEAD_DIM * nh), F32), ospec, None), (_sds(gain.shape, F32), _whole(gain.shape), 'all')])


def sgu_pre_fwd(name, pre, gain, tr):
    s_len, w2 = pre.shape
    w = w2 // 2
    return blockk(name, lambda pu, pv, g: (_gelu(pu.astype(F32)), _rms(_gelu(pv.astype(F32)), g, w)), (s_len // tr, 1),
                  [(pre, _rows(tr, w, 0)), (pre, _rows(tr, w, 1)), (gain, _whole(gain.shape))],
                  [(_sds((s_len, w), BF16), _rows(tr, w, 0), None), (_sds((s_len, w), BF16), _rows(tr, w, 0), None)])


def sgu_pre_bwd(name, pre, gain, du, dvn, tr):
    s_len, w2 = pre.shape
    w = w2 // 2

    def fn(pu, pv, g, dub, dvb):
        _, vjp_u = jax.vjp(_gelu, pu.astype(F32))
        _, vjp_v = jax.vjp(lambda a, b: _rms(_gelu(a), b, w), pv.astype(F32), g)
        dpv, dg = vjp_v(dvb.astype(F32))
        return vjp_u(dub.astype(F32))[0], dpv, dg

    spec = _rows(tr, w, 0)
    return blockk(name, fn, (s_len // tr, 1),
                  [(pre, spec), (pre, _rows(tr, w, 1)), (gain, _whole(gain.shape)), (du, spec), (dvn, spec)],
                  [(_sds((s_len, w), BF16), spec, None), (_sds((s_len, w), BF16), spec, None),
                   (_sds(gain.shape, F32), _whole(gain.shape), 'all')])


def _ws_masked(ws):
    t = ws.shape[0]
    a = lax.broadcasted_iota(jnp.int32, (t, t), 0)
    b = lax.broadcasted_iota(jnp.int32, (t, t), 1)
    return jnp.where(b <= a, ws, 0.0)


def sgu_mix_fwd(name, vn, u, ws, bs3):
    s_len, w = vn.shape
    t = ws.shape[1]

    def fn(vb, ub, wsb, bsb):
        mixed = _nn(_ws_masked(wsb).astype(BF16), vb.astype(BF16)) + bsb
        return ub.astype(F32) * mixed

    blk = pl.BlockSpec((t, t), lambda i, j: (i, j))
    return blockk(name, fn, (s_len // t, w // t),
                  [(vn, blk), (u, blk), (ws, pl.BlockSpec((None, t, t), lambda i, j: (j, 0, 0))),
                   (bs3, pl.BlockSpec((None, t, 1), lambda i, j: (j, 0, 0)))],
                  [(_sds((s_len, w), BF16), blk, None)])[0]


def sgu_mix_bwd(name, vn, u, ws, bs3, dprod):
    s_len, w = vn.shape
    t = ws.shape[1]

    def fn(vb, ub, wsb, bsb, dpb):
        wm = _ws_masked(wsb).astype(BF16)
        vb16 = vb.astype(BF16)
        mixed = _nn(wm, vb16) + bsb
        dp = dpb.astype(F32)
        du = dp * mixed
        dm = dp * ub.astype(F32)
        dmb = dm.astype(BF16)
        dvn = _tn(wm, dmb)
        dws = _ws_masked(_nt(dmb, vb16))
        return du, dvn, dws, jnp.sum(dm, axis=-1, keepdims=True)

    blk = pl.BlockSpec((t, t), lambda g, n: (n, g))
    wspec = pl.BlockSpec((None, t, t), lambda g, n: (g, 0, 0))
    bspec = pl.BlockSpec((None, t, 1), lambda g, n: (g, 0, 0))
    return blockk(name, fn, (w // t, s_len // t), [(vn, blk), (u, blk), (ws, wspec), (bs3, bspec), (dprod, blk)],
                  [(_sds((s_len, w), BF16), blk, None), (_sds((s_len, w), BF16), blk, None),
                   (_sds(ws.shape, F32), wspec, 'inner'), (_sds(bs3.shape, F32), bspec, 'inner')])


def _shift_down(x, k):
    row = lax.broadcasted_iota(jnp.int32, x.shape, 0)
    return jnp.where(row >= k, pltpu.roll(x, k, 0), 0.0)


def _shift_up(x, k):
    n = x.shape[0]
    row = lax.broadcasted_iota(jnp.int32, x.shape, 0)
    return jnp.where(row < n - k, pltpu.roll(x, n - k, 0), 0.0)


def _conv(up, cw, cb):
    return cb + cw[0:1, :] * _shift_down(up, 2) + cw[1:2, :] * _shift_down(up, 1) + cw[2:3, :] * up


def _conv_specs(s_len, tc, nf):
    g = pl.BlockSpec((s_len, tc), lambda i, j: (0, j))
    v = pl.BlockSpec((s_len, tc), lambda i, j: (0, j + nf))
    wg = pl.BlockSpec((3, tc), lambda i, j: (0, j))
    wv = pl.BlockSpec((3, tc), lambda i, j: (0, j + nf))
    bg = pl.BlockSpec((1, tc), lambda i, j: (0, j))
    bv = pl.BlockSpec((1, tc), lambda i, j: (0, j + nf))
    return g, v, wg, wv, bg, bv


def conv_fwd(name, up, cw, cb, tc=256):
    s_len, f2 = up.shape
    f = f2 // 2
    nf = f // tc
    g, v, wg, wv, bg, bv = _conv_specs(s_len, tc, nf)

    def fn(ug, uv, cwg, cwv, cbg, cbv):
        yg = _conv(ug.astype(F32), cwg, cbg)
        yv = _conv(uv.astype(F32), cwv, cbv)
        return yg * _sig(yg) * yv

    return blockk(name, fn, (1, nf), [(up, g), (up, v), (cw, wg), (cw, wv), (cb, bg), (cb, bv)],
                  [(_sds((s_len, f), BF16), g, None)])[0]


def conv_bwd(name, up, cw, cb, dact, tc=128):
    s_len, f2 = up.shape
    f = f2 // 2
    nf = f // tc
    g, v, wg, wv, bg, bv = _conv_specs(s_len, tc, nf)

    def half(dy, u, cwh):
        u1, u2 = _shift_down(u, 1), _shift_down(u, 2)
        dup = cwh[2:3, :] * dy + cwh[1:2, :] * _shift_up(dy, 1) + cwh[0:1, :] * _shift_up(dy, 2)
        dcw = jnp.concatenate([jnp.sum(dy * u2, axis=0, keepdims=True), jnp.sum(dy * u1, axis=0, keepdims=True),
                               jnp.sum(dy * u, axis=0, keepdims=True)], axis=0)
        return dup, dcw, jnp.sum(dy, axis=0, keepdims=True)

    def fn(ug, uv, cwg, cwv, cbg, cbv, da):
        ug, uv, da = ug.astype(F32), uv.astype(F32), da.astype(F32)
        yg = _conv(ug, cwg, cbg)
        yv = _conv(uv, cwv, cbv)
        sg = _sig(yg)
        dyv = da * (yg * sg)
        dyg = da * yv * (sg * (1.0 + yg * (1.0 - sg)))
        dug, dcwg, dcbg = half(dyg, ug, cwg)
        duv, dcwv, dcbv = half(dyv, uv, cwv)
        return dug, duv, dcwg, dcwv, dcbg, dcbv

    return blockk(name, fn, (1, nf), [(up, g), (up, v), (cw, wg), (cw, wv), (cb, bg), (cb, bv), (dact, g)],
                  [(_sds((s_len, f), BF16), g, None), (_sds((s_len, f), BF16), g, None),
                   (_sds((3, f), F32), wg, None), (_sds((3, f), F32), wg, None),
                   (_sds((1, f), F32), bg, None), (_sds((1, f), F32), bg, None)])


def loss_head(name, y, target, tr=256):
    s_len, d = y.shape

    def fn(yb, tb):
        e = yb - tb
        return e * (1.0 / d), jnp.sum(e * e, axis=0, keepdims=True)

    spec = _rows(tr, d, 0)
    return blockk(name, fn, (s_len // tr, 1), [(y, spec), (target, spec)],
                  [(_sds((s_len, d), F32), spec, None), (_sds((1, d), F32), _whole((1, d)), 'all')])


def _adam(w, g, m, v):
    m = ADAM_B1 * m + (1.0 - ADAM_B1) * g
    v = ADAM_B2 * v + (1.0 - ADAM_B2) * (g * g)
    m_hat = m / (1.0 - ADAM_B1 ** ADAM_STEP)
    v_hat = v / (1.0 - ADAM_B2 ** ADAM_STEP)
    delta = -ADAM_LR * (m_hat / (jnp.sqrt(v_hat) + ADAM_EPS) + ADAM_WD * w)
    return delta, m, v


def adam_sum(name, parts, w, m, v):
    n_layers, r, c = w.shape
    n_parts = parts[0].shape[0]
    assert len(parts) == n_layers
    tr = r
    for cand in (512, 256, 128, 64, 32, 16):
        if r % cand == 0 and cand * c * 4 <= 1024 * 1024:
            tr = cand
            break

    def body(*refs):
        part_refs = refs[:n_layers]
        w_ref, m_ref, v_ref, g_out, d_out, m_out, v_out = refs[n_layers:]
        layer = pl.program_id(0)
        for ll in range(n_layers):
            @pl.when(layer == ll)
            def _(ll=ll):
                g = part_refs[ll][0].astype(F32)
                for k in range(1, n_parts):
                    g = g + part_refs[ll][k].astype(F32)
                delta, m_new, v_new = _adam(w_ref[...], g, m_ref[...], v_ref[...])
                g_out[...] = g
                d_out[...] = delta
                m_out[...] = m_new
                v_out[...] = v_new

    spec = pl.BlockSpec((None, tr, c), lambda l, i: (l, i, 0))
    part_specs = [pl.BlockSpec((n_parts, tr, c), lambda l, i, ll=ll: (0, jnp.where(l == ll, i, 0), 0))
                  for ll in range(n_layers)]
    out = _sds((n_layers, r, c), F32)
    return pl.pallas_call(body, grid=(n_layers, r // tr), in_specs=part_specs + [spec] * 3, out_specs=[spec] * 4,
                          out_shape=[out] * 4, name=name, compiler_params=_cparams(2))(*parts, w, m, v)


def sum_parts(name, parts, tr=256):
    n_parts, r, c = parts.shape

    def fn(pb):
        g = pb[0]
        for k in range(1, n_parts):
            g = g + pb[k]
        return g

    return blockk(name, fn, (r // tr, 1), [(parts, pl.BlockSpec((n_parts, tr, c), lambda i, j: (0, i, 0)))],
                  [(_sds((r, c), F32), _rows(tr, c, 0), None)])[0]


def adam_flat(name, w, g, m, v, tr=256):
    r, c = w.shape
    spec = _rows(tr, c, 0)
    return blockk(name, lambda wb, gb, mb, vb: _adam(wb, gb, mb, vb), (r // tr, 1),
                  [(w, spec), (g, spec), (m, spec), (v, spec)], [(_sds((r, c), F32), spec, None)] * 3)


_ANY = pl.BlockSpec(memory_space=pl.ANY)


def _place():
    return lax.axis_index("x"), lax.axis_index("y"), lax.axis_index("c")


def _slot(px, py, pc):
    return 4 * px + 2 * py + pc


def all_gather(name, items):
    n = len(items)

    def body(*refs):
        xs, outs = refs[:n], refs[n:2 * n]
        send_sems, recv_sems, local_sems = refs[2 * n:]
        x, y, c = _place()
        me, sibling = (x, y, c), (x, y, 1 - c)
        chips = [(1 - x, y), (x, 1 - y), (1 - x, 1 - y)]

        def copy(t, k, block, to, src=None):
            dst = outs[t].at[_slot(*block)]
            return pltpu.make_async_remote_copy(src_ref=dst if src is None else src, dst_ref=dst,
                                                send_sem=send_sems.at[7 * t + k], recv_sem=recv_sems.at[7 * t + k],
                                                device_id=to, device_id_type=MESH)

        mine = [pltpu.make_async_copy(xs[t], outs[t].at[_slot(*me)], local_sems.at[t]) for t in range(n)]
        for cp in mine:
            cp.start()
        started = []
        for t in range(n):
            started.append(copy(t, 0, me, sibling, src=xs[t]))
            started += [copy(t, 1 + j, me, (*chip, c), src=xs[t]) for j, chip in enumerate(chips)]
        for cp in started:
            cp.start()
        for j, chip in enumerate(chips):
            for t in range(n):
                copy(t, 1 + j, (*chip, c), me).wait_recv()
                passed = copy(t, 4 + j, (*chip, c), sibling)
                passed.start()
                started.append(passed)
        for t in range(n):
            copy(t, 0, sibling, me).wait_recv()
            for j, chip in enumerate(chips):
                copy(t, 4 + j, (*chip, 1 - c), me).wait_recv()
        for cp in started:
            cp.wait_send()
        for cp in mine:
            cp.wait()

    return pl.pallas_call(
        body, in_specs=[_ANY] * n, out_specs=[_ANY] * n,
        out_shape=[_sds((N_DEV,) + a.shape, a.dtype) for a in items],
        scratch_shapes=[pltpu.SemaphoreType.DMA((7 * n,)), pltpu.SemaphoreType.DMA((7 * n,)), pltpu.SemaphoreType.DMA((n,))],
        name=name)(*items)


_HBM = pl.BlockSpec(memory_space=pltpu.HBM)
_SEM = pl.BlockSpec(memory_space=pltpu.SEMAPHORE)
_EFFECT = pltpu.SideEffectType.DATAFLOW_SIDE_EFFECTING


def _peer(k, x, y, c):
    return ((1 - x) if k & 4 else x, (1 - y) if k & 2 else y, (1 - c) if k & 1 else c)


def _exchange_copies(src_refs, land_refs, send_sems, recv_sems, scatter, landing):
    x, y, c = _place()
    my_slot = _slot(x, y, c)
    copies = []
    for t, (src, land) in enumerate(zip(src_refs, land_refs)):
        for k in range(1, N_DEV):
            peer = _peer(k, x, y, c)
            copies.append(pltpu.make_async_remote_copy(
                src_ref=src.at[_slot(*peer)] if scatter else src,
                dst_ref=land.at[_slot(*peer) if landing else my_slot],
                send_sem=send_sems.at[7 * t + k - 1], recv_sem=recv_sems.at[7 * t + k - 1],
                device_id=peer, device_id_type=MESH))
    return copies


def exchange_start(name, srcs, scatter):
    n = len(srcs)
    x, y, c = _place()
    my_slot = _slot(x, y, c)
    lands = []
    for s in srcs:
        own = lax.dynamic_index_in_dim(s, my_slot, 0, keepdims=True) if scatter else s[None]
        shape = s.shape if scatter else (N_DEV,) + s.shape
        lands.append(lax.dynamic_update_slice(lax.empty(shape, s.dtype), own, (my_slot,) + (0,) * (len(shape) - 1)))

    def body(*refs):
        src_refs, land_refs = refs[:n], refs[n:2 * n]
        send_sems, recv_sems = refs[2 * n:2 * n + 2]
        token = refs[-1]
        for send in _exchange_copies(src_refs, land_refs, send_sems, recv_sems, scatter, False):
            send.start()
        token[...] = jnp.zeros_like(token)

    bufs = [pltpu.with_memory_space_constraint(a, pltpu.HBM) for a in list(srcs) + lands]
    res = pl.pallas_call(
        body, name=name, in_specs=[_HBM] * (2 * n),
        out_specs=(_SEM, _SEM) + (_HBM,) * (2 * n) + (pl.BlockSpec(memory_space=pltpu.VMEM),),
        out_shape=(pltpu.SemaphoreType.DMA((7 * n,)), pltpu.SemaphoreType.DMA((7 * n,)))
        + tuple(pltpu.HBM(a.shape, a.dtype) for a in bufs) + (_sds((8, HEAD_DIM), F32),),
        input_output_aliases={i: 2 + i for i in range(2 * n)},
        compiler_params=pltpu.CompilerParams(has_side_effects=_EFFECT))(*bufs)
    return dict(name=name, n=n, scatter=scatter, sems=res[:2], bufs=res[2:2 + 2 * n]), res[-1]


def exchange_wait(handle, after):
    n, scatter = handle['n'], handle['scatter']

    def body(*refs):
        src_refs, land_refs = refs[:n], refs[n:2 * n]
        send_sems, recv_sems = refs[2 * n:2 * n + 2]
        for landed in _exchange_copies(src_refs, land_refs, send_sems, recv_sems, scatter, True):
            landed.wait_send()
            landed.wait_recv()

    bufs = handle['bufs']
    res = pl.pallas_call(
        body, name=handle['name'].replace("start", "wait"), in_specs=[_HBM] * (2 * n) + [_SEM, _SEM, _ANY],
        out_specs=(_HBM,) * (2 * n), out_shape=tuple(pltpu.HBM(a.shape, a.dtype) for a in bufs),
        input_output_aliases={i: i for i in range(2 * n)},
        compiler_params=pltpu.CompilerParams(has_side_effects=_EFFECT))(*bufs, *handle['sems'], after)
    return list(res[n:])


def _pad_lanes(a, width=HEAD_DIM):
    return jnp.pad(a, [(0, 0)] * (a.ndim - 1) + [(0, width - a.shape[-1])])


def _slabs(full, n):
    k = full.shape[0]
    return full.reshape(k, N_DEV, n).transpose(1, 0, 2)


def fox_fwd(a, w, p):
    qkv = mm_nn(a, w['qkv'], BF16, "fox_qkv")
    fl = mm_nn(a, w['f'], F32, "fox_flogit")
    bf = _pad_lanes(p['fox_b_f'])
    cum = fox_gate_fwd("fox_gate_fwd", fl, bf)
    cum_t = cum[:, :N_HEADS].T
    cums = (cum_t[:, :, None], cum_t[:, None, :])
    qn = rms_fwd("fox_qnorm", qkv, p['fox_q_gain'], HEAD_DIM, 512, HEAD_DIM, xcol=0, nh=N_HEADS)
    kn = rms_fwd("fox_knorm", qkv, p['fox_k_gain'], HEAD_DIM, 512, HEAD_DIM, xcol=N_HEADS, nh=N_HEADS)
    o, lse = attn_fwd("fox_attn_fwd", qn, kn, qkv, HEAD_DIM ** -0.5, voff=2 * N_HEADS, cum=cums, exact_o=True)
    return o, dict(a=a, qkv=qkv, fl=fl, bf=bf, cums=cums, qn=qn, kn=kn, o=o, lse=lse)


def fox_bwd(do, s, w, p):
    dqn, dkn, dv, dcs = attn_bwd("fox_attn_bwd", s['qn'], s['kn'], s['qkv'], s['o'], do, s['lse'], HEAD_DIM ** -0.5,
                                 voff=2 * N_HEADS, cum=s['cums'])
    dq, dgq = rms_bwd("fox_qnorm_bwd", s['qkv'], p['fox_q_gain'], dqn, HEAD_DIM, 512, HEAD_DIM, xcol=0, nh=N_HEADS,
                      out_dtype=BF16)
    dk, dgk = rms_bwd("fox_knorm_bwd", s['qkv'], p['fox_k_gain'], dkn, HEAD_DIM, 512, HEAD_DIM, xcol=N_HEADS,
                      nh=N_HEADS, out_dtype=BF16)
    dqkv = jnp.concatenate([dq, dk, dv.astype(BF16)], axis=1)
    dcum = _pad_lanes(-jnp.sum(dcs[:, :, 0, :], axis=1).T)
    dfl, dbf = fox_gate_bwd("fox_gate_bwd", s['fl'], s['bf'], dcum)
    da = mm_nt(dqkv, w['qkv'], F32, "fox_da_qkv")
    da = mm_nt(dfl, w['f'], F32, "fox_da_f", add=da)
    dw_qkv = mm_tn(s['a'], dqkv, 1, BF16, "fox_dw_qkv")[0]
    dw_f = mm_tn(s['a'], dfl, 1, BF16, "fox_dw_f")[0][:, :N_HEADS]
    dw = _slabs(jnp.concatenate([dw_qkv, dw_f], axis=1), 770)
    return da, dict(fox_w_in=dw), dict(fox_b_f=dbf[:, :N_HEADS], fox_q_gain=dgq, fox_k_gain=dgk)


def _rope_tables(positions):
    inv_freq = ROPE_THETA ** (-jnp.arange(0, MLA_ROPE, 2, dtype=F32) / MLA_ROPE)
    ang = positions.astype(F32)[:, None] * inv_freq
    cos, sin = jnp.cos(ang), jnp.sin(ang)
    return _pad_lanes(jnp.concatenate([cos, cos], axis=1)), _pad_lanes(jnp.concatenate([sin, sin], axis=1))


def mla_fwd(a, w, p):
    qg, kg = p['mla_q_gain'], p['mla_k_gain']
    gains = dict(qn=qg[:, :HEAD_DIM], qr=_pad_lanes(qg[:, HEAD_DIM:]), kn=kg[:, :HEAD_DIM], kr=_pad_lanes(kg[:, HEAD_DIM:]))
    cos, sin = _rope_tables(p['positions'])
    ccr = mm_nn(a, w['in'], F32, "mla_in")
    cqn = rms_fwd("mla_cq_norm", ccr, w['q_a_gain'], 512, 256, 512, xcol=0)
    ckvn = rms_fwd("mla_ckv_norm", ccr, w['kv_a_gain'], 512, 256, 512, xcol=1)
    qf = mm_nn(cqn, w['q_b'], F32, "mla_q_b")
    kvf = mm_nn(ckvn, w['kv_b'], F32, "mla_kv_b")
    q_nope = rms_fwd("mla_qnope_norm", qf, gains['qn'], HEAD_DIM, 512, HEAD_DIM, xcol=0, nh=N_HEADS)
    k_nope = rms_fwd("mla_knope_norm", kvf, gains['kn'], HEAD_DIM, 512, HEAD_DIM, xcol=0, nh=N_HEADS)
    q_rope = rope_fwd("mla_qrope", qf, gains['qr'], cos, sin, 512, N_HEADS, N_HEADS)
    k_rope = rope_fwd("mla_krope", ccr, gains['kr'], cos, sin, 512, 8, 1)
    scale = (HEAD_DIM + MLA_ROPE) ** -0.5
    o, lse = attn_fwd("mla_attn_fwd", q_nope, k_nope, kvf, scale, voff=N_HEADS, q2=q_rope, k2=k_rope)
    return o, dict(a=a, gains=gains, cos=cos, sin=sin, ccr=ccr, cqn=cqn, ckvn=ckvn, qf=qf, kvf=kvf, q_nope=q_nope,
                   k_nope=k_nope, q_rope=q_rope, k_rope=k_rope, o=o, lse=lse, scale=scale)


def mla_bwd(do, s, w, p):
    g = s['gains']
    dqn, dkn, dv, dq2, dk2 = attn_bwd("mla_attn_bwd", s['q_nope'], s['k_nope'], s['kvf'], s['o'], do, s['lse'], s['scale'],
                                      voff=N_HEADS, q2=s['q_rope'], k2=s['k_rope'])
    dqf_n, dg_qn = rms_bwd("mla_qnope_bwd", s['qf'], g['qn'], dqn, HEAD_DIM, 512, HEAD_DIM, xcol=0, nh=N_HEADS)
    dkf, dg_kn = rms_bwd("mla_knope_bwd", s['kvf'], g['kn'], dkn, HEAD_DIM, 512, HEAD_DIM, xcol=0, nh=N_HEADS)
    dqf_r, dg_qr = rope_bwd("mla_qrope_bwd", s['qf'], g['qr'], s['cos'], s['sin'], dq2, 512, N_HEADS, N_HEADS)
    dkr, dg_kr = rope_bwd("mla_krope_bwd", s['ccr'], g['kr'], s['cos'], s['sin'], dk2, 512, 8, 1)
    dqf = jnp.concatenate([dqf_n, dqf_r], axis=1)
    dkvf = jnp.concatenate([dkf, dv], axis=1)
    dcqn = mm_nt(dqf, w['q_b'], F32, "mla_dcq")
    dckvn = mm_nt(dkvf, w['kv_b'], F32, "mla_dckv")
    dw_qb = mm_tn(s['cqn'], dqf, 1, BF16, "mla_dw_qb")[0]
    dw_kvb = mm_tn(s['ckvn'], dkvf, 1, BF16, "mla_dw_kvb")[0]
    dcq, dg_qa = rms_bwd("mla_cq_bwd", s['ccr'], w['q_a_gain'], dcqn, 512, 256, 512, xcol=0)
    dckv, dg_kva = rms_bwd("mla_ckv_bwd", s['ccr'], w['kv_a_gain'], dckvn, 512, 256, 512, xcol=1)
    dccr = jnp.concatenate([dcq, dckv, dkr], axis=1)
    da = mm_nt(dccr, w['in'], F32, "mla_da")
    dw_in = mm_tn(s['a'], dccr, 1, BF16, "mla_dw_in")[0][:, :1088].reshape(N_DEV, 256, 1088)
    hp = 2
    nope = dw_qb[:, :2048].reshape(512, N_DEV, hp, HEAD_DIM)
    rope = dw_qb[:, 2048:].reshape(512, N_DEV, hp, HEAD_DIM)[..., :MLA_ROPE]
    dw_qb_s = jnp.concatenate([nope, rope], axis=-1).transpose(1, 0, 2, 3).reshape(N_DEV, 512, hp * 192)
    kk = dw_kvb[:, :2048].reshape(512, N_DEV, hp, HEAD_DIM)
    vv = dw_kvb[:, 2048:].reshape(512, N_DEV, hp, HEAD_DIM)
    dw_kvb_s = jnp.concatenate([kk, vv], axis=-1).transpose(1, 0, 2, 3).reshape(N_DEV, 512, hp * 256)
    small = dict(mla_q_a_gain=dg_qa, mla_kv_a_gain=dg_kva,
                 mla_q_gain=jnp.concatenate([dg_qn, dg_qr[:, :MLA_ROPE]], axis=1),
                 mla_k_gain=jnp.concatenate([dg_kn, dg_kr[:, :MLA_ROPE]], axis=1))
    return da, dict(mla_w_in=dw_in, mla_w_q_b=dw_qb_s, mla_w_kv_b=dw_kvb_s), small


def sb_fwd_layer(a, w, p):
    qkv = mm_nn(a, w['in'], BF16, "sb_qkv")
    qn = rms_fwd("sb_qnorm", qkv, p['sb_q_gain'], HEAD_DIM, 512, HEAD_DIM, xcol=0, nh=N_HEADS)
    kn = rms_fwd("sb_knorm", qkv, p['sb_k_gain'], HEAD_DIM, 512, HEAD_DIM, xcol=N_HEADS, nh=N_HEADS)
    o, carries = sb_fwd("sb_attn_fwd", qn, kn, qkv, HEAD_DIM ** -0.5, 2 * N_HEADS)
    return o, dict(a=a, qkv=qkv, qn=qn, kn=kn, carries=carries)


def sb_bwd_layer(do, s, w, p):
    dqn, dkn, dv = sb_bwd("sb_attn_bwd", s['qn'], s['kn'], s['qkv'], do, s['carries'], HEAD_DIM ** -0.5, 2 * N_HEADS)
    dq, dgq = rms_bwd("sb_qnorm_bwd", s['qkv'], p['sb_q_gain'], dqn, HEAD_DIM, 512, HEAD_DIM, xcol=0, nh=N_HEADS,
                      out_dtype=BF16)
    dk, dgk = rms_bwd("sb_knorm_bwd", s['qkv'], p['sb_k_gain'], dkn, HEAD_DIM, 512, HEAD_DIM, xcol=N_HEADS, nh=N_HEADS,
                      out_dtype=BF16)
    dqkv = jnp.concatenate([dq, dk, dv.astype(BF16)], axis=1)
    da = mm_nt(dqkv, w['in'], F32, "sb_da")
    dw = mm_tn(s['a'], dqkv, N_DEV, BF16, "sb_dw_in")
    return da, dict(sb_w_in=dw), dict(sb_q_gain=dgq, sb_k_gain=dgk)


def sgu_fwd(a, w, p):
    pre = mm_nn(a, w['in'], BF16, "sgu_in")
    u, vn = sgu_pre_fwd("sgu_pre_fwd", pre, w['v_gain'], 256)
    ws = p['sgu_w_s'][0]
    bs3 = p['sgu_b_s'][0][:, :, None]
    prod = sgu_mix_fwd("sgu_mix_fwd", vn, u, ws, bs3)
    return prod, dict(a=a, pre=pre, u=u, vn=vn, ws=ws, bs3=bs3)


def sgu_bwd(dprod, s, w, p):
    du, dvn, dws, dbs3 = sgu_mix_bwd("sgu_mix_bwd", s['vn'], s['u'], s['ws'], s['bs3'], dprod)
    dpu, dpv, dgv = sgu_pre_bwd("sgu_pre_bwd", s['pre'], w['v_gain'], du, dvn, 128)
    dpre = jnp.concatenate([dpu, dpv], axis=1)
    da = mm_nt(dpre, w['in'], F32, "sgu_da")
    dw = mm_tn(s['a'], dpre, N_DEV, BF16, "sgu_dw_in")
    return da, dict(sgu_w_in=dw), dict(sgu_v_gain=dgv, sgu_w_s=dws[None], sgu_b_s=dbs3[None, :, :, 0])


MIXERS = [("fox", fox_fwd, fox_bwd), ("mla", mla_fwd, mla_bwd), ("sb", sb_fwd_layer, sb_bwd_layer),
          ("sgu", sgu_fwd, sgu_bwd)]


def _pack_rows(arrays, row_mult=256):
    flat = jnp.concatenate([a.reshape(-1).astype(F32) for a in arrays])
    per = row_mult * HEAD_DIM
    total = -(-flat.shape[0] // per) * per
    return jnp.pad(flat, (0, total - flat.shape[0])).reshape(total // HEAD_DIM, HEAD_DIM)


def _unpack_rows(packed, shapes):
    flat = packed.reshape(-1)
    out, off = [], 0
    for shp in shapes:
        n = math.prod(shp)
        out.append(flat[off:off + n].reshape(shp))
        off += n
    return out


def _mixer_shards(i, p):
    name = MIXERS[i][0]
    bf = lambda a: a.astype(BF16)
    items = {'out': bf(p[name + '_w_out'][0])}
    if name == "fox":
        items['in'] = bf(p['fox_w_in'][0])
        items['small'] = _pack_rows([p[n] for n in SMALL_SHARDED], 8)
    elif name == "mla":
        items.update({'in': bf(p['mla_w_in'][0]), 'q_b': bf(p['mla_w_q_b'][0]), 'kv_b': bf(p['mla_w_kv_b'][0])})
    else:
        items['in'] = bf(p[name + '_w_in'][0])
    return items


def _ffn_shards(i, p):
    return {'up': p['ffn_w_up'][i].astype(BF16), 'down': p['ffn_w_down'][i].astype(BF16)}


def _assemble_ffn(got):
    return {'up': got['up'], 'down': got['down'].reshape(1, D_FF, -1)}


def _assemble_mixer(i, got, p):
    name = MIXERS[i][0]
    w = {'out': got['out'].reshape(1, -1, got['out'].shape[-1])}
    small = None
    if name == "fox":
        small_shapes = [p[n].shape for n in SMALL_SHARDED]
        full = got['in'].transpose(1, 0, 2).reshape(got['in'].shape[1], -1)
        w['qkv'] = full[None, :, :3 * N_HEADS * HEAD_DIM]
        w['f'] = _pad_lanes(full[:, 3 * N_HEADS * HEAD_DIM:])[None]
        parts = [_unpack_rows(got['small'][d], small_shapes) for d in range(N_DEV)]
        small = {n: jnp.concatenate([parts[d][k] for d in range(N_DEV)], axis=ax)
                 for k, (n, ax) in enumerate(SMALL_SHARDED.items())}
    elif name == "mla":
        w['in'] = _pad_lanes(got['in'].reshape(-1, 1088), 1152)[None]
        hp = 2
        qb = got['q_b'].reshape(N_DEV, 512, hp, 192).transpose(1, 0, 2, 3)
        nope = qb[..., :HEAD_DIM].reshape(512, -1)
        rope = _pad_lanes(qb[..., HEAD_DIM:]).reshape(512, -1)
        w['q_b'] = jnp.concatenate([nope, rope], axis=1)[None]
        kvb = got['kv_b'].reshape(N_DEV, 512, hp, 256).transpose(1, 0, 2, 3)
        w['kv_b'] = jnp.concatenate([kvb[..., :HEAD_DIM].reshape(512, -1), kvb[..., HEAD_DIM:].reshape(512, -1)], axis=1)[None]
    else:
        w['in'] = got['in']
    return w, small


def _train_step(p):
    x, target = p['x'][0], p['loss_target'][0]
    p = dict(p, positions=p['positions'][0])
    xi, yi, ci = _place()
    my_slot = _slot(xi, yi, ci)

    shards0 = _mixer_shards(0, p)
    got0 = dict(zip(shards0, all_gather("gather_first", list(shards0.values()))))
    pending, order_token = {}, jnp.zeros((1, 1), F32)
    for i in range(DEPTH):
        for kind, shards in (("mix", _mixer_shards(i, p) if i else None), ("ffn", _ffn_shards(i, p))):
            if shards is not None:
                handle, token = exchange_start(f"xstart_ag_{kind}{i}", list(shards.values()), False)
                pending[kind, i] = (handle, list(shards))
                order_token = order_token + token[0:1, 0:1]

    def gathered(kind, i, after):
        handle, keys = pending[kind, i]
        return dict(zip(keys, exchange_wait(handle, after)))

    h = x
    saved, weights = [], []
    small_full = None
    for i in range(DEPTH):
        if i == 0:
            w, small_full = _assemble_mixer(0, got0, p)
            gain = p['mix_norm'][0:1] + order_token
        else:
            w, _ = _assemble_mixer(i, gathered("mix", i, h), p)
            gain = p['mix_norm'][i:i + 1]
        w.update(q_a_gain=small_full['mla_q_a_gain'], kv_a_gain=small_full['mla_kv_a_gain'],
                 v_gain=small_full['sgu_v_gain'])
        conv_w = small_full['ffn_conv_w']
        a = rms_fwd(f"mix_norm_{i}", h, gain, h.shape[1], 256, h.shape[1])
        mixed, s_mix = MIXERS[i][1](a, w, p)
        w.update(_assemble_ffn(gathered("ffn", i, mixed)))
        weights.append(w)
        h1 = mm_nn(mixed, w['out'], F32, f"mix_out_{i}", add=h)
        b = rms_fwd(f"ffn_norm_{i}", h1, p['ffn_norm'][i:i + 1], h.shape[1], 256, h.shape[1])
        up = mm_nn(b, w['up'], BF16, f"ffn_up_{i}")
        act = conv_fwd(f"ffn_conv_{i}", up, conv_w[i], p['ffn_conv_b'][i:i + 1])
        h2 = mm_nn(act, w['down'], F32, f"ffn_down_{i}", add=h1)
        saved.append(dict(h=h, mixed=mixed, s_mix=s_mix, h1=h1, b=b, up=up, act=act))
        h = h2

    dh, sq = loss_head("loss_head", h, target)
    loss = lax.psum(0.5 * jnp.sum(sq) / h.shape[1], ("x", "y", "c"))

    scatters = []
    small_g = {n: [None] * p[n].shape[0] for n in ('mix_norm', 'ffn_norm', 'ffn_conv_w', 'ffn_conv_b')}
    for i in reversed(range(DEPTH)):
        w, s = weights[i], saved[i]
        name = MIXERS[i][0]
        dact = mm_nt(dh, w['down'], BF16, f"ffn_dact_{i}")
        dw_down = mm_tn(s['act'], dh, 1, BF16, f"ffn_dw_down_{i}").reshape(N_DEV, D_FF // N_DEV, -1)
        dug, duv, dcwg, dcwv, dcbg, dcbv = conv_bwd(f"ffn_conv_bwd_{i}", s['up'], conv_w[i], p['ffn_conv_b'][i:i + 1], dact)
        dup = jnp.concatenate([dug, duv], axis=1)
        dw_up = mm_tn(s['b'], dup, N_DEV, BF16, f"ffn_dw_up_{i}")
        handle, token = exchange_start(f"xstart_rs_ffn{i}", [dw_up, dw_down], True)
        scatters.append((handle, ['ffn_w_up', 'ffn_w_down']))
        db = mm_nt(dup, w['up'], F32, f"ffn_db_{i}")
        dh1, dg_ffn = rms_bwd(f"ffn_norm_bwd_{i}", s['h1'], p['ffn_norm'][i:i + 1] + token[0:1, 0:1], db, h.shape[1], 128,
                              h.shape[1], resid=dh)
        dmix = mm_nt(dh1, w['out'], BF16, f"mix_dout_{i}")
        dw_out = mm_tn(s['mixed'], dh1, 1, BF16, f"mix_dw_out_{i}").reshape(N_DEV, -1, h.shape[1])
        da, big_i, small_i = MIXERS[i][2](dmix, s['s_mix'], w, p)
        big_i[name + '_w_out'] = dw_out
        handle, token = exchange_start(f"xstart_rs_mix{i}", list(big_i.values()), True)
        scatters.append((handle, list(big_i)))
        dh, dg_mix = rms_bwd(f"mix_norm_bwd_{i}", s['h'], p['mix_norm'][i:i + 1] + token[0:1, 0:1], da, h.shape[1], 128,
                             h.shape[1], resid=dh1)
        for k, v in small_i.items():
            small_g[k] = v
        small_g['mix_norm'][i], small_g['ffn_norm'][i] = dg_mix, dg_ffn
        small_g['ffn_conv_w'][i] = jnp.concatenate([dcwg, dcwv], axis=1)[None]
        small_g['ffn_conv_b'][i] = jnp.concatenate([dcbg, dcbv], axis=1)
    for n in ('mix_norm', 'ffn_norm', 'ffn_conv_w', 'ffn_conv_b'):
        small_g[n] = jnp.concatenate(small_g[n], axis=0)
    grad_x = dh[None]

    big = {}
    for handle, names in scatters:
        for n, landed in zip(names, exchange_wait(handle, dh)):
            big.setdefault(n, []).insert(0, landed)
    grads, deltas, new_m, new_v = {}, {}, {}, {}
    for n in BIG:
        grads[n], deltas[n], new_m[n], new_v[n] = adam_sum("adam_" + n, big[n], p[n], p['m_' + n], p['v_' + n])

    full_shapes = [tuple(small_g[n].shape) for n in SMALL]
    partials = all_gather("gather_small_grads", [_pack_rows([small_g[n] for n in SMALL])])[0]
    summed = _unpack_rows(sum_parts("sum_small_grads", partials), full_shapes)
    mine = []
    for n, g in zip(SMALL, summed):
        if n in SMALL_SHARDED:
            ax = SMALL_SHARDED[n]
            g = lax.dynamic_slice_in_dim(g, my_slot * p[n].shape[ax], p[n].shape[ax], axis=ax)
        mine.append(g)
    shapes = [p[n].shape for n in SMALL]
    packed = [_pack_rows(arrs) for arrs in ([p[n] for n in SMALL], mine, [p['m_' + n] for n in SMALL], [p['v_' + n] for n in SMALL])]
    d_s, m_s, v_s = adam_flat("adam_small", *packed)
    for n, g, d, m, v in zip(SMALL, mine, _unpack_rows(d_s, shapes), _unpack_rows(m_s, shapes), _unpack_rows(v_s, shapes)):
        grads[n], deltas[n], new_m[n], new_v[n] = g, d, m, v

    return (loss, grad_x, *[grads[n] for n in WEIGHTS], *[deltas[n] for n in WEIGHTS], *[new_m[n] for n in WEIGHTS],
            *[new_v[n] for n in WEIGHTS])


def kernel(x, positions, mix_norm, ffn_norm, fox_w_in, fox_b_f, fox_q_gain, fox_k_gain, fox_w_out, mla_w_in, mla_q_a_gain, mla_kv_a_gain, mla_w_q_b, mla_w_kv_b, mla_q_gain, mla_k_gain, mla_w_out, sb_w_in, sb_q_gain, sb_k_gain, sb_w_out, sgu_w_in, sgu_v_gain, sgu_w_s, sgu_b_s, sgu_w_out, ffn_w_up, ffn_conv_w, ffn_conv_b, ffn_w_down, loss_target, m_mix_norm, m_ffn_norm, m_fox_w_in, m_fox_b_f, m_fox_q_gain, m_fox_k_gain, m_fox_w_out, m_mla_w_in, m_mla_q_a_gain, m_mla_kv_a_gain, m_mla_w_q_b, m_mla_w_kv_b, m_mla_q_gain, m_mla_k_gain, m_mla_w_out, m_sb_w_in, m_sb_q_gain, m_sb_k_gain, m_sb_w_out, m_sgu_w_in, m_sgu_v_gain, m_sgu_w_s, m_sgu_b_s, m_sgu_w_out, m_ffn_w_up, m_ffn_conv_w, m_ffn_conv_b, m_ffn_w_down, v_mix_norm, v_ffn_norm, v_fox_w_in, v_fox_b_f, v_fox_q_gain, v_fox_k_gain, v_fox_w_out, v_mla_w_in, v_mla_q_a_gain, v_mla_kv_a_gain, v_mla_w_q_b, v_mla_w_kv_b, v_mla_q_gain, v_mla_k_gain, v_mla_w_out, v_sb_w_in, v_sb_q_gain, v_sb_k_gain, v_sb_w_out, v_sgu_w_in, v_sgu_v_gain, v_sgu_w_s, v_sgu_b_s, v_sgu_w_out, v_ffn_w_up, v_ffn_conv_w, v_ffn_conv_b, v_ffn_w_down):
    args = locals()
    names = ['x', 'positions'] + WEIGHTS + ['loss_target'] + ['m_' + n for n in WEIGHTS] + ['v_' + n for n in WEIGHTS]
    return _train_step({n: args[n] for n in names})
```

```python
import functools
import math

import jax
import jax.numpy as jnp
from jax import lax
from jax.experimental import pallas as pl
from jax.experimental.pallas import tpu as pltpu

F32 = jnp.float32
BF16 = jnp.bfloat16
MESH = pl.DeviceIdType.MESH

N_DEV = 8
N_HEADS = 16
HEAD_DIM = 128
EPS = 1e-6
DEPTH = 4
D_FF = 5632
MLA_ROPE = 64
ROPE_THETA = 10000.0
VMEM_LIMIT = 48 * 1024 * 1024

ADAM_LR, ADAM_B1, ADAM_B2, ADAM_EPS, ADAM_WD, ADAM_STEP = 0.001, 0.9, 0.999, 1e-08, 0.01, 10

WEIGHTS = ['mix_norm', 'ffn_norm', 'fox_w_in', 'fox_b_f', 'fox_q_gain', 'fox_k_gain', 'fox_w_out', 'mla_w_in',
           'mla_q_a_gain', 'mla_kv_a_gain', 'mla_w_q_b', 'mla_w_kv_b', 'mla_q_gain', 'mla_k_gain', 'mla_w_out',
           'sb_w_in', 'sb_q_gain', 'sb_k_gain', 'sb_w_out', 'sgu_w_in', 'sgu_v_gain', 'sgu_w_s', 'sgu_b_s',
           'sgu_w_out', 'ffn_w_up', 'ffn_conv_w', 'ffn_conv_b', 'ffn_w_down']
BIG = ['fox_w_in', 'fox_w_out', 'mla_w_in', 'mla_w_q_b', 'mla_w_kv_b', 'mla_w_out', 'sb_w_in', 'sb_w_out',
       'sgu_w_in', 'sgu_w_out', 'ffn_w_up', 'ffn_w_down']
SMALL = [w for w in WEIGHTS if w not in BIG]
SMALL_SHARDED = {'mla_q_a_gain': 1, 'mla_kv_a_gain': 1, 'sgu_v_gain': 1, 'ffn_conv_w': 2}


def _cparams(n_grid):
    return pltpu.CompilerParams(dimension_semantics=("arbitrary",) * n_grid, vmem_limit_bytes=VMEM_LIMIT)


def _pick(n, cap):
    best = None
    t = 128
    while t <= min(n, cap):
        if n % t == 0:
            best = t
        t += 128
    return best if best is not None else n


def _mm_call(name, a, b, out_shape, a_spec, b_spec, o_spec, grid, dims, acc_shape, add=None):
    nk = grid[2]

    def body(*refs):
        a_ref, b_ref = refs[:2]
        add_ref = refs[2] if add is not None else None
        o_ref = refs[3] if add is not None else refs[2]
        prod = lax.dot_general(a_ref[...].astype(BF16), b_ref[...].astype(BF16), (dims, ((), ())),
                               preferred_element_type=F32)

        def finish(r):
            if add_ref is not None:
                r = r + add_ref[...].astype(F32)
            o_ref[...] = r.astype(o_ref.dtype)

        if nk == 1:
            finish(prod)
            return
        acc = refs[-1]
        k = pl.program_id(2)

        @pl.when(k == 0)
        def _():
            acc[...] = prod

        @pl.when(k > 0)
        def _():
            acc[...] += prod

        @pl.when(k == nk - 1)
        def _():
            finish(acc[...])

    ins = [a, b] + ([] if add is None else [add])
    in_specs = [a_spec, b_spec] + ([] if add is None else [o_spec])
    return pl.pallas_call(body, grid=grid, in_specs=in_specs, out_specs=o_spec, out_shape=out_shape,
                          scratch_shapes=[] if nk == 1 else [pltpu.VMEM(acc_shape, F32)], name=name,
                          compiler_params=_cparams(3))(*ins)


def mm_nn(a, b3, out_dtype, name, add=None, joff=0, nj=None):
    m, kk = a.shape
    _, kb, n = b3.shape
    assert kb == kk
    nj = b3.shape[0] - joff if nj is None else nj
    tn, tk = _pick(n, 1536), _pick(kk, 2048)
    tm = _pick(m, 512 if a.dtype == F32 and tk > 1024 else 1024)
    nb = n // tn
    return _mm_call(
        name, a, b3, jax.ShapeDtypeStruct((m, nj * n), out_dtype),
        pl.BlockSpec((tm, tk), lambda i, c, k: (i, k)),
        pl.BlockSpec((None, tk, tn), lambda i, c, k: (joff + c // nb, k, c % nb)),
        pl.BlockSpec((tm, tn), lambda i, c, k: (i, c)),
        (m // tm, nj * nb, kk // tk), ((1,), (0,)), (tm, tn), add=add)


def mm_nt(a, b3, out_dtype, name, add=None, joff=0, nj=None):
    m, na = a.shape
    _, ko, n = b3.shape
    nj = b3.shape[0] - joff if nj is None else nj
    assert na == nj * n
    to, tn = _pick(ko, 1024), _pick(n, 2048)
    tm = _pick(m, 512 if a.dtype == F32 and tn > 1024 else 1024)
    nb = n // tn
    return _mm_call(
        name, a, b3, jax.ShapeDtypeStruct((m, ko), out_dtype),
        pl.BlockSpec((tm, tn), lambda i, o, c: (i, c)),
        pl.BlockSpec((None, to, tn), lambda i, o, c: (joff + c // nb, o, c % nb)),
        pl.BlockSpec((tm, to), lambda i, o, c: (i, o)),
        (m // tm, ko // to, nj * nb), ((1,), (1,)), (tm, to), add=add)


def mm_tn(a, b, nj, out_dtype, name):
    s, ko = a.shape
    sb, nb_tot = b.shape
    assert sb == s and nb_tot % nj == 0
    n = nb_tot // nj
    to, tn = _pick(ko, 1024), _pick(n, 1536)
    ts = _pick(s, 1024 if F32 in (a.dtype, b.dtype) else 2048)
    nb = n // tn
    return _mm_call(
        name, a, b, jax.ShapeDtypeStruct((nj, ko, n), out_dtype),
        pl.BlockSpec((ts, to), lambda o, c, k: (k, o)),
        pl.BlockSpec((ts, tn), lambda o, c, k: (k, c)),
        pl.BlockSpec((None, to, tn), lambda o, c, k: (c // nb, o, c % nb)),
        (ko // to, nj * nb, s // ts), ((0,), (0,)), (to, tn))


def blockk(name, fn, grid, ins, outs):
    n_in = len(ins)
    accs = [o[2] for o in outs]

    def body(*refs):
        vals = fn(*[r[...] for r in refs[:n_in]])
        if not isinstance(vals, (tuple, list)):
            vals = (vals,)
        i, j = pl.program_id(0), pl.program_id(1)
        for r, v, acc in zip(refs[n_in:], vals, accs):
            if acc is None:
                r[...] = v.astype(r.dtype)
            else:
                first = (j == 0) if acc == 'inner' else jnp.logical_and(i == 0, j == 0)

                @pl.when(first)
                def _(r=r, v=v):
                    r[...] = v.astype(r.dtype)

                @pl.when(jnp.logical_not(first))
                def _(r=r, v=v):
                    r[...] += v.astype(r.dtype)

    res = pl.pallas_call(body, grid=grid, in_specs=[s for _, s in ins], out_specs=[o[1] for o in outs],
                         out_shape=[o[0] for o in outs], name=name, compiler_params=_cparams(2))(*[a for a, _ in ins])
    return res


def _sds(shape, dtype):
    return jax.ShapeDtypeStruct(tuple(shape), dtype)


def _rows(tr, w, col=0):
    if col == 'j':
        return pl.BlockSpec((tr, w), lambda i, j: (i, j))
    if callable(col):
        return pl.BlockSpec((tr, w), lambda i, j: (i, col(j)))
    return pl.BlockSpec((tr, w), lambda i, j: (i, col))


def _whole(shape):
    nd = len(shape)
    return pl.BlockSpec(tuple(shape), lambda i, j: (0,) * nd)


def _rms(x, g, n):
    ms = jnp.sum(x * x, axis=-1, keepdims=True) * (1.0 / n)
    return x * lax.rsqrt(ms + EPS) * g


def _sig(x):
    return 1.0 / (1.0 + jnp.exp(-x))


def _gelu(x):
    return 0.5 * x * (1.0 + jnp.tanh(math.sqrt(2.0 / math.pi) * (x + 0.044715 * (x * x * x))))


def _lane_iota(shape):
    return lax.broadcasted_iota(jnp.int32, shape, len(shape) - 1)


def _rope(x, cos, sin):
    lane = _lane_iota(x.shape)
    half = MLA_ROPE // 2
    swapped = jnp.where(lane < half, pltpu.roll(x, HEAD_DIM - half, 1), pltpu.roll(x, half, 1))
    sign = jnp.where(lane < half, -1.0, 1.0)
    return x * cos + swapped * (sin * sign)


def _rope_t(dy, cos, sin):
    lane = _lane_iota(dy.shape)
    half = MLA_ROPE // 2
    t = dy * sin
    swapped = jnp.where(lane < half, pltpu.roll(t, HEAD_DIM - half, 1), pltpu.roll(t, half, 1))
    sign = jnp.where(lane < half, 1.0, -1.0)
    return dy * cos + swapped * sign


def rms_fwd(name, x, gain, n, tr, width, xcol=0, nh=1, out_dtype=BF16, out_cols=None):
    r = x.shape[0]
    out_cols = width * nh if out_cols is None else out_cols
    xspec = _rows(tr, width, (lambda j: xcol + j) if nh > 1 else xcol)
    ospec = _rows(tr, width, 'j' if nh > 1 else 0)
    return blockk(name, lambda xb, g: _rms(xb.astype(F32), g, n), (r // tr, nh),
                  [(x, xspec), (gain, _whole(gain.shape))], [(_sds((r, out_cols), out_dtype), ospec, None)])[0]


def rms_bwd(name, x, gain, dy, n, tr, width, xcol=0, nh=1, dycol=0, resid=None, out_dtype=F32, also_bf16=False):
    r = x.shape[0]
    xspec = _rows(tr, width, (lambda j: xcol + j) if nh > 1 else xcol)
    dyspec = _rows(tr, width, (lambda j: dycol + j) if nh > 1 else dycol)
    ospec = _rows(tr, width, 'j' if nh > 1 else 0)

    def fn(xb, g, dyb, *rest):
        _, vjp = jax.vjp(lambda a, b: _rms(a, b, n), xb.astype(F32), g)
        dx, dg = vjp(dyb.astype(F32))
        if rest:
            dx = dx + rest[0].astype(F32)
        return ((dx,) if also_bf16 else ()) + (dx, dg)

    ins = [(x, xspec), (gain, _whole(gain.shape)), (dy, dyspec)]
    if resid is not None:
        ins.append((resid, ospec))
    outs = [(_sds((r, width * nh), out_dtype), ospec, None), (_sds(gain.shape, F32), _whole(gain.shape), 'all')]
    if also_bf16:
        outs.insert(0, (_sds((r, width * nh), BF16), ospec, None))
    return blockk(name, fn, (r // tr, nh), ins, outs)


def _nt(a, b):
    return lax.dot_general(a, b, (((1,), (1,)), ((), ())), preferred_element_type=F32)


def _tn(a, b):
    return lax.dot_general(a, b, (((0,), (0,)), ((), ())), preferred_element_type=F32)


def _nn(a, b):
    return lax.dot_general(a, b, (((1,), (0,)), ((), ())), preferred_element_type=F32)


def _attn_specs(tq, qoff, koff, voff, extra, bias, q2off):
    d = HEAD_DIM
    specs = [pl.BlockSpec((tq, d), lambda h, i, j: (i, qoff + h)),
             pl.BlockSpec((tq, d), lambda h, i, j: (jnp.minimum(i, j), koff + h)),
             pl.BlockSpec((tq, d), lambda h, i, j: (jnp.minimum(i, j), voff + h))]
    if extra:
        specs += [pl.BlockSpec((tq, d), lambda h, i, j: (i, q2off + h)),
                  pl.BlockSpec((tq, d), lambda h, i, j: (jnp.minimum(i, j), 0))]
    if bias:
        specs += [pl.BlockSpec((None, tq, 1), lambda h, i, j: (h, i, 0)),
                  pl.BlockSpec((None, 1, tq), lambda h, i, j: (h, 0, jnp.minimum(i, j)))]
    return specs


ATTN_BLOCK = 512


def _scores(q_ref, k_ref, q2_ref, k2_ref, cc_ref, cr_ref, scale, tq, diagonal):
    s = _nt(q_ref[...].astype(BF16), k_ref[...].astype(BF16))
    if q2_ref is not None:
        s = s + _nt(q2_ref[...].astype(BF16), k2_ref[...].astype(BF16))
    s = s * scale
    if cc_ref is not None:
        s = s + (cc_ref[...] - cr_ref[...])
    if not diagonal:
        return s, None
    return s, lax.broadcasted_iota(jnp.int32, (tq, tq), 1) <= lax.broadcasted_iota(jnp.int32, (tq, tq), 0)


def _on_blocks(qi, kj, step):
    @pl.when(kj < qi)
    def _():
        step(False)

    @pl.when(kj == qi)
    def _():
        step(True)


def attn_fwd(name, q, k, v, scale, *, qoff=0, koff=0, voff=0, q2=None, k2=None, q2off=0, cum=None, tq=ATTN_BLOCK,
             exact_o=False):
    s_len = q.shape[0]
    nq = s_len // tq
    extra, bias = q2 is not None, cum is not None
    n_in = 3 + 2 * extra + 2 * bias

    def body(*refs):
        q_ref, k_ref, v_ref = refs[:3]
        p = 3
        q2_ref = k2_ref = cc_ref = cr_ref = None
        if extra:
            q2_ref, k2_ref = refs[p:p + 2]
            p += 2
        if bias:
            cc_ref, cr_ref = refs[p:p + 2]
            p += 2
        o_ref, lse_ref, m_s, l_s, acc_s = refs[p:]
        qi, kj = pl.program_id(1), pl.program_id(2)

        @pl.when(kj == 0)
        def _():
            m_s[...] = jnp.full_like(m_s, -jnp.inf)
            l_s[...] = jnp.zeros_like(l_s)
            acc_s[...] = jnp.zeros_like(acc_s)

        def step(diagonal):
            s, allowed = _scores(q_ref, k_ref, q2_ref, k2_ref, cc_ref, cr_ref, scale, tq, diagonal)
            if diagonal:
                s = jnp.where(allowed, s, -jnp.inf)
            m_old = m_s[...]
            m_new = jnp.maximum(m_old, jnp.max(s, axis=-1, keepdims=True))
            alpha = jnp.exp(m_old - m_new)
            pr = jnp.exp(s - m_new)
            l_s[...] = alpha * l_s[...] + jnp.sum(pr, axis=-1, keepdims=True)
            vb = v_ref[...].astype(BF16)
            pv = _nn(pr.astype(BF16), vb)
            if exact_o:
                pv = pv + _nn((pr - pr.astype(BF16).astype(F32)).astype(BF16), vb)
            acc_s[...] = alpha * acc_s[...] + pv
            m_s[...] = m_new

        _on_blocks(qi, kj, step)

        @pl.when(kj == qi)
        def _():
            o_ref[...] = (acc_s[...] / l_s[...]).astype(o_ref.dtype)
            lse_ref[...] = m_s[...] + jnp.log(l_s[...])

    ins = [q, k, v] + ([q2, k2] if extra else []) + (list(cum) if bias else [])
    d = HEAD_DIM
    return pl.pallas_call(
        body, grid=(N_HEADS, nq, nq), in_specs=_attn_specs(tq, qoff, koff, voff, extra, bias, q2off),
        out_specs=[pl.BlockSpec((tq, d), lambda h, i, j: (i, h)), pl.BlockSpec((None, tq, 1), lambda h, i, j: (h, i, 0))],
        out_shape=[_sds((s_len, N_HEADS * d), F32 if exact_o else BF16), _sds((N_HEADS, s_len, 1), F32)],
        scratch_shapes=[pltpu.VMEM((tq, 1), F32), pltpu.VMEM((tq, 1), F32), pltpu.VMEM((tq, d), F32)],
        name=name, compiler_params=_cparams(3))(*ins)


def attn_bwd(name, q, k, v, o, do, lse, scale, *, qoff=0, koff=0, voff=0, q2=None, k2=None, q2off=0, cum=None,
             tq=ATTN_BLOCK):
    s_len = q.shape[0]
    nq = s_len // tq
    extra, bias = q2 is not None, cum is not None
    d = HEAD_DIM
    n_in = 6 + 2 * extra + 2 * bias

    def body(*refs):
        q_ref, k_ref, v_ref = refs[:3]
        p = 3
        q2_ref = k2_ref = cc_ref = cr_ref = None
        if extra:
            q2_ref, k2_ref = refs[p:p + 2]
            p += 2
        if bias:
            cc_ref, cr_ref = refs[p:p + 2]
            p += 2
        o_ref, do_ref, lse_ref = refs[p:p + 3]
        p += 3
        dq_ref, dk_ref, dv_ref = refs[p:p + 3]
        p += 3
        dq2_ref = dk2_ref = dcs_ref = None
        if extra:
            dq2_ref, dk2_ref = refs[p:p + 2]
            p += 2
        if bias:
            dcs_ref = refs[p]
            p += 1
        dq_s, delta_s = refs[p:p + 2]
        dq2_s = refs[p + 2] if extra else None
        h, qi, kj = pl.program_id(0), pl.program_id(1), pl.program_id(2)

        @pl.when(jnp.logical_and(qi == 0, kj == 0))
        def _():
            dk_ref[...] = jnp.zeros_like(dk_ref)
            dv_ref[...] = jnp.zeros_like(dv_ref)

        if extra:
            @pl.when(jnp.logical_and(h == 0, jnp.logical_and(qi == 0, kj == 0)))
            def _():
                dk2_ref[...] = jnp.zeros_like(dk2_ref)

        @pl.when(kj == 0)
        def _():
            dq_s[...] = jnp.zeros_like(dq_s)
            if extra:
                dq2_s[...] = jnp.zeros_like(dq2_s)
            delta_s[...] = jnp.sum(do_ref[...].astype(F32) * o_ref[...].astype(F32), axis=-1, keepdims=True)

        if bias:
            @pl.when(kj > qi)
            def _():
                dcs_ref[...] = jnp.zeros_like(dcs_ref)

        def step(diagonal):
            s, allowed = _scores(q_ref, k_ref, q2_ref, k2_ref, cc_ref, cr_ref, scale, tq, diagonal)
            pr = jnp.exp(s - lse_ref[...])
            if diagonal:
                pr = jnp.where(allowed, pr, 0.0)
            dob = do_ref[...].astype(BF16)
            dp = _nt(dob, v_ref[...].astype(BF16))
            ds = pr * (dp - delta_s[...])
            dsb = (ds * scale).astype(BF16)
            ks = pl.ds(pl.multiple_of(kj * tq, tq), tq)
            dq_s[...] += _nn(dsb, k_ref[...].astype(BF16))
            dk_ref[ks, :] += _tn(dsb, q_ref[...].astype(BF16))
            dv_ref[ks, :] += _tn(pr.astype(BF16), dob)
            if extra:
                dq2_s[...] += _nn(dsb, k2_ref[...].astype(BF16))
                dk2_ref[ks, :] += _tn(dsb, q2_ref[...].astype(BF16))
            if bias:
                dcs_ref[...] = jnp.sum(ds, axis=0, keepdims=True)

        _on_blocks(qi, kj, step)

        @pl.when(kj == qi)
        def _():
            dq_ref[...] = dq_s[...]
            if extra:
                dq2_ref[...] = dq2_s[...]

    ins = [q, k, v] + ([q2, k2] if extra else []) + (list(cum) if bias else []) + [o, do, lse]
    in_specs = _attn_specs(tq, qoff, koff, voff, extra, bias, q2off) + [
        pl.BlockSpec((tq, d), lambda h, i, j: (i, h)), pl.BlockSpec((tq, d), lambda h, i, j: (i, h)),
        pl.BlockSpec((None, tq, 1), lambda h, i, j: (h, i, 0))]
    full = _sds((s_len, N_HEADS * d), F32)
    out_shape = [full, full, full]
    out_specs = [pl.BlockSpec((tq, d), lambda h, i, j: (i, h)), pl.BlockSpec((s_len, d), lambda h, i, j: (0, h)),
                 pl.BlockSpec((s_len, d), lambda h, i, j: (0, h))]
    scratch = [pltpu.VMEM((tq, d), F32), pltpu.VMEM((tq, 1), F32)]
    if extra:
        out_shape += [full, _sds((s_len, d), F32)]
        out_specs += [pl.BlockSpec((tq, d), lambda h, i, j: (i, h)), pl.BlockSpec((s_len, d), lambda h, i, j: (0, 0))]
        scratch.append(pltpu.VMEM((tq, d), F32))
    if bias:
        out_shape.append(_sds((N_HEADS, nq, 1, s_len), F32))
        out_specs.append(pl.BlockSpec((None, None, 1, tq), lambda h, i, j: (h, i, 0, j)))
    return pl.pallas_call(body, grid=(N_HEADS, nq, nq), in_specs=in_specs, out_specs=out_specs, out_shape=out_shape,
                          scratch_shapes=scratch, name=name, compiler_params=_cparams(3))(*ins)


def _sb_terms(q_ref, k_ref, scale, tq, diagonal):
    z = _nt(q_ref[...].astype(BF16), k_ref[...].astype(BF16)) * scale
    lg = jnp.log(1.0 + jnp.exp(-jnp.abs(z)))
    log_keep = -(jnp.maximum(z, 0.0) + lg)
    log_beta = jnp.minimum(z, 0.0) - lg
    if not diagonal:
        return None, log_keep, log_beta
    strict = lax.broadcasted_iota(jnp.int32, (tq, tq), 1) < lax.broadcasted_iota(jnp.int32, (tq, tq), 0)
    return strict, jnp.where(strict, log_keep, 0.0), log_beta


def _tri(tq, pred):
    a = lax.broadcasted_iota(jnp.int32, (tq, tq), 0)
    b = lax.broadcasted_iota(jnp.int32, (tq, tq), 1)
    return jnp.where(pred(a, b), 1.0, 0.0).astype(BF16)


SUM_CHUNK = 256


def _lane_sums(x, later):
    n = x.shape[1]
    chunk = min(SUM_CHUNK, n)
    tri = _tri(chunk, (lambda m, j: m > j) if later else (lambda m, j: m < j))
    order = range(n // chunk - 1, -1, -1) if later else range(n // chunk)
    pieces, carry = [None] * (n // chunk), None
    for cidx in order:
        xc = x[:, cidx * chunk:(cidx + 1) * chunk]
        hi = xc.astype(BF16)
        local = _nn(hi, tri) + _nn((xc - hi.astype(F32)).astype(BF16), tri)
        total = jnp.sum(xc, axis=-1, keepdims=True)
        pieces[cidx] = local if carry is None else local + carry
        carry = total if carry is None else carry + total
    return jnp.concatenate(pieces, axis=1), carry


def sb_fwd(name, qn, kn, qkv, scale, voff, tq=ATTN_BLOCK):
    s_len = qn.shape[0]
    nq = s_len // tq
    d = HEAD_DIM

    def body(q_ref, k_ref, v_ref, o_ref, car_ref, ca_s, acc_s):
        qi, kj = pl.program_id(1), pl.program_id(2)

        @pl.when(kj == 0)
        def _():
            ca_s[...] = jnp.zeros_like(ca_s)
            acc_s[...] = jnp.zeros_like(acc_s)

        def step(diagonal):
            strict, log_keep, log_beta = _sb_terms(q_ref, k_ref, scale, tq, diagonal)
            ca = ca_s[...]
            car_ref[...] = ca
            after, total = _lane_sums(log_keep, True)
            a = jnp.exp(log_beta + (after + ca))
            if diagonal:
                a = jnp.where(strict, a, 0.0)
            acc_s[...] += _nn(a.astype(BF16), v_ref[...].astype(BF16))
            ca_s[...] = ca + total

        @pl.when(kj == 0)
        def _():
            step(True)

        @pl.when(jnp.logical_and(kj > 0, kj <= qi))
        def _():
            step(False)

        @pl.when(kj == qi)
        def _():
            o_ref[...] = acc_s[...].astype(o_ref.dtype)

    kblk = lambda i, j: jnp.maximum(i - j, 0)
    return pl.pallas_call(
        body, grid=(N_HEADS, nq, nq),
        in_specs=[pl.BlockSpec((tq, d), lambda h, i, j: (i, h)), pl.BlockSpec((tq, d), lambda h, i, j: (kblk(i, j), h)),
                  pl.BlockSpec((tq, d), lambda h, i, j: (kblk(i, j), voff + h))],
        out_specs=[pl.BlockSpec((tq, d), lambda h, i, j: (i, h)),
                   pl.BlockSpec((None, None, tq, 1), lambda h, i, j: (h, kblk(i, j), i, 0))],
        out_shape=[_sds((s_len, N_HEADS * d), BF16), _sds((N_HEADS, nq, s_len, 1), F32)],
        scratch_shapes=[pltpu.VMEM((tq, 1), F32), pltpu.VMEM((tq, d), F32)],
        name=name, compiler_params=_cparams(3))(qn, kn, qkv)


def sb_bwd(name, qn, kn, qkv, do, carries, scale, voff, tq=ATTN_BLOCK):
    s_len = qn.shape[0]
    nq = s_len // tq
    d = HEAD_DIM

    def body(q_ref, k_ref, v_ref, do_ref, car_ref, dq_ref, dk_ref, dv_ref, dq_s, cg_s):
        qi, kj = pl.program_id(1), pl.program_id(2)

        @pl.when(jnp.logical_and(qi == 0, kj == 0))
        def _():
            dk_ref[...] = jnp.zeros_like(dk_ref)
            dv_ref[...] = jnp.zeros_like(dv_ref)

        @pl.when(kj == 0)
        def _():
            dq_s[...] = jnp.zeros_like(dq_s)
            cg_s[...] = jnp.zeros_like(cg_s)

        def step(diagonal):
            strict, log_keep, log_beta = _sb_terms(q_ref, k_ref, scale, tq, diagonal)
            after, _ = _lane_sums(log_keep, True)
            a = jnp.exp(log_beta + (after + car_ref[...]))
            if diagonal:
                a = jnp.where(strict, a, 0.0)
            dob = do_ref[...].astype(BF16)
            g = a * _nt(dob, v_ref[...].astype(BF16))
            cg = cg_s[...]
            before, total = _lane_sums(g, False)
            big_g = before + cg
            cg_s[...] = cg + total
            beta = jnp.exp(log_beta)
            dz = g * (1.0 - beta) - big_g * beta
            if diagonal:
                dz = jnp.where(strict, dz, 0.0)
            dzb = (dz * scale).astype(BF16)
            ks = pl.ds(pl.multiple_of(kj * tq, tq), tq)
            dq_s[...] += _nn(dzb, k_ref[...].astype(BF16))
            dk_ref[ks, :] += _tn(dzb, q_ref[...].astype(BF16))
            dv_ref[ks, :] += _tn(a.astype(BF16), dob)

        _on_blocks(qi, kj, step)

        @pl.when(kj == qi)
        def _():
            dq_ref[...] = dq_s[...]

    kblk = lambda i, j: jnp.minimum(i, j)
    full = _sds((s_len, N_HEADS * d), F32)
    return pl.pallas_call(
        body, grid=(N_HEADS, nq, nq),
        in_specs=[pl.BlockSpec((tq, d), lambda h, i, j: (i, h)), pl.BlockSpec((tq, d), lambda h, i, j: (kblk(i, j), h)),
                  pl.BlockSpec((tq, d), lambda h, i, j: (kblk(i, j), voff + h)),
                  pl.BlockSpec((tq, d), lambda h, i, j: (i, h)),
                  pl.BlockSpec((None, None, tq, 1), lambda h, i, j: (h, kblk(i, j), i, 0))],
        out_specs=[pl.BlockSpec((tq, d), lambda h, i, j: (i, h)), pl.BlockSpec((s_len, d), lambda h, i, j: (0, h)),
                   pl.BlockSpec((s_len, d), lambda h, i, j: (0, h))],
        out_shape=[full, full, full],
        scratch_shapes=[pltpu.VMEM((tq, d), F32), pltpu.VMEM((tq, 1), F32)],
        name=name, compiler_params=_cparams(3))(qn, kn, qkv, do, carries)


def _cumsum_rows(x, reverse):
    n = x.shape[0] // HEAD_DIM
    tri = _tri(HEAD_DIM, (lambda a, b: b >= a) if reverse else (lambda a, b: b <= a))
    pieces = [None] * n
    carry = jnp.zeros((1, HEAD_DIM), F32)
    order = range(n - 1, -1, -1) if reverse else range(n)
    for blk in order:
        xb = x[blk * HEAD_DIM:(blk + 1) * HEAD_DIM, :]
        x1 = xb.astype(BF16)
        r1 = xb - x1.astype(F32)
        x2 = r1.astype(BF16)
        x3 = (r1 - x2.astype(F32)).astype(BF16)
        c = _nn(tri, x1) + _nn(tri, x2) + _nn(tri, x3) + carry
        pieces[blk] = c
        carry = c[0:1, :] if reverse else c[HEAD_DIM - 1:HEAD_DIM, :]
    return jnp.concatenate(pieces, axis=0)


def _log_sigmoid(x):
    return jnp.minimum(x, 0.0) - jnp.log(1.0 + jnp.exp(-jnp.abs(x)))


def fox_gate_fwd(name, fl, bf):
    return blockk(name, lambda f, b: _cumsum_rows(_log_sigmoid(f + b), False), (1, 1),
                  [(fl, _whole(fl.shape)), (bf, _whole(bf.shape))], [(_sds(fl.shape, F32), _whole(fl.shape), None)])[0]


def fox_gate_bwd(name, fl, bf, dcum):
    def fn(f, b, dc):
        dlogf = _cumsum_rows(dc, True)
        dfl = dlogf * _sig(-(f + b))
        return dfl, jnp.sum(dfl, axis=0, keepdims=True)

    return blockk(name, fn, (1, 1), [(fl, _whole(fl.shape)), (bf, _whole(bf.shape)), (dcum, _whole(dcum.shape))],
                  [(_sds(fl.shape, F32), _whole(fl.shape), None), (_sds(bf.shape, F32), _whole(bf.shape), None)])


def rope_fwd(name, x, gain, cos, sin, tr, xcol, nh):
    r = x.shape[0]
    xspec = _rows(tr, HEAD_DIM, lambda j: xcol + j)
    tspec = _rows(tr, HEAD_DIM, 0)
    return blockk(name, lambda xb, g, c, s: _rope(_rms(xb.astype(F32), g, MLA_ROPE), c, s), (r // tr, nh),
                  [(x, xspec), (gain, _whole(gain.shape)), (cos, tspec), (sin, tspec)],
                  [(_sds((r, HEAD_DIM * nh), BF16), _rows(tr, HEAD_DIM, 'j'), None)])[0]


def rope_bwd(name, x, gain, cos, sin, dy, tr, xcol, nh):
    r = x.shape[0]
    xspec = _rows(tr, HEAD_DIM, lambda j: xcol + j)
    tspec = _rows(tr, HEAD_DIM, 0)
    ospec = _rows(tr, HEAD_DIM, 'j')

    def fn(xb, g, c, s, dyb):
        _, vjp = jax.vjp(lambda a, b: _rms(a, b, MLA_ROPE), xb.astype(F32), g)
        return vjp(_rope_t(dyb.astype(F32), c, s))

    return blockk(name, fn, (r // tr, nh),
                  [(x, xspec), (gain, _whole(gain.shape)), (cos, tspec), (sin, tspec), (dy, ospec)],
                  [(_sds((r, HEAD_DIM * nh), F32), ospec, None), (_sds(gain.shape, F32), _whole(gain.shape), 'all')])


def sgu_pre_fwd(name, pre, gain, tr):
    s_len, w2 = pre.shape
    w = w2 // 2
    return blockk(name, lambda pu, pv, g: (_gelu(pu.astype(F32)), _rms(_gelu(pv.astype(F32)), g, w)), (s_len // tr, 1),
                  [(pre, _rows(tr, w, 0)), (pre, _rows(tr, w, 1)), (gain, _whole(gain.shape))],
                  [(_sds((s_len, w), BF16), _rows(tr, w, 0), None), (_sds((s_len, w), BF16), _rows(tr, w, 0), None)])


def sgu_pre_bwd(name, pre, gain, du, dvn, tr):
    s_len, w2 = pre.shape
    w = w2 // 2

    def fn(pu, pv, g, dub, dvb):
        _, vjp_u = jax.vjp(_gelu, pu.astype(F32))
        _, vjp_v = jax.vjp(lambda a, b: _rms(_gelu(a), b, w), pv.astype(F32), g)
        dpv, dg = vjp_v(dvb.astype(F32))
        return vjp_u(dub.astype(F32))[0], dpv, dg

    spec = _rows(tr, w, 0)
    return blockk(name, fn, (s_len // tr, 1),
                  [(pre, spec), (pre, _rows(tr, w, 1)), (gain, _whole(gain.shape)), (du, spec), (dvn, spec)],
                  [(_sds((s_len, w), BF16), spec, None), (_sds((s_len, w), BF16), spec, None),
                   (_sds(gain.shape, F32), _whole(gain.shape), 'all')])


def _ws_masked(ws):
    t = ws.shape[0]
    a = lax.broadcasted_iota(jnp.int32, (t, t), 0)
    b = lax.broadcasted_iota(jnp.int32, (t, t), 1)
    return jnp.where(b <= a, ws, 0.0)


def sgu_mix_fwd(name, vn, u, ws, bs3):
    s_len, w = vn.shape
    t = ws.shape[1]

    def fn(vb, ub, wsb, bsb):
        mixed = _nn(_ws_masked(wsb).astype(BF16), vb.astype(BF16)) + bsb
        return ub.astype(F32) * mixed

    blk = pl.BlockSpec((t, t), lambda i, j: (i, j))
    return blockk(name, fn, (s_len // t, w // t),
                  [(vn, blk), (u, blk), (ws, pl.BlockSpec((None, t, t), lambda i, j: (j, 0, 0))),
                   (bs3, pl.BlockSpec((None, t, 1), lambda i, j: (j, 0, 0)))],
                  [(_sds((s_len, w), BF16), blk, None)])[0]


def sgu_mix_bwd(name, vn, u, ws, bs3, dprod):
    s_len, w = vn.shape
    t = ws.shape[1]

    def fn(vb, ub, wsb, bsb, dpb):
        wm = _ws_masked(wsb).astype(BF16)
        vb16 = vb.astype(BF16)
        mixed = _nn(wm, vb16) + bsb
        dp = dpb.astype(F32)
        du = dp * mixed
        dm = dp * ub.astype(F32)
        dmb = dm.astype(BF16)
        dvn = _tn(wm, dmb)
        dws = _ws_masked(_nt(dmb, vb16))
        return du, dvn, dws, jnp.sum(dm, axis=-1, keepdims=True)

    blk = pl.BlockSpec((t, t), lambda g, n: (n, g))
    wspec = pl.BlockSpec((None, t, t), lambda g, n: (g, 0, 0))
    bspec = pl.BlockSpec((None, t, 1), lambda g, n: (g, 0, 0))
    return blockk(name, fn, (w // t, s_len // t), [(vn, blk), (u, blk), (ws, wspec), (bs3, bspec), (dprod, blk)],
                  [(_sds((s_len, w), BF16), blk, None), (_sds((s_len, w), BF16), blk, None),
                   (_sds(ws.shape, F32), wspec, 'inner'), (_sds(bs3.shape, F32), bspec, 'inner')])


def _shift_down(x, k):
    row = lax.broadcasted_iota(jnp.int32, x.shape, 0)
    return jnp.where(row >= k, pltpu.roll(x, k, 0), 0.0)


def _shift_up(x, k):
    n = x.shape[0]
    row = lax.broadcasted_iota(jnp.int32, x.shape, 0)
    return jnp.where(row < n - k, pltpu.roll(x, n - k, 0), 0.0)


def _conv(up, cw, cb):
    return cb + cw[0:1, :] * _shift_down(up, 2) + cw[1:2, :] * _shift_down(up, 1) + cw[2:3, :] * up


def _conv_specs(s_len, tc, nf):
    g = pl.BlockSpec((s_len, tc), lambda i, j: (0, j))
    v = pl.BlockSpec((s_len, tc), lambda i, j: (0, j + nf))
    wg = pl.BlockSpec((3, tc), lambda i, j: (0, j))
    wv = pl.BlockSpec((3, tc), lambda i, j: (0, j + nf))
    bg = pl.BlockSpec((1, tc), lambda i, j: (0, j))
    bv = pl.BlockSpec((1, tc), lambda i, j: (0, j + nf))
    return g, v, wg, wv, bg, bv


def conv_fwd(name, up, cw, cb, tc=256):
    s_len, f2 = up.shape
    f = f2 // 2
    nf = f // tc
    g, v, wg, wv, bg, bv = _conv_specs(s_len, tc, nf)

    def fn(ug, uv, cwg, cwv, cbg, cbv):
        yg = _conv(ug.astype(F32), cwg, cbg)
        yv = _conv(uv.astype(F32), cwv, cbv)
        return yg * _sig(yg) * yv

    return blockk(name, fn, (1, nf), [(up, g), (up, v), (cw, wg), (cw, wv), (cb, bg), (cb, bv)],
                  [(_sds((s_len, f), BF16), g, None)])[0]


def conv_bwd(name, up, cw, cb, dact, tc=128):
    s_len, f2 = up.shape
    f = f2 // 2
    nf = f // tc
    g, v, wg, wv, bg, bv = _conv_specs(s_len, tc, nf)

    def half(dy, u, cwh):
        u1, u2 = _shift_down(u, 1), _shift_down(u, 2)
        dup = cwh[2:3, :] * dy + cwh[1:2, :] * _shift_up(dy, 1) + cwh[0:1, :] * _shift_up(dy, 2)
        dcw = jnp.concatenate([jnp.sum(dy * u2, axis=0, keepdims=True), jnp.sum(dy * u1, axis=0, keepdims=True),
                               jnp.sum(dy * u, axis=0, keepdims=True)], axis=0)
        return dup, dcw, jnp.sum(dy, axis=0, keepdims=True)

    def fn(ug, uv, cwg, cwv, cbg, cbv, da):
        ug, uv, da = ug.astype(F32), uv.astype(F32), da.astype(F32)
        yg = _conv(ug, cwg, cbg)
        yv = _conv(uv, cwv, cbv)
        sg = _sig(yg)
        dyv = da * (yg * sg)
        dyg = da * yv * (sg * (1.0 + yg * (1.0 - sg)))
        dug, dcwg, dcbg = half(dyg, ug, cwg)
        duv, dcwv, dcbv = half(dyv, uv, cwv)
        return dug, duv, dcwg, dcwv, dcbg, dcbv

    return blockk(name, fn, (1, nf), [(up, g), (up, v), (cw, wg), (cw, wv), (cb, bg), (cb, bv), (dact, g)],
                  [(_sds((s_len, f), BF16), g, None), (_sds((s_len, f), BF16), g, None),
                   (_sds((3, f), F32), wg, None), (_sds((3, f), F32), wg, None),
                   (_sds((1, f), F32), bg, None), (_sds((1, f), F32), bg, None)])


def loss_head(name, y, target, tr=256):
    s_len, d = y.shape

    def fn(yb, tb):
        e = yb - tb
        dy = e * (1.0 / d)
        return dy, dy, jnp.sum(e * e, axis=0, keepdims=True)

    spec = _rows(tr, d, 0)
    return blockk(name, fn, (s_len // tr, 1), [(y, spec), (target, spec)],
                  [(_sds((s_len, d), BF16), spec, None), (_sds((s_len, d), F32), spec, None),
                   (_sds((1, d), F32), _whole((1, d)), 'all')])


def _adam(w, g, m, v):
    m = ADAM_B1 * m + (1.0 - ADAM_B1) * g
    v = ADAM_B2 * v + (1.0 - ADAM_B2) * (g * g)
    m_hat = m / (1.0 - ADAM_B1 ** ADAM_STEP)
    v_hat = v / (1.0 - ADAM_B2 ** ADAM_STEP)
    delta = -ADAM_LR * (m_hat / (jnp.sqrt(v_hat) + ADAM_EPS) + ADAM_WD * w)
    return delta, m, v


def adam_sum(name, parts, w, m, v):
    n_layers, r, c = w.shape
    n_parts = parts[0].shape[0]
    assert len(parts) == n_layers
    tr = r
    for cand in (512, 256, 128, 64, 32, 16):
        if r % cand == 0 and cand * c * 4 <= 1024 * 1024:
            tr = cand
            break

    def body(*refs):
        part_refs = refs[:n_layers]
        w_ref, m_ref, v_ref, g_out, d_out, m_out, v_out = refs[n_layers:]
        layer = pl.program_id(0)
        for ll in range(n_layers):
            @pl.when(layer == ll)
            def _(ll=ll):
                g = part_refs[ll][0].astype(F32)
                for k in range(1, n_parts):
                    g = g + part_refs[ll][k].astype(F32)
                delta, m_new, v_new = _adam(w_ref[...], g, m_ref[...], v_ref[...])
                g_out[...] = g
                d_out[...] = delta
                m_out[...] = m_new
                v_out[...] = v_new

    spec = pl.BlockSpec((None, tr, c), lambda l, i: (l, i, 0))
    part_specs = [pl.BlockSpec((n_parts, tr, c), lambda l, i, ll=ll: (0, jnp.where(l == ll, i, 0), 0))
                  for ll in range(n_layers)]
    out = _sds((n_layers, r, c), F32)
    return pl.pallas_call(body, grid=(n_layers, r // tr), in_specs=part_specs + [spec] * 3, out_specs=[spec] * 4,
                          out_shape=[out] * 4, name=name, compiler_params=_cparams(2))(*parts, w, m, v)


def sum_parts(name, parts, tr=256):
    n_parts, r, c = parts.shape

    def fn(pb):
        g = pb[0]
        for k in range(1, n_parts):
            g = g + pb[k]
        return g

    return blockk(name, fn, (r // tr, 1), [(parts, pl.BlockSpec((n_parts, tr, c), lambda i, j: (0, i, 0)))],
                  [(_sds((r, c), F32), _rows(tr, c, 0), None)])[0]


def adam_flat(name, w, g, m, v, tr=256):
    r, c = w.shape
    spec = _rows(tr, c, 0)
    return blockk(name, lambda wb, gb, mb, vb: _adam(wb, gb, mb, vb), (r // tr, 1),
                  [(w, spec), (g, spec), (m, spec), (v, spec)], [(_sds((r, c), F32), spec, None)] * 3)


_ANY = pl.BlockSpec(memory_space=pl.ANY)


def _place():
    return lax.axis_index("x"), lax.axis_index("y"), lax.axis_index("c")


def _slot(px, py, pc):
    return 4 * px + 2 * py + pc


def all_gather(name, items):
    n = len(items)

    def body(*refs):
        xs, outs = refs[:n], refs[n:2 * n]
        send_sems, recv_sems, local_sems = refs[2 * n:]
        x, y, c = _place()
        me, sibling = (x, y, c), (x, y, 1 - c)
        chips = [(1 - x, y), (x, 1 - y), (1 - x, 1 - y)]

        def copy(t, k, block, to, src=None):
            dst = outs[t].at[_slot(*block)]
            return pltpu.make_async_remote_copy(src_ref=dst if src is None else src, dst_ref=dst,
                                                send_sem=send_sems.at[7 * t + k], recv_sem=recv_sems.at[7 * t + k],
                                                device_id=to, device_id_type=MESH)

        mine = [pltpu.make_async_copy(xs[t], outs[t].at[_slot(*me)], local_sems.at[t]) for t in range(n)]
        for cp in mine:
            cp.start()
        started = []
        for t in range(n):
            started.append(copy(t, 0, me, sibling, src=xs[t]))
            started += [copy(t, 1 + j, me, (*chip, c), src=xs[t]) for j, chip in enumerate(chips)]
        for cp in started:
            cp.start()
        for j, chip in enumerate(chips):
            for t in range(n):
                copy(t, 1 + j, (*chip, c), me).wait_recv()
                passed = copy(t, 4 + j, (*chip, c), sibling)
                passed.start()
                started.append(passed)
        for t in range(n):
            copy(t, 0, sibling, me).wait_recv()
            for j, chip in enumerate(chips):
                copy(t, 4 + j, (*chip, 1 - c), me).wait_recv()
        for cp in started:
            cp.wait_send()
        for cp in mine:
            cp.wait()

    return pl.pallas_call(
        body, in_specs=[_ANY] * n, out_specs=[_ANY] * n,
        out_shape=[_sds((N_DEV,) + a.shape, a.dtype) for a in items],
        scratch_shapes=[pltpu.SemaphoreType.DMA((7 * n,)), pltpu.SemaphoreType.DMA((7 * n,)), pltpu.SemaphoreType.DMA((n,))],
        name=name)(*items)


_HBM = pl.BlockSpec(memory_space=pltpu.HBM)
_SEM = pl.BlockSpec(memory_space=pltpu.SEMAPHORE)
_EFFECT = pltpu.SideEffectType.DATAFLOW_SIDE_EFFECTING


def _peer(k, x, y, c):
    return ((1 - x) if k & 4 else x, (1 - y) if k & 2 else y, (1 - c) if k & 1 else c)


PEERS_ALL = (1, 2, 3, 4, 5, 6, 7)
PEERS_CHIPWISE = (1, 2, 4, 6)


def _exchange_copies(mode, refs, send_sems, recv_sems, landing):
    x, y, c = _place()
    my_slot = _slot(x, y, c)
    copies = []
    if mode == "pass_on":
        for t, land in enumerate(refs):
            for j, chip in enumerate([(1 - x, y), (x, 1 - y), (1 - x, 1 - y)]):
                slot = _slot(*chip, (1 - c) if landing else c)
                copies.append(pltpu.make_async_remote_copy(
                    src_ref=land.at[slot], dst_ref=land.at[slot], send_sem=send_sems.at[3 * t + j],
                    recv_sem=recv_sems.at[3 * t + j], device_id=(x, y, 1 - c), device_id_type=MESH))
        return copies
    n = len(refs) // 2
    for t, (src, land) in enumerate(zip(refs[:n], refs[n:])):
        for k in (PEERS_CHIPWISE if mode == "gather_chipwise" else PEERS_ALL):
            peer = _peer(k, x, y, c)
            copies.append(pltpu.make_async_remote_copy(
                src_ref=src.at[_slot(*peer)] if mode == "scatter" else src,
                dst_ref=land.at[_slot(*peer) if landing else my_slot],
                send_sem=send_sems.at[7 * t + k - 1], recv_sem=recv_sems.at[7 * t + k - 1],
                device_id=peer, device_id_type=MESH))
    return copies


def exchange_start(name, mode, arrays):
    if mode == "pass_on":
        bufs, n_sems = list(arrays), 3 * len(arrays)
    else:
        x, y, c = _place()
        my_slot = _slot(x, y, c)
        lands = []
        for s in arrays:
            own = lax.dynamic_index_in_dim(s, my_slot, 0, keepdims=True) if mode == "scatter" else s[None]
            shape = s.shape if mode == "scatter" else (N_DEV,) + s.shape
            lands.append(lax.dynamic_update_slice(lax.empty(shape, s.dtype), own, (my_slot,) + (0,) * (len(shape) - 1)))
        bufs, n_sems = list(arrays) + lands, 7 * len(arrays)
    nb = len(bufs)

    def body(*refs):
        for send in _exchange_copies(mode, refs[:nb], refs[nb], refs[nb + 1], False):
            send.start()
        refs[-1][...] = jnp.zeros_like(refs[-1])

    bufs = [pltpu.with_memory_space_constraint(a, pltpu.HBM) for a in bufs]
    res = pl.pallas_call(
        body, name=name, in_specs=[_HBM] * nb,
        out_specs=(_SEM, _SEM) + (_HBM,) * nb + (pl.BlockSpec(memory_space=pltpu.VMEM),),
        out_shape=(pltpu.SemaphoreType.DMA((n_sems,)), pltpu.SemaphoreType.DMA((n_sems,)))
        + tuple(pltpu.HBM(a.shape, a.dtype) for a in bufs) + (_sds((8, HEAD_DIM), F32),),
        input_output_aliases={i: 2 + i for i in range(nb)},
        compiler_params=pltpu.CompilerParams(has_side_effects=_EFFECT))(*bufs)
    return dict(name=name, mode=mode, sems=res[:2], bufs=res[2:2 + nb]), res[-1]


def exchange_wait(handle, after):
    mode, bufs = handle['mode'], handle['bufs']
    nb = len(bufs)

    def body(*refs):
        for landed in _exchange_copies(mode, refs[:nb], refs[nb], refs[nb + 1], True):
            landed.wait_send()
            landed.wait_recv()

    res = pl.pallas_call(
        body, name=handle['name'].replace("start", "wait"), in_specs=[_HBM] * nb + [_SEM, _SEM, _ANY],
        out_specs=(_HBM,) * nb, out_shape=tuple(pltpu.HBM(a.shape, a.dtype) for a in bufs),
        input_output_aliases={i: i for i in range(nb)},
        compiler_params=pltpu.CompilerParams(has_side_effects=_EFFECT))(*bufs, *handle['sems'], after)
    return list(res) if mode == "pass_on" else list(res[nb // 2:])


def _pad_lanes(a, width=HEAD_DIM):
    return jnp.pad(a, [(0, 0)] * (a.ndim - 1) + [(0, width - a.shape[-1])])


def _slabs(full, n):
    k = full.shape[0]
    return full.reshape(k, N_DEV, n).transpose(1, 0, 2)


def fox_fwd(a, w, p):
    qkv = mm_nn(a, w['qkv'], BF16, "fox_qkv")
    fl = mm_nn(a, w['f'], F32, "fox_flogit")
    bf = _pad_lanes(p['fox_b_f'])
    cum = fox_gate_fwd("fox_gate_fwd", fl, bf)
    cum_t = cum[:, :N_HEADS].T
    cums = (cum_t[:, :, None], cum_t[:, None, :])
    qn = rms_fwd("fox_qnorm", qkv, p['fox_q_gain'], HEAD_DIM, 512, HEAD_DIM, xcol=0, nh=N_HEADS)
    kn = rms_fwd("fox_knorm", qkv, p['fox_k_gain'], HEAD_DIM, 512, HEAD_DIM, xcol=N_HEADS, nh=N_HEADS)
    o, lse = attn_fwd("fox_attn_fwd", qn, kn, qkv, HEAD_DIM ** -0.5, voff=2 * N_HEADS, cum=cums, exact_o=True)
    return o, dict(a=a, qkv=qkv, fl=fl, bf=bf, cums=cums, qn=qn, kn=kn, o=o, lse=lse)


def fox_bwd(do, s, w, p):
    dqn, dkn, dv, dcs = attn_bwd("fox_attn_bwd", s['qn'], s['kn'], s['qkv'], s['o'], do, s['lse'], HEAD_DIM ** -0.5,
                                 voff=2 * N_HEADS, cum=s['cums'])
    dq, dgq = rms_bwd("fox_qnorm_bwd", s['qkv'], p['fox_q_gain'], dqn, HEAD_DIM, 512, HEAD_DIM, xcol=0, nh=N_HEADS,
                      out_dtype=BF16)
    dk, dgk = rms_bwd("fox_knorm_bwd", s['qkv'], p['fox_k_gain'], dkn, HEAD_DIM, 512, HEAD_DIM, xcol=N_HEADS,
                      nh=N_HEADS, out_dtype=BF16)
    dqkv = jnp.concatenate([dq, dk, dv.astype(BF16)], axis=1)
    dcum = _pad_lanes(-jnp.sum(dcs[:, :, 0, :], axis=1).T)
    dfl, dbf = fox_gate_bwd("fox_gate_bwd", s['fl'], s['bf'], dcum)
    da = mm_nt(dqkv, w['qkv'], F32, "fox_da_qkv")
    da = mm_nt(dfl, w['f'], F32, "fox_da_f", add=da)
    dw_qkv = mm_tn(s['a'], dqkv, 1, BF16, "fox_dw_qkv")[0]
    dw_f = mm_tn(s['a'], dfl, 1, BF16, "fox_dw_f")[0][:, :N_HEADS]
    dw = _slabs(jnp.concatenate([dw_qkv, dw_f], axis=1), 770)
    return da, dict(fox_w_in=dw), dict(fox_b_f=dbf[:, :N_HEADS], fox_q_gain=dgq, fox_k_gain=dgk)


def _rope_tables(positions):
    inv_freq = ROPE_THETA ** (-jnp.arange(0, MLA_ROPE, 2, dtype=F32) / MLA_ROPE)
    ang = positions.astype(F32)[:, None] * inv_freq
    cos, sin = jnp.cos(ang), jnp.sin(ang)
    return _pad_lanes(jnp.concatenate([cos, cos], axis=1)), _pad_lanes(jnp.concatenate([sin, sin], axis=1))


def mla_fwd(a, w, p):
    qg, kg = p['mla_q_gain'], p['mla_k_gain']
    gains = dict(qn=qg[:, :HEAD_DIM], qr=_pad_lanes(qg[:, HEAD_DIM:]), kn=kg[:, :HEAD_DIM], kr=_pad_lanes(kg[:, HEAD_DIM:]))
    cos, sin = _rope_tables(p['positions'])
    ccr = mm_nn(a, w['in'], F32, "mla_in")
    cqn = rms_fwd("mla_cq_norm", ccr, w['q_a_gain'], 512, 256, 512, xcol=0)
    ckvn = rms_fwd("mla_ckv_norm", ccr, w['kv_a_gain'], 512, 256, 512, xcol=1)
    qf = mm_nn(cqn, w['q_b'], F32, "mla_q_b")
    kvf = mm_nn(ckvn, w['kv_b'], F32, "mla_kv_b")
    q_nope = rms_fwd("mla_qnope_norm", qf, gains['qn'], HEAD_DIM, 512, HEAD_DIM, xcol=0, nh=N_HEADS)
    k_nope = rms_fwd("mla_knope_norm", kvf, gains['kn'], HEAD_DIM, 512, HEAD_DIM, xcol=0, nh=N_HEADS)
    q_rope = rope_fwd("mla_qrope", qf, gains['qr'], cos, sin, 512, N_HEADS, N_HEADS)
    k_rope = rope_fwd("mla_krope", ccr, gains['kr'], cos, sin, 512, 8, 1)
    scale = (HEAD_DIM + MLA_ROPE) ** -0.5
    o, lse = attn_fwd("mla_attn_fwd", q_nope, k_nope, kvf, scale, voff=N_HEADS, q2=q_rope, k2=k_rope)
    return o, dict(a=a, gains=gains, cos=cos, sin=sin, ccr=ccr, cqn=cqn, ckvn=ckvn, qf=qf, kvf=kvf, q_nope=q_nope,
                   k_nope=k_nope, q_rope=q_rope, k_rope=k_rope, o=o, lse=lse, scale=scale)


def mla_bwd(do, s, w, p):
    g = s['gains']
    dqn, dkn, dv, dq2, dk2 = attn_bwd("mla_attn_bwd", s['q_nope'], s['k_nope'], s['kvf'], s['o'], do, s['lse'], s['scale'],
                                      voff=N_HEADS, q2=s['q_rope'], k2=s['k_rope'])
    dqf_n, dg_qn = rms_bwd("mla_qnope_bwd", s['qf'], g['qn'], dqn, HEAD_DIM, 512, HEAD_DIM, xcol=0, nh=N_HEADS)
    dkf, dg_kn = rms_bwd("mla_knope_bwd", s['kvf'], g['kn'], dkn, HEAD_DIM, 512, HEAD_DIM, xcol=0, nh=N_HEADS)
    dqf_r, dg_qr = rope_bwd("mla_qrope_bwd", s['qf'], g['qr'], s['cos'], s['sin'], dq2, 512, N_HEADS, N_HEADS)
    dkr, dg_kr = rope_bwd("mla_krope_bwd", s['ccr'], g['kr'], s['cos'], s['sin'], dk2, 512, 8, 1)
    dqf = jnp.concatenate([dqf_n, dqf_r], axis=1)
    dkvf = jnp.concatenate([dkf, dv], axis=1)
    dcqn = mm_nt(dqf, w['q_b'], F32, "mla_dcq")
    dckvn = mm_nt(dkvf, w['kv_b'], F32, "mla_dckv")
    dw_qb = mm_tn(s['cqn'], dqf, 1, BF16, "mla_dw_qb")[0]
    dw_kvb = mm_tn(s['ckvn'], dkvf, 1, BF16, "mla_dw_kvb")[0]
    dcq, dg_qa = rms_bwd("mla_cq_bwd", s['ccr'], w['q_a_gain'], dcqn, 512, 256, 512, xcol=0)
    dckv, dg_kva = rms_bwd("mla_ckv_bwd", s['ccr'], w['kv_a_gain'], dckvn, 512, 256, 512, xcol=1)
    dccr = jnp.concatenate([dcq, dckv, dkr], axis=1)
    da = mm_nt(dccr, w['in'], F32, "mla_da")
    dw_in = mm_tn(s['a'], dccr, 1, BF16, "mla_dw_in")[0][:, :1088].reshape(N_DEV, 256, 1088)
    hp = 2
    nope = dw_qb[:, :2048].reshape(512, N_DEV, hp, HEAD_DIM)
    rope = dw_qb[:, 2048:].reshape(512, N_DEV, hp, HEAD_DIM)[..., :MLA_ROPE]
    dw_qb_s = jnp.concatenate([nope, rope], axis=-1).transpose(1, 0, 2, 3).reshape(N_DEV, 512, hp * 192)
    kk = dw_kvb[:, :2048].reshape(512, N_DEV, hp, HEAD_DIM)
    vv = dw_kvb[:, 2048:].reshape(512, N_DEV, hp, HEAD_DIM)
    dw_kvb_s = jnp.concatenate([kk, vv], axis=-1).transpose(1, 0, 2, 3).reshape(N_DEV, 512, hp * 256)
    small = dict(mla_q_a_gain=dg_qa, mla_kv_a_gain=dg_kva,
                 mla_q_gain=jnp.concatenate([dg_qn, dg_qr[:, :MLA_ROPE]], axis=1),
                 mla_k_gain=jnp.concatenate([dg_kn, dg_kr[:, :MLA_ROPE]], axis=1))
    return da, dict(mla_w_in=dw_in, mla_w_q_b=dw_qb_s, mla_w_kv_b=dw_kvb_s), small


def sb_fwd_layer(a, w, p):
    qkv = mm_nn(a, w['in'], BF16, "sb_qkv")
    qn = rms_fwd("sb_qnorm", qkv, p['sb_q_gain'], HEAD_DIM, 512, HEAD_DIM, xcol=0, nh=N_HEADS)
    kn = rms_fwd("sb_knorm", qkv, p['sb_k_gain'], HEAD_DIM, 512, HEAD_DIM, xcol=N_HEADS, nh=N_HEADS)
    o, carries = sb_fwd("sb_attn_fwd", qn, kn, qkv, HEAD_DIM ** -0.5, 2 * N_HEADS)
    return o, dict(a=a, qkv=qkv, qn=qn, kn=kn, carries=carries)


def sb_bwd_layer(do, s, w, p):
    dqn, dkn, dv = sb_bwd("sb_attn_bwd", s['qn'], s['kn'], s['qkv'], do, s['carries'], HEAD_DIM ** -0.5, 2 * N_HEADS)
    dq, dgq = rms_bwd("sb_qnorm_bwd", s['qkv'], p['sb_q_gain'], dqn, HEAD_DIM, 512, HEAD_DIM, xcol=0, nh=N_HEADS,
                      out_dtype=BF16)
    dk, dgk = rms_bwd("sb_knorm_bwd", s['qkv'], p['sb_k_gain'], dkn, HEAD_DIM, 512, HEAD_DIM, xcol=N_HEADS, nh=N_HEADS,
                      out_dtype=BF16)
    dqkv = jnp.concatenate([dq, dk, dv.astype(BF16)], axis=1)
    da = mm_nt(dqkv, w['in'], F32, "sb_da")
    dw = mm_tn(s['a'], dqkv, N_DEV, BF16, "sb_dw_in")
    return da, dict(sb_w_in=dw), dict(sb_q_gain=dgq, sb_k_gain=dgk)


def sgu_fwd(a, w, p):
    pre = mm_nn(a, w['in'], BF16, "sgu_in")
    u, vn = sgu_pre_fwd("sgu_pre_fwd", pre, w['v_gain'], 256)
    ws = p['sgu_w_s'][0]
    bs3 = p['sgu_b_s'][0][:, :, None]
    prod = sgu_mix_fwd("sgu_mix_fwd", vn, u, ws, bs3)
    return prod, dict(a=a, pre=pre, u=u, vn=vn, ws=ws, bs3=bs3)


def sgu_bwd(dprod, s, w, p):
    du, dvn, dws, dbs3 = sgu_mix_bwd("sgu_mix_bwd", s['vn'], s['u'], s['ws'], s['bs3'], dprod)
    dpu, dpv, dgv = sgu_pre_bwd("sgu_pre_bwd", s['pre'], w['v_gain'], du, dvn, 128)
    dpre = jnp.concatenate([dpu, dpv], axis=1)
    da = mm_nt(dpre, w['in'], F32, "sgu_da")
    dw = mm_tn(s['a'], dpre, N_DEV, BF16, "sgu_dw_in")
    return da, dict(sgu_w_in=dw), dict(sgu_v_gain=dgv, sgu_w_s=dws[None], sgu_b_s=dbs3[None, :, :, 0])


MIXERS = [("fox", fox_fwd, fox_bwd), ("mla", mla_fwd, mla_bwd), ("sb", sb_fwd_layer, sb_bwd_layer),
          ("sgu", sgu_fwd, sgu_bwd)]


def _pack_rows(arrays, row_mult=256):
    flat = jnp.concatenate([a.reshape(-1).astype(F32) for a in arrays])
    per = row_mult * HEAD_DIM
    total = -(-flat.shape[0] // per) * per
    return jnp.pad(flat, (0, total - flat.shape[0])).reshape(total // HEAD_DIM, HEAD_DIM)


def _unpack_rows(packed, shapes):
    flat = packed.reshape(-1)
    out, off = [], 0
    for shp in shapes:
        n = math.prod(shp)
        out.append(flat[off:off + n].reshape(shp))
        off += n
    return out


def _mixer_shards(i, p):
    name = MIXERS[i][0]
    bf = lambda a: a.astype(BF16)
    items = {'out': bf(p[name + '_w_out'][0])}
    if name == "fox":
        items['in'] = bf(p['fox_w_in'][0])
        items['small'] = _pack_rows([p[n] for n in SMALL_SHARDED], 8)
    elif name == "mla":
        items.update({'in': bf(p['mla_w_in'][0]), 'q_b': bf(p['mla_w_q_b'][0]), 'kv_b': bf(p['mla_w_kv_b'][0])})
    else:
        items['in'] = bf(p[name + '_w_in'][0])
    return items


def _ffn_shards(i, p):
    return {'up': p['ffn_w_up'][i].astype(BF16), 'down': p['ffn_w_down'][i].astype(BF16)}


def _assemble_ffn(got):
    return {'up': got['up'], 'down': got['down'].reshape(1, D_FF, -1)}


def _assemble_mixer(i, got, p):
    name = MIXERS[i][0]
    w = {'out': got['out'].reshape(1, -1, got['out'].shape[-1])}
    small = None
    if name == "fox":
        small_shapes = [p[n].shape for n in SMALL_SHARDED]
        full = got['in'].transpose(1, 0, 2).reshape(got['in'].shape[1], -1)
        w['qkv'] = full[None, :, :3 * N_HEADS * HEAD_DIM]
        w['f'] = _pad_lanes(full[:, 3 * N_HEADS * HEAD_DIM:])[None]
        parts = [_unpack_rows(got['small'][d], small_shapes) for d in range(N_DEV)]
        small = {n: jnp.concatenate([parts[d][k] for d in range(N_DEV)], axis=ax)
                 for k, (n, ax) in enumerate(SMALL_SHARDED.items())}
    elif name == "mla":
        w['in'] = _pad_lanes(got['in'].reshape(-1, 1088), 1152)[None]
        hp = 2
        qb = got['q_b'].reshape(N_DEV, 512, hp, 192).transpose(1, 0, 2, 3)
        nope = qb[..., :HEAD_DIM].reshape(512, -1)
        rope = _pad_lanes(qb[..., HEAD_DIM:]).reshape(512, -1)
        w['q_b'] = jnp.concatenate([nope, rope], axis=1)[None]
        kvb = got['kv_b'].reshape(N_DEV, 512, hp, 256).transpose(1, 0, 2, 3)
        w['kv_b'] = jnp.concatenate([kvb[..., :HEAD_DIM].reshape(512, -1), kvb[..., HEAD_DIM:].reshape(512, -1)], axis=1)[None]
    else:
        w['in'] = got['in']
    return w, small


def _train_step(p):
    x, target = p['x'][0], p['loss_target'][0]
    p = dict(p, positions=p['positions'][0])
    xi, yi, ci = _place()
    my_slot = _slot(xi, yi, ci)

    shards0 = _mixer_shards(0, p)
    got0 = dict(zip(shards0, all_gather("gather_first", list(shards0.values()))))
    pending, order_token = {}, jnp.zeros((1, 1), F32)
    for i in range(DEPTH):
        for kind, shards in (("mix", _mixer_shards(i, p) if i else None), ("ffn", _ffn_shards(i, p))):
            if shards is not None:
                handle, token = exchange_start(f"xstart_ag_{kind}{i}", "gather_chipwise", list(shards.values()))
                pending[kind, i] = (handle, list(shards))
                order_token = order_token + token[0:1, 0:1]

    def pass_on(kind, i, after):
        handle, keys = pending[kind, i]
        handle, token = exchange_start(f"xstart_pass_{kind}{i}", "pass_on", exchange_wait(handle, after))
        pending[kind, i] = (handle, keys)
        return token[0:1, 0:1]

    def gathered(kind, i, after):
        handle, keys = pending[kind, i]
        return dict(zip(keys, exchange_wait(handle, after)))

    h = x
    saved, weights = [], []
    small_full = None
    for i in range(DEPTH):
        if i == 0:
            w, small_full = _assemble_mixer(0, got0, p)
            gain = p['mix_norm'][0:1] + order_token
        else:
            w, _ = _assemble_mixer(i, gathered("mix", i, h), p)
            gain = p['mix_norm'][i:i + 1] + pass_on("ffn", i, h)
        w.update(q_a_gain=small_full['mla_q_a_gain'], kv_a_gain=small_full['mla_kv_a_gain'],
                 v_gain=small_full['sgu_v_gain'])
        conv_w = small_full['ffn_conv_w']
        a = rms_fwd(f"mix_norm_{i}", h, gain, h.shape[1], 256, h.shape[1])
        mixed, s_mix = MIXERS[i][1](a, w, p)
        if i == 0:
            pass_on("ffn", 0, mixed)
        w.update(_assemble_ffn(gathered("ffn", i, mixed)))
        weights.append(w)
        h1 = mm_nn(mixed, w['out'], F32, f"mix_out_{i}", add=h)
        b = rms_fwd(f"ffn_norm_{i}", h1, p['ffn_norm'][i:i + 1], h.shape[1], 256, h.shape[1])
        up = mm_nn(b, w['up'], BF16, f"ffn_up_{i}")
        conv_b = p['ffn_conv_b'][i:i + 1]
        if i + 1 < DEPTH:
            conv_b = conv_b + pass_on("mix", i + 1, up)
        act = conv_fwd(f"ffn_conv_{i}", up, conv_w[i], conv_b)
        h2 = mm_nn(act, w['down'], F32, f"ffn_down_{i}", add=h1)
        saved.append(dict(h=h, mixed=mixed, s_mix=s_mix, h1=h1, b=b, up=up, act=act))
        h = h2

    dh16, dh, sq = loss_head("loss_head", h, target)
    loss = lax.psum(0.5 * jnp.sum(sq) / h.shape[1], ("x", "y", "c"))

    scatters = []
    small_g = {n: [None] * p[n].shape[0] for n in ('mix_norm', 'ffn_norm', 'ffn_conv_w', 'ffn_conv_b')}
    for i in reversed(range(DEPTH)):
        w, s = weights[i], saved[i]
        name = MIXERS[i][0]
        dact = mm_nt(dh16, w['down'], BF16, f"ffn_dact_{i}")
        dw_down = mm_tn(s['act'], dh16, 1, BF16, f"ffn_dw_down_{i}").reshape(N_DEV, D_FF // N_DEV, -1)
        dug, duv, dcwg, dcwv, dcbg, dcbv = conv_bwd(f"ffn_conv_bwd_{i}", s['up'], conv_w[i], p['ffn_conv_b'][i:i + 1], dact)
        dup = jnp.concatenate([dug, duv], axis=1)
        dw_up = mm_tn(s['b'], dup, N_DEV, BF16, f"ffn_dw_up_{i}")
        handle, token = exchange_start(f"xstart_rs_ffn{i}", "scatter", [dw_up, dw_down])
        scatters.append((handle, ['ffn_w_up', 'ffn_w_down']))
        db = mm_nt(dup, w['up'], F32, f"ffn_db_{i}")
        dh1_16, dh1, dg_ffn = rms_bwd(f"ffn_norm_bwd_{i}", s['h1'], p['ffn_norm'][i:i + 1] + token[0:1, 0:1], db,
                                      h.shape[1], 128, h.shape[1], resid=dh, also_bf16=True)
        dmix = mm_nt(dh1_16, w['out'], BF16, f"mix_dout_{i}")
        dw_out = mm_tn(s['mixed'], dh1_16, 1, BF16, f"mix_dw_out_{i}").reshape(N_DEV, -1, h.shape[1])
        da, big_i, small_i = MIXERS[i][2](dmix, s['s_mix'], w, p)
        big_i[name + '_w_out'] = dw_out
        handle, token = exchange_start(f"xstart_rs_mix{i}", "scatter", list(big_i.values()))
        scatters.append((handle, list(big_i)))
        dh16, dh, dg_mix = rms_bwd(f"mix_norm_bwd_{i}", s['h'], p['mix_norm'][i:i + 1] + token[0:1, 0:1], da, h.shape[1],
                                   128, h.shape[1], resid=dh1, also_bf16=True)
        for k, v in small_i.items():
            small_g[k] = v
        small_g['mix_norm'][i], small_g['ffn_norm'][i] = dg_mix, dg_ffn
        small_g['ffn_conv_w'][i] = jnp.concatenate([dcwg, dcwv], axis=1)[None]
        small_g['ffn_conv_b'][i] = jnp.concatenate([dcbg, dcbv], axis=1)
    for n in ('mix_norm', 'ffn_norm', 'ffn_conv_w', 'ffn_conv_b'):
        small_g[n] = jnp.concatenate(small_g[n], axis=0)
    grad_x = dh[None]

    full_shapes = [tuple(small_g[n].shape) for n in SMALL]
    small_handle, small_token = exchange_start("xstart_small", "gather", [_pack_rows([small_g[n] for n in SMALL])])
    big = {}
    for handle, names in scatters[:-1]:
        for n, landed in zip(names, exchange_wait(handle, small_token)):
            big.setdefault(n, []).insert(0, landed)
    grads, deltas, new_m, new_v = {}, {}, {}, {}
    last_handle, last_names = scatters[-1]
    for n in BIG:
        if n not in last_names:
            grads[n], deltas[n], new_m[n], new_v[n] = adam_sum("adam_" + n, big[n], p[n], p['m_' + n], p['v_' + n])
    for n, landed in zip(last_names, exchange_wait(last_handle, new_v['ffn_w_up'])):
        grads[n], deltas[n], new_m[n], new_v[n] = adam_sum("adam_" + n, [landed], p[n], p['m_' + n], p['v_' + n])
    partials = exchange_wait(small_handle, new_v[last_names[0]])[0]
    summed = _unpack_rows(sum_parts("sum_small_grads", partials), full_shapes)
    mine = []
    for n, g in zip(SMALL, summed):
        if n in SMALL_SHARDED:
            ax = SMALL_SHARDED[n]
            g = lax.dynamic_slice_in_dim(g, my_slot * p[n].shape[ax], p[n].shape[ax], axis=ax)
        mine.append(g)
    shapes = [p[n].shape for n in SMALL]
    packed = [_pack_rows(arrs) for arrs in ([p[n] for n in SMALL], mine, [p['m_' + n] for n in SMALL], [p['v_' + n] for n in SMALL])]
    d_s, m_s, v_s = adam_flat("adam_small", *packed)
    for n, g, d, m, v in zip(SMALL, mine, _unpack_rows(d_s, shapes), _unpack_rows(m_s, shapes), _unpack_rows(v_s, shapes)):
        grads[n], deltas[n], new_m[n], new_v[n] = g, d, m, v

    return (loss, grad_x, *[grads[n] for n in WEIGHTS], *[deltas[n] for n in WEIGHTS], *[new_m[n] for n in WEIGHTS],
            *[new_v[n] for n in WEIGHTS])


def kernel(x, positions, mix_norm, ffn_norm, fox_w_in, fox_b_f, fox_q_gain, fox_k_gain, fox_w_out, mla_w_in, mla_q_a_gain, mla_kv_a_gain, mla_w_q_b, mla_w_kv_b, mla_q_gain, mla_k_gain, mla_w_out, sb_w_in, sb_q_gain, sb_k_gain, sb_w_out, sgu_w_in, sgu_v_gain, sgu_w_s, sgu_b_s, sgu_w_out, ffn_w_up, ffn_conv_w, ffn_conv_b, ffn_w_down, loss_target, m_mix_norm, m_ffn_norm, m_fox_w_in, m_fox_b_f, m_fox_q_gain, m_fox_k_gain, m_fox_w_out, m_mla_w_in, m_mla_q_a_gain, m_mla_kv_a_gain, m_mla_w_q_b, m_mla_w_kv_b, m_mla_q_gain, m_mla_k_gain, m_mla_w_out, m_sb_w_in, m_sb_q_gain, m_sb_k_gain, m_sb_w_out, m_sgu_w_in, m_sgu_v_gain, m_sgu_w_s, m_sgu_b_s, m_sgu_w_out, m_ffn_w_up, m_ffn_conv_w, m_ffn_conv_b, m_ffn_w_down, v_mix_norm, v_ffn_norm, v_fox_w_in, v_fox_b_f, v_fox_q_gain, v_fox_k_gain, v_fox_w_out, v_mla_w_in, v_mla_q_a_gain, v_mla_kv_a_gain, v_mla_w_q_b, v_mla_w_kv_b, v_mla_q_gain, v_mla_k_gain, v_mla_w_out, v_sb_w_in, v_sb_q_gain, v_sb_k_gain, v_sb_w_out, v_sgu_w_in, v_sgu_v_gain, v_sgu_w_s, v_sgu_b_s, v_sgu_w_out, v_ffn_w_up, v_ffn_conv_w, v_ffn_conv_b, v_ffn_w_down):
    args = locals()
    names = ['x', 'positions'] + WEIGHTS + ['loss_target'] + ['m_' + n for n in WEIGHTS] + ['v_' + n for n in WEIGHTS]
    return _train_step({n: args[n] for n in names})
```

```python
import functools
import math

import jax
import jax.numpy as jnp
from jax import lax
from jax.experimental import pallas as pl
from jax.experimental.pallas import tpu as pltpu

F32 = jnp.float32
BF16 = jnp.bfloat16
MESH = pl.DeviceIdType.MESH

N_DEV = 8
N_HEADS = 16
HEAD_DIM = 128
EPS = 1e-6
DEPTH = 4
D_FF = 5632
MLA_ROPE = 64
ROPE_THETA = 10000.0
VMEM_LIMIT = 48 * 1024 * 1024

ADAM_LR, ADAM_B1, ADAM_B2, ADAM_EPS, ADAM_WD, ADAM_STEP = 0.001, 0.9, 0.999, 1e-08, 0.01, 10

WEIGHTS = ['mix_norm', 'ffn_norm', 'fox_w_in', 'fox_b_f', 'fox_q_gain', 'fox_k_gain', 'fox_w_out', 'mla_w_in',
           'mla_q_a_gain', 'mla_kv_a_gain', 'mla_w_q_b', 'mla_w_kv_b', 'mla_q_gain', 'mla_k_gain', 'mla_w_out',
           'sb_w_in', 'sb_q_gain', 'sb_k_gain', 'sb_w_out', 'sgu_w_in', 'sgu_v_gain', 'sgu_w_s', 'sgu_b_s',
           'sgu_w_out', 'ffn_w_up', 'ffn_conv_w', 'ffn_conv_b', 'ffn_w_down']
BIG = ['fox_w_in', 'fox_w_out', 'mla_w_in', 'mla_w_q_b', 'mla_w_kv_b', 'mla_w_out', 'sb_w_in', 'sb_w_out',
       'sgu_w_in', 'sgu_w_out', 'ffn_w_up', 'ffn_w_down']
SMALL = [w for w in WEIGHTS if w not in BIG]
SMALL_SHARDED = {'mla_q_a_gain': 1, 'mla_kv_a_gain': 1, 'sgu_v_gain': 1, 'ffn_conv_w': 2}


def _cparams(n_grid):
    return pltpu.CompilerParams(dimension_semantics=("arbitrary",) * n_grid, vmem_limit_bytes=VMEM_LIMIT)


def _pick(n, cap):
    best = None
    t = 128
    while t <= min(n, cap):
        if n % t == 0:
            best = t
        t += 128
    return best if best is not None else n


def _mm_call(name, a, b, out_shape, a_spec, b_spec, o_spec, grid, dims, acc_shape, add=None):
    nk = grid[2]

    def body(*refs):
        a_ref, b_ref = refs[:2]
        add_ref = refs[2] if add is not None else None
        o_ref = refs[3] if add is not None else refs[2]
        prod = lax.dot_general(a_ref[...].astype(BF16), b_ref[...].astype(BF16), (dims, ((), ())),
                               preferred_element_type=F32)

        def finish(r):
            if add_ref is not None:
                r = r + add_ref[...].astype(F32)
            o_ref[...] = r.astype(o_ref.dtype)

        if nk == 1:
            finish(prod)
            return
        acc = refs[-1]
        k = pl.program_id(2)

        @pl.when(k == 0)
        def _():
            acc[...] = prod

        @pl.when(k > 0)
        def _():
            acc[...] += prod

        @pl.when(k == nk - 1)
        def _():
            finish(acc[...])

    ins = [a, b] + ([] if add is None else [add])
    in_specs = [a_spec, b_spec] + ([] if add is None else [o_spec])
    return pl.pallas_call(body, grid=grid, in_specs=in_specs, out_specs=o_spec, out_shape=out_shape,
                          scratch_shapes=[] if nk == 1 else [pltpu.VMEM(acc_shape, F32)], name=name,
                          compiler_params=_cparams(3))(*ins)


def mm_nn(a, b3, out_dtype, name, add=None, joff=0, nj=None):
    m, kk = a.shape
    _, kb, n = b3.shape
    assert kb == kk
    nj = b3.shape[0] - joff if nj is None else nj
    tn, tk = _pick(n, 1536), _pick(kk, 2048)
    tm = _pick(m, 512 if a.dtype == F32 and tk > 1024 else 1024)
    nb = n // tn
    return _mm_call(
        name, a, b3, jax.ShapeDtypeStruct((m, nj * n), out_dtype),
        pl.BlockSpec((tm, tk), lambda i, c, k: (i, k)),
        pl.BlockSpec((None, tk, tn), lambda i, c, k: (joff + c // nb, k, c % nb)),
        pl.BlockSpec((tm, tn), lambda i, c, k: (i, c)),
        (m // tm, nj * nb, kk // tk), ((1,), (0,)), (tm, tn), add=add)


def mm_nt(a, b3, out_dtype, name, add=None, joff=0, nj=None):
    m, na = a.shape
    _, ko, n = b3.shape
    nj = b3.shape[0] - joff if nj is None else nj
    assert na == nj * n
    to, tn = _pick(ko, 1024), _pick(n, 2048)
    tm = _pick(m, 512 if a.dtype == F32 and tn > 1024 else 1024)
    nb = n // tn
    return _mm_call(
        name, a, b3, jax.ShapeDtypeStruct((m, ko), out_dtype),
        pl.BlockSpec((tm, tn), lambda i, o, c: (i, c)),
        pl.BlockSpec((None, to, tn), lambda i, o, c: (joff + c // nb, o, c % nb)),
        pl.BlockSpec((tm, to), lambda i, o, c: (i, o)),
        (m // tm, ko // to, nj * nb), ((1,), (1,)), (tm, to), add=add)


def mm_tn(a, b, nj, out_dtype, name):
    s, ko = a.shape
    sb, nb_tot = b.shape
    assert sb == s and nb_tot % nj == 0
    n = nb_tot // nj
    to, tn = _pick(ko, 1024), _pick(n, 1536)
    ts = _pick(s, 1024 if F32 in (a.dtype, b.dtype) else 2048)
    nb = n // tn
    return _mm_call(
        name, a, b, jax.ShapeDtypeStruct((nj, ko, n), out_dtype),
        pl.BlockSpec((ts, to), lambda o, c, k: (k, o)),
        pl.BlockSpec((ts, tn), lambda o, c, k: (k, c)),
        pl.BlockSpec((None, to, tn), lambda o, c, k: (c // nb, o, c % nb)),
        (ko // to, nj * nb, s // ts), ((0,), (0,)), (to, tn))


def blockk(name, fn, grid, ins, outs):
    n_in = len(ins)
    accs = [o[2] for o in outs]

    def body(*refs):
        vals = fn(*[r[...] for r in refs[:n_in]])
        if not isinstance(vals, (tuple, list)):
            vals = (vals,)
        i, j = pl.program_id(0), pl.program_id(1)
        for r, v, acc in zip(refs[n_in:], vals, accs):
            if acc is None:
                r[...] = v.astype(r.dtype)
            else:
                first = (j == 0) if acc == 'inner' else jnp.logical_and(i == 0, j == 0)

                @pl.when(first)
                def _(r=r, v=v):
                    r[...] = v.astype(r.dtype)

                @pl.when(jnp.logical_not(first))
                def _(r=r, v=v):
                    r[...] += v.astype(r.dtype)

    res = pl.pallas_call(body, grid=grid, in_specs=[s for _, s in ins], out_specs=[o[1] for o in outs],
                         out_shape=[o[0] for o in outs], name=name, compiler_params=_cparams(2))(*[a for a, _ in ins])
    return res


def _sds(shape, dtype):
    return jax.ShapeDtypeStruct(tuple(shape), dtype)


def _rows(tr, w, col=0):
    if col == 'j':
        return pl.BlockSpec((tr, w), lambda i, j: (i, j))
    if callable(col):
        return pl.BlockSpec((tr, w), lambda i, j: (i, col(j)))
    return pl.BlockSpec((tr, w), lambda i, j: (i, col))


def _whole(shape):
    nd = len(shape)
    return pl.BlockSpec(tuple(shape), lambda i, j: (0,) * nd)


def _rms(x, g, n):
    ms = jnp.sum(x * x, axis=-1, keepdims=True) * (1.0 / n)
    return x * lax.rsqrt(ms + EPS) * g


def _sig(x):
    return 1.0 / (1.0 + jnp.exp(-x))


def _gelu(x):
    return 0.5 * x * (1.0 + jnp.tanh(math.sqrt(2.0 / math.pi) * (x + 0.044715 * (x * x * x))))


def _lane_iota(shape):
    return lax.broadcasted_iota(jnp.int32, shape, len(shape) - 1)


def _rope(x, cos, sin):
    lane = _lane_iota(x.shape)
    half = MLA_ROPE // 2
    swapped = jnp.where(lane < half, pltpu.roll(x, HEAD_DIM - half, 1), pltpu.roll(x, half, 1))
    sign = jnp.where(lane < half, -1.0, 1.0)
    return x * cos + swapped * (sin * sign)


def _rope_t(dy, cos, sin):
    lane = _lane_iota(dy.shape)
    half = MLA_ROPE // 2
    t = dy * sin
    swapped = jnp.where(lane < half, pltpu.roll(t, HEAD_DIM - half, 1), pltpu.roll(t, half, 1))
    sign = jnp.where(lane < half, 1.0, -1.0)
    return dy * cos + swapped * sign


def rms_fwd(name, x, gain, n, tr, width, xcol=0, nh=1, out_dtype=BF16, out_cols=None):
    r = x.shape[0]
    tr = r if nh > 1 else tr
    out_cols = width * nh if out_cols is None else out_cols
    xspec = _rows(tr, width, (lambda j: xcol + j) if nh > 1 else xcol)
    ospec = _rows(tr, width, 'j' if nh > 1 else 0)
    return blockk(name, lambda xb, g: _rms(xb.astype(F32), g, n), (r // tr, nh),
                  [(x, xspec), (gain, _whole(gain.shape))], [(_sds((r, out_cols), out_dtype), ospec, None)])[0]


def rms_bwd(name, x, gain, dy, n, tr, width, xcol=0, nh=1, dycol=0, resid=None, out_dtype=F32, also_bf16=False):
    r = x.shape[0]
    tr = r if nh > 1 else tr
    xspec = _rows(tr, width, (lambda j: xcol + j) if nh > 1 else xcol)
    dyspec = _rows(tr, width, (lambda j: dycol + j) if nh > 1 else dycol)
    ospec = _rows(tr, width, 'j' if nh > 1 else 0)

    def fn(xb, g, dyb, *rest):
        _, vjp = jax.vjp(lambda a, b: _rms(a, b, n), xb.astype(F32), g)
        dx, dg = vjp(dyb.astype(F32))
        if rest:
            dx = dx + rest[0].astype(F32)
        return ((dx,) if also_bf16 else ()) + (dx, dg)

    ins = [(x, xspec), (gain, _whole(gain.shape)), (dy, dyspec)]
    if resid is not None:
        ins.append((resid, ospec))
    outs = [(_sds((r, width * nh), out_dtype), ospec, None), (_sds(gain.shape, F32), _whole(gain.shape), 'all')]
    if also_bf16:
        outs.insert(0, (_sds((r, width * nh), BF16), ospec, None))
    return blockk(name, fn, (r // tr, nh), ins, outs)


def _nt(a, b):
    return lax.dot_general(a, b, (((1,), (1,)), ((), ())), preferred_element_type=F32)


def _tn(a, b):
    return lax.dot_general(a, b, (((0,), (0,)), ((), ())), preferred_element_type=F32)


def _nn(a, b):
    return lax.dot_general(a, b, (((1,), (0,)), ((), ())), preferred_element_type=F32)


ATTN_BLOCK = 512
ATTN_HEADS = 2
ATTN_GRID_HEADS = N_HEADS // ATTN_HEADS


def _head_cols(off):
    assert off % ATTN_HEADS == 0
    return off // ATTN_HEADS


def _hd(ref, hh):
    return ref[:, hh * HEAD_DIM:(hh + 1) * HEAD_DIM]


def _attn_specs(tq, qoff, koff, voff, extra, bias, q2off):
    w = HEAD_DIM * ATTN_HEADS
    qc, kc, vc, q2c = (_head_cols(o) for o in (qoff, koff, voff, q2off))
    specs = [pl.BlockSpec((tq, w), lambda h, i, j: (i, qc + h)),
             pl.BlockSpec((tq, w), lambda h, i, j: (jnp.minimum(i, j), kc + h)),
             pl.BlockSpec((tq, w), lambda h, i, j: (jnp.minimum(i, j), vc + h))]
    if extra:
        specs += [pl.BlockSpec((tq, w), lambda h, i, j: (i, q2c + h)),
                  pl.BlockSpec((tq, HEAD_DIM), lambda h, i, j: (jnp.minimum(i, j), 0))]
    if bias:
        specs += [pl.BlockSpec((ATTN_HEADS, tq, 1), lambda h, i, j: (h, i, 0)),
                  pl.BlockSpec((ATTN_HEADS, 1, tq), lambda h, i, j: (h, 0, jnp.minimum(i, j)))]
    return specs


def _scores(q, k, q2, k2, cc, cr, scale, tq, diagonal):
    s = _nt(q.astype(BF16), k.astype(BF16))
    if q2 is not None:
        s = s + _nt(q2.astype(BF16), k2.astype(BF16))
    s = s * scale
    if cc is not None:
        s = s + (cc - cr)
    if not diagonal:
        return s, None
    return s, lax.broadcasted_iota(jnp.int32, (tq, tq), 1) <= lax.broadcasted_iota(jnp.int32, (tq, tq), 0)


def _on_blocks(qi, kj, step):
    @pl.when(kj < qi)
    def _():
        step(False)

    @pl.when(kj == qi)
    def _():
        step(True)


def attn_fwd(name, q, k, v, scale, *, qoff=0, koff=0, voff=0, q2=None, k2=None, q2off=0, cum=None, tq=ATTN_BLOCK,
             exact_o=False):
    s_len = q.shape[0]
    nq = s_len // tq
    extra, bias = q2 is not None, cum is not None
    n_in = 3 + 2 * extra + 2 * bias

    def body(*refs):
        q_ref, k_ref, v_ref = refs[:3]
        p = 3
        q2_ref = k2_ref = cc_ref = cr_ref = None
        if extra:
            q2_ref, k2_ref = refs[p:p + 2]
            p += 2
        if bias:
            cc_ref, cr_ref = refs[p:p + 2]
            p += 2
        o_ref, lse_ref, m_s, l_s, acc_s = refs[p:]
        qi, kj = pl.program_id(1), pl.program_id(2)

        @pl.when(kj == 0)
        def _():
            m_s[...] = jnp.full_like(m_s, -jnp.inf)
            l_s[...] = jnp.zeros_like(l_s)
            acc_s[...] = jnp.zeros_like(acc_s)

        def step(diagonal):
            for hh in range(ATTN_HEADS):
                s, allowed = _scores(_hd(q_ref, hh), _hd(k_ref, hh), _hd(q2_ref, hh) if extra else None,
                                     k2_ref[...] if extra else None, cc_ref[hh] if bias else None,
                                     cr_ref[hh] if bias else None, scale, tq, diagonal)
                if diagonal:
                    s = jnp.where(allowed, s, -jnp.inf)
                m_old = m_s[hh]
                m_new = jnp.maximum(m_old, jnp.max(s, axis=-1, keepdims=True))
                alpha = jnp.exp(m_old - m_new)
                pr = jnp.exp(s - m_new)
                l_s[hh] = alpha * l_s[hh] + jnp.sum(pr, axis=-1, keepdims=True)
                vb = _hd(v_ref, hh).astype(BF16)
                pv = _nn(pr.astype(BF16), vb)
                if exact_o:
                    pv = pv + _nn((pr - pr.astype(BF16).astype(F32)).astype(BF16), vb)
                acc_s[hh] = alpha * acc_s[hh] + pv
                m_s[hh] = m_new

        _on_blocks(qi, kj, step)

        @pl.when(kj == qi)
        def _():
            for hh in range(ATTN_HEADS):
                o_ref[:, hh * HEAD_DIM:(hh + 1) * HEAD_DIM] = (acc_s[hh] / l_s[hh]).astype(o_ref.dtype)
                lse_ref[hh] = m_s[hh] + jnp.log(l_s[hh])

    ins = [q, k, v] + ([q2, k2] if extra else []) + (list(cum) if bias else [])
    d, w = HEAD_DIM, HEAD_DIM * ATTN_HEADS
    return pl.pallas_call(
        body, grid=(ATTN_GRID_HEADS, nq, nq), in_specs=_attn_specs(tq, qoff, koff, voff, extra, bias, q2off),
        out_specs=[pl.BlockSpec((tq, w), lambda h, i, j: (i, h)),
                   pl.BlockSpec((ATTN_HEADS, tq, 1), lambda h, i, j: (h, i, 0))],
        out_shape=[_sds((s_len, N_HEADS * d), F32 if exact_o else BF16), _sds((N_HEADS, s_len, 1), F32)],
        scratch_shapes=[pltpu.VMEM((ATTN_HEADS, tq, 1), F32), pltpu.VMEM((ATTN_HEADS, tq, 1), F32),
                        pltpu.VMEM((ATTN_HEADS, tq, d), F32)],
        name=name, compiler_params=_cparams(3))(*ins)


def attn_bwd(name, q, k, v, o, do, lse, scale, *, qoff=0, koff=0, voff=0, q2=None, k2=None, q2off=0, cum=None,
             tq=ATTN_BLOCK):
    s_len = q.shape[0]
    nq = s_len // tq
    extra, bias = q2 is not None, cum is not None
    d = HEAD_DIM
    n_in = 6 + 2 * extra + 2 * bias

    def body(*refs):
        q_ref, k_ref, v_ref = refs[:3]
        p = 3
        q2_ref = k2_ref = cc_ref = cr_ref = None
        if extra:
            q2_ref, k2_ref = refs[p:p + 2]
            p += 2
        if bias:
            cc_ref, cr_ref = refs[p:p + 2]
            p += 2
        o_ref, do_ref, lse_ref = refs[p:p + 3]
        p += 3
        dq_ref, dk_ref, dv_ref = refs[p:p + 3]
        p += 3
        dq2_ref = dk2_ref = dcs_ref = None
        if extra:
            dq2_ref, dk2_ref = refs[p:p + 2]
            p += 2
        if bias:
            dcs_ref = refs[p]
            p += 1
        dq_s, delta_s = refs[p:p + 2]
        dq2_s = refs[p + 2] if extra else None
        h, qi, kj = pl.program_id(0), pl.program_id(1), pl.program_id(2)

        @pl.when(jnp.logical_and(qi == 0, kj == 0))
        def _():
            dk_ref[...] = jnp.zeros_like(dk_ref)
            dv_ref[...] = jnp.zeros_like(dv_ref)

        if extra:
            @pl.when(jnp.logical_and(h == 0, jnp.logical_and(qi == 0, kj == 0)))
            def _():
                dk2_ref[...] = jnp.zeros_like(dk2_ref)

        @pl.when(kj == 0)
        def _():
            dq_s[...] = jnp.zeros_like(dq_s)
            if extra:
                dq2_s[...] = jnp.zeros_like(dq2_s)
            for hh in range(ATTN_HEADS):
                delta_s[hh] = jnp.sum(_hd(do_ref, hh).astype(F32) * _hd(o_ref, hh).astype(F32), axis=-1, keepdims=True)

        if bias:
            @pl.when(kj > qi)
            def _():
                dcs_ref[...] = jnp.zeros_like(dcs_ref)

        def step(diagonal):
            ks = pl.ds(pl.multiple_of(kj * tq, tq), tq)
            for hh in range(ATTN_HEADS):
                cols = slice(hh * HEAD_DIM, (hh + 1) * HEAD_DIM)
                qh, kh = _hd(q_ref, hh).astype(BF16), _hd(k_ref, hh).astype(BF16)
                q2h = _hd(q2_ref, hh).astype(BF16) if extra else None
                k2h = k2_ref[...].astype(BF16) if extra else None
                s, allowed = _scores(qh, kh, q2h, k2h, cc_ref[hh] if bias else None, cr_ref[hh] if bias else None,
                                     scale, tq, diagonal)
                pr = jnp.exp(s - lse_ref[hh])
                if diagonal:
                    pr = jnp.where(allowed, pr, 0.0)
                dob = _hd(do_ref, hh).astype(BF16)
                dp = _nt(dob, _hd(v_ref, hh).astype(BF16))
                ds = pr * (dp - delta_s[hh])
                dsb = (ds * scale).astype(BF16)
                dq_s[hh] += _nn(dsb, kh)
                dk_ref[ks, cols] += _tn(dsb, qh)
                dv_ref[ks, cols] += _tn(pr.astype(BF16), dob)
                if extra:
                    dq2_s[hh] += _nn(dsb, k2h)
                    dk2_ref[ks, :] += _tn(dsb, q2h)
                if bias:
                    dcs_ref[hh] = jnp.sum(ds, axis=0, keepdims=True)

        _on_blocks(qi, kj, step)

        @pl.when(kj == qi)
        def _():
            for hh in range(ATTN_HEADS):
                cols = slice(hh * HEAD_DIM, (hh + 1) * HEAD_DIM)
                dq_ref[:, cols] = dq_s[hh]
                if extra:
                    dq2_ref[:, cols] = dq2_s[hh]

    w = HEAD_DIM * ATTN_HEADS
    ins = [q, k, v] + ([q2, k2] if extra else []) + (list(cum) if bias else []) + [o, do, lse]
    in_specs = _attn_specs(tq, qoff, koff, voff, extra, bias, q2off) + [
        pl.BlockSpec((tq, w), lambda h, i, j: (i, h)), pl.BlockSpec((tq, w), lambda h, i, j: (i, h)),
        pl.BlockSpec((ATTN_HEADS, tq, 1), lambda h, i, j: (h, i, 0))]
    full = _sds((s_len, N_HEADS * d), F32)
    out_shape = [full, full, full]
    out_specs = [pl.BlockSpec((tq, w), lambda h, i, j: (i, h)), pl.BlockSpec((s_len, w), lambda h, i, j: (0, h)),
                 pl.BlockSpec((s_len, w), lambda h, i, j: (0, h))]
    scratch = [pltpu.VMEM((ATTN_HEADS, tq, d), F32), pltpu.VMEM((ATTN_HEADS, tq, 1), F32)]
    if extra:
        out_shape += [full, _sds((s_len, d), F32)]
        out_specs += [pl.BlockSpec((tq, w), lambda h, i, j: (i, h)), pl.BlockSpec((s_len, d), lambda h, i, j: (0, 0))]
        scratch.append(pltpu.VMEM((ATTN_HEADS, tq, d), F32))
    if bias:
        out_shape.append(_sds((N_HEADS, nq, 1, s_len), F32))
        out_specs.append(pl.BlockSpec((ATTN_HEADS, None, 1, tq), lambda h, i, j: (h, i, 0, j)))
    return pl.pallas_call(body, grid=(ATTN_GRID_HEADS, nq, nq), in_specs=in_specs, out_specs=out_specs,
                          out_shape=out_shape, scratch_shapes=scratch, name=name, compiler_params=_cparams(3))(*ins)


def _sb_terms(q, k, scale, tq, diagonal):
    z = _nt(q.astype(BF16), k.astype(BF16)) * scale
    lg = jnp.log(1.0 + jnp.exp(-jnp.abs(z)))
    log_keep = -(jnp.maximum(z, 0.0) + lg)
    log_beta = jnp.minimum(z, 0.0) - lg
    if not diagonal:
        return None, log_keep, log_beta
    strict = lax.broadcasted_iota(jnp.int32, (tq, tq), 1) < lax.broadcasted_iota(jnp.int32, (tq, tq), 0)
    return strict, jnp.where(strict, log_keep, 0.0), log_beta


def _tri(tq, pred):
    a = lax.broadcasted_iota(jnp.int32, (tq, tq), 0)
    b = lax.broadcasted_iota(jnp.int32, (tq, tq), 1)
    return jnp.where(pred(a, b), 1.0, 0.0).astype(BF16)


SUM_CHUNK = 256


def _lane_sums(x, later):
    n = x.shape[1]
    chunk = min(SUM_CHUNK, n)
    tri = _tri(chunk, (lambda m, j: m > j) if later else (lambda m, j: m < j))
    order = range(n // chunk - 1, -1, -1) if later else range(n // chunk)
    pieces, carry = [None] * (n // chunk), None
    for cidx in order:
        xc = x[:, cidx * chunk:(cidx + 1) * chunk]
        hi = xc.astype(BF16)
        local = _nn(hi, tri) + _nn((xc - hi.astype(F32)).astype(BF16), tri)
        total = jnp.sum(xc, axis=-1, keepdims=True)
        pieces[cidx] = local if carry is None else local + carry
        carry = total if carry is None else carry + total
    return jnp.concatenate(pieces, axis=1), carry


def sb_fwd(name, qn, kn, qkv, scale, voff, tq=ATTN_BLOCK):
    s_len = qn.shape[0]
    nq = s_len // tq
    d = HEAD_DIM

    def body(q_ref, k_ref, v_ref, o_ref, car_ref, ca_s, acc_s):
        qi, kj = pl.program_id(1), pl.program_id(2)

        @pl.when(kj == 0)
        def _():
            ca_s[...] = jnp.zeros_like(ca_s)
            acc_s[...] = jnp.zeros_like(acc_s)

        def step(diagonal):
            for hh in range(ATTN_HEADS):
                strict, log_keep, log_beta = _sb_terms(_hd(q_ref, hh), _hd(k_ref, hh), scale, tq, diagonal)
                ca = ca_s[hh]
                car_ref[hh] = ca
                after, total = _lane_sums(log_keep, True)
                a = jnp.exp(log_beta + (after + ca))
                if diagonal:
                    a = jnp.where(strict, a, 0.0)
                acc_s[hh] += _nn(a.astype(BF16), _hd(v_ref, hh).astype(BF16))
                ca_s[hh] = ca + total

        @pl.when(kj == 0)
        def _():
            step(True)

        @pl.when(jnp.logical_and(kj > 0, kj <= qi))
        def _():
            step(False)

        @pl.when(kj == qi)
        def _():
            for hh in range(ATTN_HEADS):
                o_ref[:, hh * HEAD_DIM:(hh + 1) * HEAD_DIM] = acc_s[hh].astype(o_ref.dtype)

    kblk = lambda i, j: jnp.maximum(i - j, 0)
    w, vc = HEAD_DIM * ATTN_HEADS, _head_cols(voff)
    return pl.pallas_call(
        body, grid=(ATTN_GRID_HEADS, nq, nq),
        in_specs=[pl.BlockSpec((tq, w), lambda h, i, j: (i, h)), pl.BlockSpec((tq, w), lambda h, i, j: (kblk(i, j), h)),
                  pl.BlockSpec((tq, w), lambda h, i, j: (kblk(i, j), vc + h))],
        out_specs=[pl.BlockSpec((tq, w), lambda h, i, j: (i, h)),
                   pl.BlockSpec((ATTN_HEADS, None, tq, 1), lambda h, i, j: (h, kblk(i, j), i, 0))],
        out_shape=[_sds((s_len, N_HEADS * d), BF16), _sds((N_HEADS, nq, s_len, 1), F32)],
        scratch_shapes=[pltpu.VMEM((ATTN_HEADS, tq, 1), F32), pltpu.VMEM((ATTN_HEADS, tq, d), F32)],
        name=name, compiler_params=_cparams(3))(qn, kn, qkv)


def sb_bwd(name, qn, kn, qkv, do, carries, scale, voff, tq=ATTN_BLOCK):
    s_len = qn.shape[0]
    nq = s_len // tq
    d = HEAD_DIM

    def body(q_ref, k_ref, v_ref, do_ref, car_ref, dq_ref, dk_ref, dv_ref, dq_s, cg_s):
        qi, kj = pl.program_id(1), pl.program_id(2)

        @pl.when(jnp.logical_and(qi == 0, kj == 0))
        def _():
            dk_ref[...] = jnp.zeros_like(dk_ref)
            dv_ref[...] = jnp.zeros_like(dv_ref)

        @pl.when(kj == 0)
        def _():
            dq_s[...] = jnp.zeros_like(dq_s)
            cg_s[...] = jnp.zeros_like(cg_s)

        def step(diagonal):
            ks = pl.ds(pl.multiple_of(kj * tq, tq), tq)
            for hh in range(ATTN_HEADS):
                cols = slice(hh * HEAD_DIM, (hh + 1) * HEAD_DIM)
                qh, kh = _hd(q_ref, hh).astype(BF16), _hd(k_ref, hh).astype(BF16)
                strict, log_keep, log_beta = _sb_terms(qh, kh, scale, tq, diagonal)
                after, _ = _lane_sums(log_keep, True)
                a = jnp.exp(log_beta + (after + car_ref[hh]))
                if diagonal:
                    a = jnp.where(strict, a, 0.0)
                dob = _hd(do_ref, hh).astype(BF16)
                g = a * _nt(dob, _hd(v_ref, hh).astype(BF16))
                cg = cg_s[hh]
                before, total = _lane_sums(g, False)
                big_g = before + cg
                cg_s[hh] = cg + total
                beta = jnp.exp(log_beta)
                dz = g * (1.0 - beta) - big_g * beta
                if diagonal:
                    dz = jnp.where(strict, dz, 0.0)
                dzb = (dz * scale).astype(BF16)
                dq_s[hh] += _nn(dzb, kh)
                dk_ref[ks, cols] += _tn(dzb, qh)
                dv_ref[ks, cols] += _tn(a.astype(BF16), dob)

        _on_blocks(qi, kj, step)

        @pl.when(kj == qi)
        def _():
            for hh in range(ATTN_HEADS):
                dq_ref[:, hh * HEAD_DIM:(hh + 1) * HEAD_DIM] = dq_s[hh]

    kblk = lambda i, j: jnp.minimum(i, j)
    full = _sds((s_len, N_HEADS * d), F32)
    w, vc = HEAD_DIM * ATTN_HEADS, _head_cols(voff)
    return pl.pallas_call(
        body, grid=(ATTN_GRID_HEADS, nq, nq),
        in_specs=[pl.BlockSpec((tq, w), lambda h, i, j: (i, h)), pl.BlockSpec((tq, w), lambda h, i, j: (kblk(i, j), h)),
                  pl.BlockSpec((tq, w), lambda h, i, j: (kblk(i, j), vc + h)),
                  pl.BlockSpec((tq, w), lambda h, i, j: (i, h)),
                  pl.BlockSpec((ATTN_HEADS, None, tq, 1), lambda h, i, j: (h, kblk(i, j), i, 0))],
        out_specs=[pl.BlockSpec((tq, w), lambda h, i, j: (i, h)), pl.BlockSpec((s_len, w), lambda h, i, j: (0, h)),
                   pl.BlockSpec((s_len, w), lambda h, i, j: (0, h))],
        out_shape=[full, full, full],
        scratch_shapes=[pltpu.VMEM((ATTN_HEADS, tq, d), F32), pltpu.VMEM((ATTN_HEADS, tq, 1), F32)],
        name=name, compiler_params=_cparams(3))(qn, kn, qkv, do, carries)


def _cumsum_rows(x, reverse):
    n = x.shape[0] // HEAD_DIM
    tri = _tri(HEAD_DIM, (lambda a, b: b >= a) if reverse else (lambda a, b: b <= a))
    pieces = [None] * n
    carry = jnp.zeros((1, HEAD_DIM), F32)
    order = range(n - 1, -1, -1) if reverse else range(n)
    for blk in order:
        xb = x[blk * HEAD_DIM:(blk + 1) * HEAD_DIM, :]
        x1 = xb.astype(BF16)
        r1 = xb - x1.astype(F32)
        x2 = r1.astype(BF16)
        x3 = (r1 - x2.astype(F32)).astype(BF16)
        c = _nn(tri, x1) + _nn(tri, x2) + _nn(tri, x3) + carry
        pieces[blk] = c
        carry = c[0:1, :] if reverse else c[HEAD_DIM - 1:HEAD_DIM, :]
    return jnp.concatenate(pieces, axis=0)


def _log_sigmoid(x):
    return jnp.minimum(x, 0.0) - jnp.log(1.0 + jnp.exp(-jnp.abs(x)))


def fox_gate_fwd(name, fl, bf):
    return blockk(name, lambda f, b: _cumsum_rows(_log_sigmoid(f + b), False), (1, 1),
                  [(fl, _whole(fl.shape)), (bf, _whole(bf.shape))], [(_sds(fl.shape, F32), _whole(fl.shape), None)])[0]


def fox_gate_bwd(name, fl, bf, dcum):
    def fn(f, b, dc):
        dlogf = _cumsum_rows(dc, True)
        dfl = dlogf * _sig(-(f + b))
        return dfl, jnp.sum(dfl, axis=0, keepdims=True)

    return blockk(name, fn, (1, 1), [(fl, _whole(fl.shape)), (bf, _whole(bf.shape)), (dcum, _whole(dcum.shape))],
                  [(_sds(fl.shape, F32), _whole(fl.shape), None), (_sds(bf.shape, F32), _whole(bf.shape), None)])


def rope_fwd(name, x, gain, cos, sin, tr, xcol, nh):
    r = x.shape[0]
    tr = r if nh > 1 else tr
    xspec = _rows(tr, HEAD_DIM, lambda j: xcol + j)
    tspec = _rows(tr, HEAD_DIM, 0)
    return blockk(name, lambda xb, g, c, s: _rope(_rms(xb.astype(F32), g, MLA_ROPE), c, s), (r // tr, nh),
                  [(x, xspec), (gain, _whole(gain.shape)), (cos, tspec), (sin, tspec)],
                  [(_sds((r, HEAD_DIM * nh), BF16), _rows(tr, HEAD_DIM, 'j'), None)])[0]


def rope_bwd(name, x, gain, cos, sin, dy, tr, xcol, nh):
    r = x.shape[0]
    tr = r if nh > 1 else tr
    xspec = _rows(tr, HEAD_DIM, lambda j: xcol + j)
    tspec = _rows(tr, HEAD_DIM, 0)
    ospec = _rows(tr, HEAD_DIM, 'j')

    def fn(xb, g, c, s, dyb):
        _, vjp = jax.vjp(lambda a, b: _rms(a, b, MLA_ROPE), xb.astype(F32), g)
        return vjp(_rope_t(dyb.astype(F32), c, s))

    return blockk(name, fn, (r // tr, nh),
                  [(x, xspec), (gain, _whole(gain.shape)), (cos, tspec), (sin, tspec), (dy, ospec)],
                  [(_sds((r, HEAD_DIM * nh), F32), ospec, None), (_sds(gain.shape, F32), _whole(gain.shape), 'all')])


def sgu_pre_fwd(name, pre, gain, tr):
    s_len, w2 = pre.shape
    w = w2 // 2
    return blockk(name, lambda pu, pv, g: (_gelu(pu.astype(F32)), _rms(_gelu(pv.astype(F32)), g, w)), (s_len // tr, 1),
                  [(pre, _rows(tr, w, 0)), (pre, _rows(tr, w, 1)), (gain, _whole(gain.shape))],
                  [(_sds((s_len, w), BF16), _rows(tr, w, 0), None), (_sds((s_len, w), BF16), _rows(tr, w, 0), None)])


def sgu_pre_bwd(name, pre, gain, du, dvn, tr):
    s_len, w2 = pre.shape
    w = w2 // 2

    def fn(pu, pv, g, dub, dvb):
        _, vjp_u = jax.vjp(_gelu, pu.astype(F32))
        _, vjp_v = jax.vjp(lambda a, b: _rms(_gelu(a), b, w), pv.astype(F32), g)
        dpv, dg = vjp_v(dvb.astype(F32))
        return vjp_u(dub.astype(F32))[0], dpv, dg

    spec = _rows(tr, w, 0)
    return blockk(name, fn, (s_len // tr, 1),
                  [(pre, spec), (pre, _rows(tr, w, 1)), (gain, _whole(gain.shape)), (du, spec), (dvn, spec)],
                  [(_sds((s_len, w), BF16), spec, None), (_sds((s_len, w), BF16), spec, None),
                   (_sds(gain.shape, F32), _whole(gain.shape), 'all')])


def _ws_masked(ws):
    t = ws.shape[0]
    a = lax.broadcasted_iota(jnp.int32, (t, t), 0)
    b = lax.broadcasted_iota(jnp.int32, (t, t), 1)
    return jnp.where(b <= a, ws, 0.0)


def sgu_mix_fwd(name, vn, u, ws, bs3):
    s_len, w = vn.shape
    t = ws.shape[1]

    def fn(vb, ub, wsb, bsb):
        mixed = _nn(_ws_masked(wsb).astype(BF16), vb.astype(BF16)) + bsb
        return ub.astype(F32) * mixed

    blk = pl.BlockSpec((t, t), lambda i, j: (i, j))
    return blockk(name, fn, (s_len // t, w // t),
                  [(vn, blk), (u, blk), (ws, pl.BlockSpec((None, t, t), lambda i, j: (j, 0, 0))),
                   (bs3, pl.BlockSpec((None, t, 1), lambda i, j: (j, 0, 0)))],
                  [(_sds((s_len, w), BF16), blk, None)])[0]


def sgu_mix_bwd(name, vn, u, ws, bs3, dprod):
    s_len, w = vn.shape
    t = ws.shape[1]

    def fn(vb, ub, wsb, bsb, dpb):
        wm = _ws_masked(wsb).astype(BF16)
        vb16 = vb.astype(BF16)
        mixed = _nn(wm, vb16) + bsb
        dp = dpb.astype(F32)
        du = dp * mixed
        dm = dp * ub.astype(F32)
        dmb = dm.astype(BF16)
        dvn = _tn(wm, dmb)
        dws = _ws_masked(_nt(dmb, vb16))
        return du, dvn, dws, jnp.sum(dm, axis=-1, keepdims=True)

    blk = pl.BlockSpec((t, t), lambda g, n: (n, g))
    wspec = pl.BlockSpec((None, t, t), lambda g, n: (g, 0, 0))
    bspec = pl.BlockSpec((None, t, 1), lambda g, n: (g, 0, 0))
    return blockk(name, fn, (w // t, s_len // t), [(vn, blk), (u, blk), (ws, wspec), (bs3, bspec), (dprod, blk)],
                  [(_sds((s_len, w), BF16), blk, None), (_sds((s_len, w), BF16), blk, None),
                   (_sds(ws.shape, F32), wspec, 'inner'), (_sds(bs3.shape, F32), bspec, 'inner')])


def _shift_down(x, k):
    row = lax.broadcasted_iota(jnp.int32, x.shape, 0)
    return jnp.where(row >= k, pltpu.roll(x, k, 0), 0.0)


def _shift_up(x, k):
    n = x.shape[0]
    row = lax.broadcasted_iota(jnp.int32, x.shape, 0)
    return jnp.where(row < n - k, pltpu.roll(x, n - k, 0), 0.0)


def _conv(up, cw, cb):
    return cb + cw[0:1, :] * _shift_down(up, 2) + cw[1:2, :] * _shift_down(up, 1) + cw[2:3, :] * up


def _conv_specs(s_len, tc, nf):
    g = pl.BlockSpec((s_len, tc), lambda i, j: (0, j))
    v = pl.BlockSpec((s_len, tc), lambda i, j: (0, j + nf))
    wg = pl.BlockSpec((3, tc), lambda i, j: (0, j))
    wv = pl.BlockSpec((3, tc), lambda i, j: (0, j + nf))
    bg = pl.BlockSpec((1, tc), lambda i, j: (0, j))
    bv = pl.BlockSpec((1, tc), lambda i, j: (0, j + nf))
    return g, v, wg, wv, bg, bv


def conv_fwd(name, up, cw, cb, tc=256):
    s_len, f2 = up.shape
    f = f2 // 2
    nf = f // tc
    g, v, wg, wv, bg, bv = _conv_specs(s_len, tc, nf)

    def fn(ug, uv, cwg, cwv, cbg, cbv):
        yg = _conv(ug.astype(F32), cwg, cbg)
        yv = _conv(uv.astype(F32), cwv, cbv)
        return yg * _sig(yg) * yv

    return blockk(name, fn, (1, nf), [(up, g), (up, v), (cw, wg), (cw, wv), (cb, bg), (cb, bv)],
                  [(_sds((s_len, f), BF16), g, None)])[0]


def conv_bwd(name, up, cw, cb, dact, tc=128):
    s_len, f2 = up.shape
    f = f2 // 2
    nf = f // tc
    g, v, wg, wv, bg, bv = _conv_specs(s_len, tc, nf)

    def half(dy, u, cwh):
        u1, u2 = _shift_down(u, 1), _shift_down(u, 2)
        dup = cwh[2:3, :] * dy + cwh[1:2, :] * _shift_up(dy, 1) + cwh[0:1, :] * _shift_up(dy, 2)
        dcw = jnp.concatenate([jnp.sum(dy * u2, axis=0, keepdims=True), jnp.sum(dy * u1, axis=0, keepdims=True),
                               jnp.sum(dy * u, axis=0, keepdims=True)], axis=0)
        return dup, dcw, jnp.sum(dy, axis=0, keepdims=True)

    def fn(ug, uv, cwg, cwv, cbg, cbv, da):
        ug, uv, da = ug.astype(F32), uv.astype(F32), da.astype(F32)
        yg = _conv(ug, cwg, cbg)
        yv = _conv(uv, cwv, cbv)
        sg = _sig(yg)
        dyv = da * (yg * sg)
        dyg = da * yv * (sg * (1.0 + yg * (1.0 - sg)))
        dug, dcwg, dcbg = half(dyg, ug, cwg)
        duv, dcwv, dcbv = half(dyv, uv, cwv)
        return dug, duv, dcwg, dcwv, dcbg, dcbv

    return blockk(name, fn, (1, nf), [(up, g), (up, v), (cw, wg), (cw, wv), (cb, bg), (cb, bv), (dact, g)],
                  [(_sds((s_len, f), BF16), g, None), (_sds((s_len, f), BF16), g, None),
                   (_sds((3, f), F32), wg, None), (_sds((3, f), F32), wg, None),
                   (_sds((1, f), F32), bg, None), (_sds((1, f), F32), bg, None)])


def loss_head(name, y, target, tr=256):
    s_len, d = y.shape

    def fn(yb, tb):
        e = yb - tb
        dy = e * (1.0 / d)
        return dy, dy, jnp.sum(e * e, axis=0, keepdims=True)

    spec = _rows(tr, d, 0)
    return blockk(name, fn, (s_len // tr, 1), [(y, spec), (target, spec)],
                  [(_sds((s_len, d), BF16), spec, None), (_sds((s_len, d), F32), spec, None),
                   (_sds((1, d), F32), _whole((1, d)), 'all')])


def _adam(w, g, m, v):
    m = ADAM_B1 * m + (1.0 - ADAM_B1) * g
    v = ADAM_B2 * v + (1.0 - ADAM_B2) * (g * g)
    m_hat = m / (1.0 - ADAM_B1 ** ADAM_STEP)
    v_hat = v / (1.0 - ADAM_B2 ** ADAM_STEP)
    delta = -ADAM_LR * (m_hat / (jnp.sqrt(v_hat) + ADAM_EPS) + ADAM_WD * w)
    return delta, m, v


def adam_sum(name, parts, w, m, v):
    n_layers, r, c = w.shape
    n_parts = parts[0].shape[0]
    assert len(parts) == n_layers
    tr = r
    for cand in (512, 256, 128, 64, 32, 16):
        if r % cand == 0 and cand * c * 4 <= 1024 * 1024:
            tr = cand
            break

    def body(*refs):
        part_refs = refs[:n_layers]
        w_ref, m_ref, v_ref, g_out, d_out, m_out, v_out = refs[n_layers:]
        layer = pl.program_id(0)
        for ll in range(n_layers):
            @pl.when(layer == ll)
            def _(ll=ll):
                g = part_refs[ll][0].astype(F32)
                for k in range(1, n_parts):
                    g = g + part_refs[ll][k].astype(F32)
                delta, m_new, v_new = _adam(w_ref[...], g, m_ref[...], v_ref[...])
                g_out[...] = g
                d_out[...] = delta
                m_out[...] = m_new
                v_out[...] = v_new

    spec = pl.BlockSpec((None, tr, c), lambda l, i: (l, i, 0))
    part_specs = [pl.BlockSpec((n_parts, tr, c), lambda l, i, ll=ll: (0, jnp.where(l == ll, i, 0), 0))
                  for ll in range(n_layers)]
    out = _sds((n_layers, r, c), F32)
    return pl.pallas_call(body, grid=(n_layers, r // tr), in_specs=part_specs + [spec] * 3, out_specs=[spec] * 4,
                          out_shape=[out] * 4, name=name, compiler_params=_cparams(2))(*parts, w, m, v)


def sum_parts(name, parts, tr=256):
    n_parts, r, c = parts.shape

    def fn(pb):
        g = pb[0]
        for k in range(1, n_parts):
            g = g + pb[k]
        return g

    return blockk(name, fn, (r // tr, 1), [(parts, pl.BlockSpec((n_parts, tr, c), lambda i, j: (0, i, 0)))],
                  [(_sds((r, c), F32), _rows(tr, c, 0), None)])[0]


def adam_flat(name, w, g, m, v, tr=256):
    r, c = w.shape
    spec = _rows(tr, c, 0)
    return blockk(name, lambda wb, gb, mb, vb: _adam(wb, gb, mb, vb), (r // tr, 1),
                  [(w, spec), (g, spec), (m, spec), (v, spec)], [(_sds((r, c), F32), spec, None)] * 3)


_ANY = pl.BlockSpec(memory_space=pl.ANY)


def _place():
    return lax.axis_index("x"), lax.axis_index("y"), lax.axis_index("c")


def _slot(px, py, pc):
    return 4 * px + 2 * py + pc


def all_gather(name, items):
    n = len(items)

    def body(*refs):
        xs, outs = refs[:n], refs[n:2 * n]
        send_sems, recv_sems, local_sems = refs[2 * n:]
        x, y, c = _place()
        me, sibling = (x, y, c), (x, y, 1 - c)
        chips = [(1 - x, y), (x, 1 - y), (1 - x, 1 - y)]

        def copy(t, k, block, to, src=None):
            dst = outs[t].at[_slot(*block)]
            return pltpu.make_async_remote_copy(src_ref=dst if src is None else src, dst_ref=dst,
                                                send_sem=send_sems.at[7 * t + k], recv_sem=recv_sems.at[7 * t + k],
                                                device_id=to, device_id_type=MESH)

        mine = [pltpu.make_async_copy(xs[t], outs[t].at[_slot(*me)], local_sems.at[t]) for t in range(n)]
        for cp in mine:
            cp.start()
        started = []
        for t in range(n):
            started.append(copy(t, 0, me, sibling, src=xs[t]))
            started += [copy(t, 1 + j, me, (*chip, c), src=xs[t]) for j, chip in enumerate(chips)]
        for cp in started:
            cp.start()
        for j, chip in enumerate(chips):
            for t in range(n):
                copy(t, 1 + j, (*chip, c), me).wait_recv()
                passed = copy(t, 4 + j, (*chip, c), sibling)
                passed.start()
                started.append(passed)
        for t in range(n):
            copy(t, 0, sibling, me).wait_recv()
            for j, chip in enumerate(chips):
                copy(t, 4 + j, (*chip, 1 - c), me).wait_recv()
        for cp in started:
            cp.wait_send()
        for cp in mine:
            cp.wait()

    return pl.pallas_call(
        body, in_specs=[_ANY] * n, out_specs=[_ANY] * n,
        out_shape=[_sds((N_DEV,) + a.shape, a.dtype) for a in items],
        scratch_shapes=[pltpu.SemaphoreType.DMA((7 * n,)), pltpu.SemaphoreType.DMA((7 * n,)), pltpu.SemaphoreType.DMA((n,))],
        name=name)(*items)


_HBM = pl.BlockSpec(memory_space=pltpu.HBM)
_SEM = pl.BlockSpec(memory_space=pltpu.SEMAPHORE)
_EFFECT = pltpu.SideEffectType.DATAFLOW_SIDE_EFFECTING


def _peer(k, x, y, c):
    return ((1 - x) if k & 4 else x, (1 - y) if k & 2 else y, (1 - c) if k & 1 else c)


PEERS_ALL = (1, 2, 3, 4, 5, 6, 7)
PEERS_CHIPWISE = (1, 2, 4, 6)


def _exchange_copies(mode, refs, send_sems, recv_sems, landing):
    x, y, c = _place()
    my_slot = _slot(x, y, c)
    copies = []
    if mode == "pass_on":
        for t, land in enumerate(refs):
            for j, chip in enumerate([(1 - x, y), (x, 1 - y), (1 - x, 1 - y)]):
                slot = _slot(*chip, (1 - c) if landing else c)
                copies.append(pltpu.make_async_remote_copy(
                    src_ref=land.at[slot], dst_ref=land.at[slot], send_sem=send_sems.at[3 * t + j],
                    recv_sem=recv_sems.at[3 * t + j], device_id=(x, y, 1 - c), device_id_type=MESH))
        return copies
    n = len(refs) // 2
    for t, (src, land) in enumerate(zip(refs[:n], refs[n:])):
        for k in (PEERS_CHIPWISE if mode == "gather_chipwise" else PEERS_ALL):
            peer = _peer(k, x, y, c)
            copies.append(pltpu.make_async_remote_copy(
                src_ref=src.at[_slot(*peer)] if mode == "scatter" else src,
                dst_ref=land.at[_slot(*peer) if landing else my_slot],
                send_sem=send_sems.at[7 * t + k - 1], recv_sem=recv_sems.at[7 * t + k - 1],
                device_id=peer, device_id_type=MESH))
    return copies


def landing_buffers(name, arrays, scatter):
    n = len(arrays)

    def body(*refs):
        srcs, outs, sems = refs[:n], refs[n:2 * n], refs[2 * n]
        my_slot = _slot(*_place())
        copies = [pltpu.make_async_copy(srcs[t].at[my_slot] if scatter else srcs[t], outs[t].at[my_slot], sems.at[t])
                  for t in range(n)]
        for cp in copies:
            cp.start()
        for cp in copies:
            cp.wait()

    return pl.pallas_call(
        body, in_specs=[_ANY] * n, out_specs=[_ANY] * n, name=name,
        out_shape=[_sds(a.shape if scatter else (N_DEV,) + a.shape, a.dtype) for a in arrays],
        scratch_shapes=[pltpu.SemaphoreType.DMA((n,))])(*arrays)


def exchange_start(name, mode, arrays):
    if mode == "pass_on":
        bufs, n_sems = list(arrays), 3 * len(arrays)
    else:
        lands = landing_buffers(name.replace("xstart", "place"), arrays, mode == "scatter")
        bufs, n_sems = list(arrays) + list(lands), 7 * len(arrays)
    nb = len(bufs)

    def body(*refs):
        for send in _exchange_copies(mode, refs[:nb], refs[nb], refs[nb + 1], False):
            send.start()
        refs[-1][...] = jnp.zeros_like(refs[-1])

    bufs = [pltpu.with_memory_space_constraint(a, pltpu.HBM) for a in bufs]
    res = pl.pallas_call(
        body, name=name, in_specs=[_HBM] * nb,
        out_specs=(_SEM, _SEM) + (_HBM,) * nb + (pl.BlockSpec(memory_space=pltpu.VMEM),),
        out_shape=(pltpu.SemaphoreType.DMA((n_sems,)), pltpu.SemaphoreType.DMA((n_sems,)))
        + tuple(pltpu.HBM(a.shape, a.dtype) for a in bufs) + (_sds((8, HEAD_DIM), F32),),
        input_output_aliases={i: 2 + i for i in range(nb)},
        compiler_params=pltpu.CompilerParams(has_side_effects=_EFFECT))(*bufs)
    return dict(name=name, mode=mode, sems=res[:2], bufs=res[2:2 + nb]), res[-1]


def exchange_wait(handle, after):
    mode, bufs = handle['mode'], handle['bufs']
    nb = len(bufs)

    def body(*refs):
        for landed in _exchange_copies(mode, refs[:nb], refs[nb], refs[nb + 1], True):
            landed.wait_send()
            landed.wait_recv()

    res = pl.pallas_call(
        body, name=handle['name'].replace("start", "wait"), in_specs=[_HBM] * nb + [_SEM, _SEM, _ANY],
        out_specs=(_HBM,) * nb, out_shape=tuple(pltpu.HBM(a.shape, a.dtype) for a in bufs),
        input_output_aliases={i: i for i in range(nb)},
        compiler_params=pltpu.CompilerParams(has_side_effects=_EFFECT))(*bufs, *handle['sems'], after)
    return list(res) if mode == "pass_on" else list(res[nb // 2:])


def _pad_lanes(a, width=HEAD_DIM):
    return jnp.pad(a, [(0, 0)] * (a.ndim - 1) + [(0, width - a.shape[-1])])


def _slabs(full, n):
    k = full.shape[0]
    return full.reshape(k, N_DEV, n).transpose(1, 0, 2)


def fox_fwd(a, w, p):
    qkv = mm_nn(a, w['qkv'], BF16, "fox_qkv")
    fl = mm_nn(a, w['f'], F32, "fox_flogit")
    bf = _pad_lanes(p['fox_b_f'])
    cum = fox_gate_fwd("fox_gate_fwd", fl, bf)
    cum_t = cum[:, :N_HEADS].T
    cums = (cum_t[:, :, None], cum_t[:, None, :])
    qn = rms_fwd("fox_qnorm", qkv, p['fox_q_gain'], HEAD_DIM, 512, HEAD_DIM, xcol=0, nh=N_HEADS)
    kn = rms_fwd("fox_knorm", qkv, p['fox_k_gain'], HEAD_DIM, 512, HEAD_DIM, xcol=N_HEADS, nh=N_HEADS)
    o, lse = attn_fwd("fox_attn_fwd", qn, kn, qkv, HEAD_DIM ** -0.5, voff=2 * N_HEADS, cum=cums, exact_o=True)
    return o, dict(a=a, qkv=qkv, fl=fl, bf=bf, cums=cums, qn=qn, kn=kn, o=o, lse=lse)


def fox_bwd(do, s, w, p):
    dqn, dkn, dv, dcs = attn_bwd("fox_attn_bwd", s['qn'], s['kn'], s['qkv'], s['o'], do, s['lse'], HEAD_DIM ** -0.5,
                                 voff=2 * N_HEADS, cum=s['cums'])
    dq, dgq = rms_bwd("fox_qnorm_bwd", s['qkv'], p['fox_q_gain'], dqn, HEAD_DIM, 512, HEAD_DIM, xcol=0, nh=N_HEADS,
                      out_dtype=BF16)
    dk, dgk = rms_bwd("fox_knorm_bwd", s['qkv'], p['fox_k_gain'], dkn, HEAD_DIM, 512, HEAD_DIM, xcol=N_HEADS,
                      nh=N_HEADS, out_dtype=BF16)
    dqkv = jnp.concatenate([dq, dk, dv.astype(BF16)], axis=1)
    dcum = _pad_lanes(-jnp.sum(dcs[:, :, 0, :], axis=1).T)
    dfl, dbf = fox_gate_bwd("fox_gate_bwd", s['fl'], s['bf'], dcum)
    da = mm_nt(dqkv, w['qkv'], F32, "fox_da_qkv")
    da = mm_nt(dfl, w['f'], F32, "fox_da_f", add=da)
    dw_qkv = mm_tn(s['a'], dqkv, 1, BF16, "fox_dw_qkv")[0]
    dw_f = mm_tn(s['a'], dfl, 1, BF16, "fox_dw_f")[0][:, :N_HEADS]
    dw = _slabs(jnp.concatenate([dw_qkv, dw_f], axis=1), 770)
    return da, dict(fox_w_in=dw), dict(fox_b_f=dbf[:, :N_HEADS], fox_q_gain=dgq, fox_k_gain=dgk)


def _rope_tables(positions):
    inv_freq = ROPE_THETA ** (-jnp.arange(0, MLA_ROPE, 2, dtype=F32) / MLA_ROPE)
    ang = positions.astype(F32)[:, None] * inv_freq
    cos, sin = jnp.cos(ang), jnp.sin(ang)
    return _pad_lanes(jnp.concatenate([cos, cos], axis=1)), _pad_lanes(jnp.concatenate([sin, sin], axis=1))


def mla_fwd(a, w, p):
    qg, kg = p['mla_q_gain'], p['mla_k_gain']
    gains = dict(qn=qg[:, :HEAD_DIM], qr=_pad_lanes(qg[:, HEAD_DIM:]), kn=kg[:, :HEAD_DIM], kr=_pad_lanes(kg[:, HEAD_DIM:]))
    cos, sin = _rope_tables(p['positions'])
    ccr = mm_nn(a, w['in'], F32, "mla_in")
    cqn = rms_fwd("mla_cq_norm", ccr, w['q_a_gain'], 512, 256, 512, xcol=0)
    ckvn = rms_fwd("mla_ckv_norm", ccr, w['kv_a_gain'], 512, 256, 512, xcol=1)
    qf = mm_nn(cqn, w['q_b'], F32, "mla_q_b")
    kvf = mm_nn(ckvn, w['kv_b'], F32, "mla_kv_b")
    q_nope = rms_fwd("mla_qnope_norm", qf, gains['qn'], HEAD_DIM, 512, HEAD_DIM, xcol=0, nh=N_HEADS)
    k_nope = rms_fwd("mla_knope_norm", kvf, gains['kn'], HEAD_DIM, 512, HEAD_DIM, xcol=0, nh=N_HEADS)
    q_rope = rope_fwd("mla_qrope", qf, gains['qr'], cos, sin, 512, N_HEADS, N_HEADS)
    k_rope = rope_fwd("mla_krope", ccr, gains['kr'], cos, sin, 512, 8, 1)
    scale = (HEAD_DIM + MLA_ROPE) ** -0.5
    o, lse = attn_fwd("mla_attn_fwd", q_nope, k_nope, kvf, scale, voff=N_HEADS, q2=q_rope, k2=k_rope)
    return o, dict(a=a, gains=gains, cos=cos, sin=sin, ccr=ccr, cqn=cqn, ckvn=ckvn, qf=qf, kvf=kvf, q_nope=q_nope,
                   k_nope=k_nope, q_rope=q_rope, k_rope=k_rope, o=o, lse=lse, scale=scale)


def mla_bwd(do, s, w, p):
    g = s['gains']
    dqn, dkn, dv, dq2, dk2 = attn_bwd("mla_attn_bwd", s['q_nope'], s['k_nope'], s['kvf'], s['o'], do, s['lse'], s['scale'],
                                      voff=N_HEADS, q2=s['q_rope'], k2=s['k_rope'])
    dqf_n, dg_qn = rms_bwd("mla_qnope_bwd", s['qf'], g['qn'], dqn, HEAD_DIM, 512, HEAD_DIM, xcol=0, nh=N_HEADS)
    dkf, dg_kn = rms_bwd("mla_knope_bwd", s['kvf'], g['kn'], dkn, HEAD_DIM, 512, HEAD_DIM, xcol=0, nh=N_HEADS)
    dqf_r, dg_qr = rope_bwd("mla_qrope_bwd", s['qf'], g['qr'], s['cos'], s['sin'], dq2, 512, N_HEADS, N_HEADS)
    dkr, dg_kr = rope_bwd("mla_krope_bwd", s['ccr'], g['kr'], s['cos'], s['sin'], dk2, 512, 8, 1)
    dqf = jnp.concatenate([dqf_n, dqf_r], axis=1)
    dkvf = jnp.concatenate([dkf, dv], axis=1)
    dcqn = mm_nt(dqf, w['q_b'], F32, "mla_dcq")
    dckvn = mm_nt(dkvf, w['kv_b'], F32, "mla_dckv")
    dw_qb = mm_tn(s['cqn'], dqf, 1, BF16, "mla_dw_qb")[0]
    dw_kvb = mm_tn(s['ckvn'], dkvf, 1, BF16, "mla_dw_kvb")[0]
    dcq, dg_qa = rms_bwd("mla_cq_bwd", s['ccr'], w['q_a_gain'], dcqn, 512, 256, 512, xcol=0)
    dckv, dg_kva = rms_bwd("mla_ckv_bwd", s['ccr'], w['kv_a_gain'], dckvn, 512, 256, 512, xcol=1)
    dccr = jnp.concatenate([dcq, dckv, dkr], axis=1)
    da = mm_nt(dccr, w['in'], F32, "mla_da")
    dw_in = mm_tn(s['a'], dccr, 1, BF16, "mla_dw_in")[0][:, :1088].reshape(N_DEV, 256, 1088)
    hp = 2
    nope = dw_qb[:, :2048].reshape(512, N_DEV, hp, HEAD_DIM)
    rope = dw_qb[:, 2048:].reshape(512, N_DEV, hp, HEAD_DIM)[..., :MLA_ROPE]
    dw_qb_s = jnp.concatenate([nope, rope], axis=-1).transpose(1, 0, 2, 3).reshape(N_DEV, 512, hp * 192)
    kk = dw_kvb[:, :2048].reshape(512, N_DEV, hp, HEAD_DIM)
    vv = dw_kvb[:, 2048:].reshape(512, N_DEV, hp, HEAD_DIM)
    dw_kvb_s = jnp.concatenate([kk, vv], axis=-1).transpose(1, 0, 2, 3).reshape(N_DEV, 512, hp * 256)
    small = dict(mla_q_a_gain=dg_qa, mla_kv_a_gain=dg_kva,
                 mla_q_gain=jnp.concatenate([dg_qn, dg_qr[:, :MLA_ROPE]], axis=1),
                 mla_k_gain=jnp.concatenate([dg_kn, dg_kr[:, :MLA_ROPE]], axis=1))
    return da, dict(mla_w_in=dw_in, mla_w_q_b=dw_qb_s, mla_w_kv_b=dw_kvb_s), small


def sb_fwd_layer(a, w, p):
    qkv = mm_nn(a, w['in'], BF16, "sb_qkv")
    qn = rms_fwd("sb_qnorm", qkv, p['sb_q_gain'], HEAD_DIM, 512, HEAD_DIM, xcol=0, nh=N_HEADS)
    kn = rms_fwd("sb_knorm", qkv, p['sb_k_gain'], HEAD_DIM, 512, HEAD_DIM, xcol=N_HEADS, nh=N_HEADS)
    o, carries = sb_fwd("sb_attn_fwd", qn, kn, qkv, HEAD_DIM ** -0.5, 2 * N_HEADS)
    return o, dict(a=a, qkv=qkv, qn=qn, kn=kn, carries=carries)


def sb_bwd_layer(do, s, w, p):
    dqn, dkn, dv = sb_bwd("sb_attn_bwd", s['qn'], s['kn'], s['qkv'], do, s['carries'], HEAD_DIM ** -0.5, 2 * N_HEADS)
    dq, dgq = rms_bwd("sb_qnorm_bwd", s['qkv'], p['sb_q_gain'], dqn, HEAD_DIM, 512, HEAD_DIM, xcol=0, nh=N_HEADS,
                      out_dtype=BF16)
    dk, dgk = rms_bwd("sb_knorm_bwd", s['qkv'], p['sb_k_gain'], dkn, HEAD_DIM, 512, HEAD_DIM, xcol=N_HEADS, nh=N_HEADS,
                      out_dtype=BF16)
    dqkv = jnp.concatenate([dq, dk, dv.astype(BF16)], axis=1)
    da = mm_nt(dqkv, w['in'], F32, "sb_da")
    dw = mm_tn(s['a'], dqkv, N_DEV, BF16, "sb_dw_in")
    return da, dict(sb_w_in=dw), dict(sb_q_gain=dgq, sb_k_gain=dgk)


def sgu_fwd(a, w, p):
    pre = mm_nn(a, w['in'], BF16, "sgu_in")
    u, vn = sgu_pre_fwd("sgu_pre_fwd", pre, w['v_gain'], 256)
    ws = p['sgu_w_s'][0]
    bs3 = p['sgu_b_s'][0][:, :, None]
    prod = sgu_mix_fwd("sgu_mix_fwd", vn, u, ws, bs3)
    return prod, dict(a=a, pre=pre, u=u, vn=vn, ws=ws, bs3=bs3)


def sgu_bwd(dprod, s, w, p):
    du, dvn, dws, dbs3 = sgu_mix_bwd("sgu_mix_bwd", s['vn'], s['u'], s['ws'], s['bs3'], dprod)
    dpu, dpv, dgv = sgu_pre_bwd("sgu_pre_bwd", s['pre'], w['v_gain'], du, dvn, 128)
    dpre = jnp.concatenate([dpu, dpv], axis=1)
    da = mm_nt(dpre, w['in'], F32, "sgu_da")
    dw = mm_tn(s['a'], dpre, N_DEV, BF16, "sgu_dw_in")
    return da, dict(sgu_w_in=dw), dict(sgu_v_gain=dgv, sgu_w_s=dws[None], sgu_b_s=dbs3[None, :, :, 0])


MIXERS = [("fox", fox_fwd, fox_bwd), ("mla", mla_fwd, mla_bwd), ("sb", sb_fwd_layer, sb_bwd_layer),
          ("sgu", sgu_fwd, sgu_bwd)]


def _pack_rows(arrays, row_mult=256):
    flat = jnp.concatenate([a.reshape(-1).astype(F32) for a in arrays])
    per = row_mult * HEAD_DIM
    total = -(-flat.shape[0] // per) * per
    return jnp.pad(flat, (0, total - flat.shape[0])).reshape(total // HEAD_DIM, HEAD_DIM)


def _unpack_rows(packed, shapes):
    flat = packed.reshape(-1)
    out, off = [], 0
    for shp in shapes:
        n = math.prod(shp)
        out.append(flat[off:off + n].reshape(shp))
        off += n
    return out


def _mixer_shards(i, p):
    name = MIXERS[i][0]
    bf = lambda a: a.astype(BF16)
    items = {'out': bf(p[name + '_w_out'][0])}
    if name == "fox":
        items['in'] = bf(p['fox_w_in'][0])
        items['small'] = _pack_rows([p[n] for n in SMALL_SHARDED], 8)
    elif name == "mla":
        items.update({'in': bf(p['mla_w_in'][0]), 'q_b': bf(p['mla_w_q_b'][0]), 'kv_b': bf(p['mla_w_kv_b'][0])})
    else:
        items['in'] = bf(p[name + '_w_in'][0])
    return items


def _ffn_shards(i, p):
    return {'up': p['ffn_w_up'][i].astype(BF16), 'down': p['ffn_w_down'][i].astype(BF16)}


def _assemble_ffn(got):
    return {'up': got['up'], 'down': got['down'].reshape(1, D_FF, -1)}


def _assemble_mixer(i, got, p):
    name = MIXERS[i][0]
    w = {'out': got['out'].reshape(1, -1, got['out'].shape[-1])}
    small = None
    if name == "fox":
        small_shapes = [p[n].shape for n in SMALL_SHARDED]
        full = got['in'].transpose(1, 0, 2).reshape(got['in'].shape[1], -1)
        w['qkv'] = full[None, :, :3 * N_HEADS * HEAD_DIM]
        w['f'] = _pad_lanes(full[:, 3 * N_HEADS * HEAD_DIM:])[None]
        parts = [_unpack_rows(got['small'][d], small_shapes) for d in range(N_DEV)]
        small = {n: jnp.concatenate([parts[d][k] for d in range(N_DEV)], axis=ax)
                 for k, (n, ax) in enumerate(SMALL_SHARDED.items())}
    elif name == "mla":
        w['in'] = _pad_lanes(got['in'].reshape(-1, 1088), 1152)[None]
        hp = 2
        qb = got['q_b'].reshape(N_DEV, 512, hp, 192).transpose(1, 0, 2, 3)
        nope = qb[..., :HEAD_DIM].reshape(512, -1)
        rope = _pad_lanes(qb[..., HEAD_DIM:]).reshape(512, -1)
        w['q_b'] = jnp.concatenate([nope, rope], axis=1)[None]
        kvb = got['kv_b'].reshape(N_DEV, 512, hp, 256).transpose(1, 0, 2, 3)
        w['kv_b'] = jnp.concatenate([kvb[..., :HEAD_DIM].reshape(512, -1), kvb[..., HEAD_DIM:].reshape(512, -1)], axis=1)[None]
    else:
        w['in'] = got['in']
    return w, small


def _train_step(p):
    x, target = p['x'][0], p['loss_target'][0]
    p = dict(p, positions=p['positions'][0])
    xi, yi, ci = _place()
    my_slot = _slot(xi, yi, ci)

    shards0 = _mixer_shards(0, p)
    got0 = dict(zip(shards0, all_gather("gather_first", list(shards0.values()))))
    pending, order_token = {}, jnp.zeros((1, 1), F32)
    for i in range(DEPTH):
        for kind, shards in (("mix", _mixer_shards(i, p) if i else None), ("ffn", _ffn_shards(i, p))):
            if shards is not None:
                handle, token = exchange_start(f"xstart_ag_{kind}{i}", "gather_chipwise", list(shards.values()))
                pending[kind, i] = (handle, list(shards))
                order_token = order_token + token[0:1, 0:1]

    def pass_on(kind, i, after):
        handle, keys = pending[kind, i]
        handle, token = exchange_start(f"xstart_pass_{kind}{i}", "pass_on", exchange_wait(handle, after))
        pending[kind, i] = (handle, keys)
        return token[0:1, 0:1]

    def gathered(kind, i, after):
        handle, keys = pending[kind, i]
        return dict(zip(keys, exchange_wait(handle, after)))

    h = x
    saved, weights = [], []
    small_full = None
    for i in range(DEPTH):
        if i == 0:
            w, small_full = _assemble_mixer(0, got0, p)
            gain = p['mix_norm'][0:1] + order_token
        else:
            w, _ = _assemble_mixer(i, gathered("mix", i, h), p)
            gain = p['mix_norm'][i:i + 1] + pass_on("ffn", i, h)
        w.update(q_a_gain=small_full['mla_q_a_gain'], kv_a_gain=small_full['mla_kv_a_gain'],
                 v_gain=small_full['sgu_v_gain'])
        conv_w = small_full['ffn_conv_w']
        a = rms_fwd(f"mix_norm_{i}", h, gain, h.shape[1], 256, h.shape[1])
        mixed, s_mix = MIXERS[i][1](a, w, p)
        if i == 0:
            pass_on("ffn", 0, mixed)
        w.update(_assemble_ffn(gathered("ffn", i, mixed)))
        weights.append(w)
        h1 = mm_nn(mixed, w['out'], F32, f"mix_out_{i}", add=h)
        b = rms_fwd(f"ffn_norm_{i}", h1, p['ffn_norm'][i:i + 1], h.shape[1], 256, h.shape[1])
        up = mm_nn(b, w['up'], BF16, f"ffn_up_{i}")
        conv_b = p['ffn_conv_b'][i:i + 1]
        if i + 1 < DEPTH:
            conv_b = conv_b + pass_on("mix", i + 1, up)
        act = conv_fwd(f"ffn_conv_{i}", up, conv_w[i], conv_b)
        h2 = mm_nn(act, w['down'], F32, f"ffn_down_{i}", add=h1)
        saved.append(dict(h=h, mixed=mixed, s_mix=s_mix, h1=h1, b=b, up=up, act=act))
        h = h2

    dh16, dh, sq = loss_head("loss_head", h, target)
    loss = lax.psum(0.5 * jnp.sum(sq) / h.shape[1], ("x", "y", "c"))

    scatters = []
    small_g = {n: [None] * p[n].shape[0] for n in ('mix_norm', 'ffn_norm', 'ffn_conv_w', 'ffn_conv_b')}
    for i in reversed(range(DEPTH)):
        w, s = weights[i], saved[i]
        name = MIXERS[i][0]
        dact = mm_nt(dh16, w['down'], BF16, f"ffn_dact_{i}")
        dw_down = mm_tn(s['act'], dh16, 1, BF16, f"ffn_dw_down_{i}").reshape(N_DEV, D_FF // N_DEV, -1)
        dug, duv, dcwg, dcwv, dcbg, dcbv = conv_bwd(f"ffn_conv_bwd_{i}", s['up'], conv_w[i], p['ffn_conv_b'][i:i + 1], dact)
        dup = jnp.concatenate([dug, duv], axis=1)
        dw_up = mm_tn(s['b'], dup, N_DEV, BF16, f"ffn_dw_up_{i}")
        handle, token = exchange_start(f"xstart_rs_ffn{i}", "scatter", [dw_up, dw_down])
        scatters.append((handle, ['ffn_w_up', 'ffn_w_down']))
        db = mm_nt(dup, w['up'], F32, f"ffn_db_{i}")
        dh1_16, dh1, dg_ffn = rms_bwd(f"ffn_norm_bwd_{i}", s['h1'], p['ffn_norm'][i:i + 1] + token[0:1, 0:1], db,
                                      h.shape[1], 128, h.shape[1], resid=dh, also_bf16=True)
        dmix = mm_nt(dh1_16, w['out'], BF16, f"mix_dout_{i}")
        dw_out = mm_tn(s['mixed'], dh1_16, 1, BF16, f"mix_dw_out_{i}").reshape(N_DEV, -1, h.shape[1])
        da, big_i, small_i = MIXERS[i][2](dmix, s['s_mix'], w, p)
        big_i[name + '_w_out'] = dw_out
        handle, token = exchange_start(f"xstart_rs_mix{i}", "scatter", list(big_i.values()))
        scatters.append((handle, list(big_i)))
        dh16, dh, dg_mix = rms_bwd(f"mix_norm_bwd_{i}", s['h'], p['mix_norm'][i:i + 1] + token[0:1, 0:1], da, h.shape[1],
                                   128, h.shape[1], resid=dh1, also_bf16=True)
        for k, v in small_i.items():
            small_g[k] = v
        small_g['mix_norm'][i], small_g['ffn_norm'][i] = dg_mix, dg_ffn
        small_g['ffn_conv_w'][i] = jnp.concatenate([dcwg, dcwv], axis=1)[None]
        small_g['ffn_conv_b'][i] = jnp.concatenate([dcbg, dcbv], axis=1)
    for n in ('mix_norm', 'ffn_norm', 'ffn_conv_w', 'ffn_conv_b'):
        small_g[n] = jnp.concatenate(small_g[n], axis=0)
    grad_x = dh[None]

    full_shapes = [tuple(small_g[n].shape) for n in SMALL]
    small_handle, small_token = exchange_start("xstart_small", "gather", [_pack_rows([small_g[n] for n in SMALL])])
    big = {}
    for handle, names in scatters[:-1]:
        for n, landed in zip(names, exchange_wait(handle, small_token)):
            big.setdefault(n, []).insert(0, landed)
    grads, deltas, new_m, new_v = {}, {}, {}, {}
    last_handle, last_names = scatters[-1]
    for n in BIG:
        if n not in last_names:
            grads[n], deltas[n], new_m[n], new_v[n] = adam_sum("adam_" + n, big[n], p[n], p['m_' + n], p['v_' + n])
    for n, landed in zip(last_names, exchange_wait(last_handle, new_v['ffn_w_up'])):
        grads[n], deltas[n], new_m[n], new_v[n] = adam_sum("adam_" + n, [landed], p[n], p['m_' + n], p['v_' + n])
    partials = exchange_wait(small_handle, new_v[last_names[0]])[0]
    summed = _unpack_rows(sum_parts("sum_small_grads", partials), full_shapes)
    mine = []
    for n, g in zip(SMALL, summed):
        if n in SMALL_SHARDED:
            ax = SMALL_SHARDED[n]
            g = lax.dynamic_slice_in_dim(g, my_slot * p[n].shape[ax], p[n].shape[ax], axis=ax)
        mine.append(g)
    shapes = [p[n].shape for n in SMALL]
    packed = [_pack_rows(arrs) for arrs in ([p[n] for n in SMALL], mine, [p['m_' + n] for n in SMALL], [p['v_' + n] for n in SMALL])]
    d_s, m_s, v_s = adam_flat("adam_small", *packed)
    for n, g, d, m, v in zip(SMALL, mine, _unpack_rows(d_s, shapes), _unpack_rows(m_s, shapes), _unpack_rows(v_s, shapes)):
        grads[n], deltas[n], new_m[n], new_v[n] = g, d, m, v

    return (loss, grad_x, *[grads[n] for n in WEIGHTS], *[deltas[n] for n in WEIGHTS], *[new_m[n] for n in WEIGHTS],
            *[new_v[n] for n in WEIGHTS])


def kernel(x, positions, mix_norm, ffn_norm, fox_w_in, fox_b_f, fox_q_gain, fox_k_gain, fox_w_out, mla_w_in, mla_q_a_gain, mla_kv_a_gain, mla_w_q_b, mla_w_kv_b, mla_q_gain, mla_k_gain, mla_w_out, sb_w_in, sb_q_gain, sb_k_gain, sb_w_out, sgu_w_in, sgu_v_gain, sgu_w_s, sgu_b_s, sgu_w_out, ffn_w_up, ffn_conv_w, ffn_conv_b, ffn_w_down, loss_target, m_mix_norm, m_ffn_norm, m_fox_w_in, m_fox_b_f, m_fox_q_gain, m_fox_k_gain, m_fox_w_out, m_mla_w_in, m_mla_q_a_gain, m_mla_kv_a_gain, m_mla_w_q_b, m_mla_w_kv_b, m_mla_q_gain, m_mla_k_gain, m_mla_w_out, m_sb_w_in, m_sb_q_gain, m_sb_k_gain, m_sb_w_out, m_sgu_w_in, m_sgu_v_gain, m_sgu_w_s, m_sgu_b_s, m_sgu_w_out, m_ffn_w_up, m_ffn_conv_w, m_ffn_conv_b, m_ffn_w_down, v_mix_norm, v_ffn_norm, v_fox_w_in, v_fox_b_f, v_fox_q_gain, v_fox_k_gain, v_fox_w_out, v_mla_w_in, v_mla_q_a_gain, v_mla_kv_a_gain, v_mla_w_q_b, v_mla_w_kv_b, v_mla_q_gain, v_mla_k_gain, v_mla_w_out, v_sb_w_in, v_sb_q_gain, v_sb_k_gain, v_sb_w_out, v_sgu_w_in, v_sgu_v_gain, v_sgu_w_s, v_sgu_b_s, v_sgu_w_out, v_ffn_w_up, v_ffn_conv_w, v_ffn_conv_b, v_ffn_w_down):
    args = locals()
    names = ['x', 'positions'] + WEIGHTS + ['loss_target'] + ['m_' + n for n in WEIGHTS] + ['v_' + n for n in WEIGHTS]
    return _train_step({n: args[n] for n in names})
```

```python
import functools
import math

import jax
import jax.numpy as jnp
from jax import lax
from jax.experimental import pallas as pl
from jax.experimental.pallas import tpu as pltpu

F32 = jnp.float32
BF16 = jnp.bfloat16
MESH = pl.DeviceIdType.MESH

N_DEV = 8
N_HEADS = 16
HEAD_DIM = 128
EPS = 1e-6
DEPTH = 4
D_FF = 5632
MLA_ROPE = 64
ROPE_THETA = 10000.0
VMEM_LIMIT = 48 * 1024 * 1024

ADAM_LR, ADAM_B1, ADAM_B2, ADAM_EPS, ADAM_WD, ADAM_STEP = 0.001, 0.9, 0.999, 1e-08, 0.01, 10

WEIGHTS = ['mix_norm', 'ffn_norm', 'fox_w_in', 'fox_b_f', 'fox_q_gain', 'fox_k_gain', 'fox_w_out', 'mla_w_in',
           'mla_q_a_gain', 'mla_kv_a_gain', 'mla_w_q_b', 'mla_w_kv_b', 'mla_q_gain', 'mla_k_gain', 'mla_w_out',
           'sb_w_in', 'sb_q_gain', 'sb_k_gain', 'sb_w_out', 'sgu_w_in', 'sgu_v_gain', 'sgu_w_s', 'sgu_b_s',
           'sgu_w_out', 'ffn_w_up', 'ffn_conv_w', 'ffn_conv_b', 'ffn_w_down']
BIG = ['fox_w_in', 'fox_w_out', 'mla_w_in', 'mla_w_q_b', 'mla_w_kv_b', 'mla_w_out', 'sb_w_in', 'sb_w_out',
       'sgu_w_in', 'sgu_w_out', 'ffn_w_up', 'ffn_w_down']
SMALL = [w for w in WEIGHTS if w not in BIG]
SMALL_SHARDED = {'mla_q_a_gain': 1, 'mla_kv_a_gain': 1, 'sgu_v_gain': 1, 'ffn_conv_w': 2}


def _cparams(n_grid):
    return pltpu.CompilerParams(dimension_semantics=("arbitrary",) * n_grid, vmem_limit_bytes=VMEM_LIMIT)


def _pick(n, cap):
    best = None
    t = 128
    while t <= min(n, cap):
        if n % t == 0:
            best = t
        t += 128
    return best if best is not None else n


def _mm_call(name, a, b, out_shape, a_spec, b_spec, o_spec, grid, dims, acc_shape, add=None):
    nk = grid[2]

    def body(*refs):
        a_ref, b_ref = refs[:2]
        add_ref = refs[2] if add is not None else None
        o_ref = refs[3] if add is not None else refs[2]
        prod = lax.dot_general(a_ref[...].astype(BF16), b_ref[...].astype(BF16), (dims, ((), ())),
                               preferred_element_type=F32)

        def finish(r):
            if add_ref is not None:
                r = r + add_ref[...].astype(F32)
            o_ref[...] = r.astype(o_ref.dtype)

        if nk == 1:
            finish(prod)
            return
        acc = refs[-1]
        k = pl.program_id(2)

        @pl.when(k == 0)
        def _():
            acc[...] = prod

        @pl.when(k > 0)
        def _():
            acc[...] += prod

        @pl.when(k == nk - 1)
        def _():
            finish(acc[...])

    ins = [a, b] + ([] if add is None else [add])
    in_specs = [a_spec, b_spec] + ([] if add is None else [o_spec])
    return pl.pallas_call(body, grid=grid, in_specs=in_specs, out_specs=o_spec, out_shape=out_shape,
                          scratch_shapes=[] if nk == 1 else [pltpu.VMEM(acc_shape, F32)], name=name,
                          compiler_params=_cparams(3))(*ins)


def mm_nn(a, b3, out_dtype, name, add=None, joff=0, nj=None):
    m, kk = a.shape
    _, kb, n = b3.shape
    assert kb == kk
    nj = b3.shape[0] - joff if nj is None else nj
    tn, tk = _pick(n, 1536), _pick(kk, 2048)
    tm = _pick(m, 512 if a.dtype == F32 and tk > 1024 else 1024)
    nb = n // tn
    return _mm_call(
        name, a, b3, jax.ShapeDtypeStruct((m, nj * n), out_dtype),
        pl.BlockSpec((tm, tk), lambda i, c, k: (i, k)),
        pl.BlockSpec((None, tk, tn), lambda i, c, k: (joff + c // nb, k, c % nb)),
        pl.BlockSpec((tm, tn), lambda i, c, k: (i, c)),
        (m // tm, nj * nb, kk // tk), ((1,), (0,)), (tm, tn), add=add)


def mm_nt(a, b3, out_dtype, name, add=None, joff=0, nj=None):
    m, na = a.shape
    _, ko, n = b3.shape
    nj = b3.shape[0] - joff if nj is None else nj
    assert na == nj * n
    to, tn = _pick(ko, 1024), _pick(n, 2048)
    tm = _pick(m, 512 if a.dtype == F32 and tn > 1024 else 1024)
    nb = n // tn
    return _mm_call(
        name, a, b3, jax.ShapeDtypeStruct((m, ko), out_dtype),
        pl.BlockSpec((tm, tn), lambda i, o, c: (i, c)),
        pl.BlockSpec((None, to, tn), lambda i, o, c: (joff + c // nb, o, c % nb)),
        pl.BlockSpec((tm, to), lambda i, o, c: (i, o)),
        (m // tm, ko // to, nj * nb), ((1,), (1,)), (tm, to), add=add)


def mm_tn(a, b, nj, out_dtype, name):
    s, ko = a.shape
    sb, nb_tot = b.shape
    assert sb == s and nb_tot % nj == 0
    n = nb_tot // nj
    to, tn = _pick(ko, 1024), _pick(n, 1536)
    ts = _pick(s, 1024 if F32 in (a.dtype, b.dtype) else 2048)
    nb = n // tn
    return _mm_call(
        name, a, b, jax.ShapeDtypeStruct((nj, ko, n), out_dtype),
        pl.BlockSpec((ts, to), lambda o, c, k: (k, o)),
        pl.BlockSpec((ts, tn), lambda o, c, k: (k, c)),
        pl.BlockSpec((None, to, tn), lambda o, c, k: (c // nb, o, c % nb)),
        (ko // to, nj * nb, s // ts), ((0,), (0,)), (to, tn))


def blockk(name, fn, grid, ins, outs):
    n_in = len(ins)
    accs = [o[2] for o in outs]

    def body(*refs):
        vals = fn(*[r[...] for r in refs[:n_in]])
        if not isinstance(vals, (tuple, list)):
            vals = (vals,)
        i, j = pl.program_id(0), pl.program_id(1)
        for r, v, acc in zip(refs[n_in:], vals, accs):
            if acc is None:
                r[...] = v.astype(r.dtype)
            else:
                first = (j == 0) if acc == 'inner' else jnp.logical_and(i == 0, j == 0)

                @pl.when(first)
                def _(r=r, v=v):
                    r[...] = v.astype(r.dtype)

                @pl.when(jnp.logical_not(first))
                def _(r=r, v=v):
                    r[...] += v.astype(r.dtype)

    res = pl.pallas_call(body, grid=grid, in_specs=[s for _, s in ins], out_specs=[o[1] for o in outs],
                         out_shape=[o[0] for o in outs], name=name, compiler_params=_cparams(2))(*[a for a, _ in ins])
    return res


def _sds(shape, dtype):
    return jax.ShapeDtypeStruct(tuple(shape), dtype)


def _rows(tr, w, col=0):
    if col == 'j':
        return pl.BlockSpec((tr, w), lambda i, j: (i, j))
    if callable(col):
        return pl.BlockSpec((tr, w), lambda i, j: (i, col(j)))
    return pl.BlockSpec((tr, w), lambda i, j: (i, col))


def _whole(shape):
    nd = len(shape)
    return pl.BlockSpec(tuple(shape), lambda i, j: (0,) * nd)


def _rms(x, g, n):
    ms = jnp.sum(x * x, axis=-1, keepdims=True) * (1.0 / n)
    return x * lax.rsqrt(ms + EPS) * g


def _sig(x):
    return 1.0 / (1.0 + jnp.exp(-x))


def _gelu(x):
    return 0.5 * x * (1.0 + jnp.tanh(math.sqrt(2.0 / math.pi) * (x + 0.044715 * (x * x * x))))


def _lane_iota(shape):
    return lax.broadcasted_iota(jnp.int32, shape, len(shape) - 1)


def _rope(x, cos, sin):
    lane = _lane_iota(x.shape)
    half = MLA_ROPE // 2
    swapped = jnp.where(lane < half, pltpu.roll(x, HEAD_DIM - half, 1), pltpu.roll(x, half, 1))
    sign = jnp.where(lane < half, -1.0, 1.0)
    return x * cos + swapped * (sin * sign)


def _rope_t(dy, cos, sin):
    lane = _lane_iota(dy.shape)
    half = MLA_ROPE // 2
    t = dy * sin
    swapped = jnp.where(lane < half, pltpu.roll(t, HEAD_DIM - half, 1), pltpu.roll(t, half, 1))
    sign = jnp.where(lane < half, 1.0, -1.0)
    return dy * cos + swapped * sign


def rms_fwd(name, x, gain, n, tr, width, xcol=0, nh=1, out_dtype=BF16, out_cols=None):
    r = x.shape[0]
    tr = r if nh > 1 else tr
    out_cols = width * nh if out_cols is None else out_cols
    xspec = _rows(tr, width, (lambda j: xcol + j) if nh > 1 else xcol)
    ospec = _rows(tr, width, 'j' if nh > 1 else 0)
    return blockk(name, lambda xb, g: _rms(xb.astype(F32), g, n), (r // tr, nh),
                  [(x, xspec), (gain, _whole(gain.shape))], [(_sds((r, out_cols), out_dtype), ospec, None)])[0]


def rms_bwd(name, x, gain, dy, n, tr, width, xcol=0, nh=1, dycol=0, resid=None, out_dtype=F32, also_bf16=False):
    r = x.shape[0]
    tr = r if nh > 1 else tr
    xspec = _rows(tr, width, (lambda j: xcol + j) if nh > 1 else xcol)
    dyspec = _rows(tr, width, (lambda j: dycol + j) if nh > 1 else dycol)
    ospec = _rows(tr, width, 'j' if nh > 1 else 0)

    def fn(xb, g, dyb, *rest):
        _, vjp = jax.vjp(lambda a, b: _rms(a, b, n), xb.astype(F32), g)
        dx, dg = vjp(dyb.astype(F32))
        if rest:
            dx = dx + rest[0].astype(F32)
        return ((dx,) if also_bf16 else ()) + (dx, dg)

    ins = [(x, xspec), (gain, _whole(gain.shape)), (dy, dyspec)]
    if resid is not None:
        ins.append((resid, ospec))
    outs = [(_sds((r, width * nh), out_dtype), ospec, None), (_sds(gain.shape, F32), _whole(gain.shape), 'all')]
    if also_bf16:
        outs.insert(0, (_sds((r, width * nh), BF16), ospec, None))
    return blockk(name, fn, (r // tr, nh), ins, outs)


def _nt(a, b):
    return lax.dot_general(a, b, (((1,), (1,)), ((), ())), preferred_element_type=F32)


def _tn(a, b):
    return lax.dot_general(a, b, (((0,), (0,)), ((), ())), preferred_element_type=F32)


def _nn(a, b):
    return lax.dot_general(a, b, (((1,), (0,)), ((), ())), preferred_element_type=F32)


ATTN_BLOCK = 512
ATTN_HEADS = 2
ATTN_GRID_HEADS = N_HEADS // ATTN_HEADS


def _head_cols(off):
    assert off % ATTN_HEADS == 0
    return off // ATTN_HEADS


def _hd(ref, hh):
    return ref[:, hh * HEAD_DIM:(hh + 1) * HEAD_DIM]


def _attn_specs(tq, qoff, koff, voff, extra, bias, q2off):
    w = HEAD_DIM * ATTN_HEADS
    qc, kc, vc, q2c = (_head_cols(o) for o in (qoff, koff, voff, q2off))
    specs = [pl.BlockSpec((tq, w), lambda h, i, j: (i, qc + h)),
             pl.BlockSpec((tq, w), lambda h, i, j: (jnp.minimum(i, j), kc + h)),
             pl.BlockSpec((tq, w), lambda h, i, j: (jnp.minimum(i, j), vc + h))]
    if extra:
        specs += [pl.BlockSpec((tq, w), lambda h, i, j: (i, q2c + h)),
                  pl.BlockSpec((tq, HEAD_DIM), lambda h, i, j: (jnp.minimum(i, j), 0))]
    if bias:
        specs += [pl.BlockSpec((ATTN_HEADS, tq, 1), lambda h, i, j: (h, i, 0)),
                  pl.BlockSpec((ATTN_HEADS, 1, tq), lambda h, i, j: (h, 0, jnp.minimum(i, j)))]
    return specs


def _scores(q, k, q2, k2, cc, cr, scale, tq, diagonal):
    s = _nt(q.astype(BF16), k.astype(BF16))
    if q2 is not None:
        s = s + _nt(q2.astype(BF16), k2.astype(BF16))
    s = s * scale
    if cc is not None:
        s = s + (cc - cr)
    if not diagonal:
        return s, None
    return s, lax.broadcasted_iota(jnp.int32, (tq, tq), 1) <= lax.broadcasted_iota(jnp.int32, (tq, tq), 0)


def _on_blocks(qi, kj, step):
    @pl.when(kj < qi)
    def _():
        step(False)

    @pl.when(kj == qi)
    def _():
        step(True)


def attn_fwd(name, q, k, v, scale, *, qoff=0, koff=0, voff=0, q2=None, k2=None, q2off=0, cum=None, tq=ATTN_BLOCK,
             exact_o=False):
    s_len = q.shape[0]
    nq = s_len // tq
    extra, bias = q2 is not None, cum is not None
    n_in = 3 + 2 * extra + 2 * bias

    def body(*refs):
        q_ref, k_ref, v_ref = refs[:3]
        p = 3
        q2_ref = k2_ref = cc_ref = cr_ref = None
        if extra:
            q2_ref, k2_ref = refs[p:p + 2]
            p += 2
        if bias:
            cc_ref, cr_ref = refs[p:p + 2]
            p += 2
        o_ref, lse_ref, m_s, l_s, acc_s = refs[p:]
        qi, kj = pl.program_id(1), pl.program_id(2)

        @pl.when(kj == 0)
        def _():
            m_s[...] = jnp.full_like(m_s, -jnp.inf)
            l_s[...] = jnp.zeros_like(l_s)
            acc_s[...] = jnp.zeros_like(acc_s)

        def step(diagonal):
            for hh in range(ATTN_HEADS):
                s, allowed = _scores(_hd(q_ref, hh), _hd(k_ref, hh), _hd(q2_ref, hh) if extra else None,
                                     k2_ref[...] if extra else None, cc_ref[hh] if bias else None,
                                     cr_ref[hh] if bias else None, scale, tq, diagonal)
                if diagonal:
                    s = jnp.where(allowed, s, -jnp.inf)
                m_old = m_s[hh]
                m_new = jnp.maximum(m_old, jnp.max(s, axis=-1, keepdims=True))
                alpha = jnp.exp(m_old - m_new)
                pr = jnp.exp(s - m_new)
                l_s[hh] = alpha * l_s[hh] + jnp.sum(pr, axis=-1, keepdims=True)
                vb = _hd(v_ref, hh).astype(BF16)
                pv = _nn(pr.astype(BF16), vb)
                if exact_o:
                    pv = pv + _nn((pr - pr.astype(BF16).astype(F32)).astype(BF16), vb)
                acc_s[hh] = alpha * acc_s[hh] + pv
                m_s[hh] = m_new

        _on_blocks(qi, kj, step)

        @pl.when(kj == qi)
        def _():
            for hh in range(ATTN_HEADS):
                o_ref[:, hh * HEAD_DIM:(hh + 1) * HEAD_DIM] = (acc_s[hh] / l_s[hh]).astype(o_ref.dtype)
                lse_ref[hh] = m_s[hh] + jnp.log(l_s[hh])

    ins = [q, k, v] + ([q2, k2] if extra else []) + (list(cum) if bias else [])
    d, w = HEAD_DIM, HEAD_DIM * ATTN_HEADS
    return pl.pallas_call(
        body, grid=(ATTN_GRID_HEADS, nq, nq), in_specs=_attn_specs(tq, qoff, koff, voff, extra, bias, q2off),
        out_specs=[pl.BlockSpec((tq, w), lambda h, i, j: (i, h)),
                   pl.BlockSpec((ATTN_HEADS, tq, 1), lambda h, i, j: (h, i, 0))],
        out_shape=[_sds((s_len, N_HEADS * d), F32 if exact_o else BF16), _sds((N_HEADS, s_len, 1), F32)],
        scratch_shapes=[pltpu.VMEM((ATTN_HEADS, tq, 1), F32), pltpu.VMEM((ATTN_HEADS, tq, 1), F32),
                        pltpu.VMEM((ATTN_HEADS, tq, d), F32)],
        name=name, compiler_params=_cparams(3))(*ins)


def attn_bwd(name, q, k, v, o, do, lse, scale, *, qoff=0, koff=0, voff=0, q2=None, k2=None, q2off=0, cum=None,
             tq=ATTN_BLOCK):
    s_len = q.shape[0]
    nq = s_len // tq
    extra, bias = q2 is not None, cum is not None
    d = HEAD_DIM
    n_in = 6 + 2 * extra + 2 * bias

    def body(*refs):
        q_ref, k_ref, v_ref = refs[:3]
        p = 3
        q2_ref = k2_ref = cc_ref = cr_ref = None
        if extra:
            q2_ref, k2_ref = refs[p:p + 2]
            p += 2
        if bias:
            cc_ref, cr_ref = refs[p:p + 2]
            p += 2
        o_ref, do_ref, lse_ref = refs[p:p + 3]
        p += 3
        dq_ref, dk_ref, dv_ref = refs[p:p + 3]
        p += 3
        dq2_ref = dk2_ref = dcs_ref = None
        if extra:
            dq2_ref, dk2_ref = refs[p:p + 2]
            p += 2
        if bias:
            dcs_ref = refs[p]
            p += 1
        dq_s, delta_s = refs[p:p + 2]
        dq2_s = refs[p + 2] if extra else None
        h, qi, kj = pl.program_id(0), pl.program_id(1), pl.program_id(2)

        @pl.when(jnp.logical_and(qi == 0, kj == 0))
        def _():
            dk_ref[...] = jnp.zeros_like(dk_ref)
            dv_ref[...] = jnp.zeros_like(dv_ref)

        if extra:
            @pl.when(jnp.logical_and(h == 0, jnp.logical_and(qi == 0, kj == 0)))
            def _():
                dk2_ref[...] = jnp.zeros_like(dk2_ref)

        @pl.when(kj == 0)
        def _():
            dq_s[...] = jnp.zeros_like(dq_s)
            if extra:
                dq2_s[...] = jnp.zeros_like(dq2_s)
            for hh in range(ATTN_HEADS):
                delta_s[hh] = jnp.sum(_hd(do_ref, hh).astype(F32) * _hd(o_ref, hh).astype(F32), axis=-1, keepdims=True)

        if bias:
            @pl.when(kj > qi)
            def _():
                dcs_ref[...] = jnp.zeros_like(dcs_ref)

        def step(diagonal):
            ks = pl.ds(pl.multiple_of(kj * tq, tq), tq)
            for hh in range(ATTN_HEADS):
                cols = slice(hh * HEAD_DIM, (hh + 1) * HEAD_DIM)
                qh, kh = _hd(q_ref, hh).astype(BF16), _hd(k_ref, hh).astype(BF16)
                q2h = _hd(q2_ref, hh).astype(BF16) if extra else None
                k2h = k2_ref[...].astype(BF16) if extra else None
                s, allowed = _scores(qh, kh, q2h, k2h, cc_ref[hh] if bias else None, cr_ref[hh] if bias else None,
                                     scale, tq, diagonal)
                pr = jnp.exp(s - lse_ref[hh])
                if diagonal:
                    pr = jnp.where(allowed, pr, 0.0)
                dob = _hd(do_ref, hh).astype(BF16)
                dp = _nt(dob, _hd(v_ref, hh).astype(BF16))
                ds = pr * (dp - delta_s[hh])
                dsb = (ds * scale).astype(BF16)
                dq_s[hh] += _nn(dsb, kh)
                dk_ref[ks, cols] += _tn(dsb, qh)
                dv_ref[ks, cols] += _tn(pr.astype(BF16), dob)
                if extra:
                    dq2_s[hh] += _nn(dsb, k2h)
                    dk2_ref[ks, :] += _tn(dsb, q2h)
                if bias:
                    dcs_ref[hh] = jnp.sum(ds, axis=0, keepdims=True)

        _on_blocks(qi, kj, step)

        @pl.when(kj == qi)
        def _():
            for hh in range(ATTN_HEADS):
                cols = slice(hh * HEAD_DIM, (hh + 1) * HEAD_DIM)
                dq_ref[:, cols] = dq_s[hh]
                if extra:
                    dq2_ref[:, cols] = dq2_s[hh]

    w = HEAD_DIM * ATTN_HEADS
    ins = [q, k, v] + ([q2, k2] if extra else []) + (list(cum) if bias else []) + [o, do, lse]
    in_specs = _attn_specs(tq, qoff, koff, voff, extra, bias, q2off) + [
        pl.BlockSpec((tq, w), lambda h, i, j: (i, h)), pl.BlockSpec((tq, w), lambda h, i, j: (i, h)),
        pl.BlockSpec((ATTN_HEADS, tq, 1), lambda h, i, j: (h, i, 0))]
    full = _sds((s_len, N_HEADS * d), F32)
    out_shape = [full, full, full]
    out_specs = [pl.BlockSpec((tq, w), lambda h, i, j: (i, h)), pl.BlockSpec((s_len, w), lambda h, i, j: (0, h)),
                 pl.BlockSpec((s_len, w), lambda h, i, j: (0, h))]
    scratch = [pltpu.VMEM((ATTN_HEADS, tq, d), F32), pltpu.VMEM((ATTN_HEADS, tq, 1), F32)]
    if extra:
        out_shape += [full, _sds((s_len, d), F32)]
        out_specs += [pl.BlockSpec((tq, w), lambda h, i, j: (i, h)), pl.BlockSpec((s_len, d), lambda h, i, j: (0, 0))]
        scratch.append(pltpu.VMEM((ATTN_HEADS, tq, d), F32))
    if bias:
        out_shape.append(_sds((N_HEADS, nq, 1, s_len), F32))
        out_specs.append(pl.BlockSpec((ATTN_HEADS, None, 1, tq), lambda h, i, j: (h, i, 0, j)))
    return pl.pallas_call(body, grid=(ATTN_GRID_HEADS, nq, nq), in_specs=in_specs, out_specs=out_specs,
                          out_shape=out_shape, scratch_shapes=scratch, name=name, compiler_params=_cparams(3))(*ins)


def _sb_terms(q, k, scale, tq, diagonal):
    z = _nt(q.astype(BF16), k.astype(BF16)) * scale
    lg = jnp.log(1.0 + jnp.exp(-jnp.abs(z)))
    log_keep = -(jnp.maximum(z, 0.0) + lg)
    log_beta = jnp.minimum(z, 0.0) - lg
    if not diagonal:
        return None, log_keep, log_beta
    strict = lax.broadcasted_iota(jnp.int32, (tq, tq), 1) < lax.broadcasted_iota(jnp.int32, (tq, tq), 0)
    return strict, jnp.where(strict, log_keep, 0.0), log_beta


def _tri(tq, pred):
    a = lax.broadcasted_iota(jnp.int32, (tq, tq), 0)
    b = lax.broadcasted_iota(jnp.int32, (tq, tq), 1)
    return jnp.where(pred(a, b), 1.0, 0.0).astype(BF16)


SUM_CHUNK = 256


def _lane_sums(x, later):
    n = x.shape[1]
    chunk = min(SUM_CHUNK, n)
    tri = _tri(chunk, (lambda m, j: m > j) if later else (lambda m, j: m < j))
    order = range(n // chunk - 1, -1, -1) if later else range(n // chunk)
    pieces, carry = [None] * (n // chunk), None
    for cidx in order:
        xc = x[:, cidx * chunk:(cidx + 1) * chunk]
        hi = xc.astype(BF16)
        local = _nn(hi, tri) + _nn((xc - hi.astype(F32)).astype(BF16), tri)
        total = jnp.sum(xc, axis=-1, keepdims=True)
        pieces[cidx] = local if carry is None else local + carry
        carry = total if carry is None else carry + total
    return jnp.concatenate(pieces, axis=1), carry


def sb_fwd(name, qn, kn, qkv, scale, voff, tq=ATTN_BLOCK):
    s_len = qn.shape[0]
    nq = s_len // tq
    d = HEAD_DIM

    def body(q_ref, k_ref, v_ref, o_ref, car_ref, ca_s, acc_s):
        qi, kj = pl.program_id(1), pl.program_id(2)

        @pl.when(kj == 0)
        def _():
            ca_s[...] = jnp.zeros_like(ca_s)
            acc_s[...] = jnp.zeros_like(acc_s)

        def step(diagonal):
            for hh in range(ATTN_HEADS):
                strict, log_keep, log_beta = _sb_terms(_hd(q_ref, hh), _hd(k_ref, hh), scale, tq, diagonal)
                ca = ca_s[hh]
                car_ref[hh] = ca
                after, total = _lane_sums(log_keep, True)
                a = jnp.exp(log_beta + (after + ca))
                if diagonal:
                    a = jnp.where(strict, a, 0.0)
                acc_s[hh] += _nn(a.astype(BF16), _hd(v_ref, hh).astype(BF16))
                ca_s[hh] = ca + total

        @pl.when(kj == 0)
        def _():
            step(True)

        @pl.when(jnp.logical_and(kj > 0, kj <= qi))
        def _():
            step(False)

        @pl.when(kj == qi)
        def _():
            for hh in range(ATTN_HEADS):
                o_ref[:, hh * HEAD_DIM:(hh + 1) * HEAD_DIM] = acc_s[hh].astype(o_ref.dtype)

    kblk = lambda i, j: jnp.maximum(i - j, 0)
    w, vc = HEAD_DIM * ATTN_HEADS, _head_cols(voff)
    return pl.pallas_call(
        body, grid=(ATTN_GRID_HEADS, nq, nq),
        in_specs=[pl.BlockSpec((tq, w), lambda h, i, j: (i, h)), pl.BlockSpec((tq, w), lambda h, i, j: (kblk(i, j), h)),
                  pl.BlockSpec((tq, w), lambda h, i, j: (kblk(i, j), vc + h))],
        out_specs=[pl.BlockSpec((tq, w), lambda h, i, j: (i, h)),
                   pl.BlockSpec((ATTN_HEADS, None, tq, 1), lambda h, i, j: (h, kblk(i, j), i, 0))],
        out_shape=[_sds((s_len, N_HEADS * d), BF16), _sds((N_HEADS, nq, s_len, 1), F32)],
        scratch_shapes=[pltpu.VMEM((ATTN_HEADS, tq, 1), F32), pltpu.VMEM((ATTN_HEADS, tq, d), F32)],
        name=name, compiler_params=_cparams(3))(qn, kn, qkv)


def sb_bwd(name, qn, kn, qkv, do, carries, scale, voff, tq=ATTN_BLOCK):
    s_len = qn.shape[0]
    nq = s_len // tq
    d = HEAD_DIM

    def body(q_ref, k_ref, v_ref, do_ref, car_ref, dq_ref, dk_ref, dv_ref, dq_s, cg_s):
        qi, kj = pl.program_id(1), pl.program_id(2)

        @pl.when(jnp.logical_and(qi == 0, kj == 0))
        def _():
            dk_ref[...] = jnp.zeros_like(dk_ref)
            dv_ref[...] = jnp.zeros_like(dv_ref)

        @pl.when(kj == 0)
        def _():
            dq_s[...] = jnp.zeros_like(dq_s)
            cg_s[...] = jnp.zeros_like(cg_s)

        def step(diagonal):
            ks = pl.ds(pl.multiple_of(kj * tq, tq), tq)
            for hh in range(ATTN_HEADS):
                cols = slice(hh * HEAD_DIM, (hh + 1) * HEAD_DIM)
                qh, kh = _hd(q_ref, hh).astype(BF16), _hd(k_ref, hh).astype(BF16)
                strict, log_keep, log_beta = _sb_terms(qh, kh, scale, tq, diagonal)
                after, _ = _lane_sums(log_keep, True)
                a = jnp.exp(log_beta + (after + car_ref[hh]))
                if diagonal:
                    a = jnp.where(strict, a, 0.0)
                dob = _hd(do_ref, hh).astype(BF16)
                g = a * _nt(dob, _hd(v_ref, hh).astype(BF16))
                cg = cg_s[hh]
                before, total = _lane_sums(g, False)
                big_g = before + cg
                cg_s[hh] = cg + total
                beta = jnp.exp(log_beta)
                dz = g * (1.0 - beta) - big_g * beta
                if diagonal:
                    dz = jnp.where(strict, dz, 0.0)
                dzb = (dz * scale).astype(BF16)
                dq_s[hh] += _nn(dzb, kh)
                dk_ref[ks, cols] += _tn(dzb, qh)
                dv_ref[ks, cols] += _tn(a.astype(BF16), dob)

        _on_blocks(qi, kj, step)

        @pl.when(kj == qi)
        def _():
            for hh in range(ATTN_HEADS):
                dq_ref[:, hh * HEAD_DIM:(hh + 1) * HEAD_DIM] = dq_s[hh]

    kblk = lambda i, j: jnp.minimum(i, j)
    full = _sds((s_len, N_HEADS * d), F32)
    w, vc = HEAD_DIM * ATTN_HEADS, _head_cols(voff)
    return pl.pallas_call(
        body, grid=(ATTN_GRID_HEADS, nq, nq),
        in_specs=[pl.BlockSpec((tq, w), lambda h, i, j: (i, h)), pl.BlockSpec((tq, w), lambda h, i, j: (kblk(i, j), h)),
                  pl.BlockSpec((tq, w), lambda h, i, j: (kblk(i, j), vc + h)),
                  pl.BlockSpec((tq, w), lambda h, i, j: (i, h)),
                  pl.BlockSpec((ATTN_HEADS, None, tq, 1), lambda h, i, j: (h, kblk(i, j), i, 0))],
        out_specs=[pl.BlockSpec((tq, w), lambda h, i, j: (i, h)), pl.BlockSpec((s_len, w), lambda h, i, j: (0, h)),
                   pl.BlockSpec((s_len, w), lambda h, i, j: (0, h))],
        out_shape=[full, full, full],
        scratch_shapes=[pltpu.VMEM((ATTN_HEADS, tq, d), F32), pltpu.VMEM((ATTN_HEADS, tq, 1), F32)],
        name=name, compiler_params=_cparams(3))(qn, kn, qkv, do, carries)


def _cumsum_rows(x, reverse):
    n = x.shape[0] // HEAD_DIM
    tri = _tri(HEAD_DIM, (lambda a, b: b >= a) if reverse else (lambda a, b: b <= a))
    pieces = [None] * n
    carry = jnp.zeros((1, HEAD_DIM), F32)
    order = range(n - 1, -1, -1) if reverse else range(n)
    for blk in order:
        xb = x[blk * HEAD_DIM:(blk + 1) * HEAD_DIM, :]
        x1 = xb.astype(BF16)
        r1 = xb - x1.astype(F32)
        x2 = r1.astype(BF16)
        x3 = (r1 - x2.astype(F32)).astype(BF16)
        c = _nn(tri, x1) + _nn(tri, x2) + _nn(tri, x3) + carry
        pieces[blk] = c
        carry = c[0:1, :] if reverse else c[HEAD_DIM - 1:HEAD_DIM, :]
    return jnp.concatenate(pieces, axis=0)


def _log_sigmoid(x):
    return jnp.minimum(x, 0.0) - jnp.log(1.0 + jnp.exp(-jnp.abs(x)))


def fox_gate_fwd(name, fl, bf):
    return blockk(name, lambda f, b: _cumsum_rows(_log_sigmoid(f + b), False), (1, 1),
                  [(fl, _whole(fl.shape)), (bf, _whole(bf.shape))], [(_sds(fl.shape, F32), _whole(fl.shape), None)])[0]


def fox_gate_bwd(name, fl, bf, dcum):
    def fn(f, b, dc):
        dlogf = _cumsum_rows(dc, True)
        dfl = dlogf * _sig(-(f + b))
        return dfl, jnp.sum(dfl, axis=0, keepdims=True)

    return blockk(name, fn, (1, 1), [(fl, _whole(fl.shape)), (bf, _whole(bf.shape)), (dcum, _whole(dcum.shape))],
                  [(_sds(fl.shape, F32), _whole(fl.shape), None), (_sds(bf.shape, F32), _whole(bf.shape), None)])


def rope_fwd(name, x, gain, cos, sin, tr, xcol, nh):
    r = x.shape[0]
    tr = r if nh > 1 else tr
    xspec = _rows(tr, HEAD_DIM, lambda j: xcol + j)
    tspec = _rows(tr, HEAD_DIM, 0)
    return blockk(name, lambda xb, g, c, s: _rope(_rms(xb.astype(F32), g, MLA_ROPE), c, s), (r // tr, nh),
                  [(x, xspec), (gain, _whole(gain.shape)), (cos, tspec), (sin, tspec)],
                  [(_sds((r, HEAD_DIM * nh), BF16), _rows(tr, HEAD_DIM, 'j'), None)])[0]


def rope_bwd(name, x, gain, cos, sin, dy, tr, xcol, nh):
    r = x.shape[0]
    tr = r if nh > 1 else tr
    xspec = _rows(tr, HEAD_DIM, lambda j: xcol + j)
    tspec = _rows(tr, HEAD_DIM, 0)
    ospec = _rows(tr, HEAD_DIM, 'j')

    def fn(xb, g, c, s, dyb):
        _, vjp = jax.vjp(lambda a, b: _rms(a, b, MLA_ROPE), xb.astype(F32), g)
        return vjp(_rope_t(dyb.astype(F32), c, s))

    return blockk(name, fn, (r // tr, nh),
                  [(x, xspec), (gain, _whole(gain.shape)), (cos, tspec), (sin, tspec), (dy, ospec)],
                  [(_sds((r, HEAD_DIM * nh), F32), ospec, None), (_sds(gain.shape, F32), _whole(gain.shape), 'all')])


def sgu_pre_fwd(name, pre, gain, tr):
    s_len, w2 = pre.shape
    w = w2 // 2
    return blockk(name, lambda pu, pv, g: (_gelu(pu.astype(F32)), _rms(_gelu(pv.astype(F32)), g, w)), (s_len // tr, 1),
                  [(pre, _rows(tr, w, 0)), (pre, _rows(tr, w, 1)), (gain, _whole(gain.shape))],
                  [(_sds((s_len, w), BF16), _rows(tr, w, 0), None), (_sds((s_len, w), BF16), _rows(tr, w, 0), None)])


def sgu_pre_bwd(name, pre, gain, du, dvn, tr):
    s_len, w2 = pre.shape
    w = w2 // 2

    def fn(pu, pv, g, dub, dvb):
        _, vjp_u = jax.vjp(_gelu, pu.astype(F32))
        _, vjp_v = jax.vjp(lambda a, b: _rms(_gelu(a), b, w), pv.astype(F32), g)
        dpv, dg = vjp_v(dvb.astype(F32))
        return vjp_u(dub.astype(F32))[0], dpv, dg

    spec = _rows(tr, w, 0)
    return blockk(name, fn, (s_len // tr, 1),
                  [(pre, spec), (pre, _rows(tr, w, 1)), (gain, _whole(gain.shape)), (du, spec), (dvn, spec)],
                  [(_sds((s_len, w), BF16), spec, None), (_sds((s_len, w), BF16), spec, None),
                   (_sds(gain.shape, F32), _whole(gain.shape), 'all')])


def _ws_masked(ws):
    t = ws.shape[0]
    a = lax.broadcasted_iota(jnp.int32, (t, t), 0)
    b = lax.broadcasted_iota(jnp.int32, (t, t), 1)
    return jnp.where(b <= a, ws, 0.0)


def sgu_mix_fwd(name, vn, u, ws, bs3):
    s_len, w = vn.shape
    t = ws.shape[1]

    def fn(vb, ub, wsb, bsb):
        mixed = _nn(_ws_masked(wsb).astype(BF16), vb.astype(BF16)) + bsb
        return ub.astype(F32) * mixed

    blk = pl.BlockSpec((t, t), lambda i, j: (i, j))
    return blockk(name, fn, (s_len // t, w // t),
                  [(vn, blk), (u, blk), (ws, pl.BlockSpec((None, t, t), lambda i, j: (j, 0, 0))),
                   (bs3, pl.BlockSpec((None, t, 1), lambda i, j: (j, 0, 0)))],
                  [(_sds((s_len, w), BF16), blk, None)])[0]


def sgu_mix_bwd(name, vn, u, ws, bs3, dprod):
    s_len, w = vn.shape
    t = ws.shape[1]

    def fn(vb, ub, wsb, bsb, dpb):
        wm = _ws_masked(wsb).astype(BF16)
        vb16 = vb.astype(BF16)
        mixed = _nn(wm, vb16) + bsb
        dp = dpb.astype(F32)
        du = dp * mixed
        dm = dp * ub.astype(F32)
        dmb = dm.astype(BF16)
        dvn = _tn(wm, dmb)
        dws = _ws_masked(_nt(dmb, vb16))
        return du, dvn, dws, jnp.sum(dm, axis=-1, keepdims=True)

    blk = pl.BlockSpec((t, t), lambda g, n: (n, g))
    wspec = pl.BlockSpec((None, t, t), lambda g, n: (g, 0, 0))
    bspec = pl.BlockSpec((None, t, 1), lambda g, n: (g, 0, 0))
    return blockk(name, fn, (w // t, s_len // t), [(vn, blk), (u, blk), (ws, wspec), (bs3, bspec), (dprod, blk)],
                  [(_sds((s_len, w), BF16), blk, None), (_sds((s_len, w), BF16), blk, None),
                   (_sds(ws.shape, F32), wspec, 'inner'), (_sds(bs3.shape, F32), bspec, 'inner')])


def _shift_down(x, k):
    row = lax.broadcasted_iota(jnp.int32, x.shape, 0)
    return jnp.where(row >= k, pltpu.roll(x, k, 0), 0.0)


def _shift_up(x, k):
    n = x.shape[0]
    row = lax.broadcasted_iota(jnp.int32, x.shape, 0)
    return jnp.where(row < n - k, pltpu.roll(x, n - k, 0), 0.0)


def _conv(up, cw, cb):
    return cb + cw[0:1, :] * _shift_down(up, 2) + cw[1:2, :] * _shift_down(up, 1) + cw[2:3, :] * up


def _conv_specs(s_len, tc, nf):
    g = pl.BlockSpec((s_len, tc), lambda i, j: (0, j))
    v = pl.BlockSpec((s_len, tc), lambda i, j: (0, j + nf))
    wg = pl.BlockSpec((3, tc), lambda i, j: (0, j))
    wv = pl.BlockSpec((3, tc), lambda i, j: (0, j + nf))
    bg = pl.BlockSpec((1, tc), lambda i, j: (0, j))
    bv = pl.BlockSpec((1, tc), lambda i, j: (0, j + nf))
    return g, v, wg, wv, bg, bv


def conv_fwd(name, up, cw, cb, tc=256):
    s_len, f2 = up.shape
    f = f2 // 2
    nf = f // tc
    g, v, wg, wv, bg, bv = _conv_specs(s_len, tc, nf)

    def fn(ug, uv, cwg, cwv, cbg, cbv):
        yg = _conv(ug.astype(F32), cwg, cbg)
        yv = _conv(uv.astype(F32), cwv, cbv)
        return yg * _sig(yg) * yv

    return blockk(name, fn, (1, nf), [(up, g), (up, v), (cw, wg), (cw, wv), (cb, bg), (cb, bv)],
                  [(_sds((s_len, f), BF16), g, None)])[0]


def conv_bwd(name, up, cw, cb, dact, tc=128):
    s_len, f2 = up.shape
    f = f2 // 2
    nf = f // tc
    g, v, wg, wv, bg, bv = _conv_specs(s_len, tc, nf)

    def half(dy, u, cwh):
        u1, u2 = _shift_down(u, 1), _shift_down(u, 2)
        dup = cwh[2:3, :] * dy + cwh[1:2, :] * _shift_up(dy, 1) + cwh[0:1, :] * _shift_up(dy, 2)
        dcw = jnp.concatenate([jnp.sum(dy * u2, axis=0, keepdims=True), jnp.sum(dy * u1, axis=0, keepdims=True),
                               jnp.sum(dy * u, axis=0, keepdims=True)], axis=0)
        return dup, dcw, jnp.sum(dy, axis=0, keepdims=True)

    def fn(ug, uv, cwg, cwv, cbg, cbv, da):
        ug, uv, da = ug.astype(F32), uv.astype(F32), da.astype(F32)
        yg = _conv(ug, cwg, cbg)
        yv = _conv(uv, cwv, cbv)
        sg = _sig(yg)
        dyv = da * (yg * sg)
        dyg = da * yv * (sg * (1.0 + yg * (1.0 - sg)))
        dug, dcwg, dcbg = half(dyg, ug, cwg)
        duv, dcwv, dcbv = half(dyv, uv, cwv)
        return dug, duv, dcwg, dcwv, dcbg, dcbv

    return blockk(name, fn, (1, nf), [(up, g), (up, v), (cw, wg), (cw, wv), (cb, bg), (cb, bv), (dact, g)],
                  [(_sds((s_len, f), BF16), g, None), (_sds((s_len, f), BF16), g, None),
                   (_sds((3, f), F32), wg, None), (_sds((3, f), F32), wg, None),
                   (_sds((1, f), F32), bg, None), (_sds((1, f), F32), bg, None)])


def loss_head(name, y, target, tr=256):
    s_len, d = y.shape

    def fn(yb, tb):
        e = yb - tb
        dy = e * (1.0 / d)
        return dy, dy, jnp.sum(e * e, axis=0, keepdims=True)

    spec = _rows(tr, d, 0)
    return blockk(name, fn, (s_len // tr, 1), [(y, spec), (target, spec)],
                  [(_sds((s_len, d), BF16), spec, None), (_sds((s_len, d), F32), spec, None),
                   (_sds((1, d), F32), _whole((1, d)), 'all')])


def _adam(w, g, m, v):
    m = ADAM_B1 * m + (1.0 - ADAM_B1) * g
    v = ADAM_B2 * v + (1.0 - ADAM_B2) * (g * g)
    m_hat = m / (1.0 - ADAM_B1 ** ADAM_STEP)
    v_hat = v / (1.0 - ADAM_B2 ** ADAM_STEP)
    delta = -ADAM_LR * (m_hat / (jnp.sqrt(v_hat) + ADAM_EPS) + ADAM_WD * w)
    return delta, m, v


def adam_sum(name, parts, w, m, v):
    n_layers, r, c = w.shape
    n_parts = parts[0].shape[0]
    assert len(parts) == n_layers
    tr = r
    for cand in (512, 256, 128, 64, 32, 16):
        if r % cand == 0 and cand * c * 4 <= 1024 * 1024:
            tr = cand
            break

    def body(*refs):
        part_refs = refs[:n_layers]
        w_ref, m_ref, v_ref, g_out, d_out, m_out, v_out = refs[n_layers:]
        layer = pl.program_id(0)
        for ll in range(n_layers):
            @pl.when(layer == ll)
            def _(ll=ll):
                g = part_refs[ll][0].astype(F32)
                for k in range(1, n_parts):
                    g = g + part_refs[ll][k].astype(F32)
                delta, m_new, v_new = _adam(w_ref[...], g, m_ref[...], v_ref[...])
                g_out[...] = g
                d_out[...] = delta
                m_out[...] = m_new
                v_out[...] = v_new

    spec = pl.BlockSpec((None, tr, c), lambda l, i: (l, i, 0))
    part_specs = [pl.BlockSpec((n_parts, tr, c), lambda l, i, ll=ll: (0, jnp.where(l == ll, i, 0), 0))
                  for ll in range(n_layers)]
    out = _sds((n_layers, r, c), F32)
    return pl.pallas_call(body, grid=(n_layers, r // tr), in_specs=part_specs + [spec] * 3, out_specs=[spec] * 4,
                          out_shape=[out] * 4, name=name, compiler_params=_cparams(2))(*parts, w, m, v)


def sum_parts(name, parts, tr=256):
    n_parts, r, c = parts.shape

    def fn(pb):
        g = pb[0]
        for k in range(1, n_parts):
            g = g + pb[k]
        return g

    return blockk(name, fn, (r // tr, 1), [(parts, pl.BlockSpec((n_parts, tr, c), lambda i, j: (0, i, 0)))],
                  [(_sds((r, c), F32), _rows(tr, c, 0), None)])[0]


def adam_flat(name, w, g, m, v, tr=256):
    r, c = w.shape
    spec = _rows(tr, c, 0)
    return blockk(name, lambda wb, gb, mb, vb: _adam(wb, gb, mb, vb), (r // tr, 1),
                  [(w, spec), (g, spec), (m, spec), (v, spec)], [(_sds((r, c), F32), spec, None)] * 3)


_ANY = pl.BlockSpec(memory_space=pl.ANY)


def _place():
    return lax.axis_index("x"), lax.axis_index("y"), lax.axis_index("c")


def _slot(px, py, pc):
    return 4 * px + 2 * py + pc


def all_gather(name, items):
    n = len(items)

    def body(*refs):
        xs, outs = refs[:n], refs[n:2 * n]
        send_sems, recv_sems, local_sems = refs[2 * n:]
        x, y, c = _place()
        me, sibling = (x, y, c), (x, y, 1 - c)
        chips = [(1 - x, y), (x, 1 - y), (1 - x, 1 - y)]

        def copy(t, k, block, to, src=None):
            dst = outs[t].at[_slot(*block)]
            return pltpu.make_async_remote_copy(src_ref=dst if src is None else src, dst_ref=dst,
                                                send_sem=send_sems.at[7 * t + k], recv_sem=recv_sems.at[7 * t + k],
                                                device_id=to, device_id_type=MESH)

        mine = [pltpu.make_async_copy(xs[t], outs[t].at[_slot(*me)], local_sems.at[t]) for t in range(n)]
        for cp in mine:
            cp.start()
        started = []
        for t in range(n):
            started.append(copy(t, 0, me, sibling, src=xs[t]))
            started += [copy(t, 1 + j, me, (*chip, c), src=xs[t]) for j, chip in enumerate(chips)]
        for cp in started:
            cp.start()
        for j, chip in enumerate(chips):
            for t in range(n):
                copy(t, 1 + j, (*chip, c), me).wait_recv()
                passed = copy(t, 4 + j, (*chip, c), sibling)
                passed.start()
                started.append(passed)
        for t in range(n):
            copy(t, 0, sibling, me).wait_recv()
            for j, chip in enumerate(chips):
                copy(t, 4 + j, (*chip, 1 - c), me).wait_recv()
        for cp in started:
            cp.wait_send()
        for cp in mine:
            cp.wait()

    return pl.pallas_call(
        body, in_specs=[_ANY] * n, out_specs=[_ANY] * n,
        out_shape=[_sds((N_DEV,) + a.shape, a.dtype) for a in items],
        scratch_shapes=[pltpu.SemaphoreType.DMA((7 * n,)), pltpu.SemaphoreType.DMA((7 * n,)), pltpu.SemaphoreType.DMA((n,))],
        name=name)(*items)


_HBM = pl.BlockSpec(memory_space=pltpu.HBM)
_SEM = pl.BlockSpec(memory_space=pltpu.SEMAPHORE)
_EFFECT = pltpu.SideEffectType.DATAFLOW_SIDE_EFFECTING


def _peer(k, x, y, c):
    return ((1 - x) if k & 4 else x, (1 - y) if k & 2 else y, (1 - c) if k & 1 else c)


PEERS_ALL = (1, 2, 3, 4, 5, 6, 7)
PEERS_CHIPWISE = (1, 2, 4, 6)


def _exchange_copies(mode, refs, send_sems, recv_sems, landing):
    x, y, c = _place()
    my_slot = _slot(x, y, c)
    copies = []
    if mode == "pass_on":
        for t, land in enumerate(refs):
            for j, chip in enumerate([(1 - x, y), (x, 1 - y), (1 - x, 1 - y)]):
                slot = _slot(*chip, (1 - c) if landing else c)
                copies.append(pltpu.make_async_remote_copy(
                    src_ref=land.at[slot], dst_ref=land.at[slot], send_sem=send_sems.at[3 * t + j],
                    recv_sem=recv_sems.at[3 * t + j], device_id=(x, y, 1 - c), device_id_type=MESH))
        return copies
    n = len(refs) // 2
    for t, (src, land) in enumerate(zip(refs[:n], refs[n:])):
        for k in (PEERS_CHIPWISE if mode == "gather_chipwise" else PEERS_ALL):
            peer = _peer(k, x, y, c)
            copies.append(pltpu.make_async_remote_copy(
                src_ref=src.at[_slot(*peer)] if mode == "scatter" else src,
                dst_ref=land.at[_slot(*peer) if landing else my_slot],
                send_sem=send_sems.at[7 * t + k - 1], recv_sem=recv_sems.at[7 * t + k - 1],
                device_id=peer, device_id_type=MESH))
    return copies


def exchange_start(name, mode, arrays):
    n = len(arrays)
    passing = mode == "pass_on"
    n_sems = (3 if passing else 7) * n
    lands = [] if passing else [pltpu.HBM(a.shape if mode == "scatter" else (N_DEV,) + a.shape, a.dtype) for a in arrays]

    def body(*refs):
        srcs, outs = refs[:n], refs[n + 2:]
        bufs = list(srcs) if passing else list(srcs) + list(outs[n:2 * n])
        for send in _exchange_copies(mode, bufs, refs[n], refs[n + 1], False):
            send.start()
        refs[-1][...] = jnp.zeros_like(refs[-1])

    arrays = [pltpu.with_memory_space_constraint(a, pltpu.HBM) for a in arrays]
    res = pl.pallas_call(
        body, name=name, in_specs=[_HBM] * n,
        out_specs=(_SEM, _SEM) + (_HBM,) * (n + len(lands)) + (pl.BlockSpec(memory_space=pltpu.VMEM),),
        out_shape=(pltpu.SemaphoreType.DMA((n_sems,)), pltpu.SemaphoreType.DMA((n_sems,)))
        + tuple(pltpu.HBM(a.shape, a.dtype) for a in arrays) + tuple(lands) + (_sds((8, HEAD_DIM), F32),),
        input_output_aliases={i: 2 + i for i in range(n)},
        compiler_params=pltpu.CompilerParams(has_side_effects=_EFFECT))(*arrays)
    return dict(name=name, mode=mode, sems=res[:2], bufs=res[2:-1]), res[-1]


def exchange_wait(handle, after):
    mode, bufs = handle['mode'], handle['bufs']
    nb = len(bufs)

    def body(*refs):
        for landed in _exchange_copies(mode, refs[:nb], refs[nb], refs[nb + 1], True):
            landed.wait_send()
            landed.wait_recv()

    res = pl.pallas_call(
        body, name=handle['name'].replace("start", "wait"), in_specs=[_HBM] * nb + [_SEM, _SEM, _ANY],
        out_specs=(_HBM,) * nb, out_shape=tuple(pltpu.HBM(a.shape, a.dtype) for a in bufs),
        input_output_aliases={i: i for i in range(nb)},
        compiler_params=pltpu.CompilerParams(has_side_effects=_EFFECT))(*bufs, *handle['sems'], after)
    if mode == "pass_on":
        return list(res)
    my_slot = _slot(*_place())
    filled = []
    for src, land in zip(res[:nb // 2], res[nb // 2:]):
        own = lax.dynamic_index_in_dim(src, my_slot, 0, keepdims=True) if mode == "scatter" else src[None]
        filled.append(lax.dynamic_update_slice(land, own, (my_slot,) + (0,) * (land.ndim - 1)))
    return filled


def _pad_lanes(a, width=HEAD_DIM):
    return jnp.pad(a, [(0, 0)] * (a.ndim - 1) + [(0, width - a.shape[-1])])


def _slabs(full, n):
    k = full.shape[0]
    return full.reshape(k, N_DEV, n).transpose(1, 0, 2)


def fox_fwd(a, w, p):
    qkv = mm_nn(a, w['qkv'], BF16, "fox_qkv")
    fl = mm_nn(a, w['f'], F32, "fox_flogit")
    bf = _pad_lanes(p['fox_b_f'])
    cum = fox_gate_fwd("fox_gate_fwd", fl, bf)
    cum_t = cum[:, :N_HEADS].T
    cums = (cum_t[:, :, None], cum_t[:, None, :])
    qn = rms_fwd("fox_qnorm", qkv, p['fox_q_gain'], HEAD_DIM, 512, HEAD_DIM, xcol=0, nh=N_HEADS)
    kn = rms_fwd("fox_knorm", qkv, p['fox_k_gain'], HEAD_DIM, 512, HEAD_DIM, xcol=N_HEADS, nh=N_HEADS)
    o, lse = attn_fwd("fox_attn_fwd", qn, kn, qkv, HEAD_DIM ** -0.5, voff=2 * N_HEADS, cum=cums, exact_o=True)
    return o, dict(a=a, qkv=qkv, fl=fl, bf=bf, cums=cums, qn=qn, kn=kn, o=o, lse=lse)


def fox_bwd(do, s, w, p):
    dqn, dkn, dv, dcs = attn_bwd("fox_attn_bwd", s['qn'], s['kn'], s['qkv'], s['o'], do, s['lse'], HEAD_DIM ** -0.5,
                                 voff=2 * N_HEADS, cum=s['cums'])
    dq, dgq = rms_bwd("fox_qnorm_bwd", s['qkv'], p['fox_q_gain'], dqn, HEAD_DIM, 512, HEAD_DIM, xcol=0, nh=N_HEADS,
                      out_dtype=BF16)
    dk, dgk = rms_bwd("fox_knorm_bwd", s['qkv'], p['fox_k_gain'], dkn, HEAD_DIM, 512, HEAD_DIM, xcol=N_HEADS,
                      nh=N_HEADS, out_dtype=BF16)
    dqkv = jnp.concatenate([dq, dk, dv.astype(BF16)], axis=1)
    dcum = _pad_lanes(-jnp.sum(dcs[:, :, 0, :], axis=1).T)
    dfl, dbf = fox_gate_bwd("fox_gate_bwd", s['fl'], s['bf'], dcum)
    da = mm_nt(dqkv, w['qkv'], F32, "fox_da_qkv")
    da = mm_nt(dfl, w['f'], F32, "fox_da_f", add=da)
    dw_qkv = mm_tn(s['a'], dqkv, 1, BF16, "fox_dw_qkv")[0]
    dw_f = mm_tn(s['a'], dfl, 1, BF16, "fox_dw_f")[0][:, :N_HEADS]
    dw = _slabs(jnp.concatenate([dw_qkv, dw_f], axis=1), 770)
    return da, dict(fox_w_in=dw), dict(fox_b_f=dbf[:, :N_HEADS], fox_q_gain=dgq, fox_k_gain=dgk)


def _rope_tables(positions):
    inv_freq = ROPE_THETA ** (-jnp.arange(0, MLA_ROPE, 2, dtype=F32) / MLA_ROPE)
    ang = positions.astype(F32)[:, None] * inv_freq
    cos, sin = jnp.cos(ang), jnp.sin(ang)
    return _pad_lanes(jnp.concatenate([cos, cos], axis=1)), _pad_lanes(jnp.concatenate([sin, sin], axis=1))


def mla_fwd(a, w, p):
    qg, kg = p['mla_q_gain'], p['mla_k_gain']
    gains = dict(qn=qg[:, :HEAD_DIM], qr=_pad_lanes(qg[:, HEAD_DIM:]), kn=kg[:, :HEAD_DIM], kr=_pad_lanes(kg[:, HEAD_DIM:]))
    cos, sin = _rope_tables(p['positions'])
    ccr = mm_nn(a, w['in'], F32, "mla_in")
    cqn = rms_fwd("mla_cq_norm", ccr, w['q_a_gain'], 512, 256, 512, xcol=0)
    ckvn = rms_fwd("mla_ckv_norm", ccr, w['kv_a_gain'], 512, 256, 512, xcol=1)
    qf = mm_nn(cqn, w['q_b'], F32, "mla_q_b")
    kvf = mm_nn(ckvn, w['kv_b'], F32, "mla_kv_b")
    q_nope = rms_fwd("mla_qnope_norm", qf, gains['qn'], HEAD_DIM, 512, HEAD_DIM, xcol=0, nh=N_HEADS)
    k_nope = rms_fwd("mla_knope_norm", kvf, gains['kn'], HEAD_DIM, 512, HEAD_DIM, xcol=0, nh=N_HEADS)
    q_rope = rope_fwd("mla_qrope", qf, gains['qr'], cos, sin, 512, N_HEADS, N_HEADS)
    k_rope = rope_fwd("mla_krope", ccr, gains['kr'], cos, sin, 512, 8, 1)
    scale = (HEAD_DIM + MLA_ROPE) ** -0.5
    o, lse = attn_fwd("mla_attn_fwd", q_nope, k_nope, kvf, scale, voff=N_HEADS, q2=q_rope, k2=k_rope)
    return o, dict(a=a, gains=gains, cos=cos, sin=sin, ccr=ccr, cqn=cqn, ckvn=ckvn, qf=qf, kvf=kvf, q_nope=q_nope,
                   k_nope=k_nope, q_rope=q_rope, k_rope=k_rope, o=o, lse=lse, scale=scale)


def mla_bwd(do, s, w, p):
    g = s['gains']
    dqn, dkn, dv, dq2, dk2 = attn_bwd("mla_attn_bwd", s['q_nope'], s['k_nope'], s['kvf'], s['o'], do, s['lse'], s['scale'],
                                      voff=N_HEADS, q2=s['q_rope'], k2=s['k_rope'])
    dqf_n, dg_qn = rms_bwd("mla_qnope_bwd", s['qf'], g['qn'], dqn, HEAD_DIM, 512, HEAD_DIM, xcol=0, nh=N_HEADS)
    dkf, dg_kn = rms_bwd("mla_knope_bwd", s['kvf'], g['kn'], dkn, HEAD_DIM, 512, HEAD_DIM, xcol=0, nh=N_HEADS)
    dqf_r, dg_qr = rope_bwd("mla_qrope_bwd", s['qf'], g['qr'], s['cos'], s['sin'], dq2, 512, N_HEADS, N_HEADS)
    dkr, dg_kr = rope_bwd("mla_krope_bwd", s['ccr'], g['kr'], s['cos'], s['sin'], dk2, 512, 8, 1)
    dqf = jnp.concatenate([dqf_n, dqf_r], axis=1)
    dkvf = jnp.concatenate([dkf, dv], axis=1)
    dcqn = mm_nt(dqf, w['q_b'], F32, "mla_dcq")
    dckvn = mm_nt(dkvf, w['kv_b'], F32, "mla_dckv")
    dw_qb = mm_tn(s['cqn'], dqf, 1, BF16, "mla_dw_qb")[0]
    dw_kvb = mm_tn(s['ckvn'], dkvf, 1, BF16, "mla_dw_kvb")[0]
    dcq, dg_qa = rms_bwd("mla_cq_bwd", s['ccr'], w['q_a_gain'], dcqn, 512, 256, 512, xcol=0)
    dckv, dg_kva = rms_bwd("mla_ckv_bwd", s['ccr'], w['kv_a_gain'], dckvn, 512, 256, 512, xcol=1)
    dccr = jnp.concatenate([dcq, dckv, dkr], axis=1)
    da = mm_nt(dccr, w['in'], F32, "mla_da")
    dw_in = mm_tn(s['a'], dccr, 1, BF16, "mla_dw_in")[0][:, :1088].reshape(N_DEV, 256, 1088)
    hp = 2
    nope = dw_qb[:, :2048].reshape(512, N_DEV, hp, HEAD_DIM)
    rope = dw_qb[:, 2048:].reshape(512, N_DEV, hp, HEAD_DIM)[..., :MLA_ROPE]
    dw_qb_s = jnp.concatenate([nope, rope], axis=-1).transpose(1, 0, 2, 3).reshape(N_DEV, 512, hp * 192)
    kk = dw_kvb[:, :2048].reshape(512, N_DEV, hp, HEAD_DIM)
    vv = dw_kvb[:, 2048:].reshape(512, N_DEV, hp, HEAD_DIM)
    dw_kvb_s = jnp.concatenate([kk, vv], axis=-1).transpose(1, 0, 2, 3).reshape(N_DEV, 512, hp * 256)
    small = dict(mla_q_a_gain=dg_qa, mla_kv_a_gain=dg_kva,
                 mla_q_gain=jnp.concatenate([dg_qn, dg_qr[:, :MLA_ROPE]], axis=1),
                 mla_k_gain=jnp.concatenate([dg_kn, dg_kr[:, :MLA_ROPE]], axis=1))
    return da, dict(mla_w_in=dw_in, mla_w_q_b=dw_qb_s, mla_w_kv_b=dw_kvb_s), small


def sb_fwd_layer(a, w, p):
    qkv = mm_nn(a, w['in'], BF16, "sb_qkv")
    qn = rms_fwd("sb_qnorm", qkv, p['sb_q_gain'], HEAD_DIM, 512, HEAD_DIM, xcol=0, nh=N_HEADS)
    kn = rms_fwd("sb_knorm", qkv, p['sb_k_gain'], HEAD_DIM, 512, HEAD_DIM, xcol=N_HEADS, nh=N_HEADS)
    o, carries = sb_fwd("sb_attn_fwd", qn, kn, qkv, HEAD_DIM ** -0.5, 2 * N_HEADS)
    return o, dict(a=a, qkv=qkv, qn=qn, kn=kn, carries=carries)


def sb_bwd_layer(do, s, w, p):
    dqn, dkn, dv = sb_bwd("sb_attn_bwd", s['qn'], s['kn'], s['qkv'], do, s['carries'], HEAD_DIM ** -0.5, 2 * N_HEADS)
    dq, dgq = rms_bwd("sb_qnorm_bwd", s['qkv'], p['sb_q_gain'], dqn, HEAD_DIM, 512, HEAD_DIM, xcol=0, nh=N_HEADS,
                      out_dtype=BF16)
    dk, dgk = rms_bwd("sb_knorm_bwd", s['qkv'], p['sb_k_gain'], dkn, HEAD_DIM, 512, HEAD_DIM, xcol=N_HEADS, nh=N_HEADS,
                      out_dtype=BF16)
    dqkv = jnp.concatenate([dq, dk, dv.astype(BF16)], axis=1)
    da = mm_nt(dqkv, w['in'], F32, "sb_da")
    dw = mm_tn(s['a'], dqkv, N_DEV, BF16, "sb_dw_in")
    return da, dict(sb_w_in=dw), dict(sb_q_gain=dgq, sb_k_gain=dgk)


def sgu_fwd(a, w, p):
    pre = mm_nn(a, w['in'], BF16, "sgu_in")
    u, vn = sgu_pre_fwd("sgu_pre_fwd", pre, w['v_gain'], 256)
    ws = p['sgu_w_s'][0]
    bs3 = p['sgu_b_s'][0][:, :, None]
    prod = sgu_mix_fwd("sgu_mix_fwd", vn, u, ws, bs3)
    return prod, dict(a=a, pre=pre, u=u, vn=vn, ws=ws, bs3=bs3)


def sgu_bwd(dprod, s, w, p):
    du, dvn, dws, dbs3 = sgu_mix_bwd("sgu_mix_bwd", s['vn'], s['u'], s['ws'], s['bs3'], dprod)
    dpu, dpv, dgv = sgu_pre_bwd("sgu_pre_bwd", s['pre'], w['v_gain'], du, dvn, 128)
    dpre = jnp.concatenate([dpu, dpv], axis=1)
    da = mm_nt(dpre, w['in'], F32, "sgu_da")
    dw = mm_tn(s['a'], dpre, N_DEV, BF16, "sgu_dw_in")
    return da, dict(sgu_w_in=dw), dict(sgu_v_gain=dgv, sgu_w_s=dws[None], sgu_b_s=dbs3[None, :, :, 0])


MIXERS = [("fox", fox_fwd, fox_bwd), ("mla", mla_fwd, mla_bwd), ("sb", sb_fwd_layer, sb_bwd_layer),
          ("sgu", sgu_fwd, sgu_bwd)]


def _pack_rows(arrays, row_mult=256):
    flat = jnp.concatenate([a.reshape(-1).astype(F32) for a in arrays])
    per = row_mult * HEAD_DIM
    total = -(-flat.shape[0] // per) * per
    return jnp.pad(flat, (0, total - flat.shape[0])).reshape(total // HEAD_DIM, HEAD_DIM)


def _unpack_rows(packed, shapes):
    flat = packed.reshape(-1)
    out, off = [], 0
    for shp in shapes:
        n = math.prod(shp)
        out.append(flat[off:off + n].reshape(shp))
        off += n
    return out


def _mixer_shards(i, p):
    name = MIXERS[i][0]
    bf = lambda a: a.astype(BF16)
    items = {'out': bf(p[name + '_w_out'][0])}
    if name == "fox":
        items['in'] = bf(p['fox_w_in'][0])
        items['small'] = _pack_rows([p[n] for n in SMALL_SHARDED], 8)
    elif name == "mla":
        items.update({'in': bf(p['mla_w_in'][0]), 'q_b': bf(p['mla_w_q_b'][0]), 'kv_b': bf(p['mla_w_kv_b'][0])})
    else:
        items['in'] = bf(p[name + '_w_in'][0])
    return items


def _ffn_shards(i, p):
    return {'up': p['ffn_w_up'][i].astype(BF16), 'down': p['ffn_w_down'][i].astype(BF16)}


def _assemble_ffn(got):
    return {'up': got['up'], 'down': got['down'].reshape(1, D_FF, -1)}


def _assemble_mixer(i, got, p):
    name = MIXERS[i][0]
    w = {'out': got['out'].reshape(1, -1, got['out'].shape[-1])}
    small = None
    if name == "fox":
        small_shapes = [p[n].shape for n in SMALL_SHARDED]
        full = got['in'].transpose(1, 0, 2).reshape(got['in'].shape[1], -1)
        w['qkv'] = full[None, :, :3 * N_HEADS * HEAD_DIM]
        w['f'] = _pad_lanes(full[:, 3 * N_HEADS * HEAD_DIM:])[None]
        parts = [_unpack_rows(got['small'][d], small_shapes) for d in range(N_DEV)]
        small = {n: jnp.concatenate([parts[d][k] for d in range(N_DEV)], axis=ax)
                 for k, (n, ax) in enumerate(SMALL_SHARDED.items())}
    elif name == "mla":
        w['in'] = _pad_lanes(got['in'].reshape(-1, 1088), 1152)[None]
        hp = 2
        qb = got['q_b'].reshape(N_DEV, 512, hp, 192).transpose(1, 0, 2, 3)
        nope = qb[..., :HEAD_DIM].reshape(512, -1)
        rope = _pad_lanes(qb[..., HEAD_DIM:]).reshape(512, -1)
        w['q_b'] = jnp.concatenate([nope, rope], axis=1)[None]
        kvb = got['kv_b'].reshape(N_DEV, 512, hp, 256).transpose(1, 0, 2, 3)
        w['kv_b'] = jnp.concatenate([kvb[..., :HEAD_DIM].reshape(512, -1), kvb[..., HEAD_DIM:].reshape(512, -1)], axis=1)[None]
    else:
        w['in'] = got['in']
    return w, small


def _train_step(p):
    x, target = p['x'][0], p['loss_target'][0]
    p = dict(p, positions=p['positions'][0])
    xi, yi, ci = _place()
    my_slot = _slot(xi, yi, ci)

    shards0 = _mixer_shards(0, p)
    got0 = dict(zip(shards0, all_gather("gather_first", list(shards0.values()))))
    pending, order_token = {}, jnp.zeros((1, 1), F32)
    for i in range(DEPTH):
        for kind, shards in (("mix", _mixer_shards(i, p) if i else None), ("ffn", _ffn_shards(i, p))):
            if shards is not None:
                handle, token = exchange_start(f"xstart_ag_{kind}{i}", "gather_chipwise", list(shards.values()))
                pending[kind, i] = (handle, list(shards))
                order_token = order_token + token[0:1, 0:1]

    def pass_on(kind, i, after):
        handle, keys = pending[kind, i]
        handle, token = exchange_start(f"xstart_pass_{kind}{i}", "pass_on", exchange_wait(handle, after))
        pending[kind, i] = (handle, keys)
        return token[0:1, 0:1]

    def gathered(kind, i, after):
        handle, keys = pending[kind, i]
        return dict(zip(keys, exchange_wait(handle, after)))

    h = x
    saved, weights = [], []
    small_full = None
    for i in range(DEPTH):
        if i == 0:
            w, small_full = _assemble_mixer(0, got0, p)
            gain = p['mix_norm'][0:1] + order_token
        else:
            w, _ = _assemble_mixer(i, gathered("mix", i, h), p)
            gain = p['mix_norm'][i:i + 1] + pass_on("ffn", i, h)
        w.update(q_a_gain=small_full['mla_q_a_gain'], kv_a_gain=small_full['mla_kv_a_gain'],
                 v_gain=small_full['sgu_v_gain'])
        conv_w = small_full['ffn_conv_w']
        a = rms_fwd(f"mix_norm_{i}", h, gain, h.shape[1], 256, h.shape[1])
        mixed, s_mix = MIXERS[i][1](a, w, p)
        if i == 0:
            pass_on("ffn", 0, mixed)
        w.update(_assemble_ffn(gathered("ffn", i, mixed)))
        weights.append(w)
        h1 = mm_nn(mixed, w['out'], F32, f"mix_out_{i}", add=h)
        b = rms_fwd(f"ffn_norm_{i}", h1, p['ffn_norm'][i:i + 1], h.shape[1], 256, h.shape[1])
        up = mm_nn(b, w['up'], BF16, f"ffn_up_{i}")
        conv_b = p['ffn_conv_b'][i:i + 1]
        if i + 1 < DEPTH:
            conv_b = conv_b + pass_on("mix", i + 1, up)
        act = conv_fwd(f"ffn_conv_{i}", up, conv_w[i], conv_b)
        h2 = mm_nn(act, w['down'], F32, f"ffn_down_{i}", add=h1)
        saved.append(dict(h=h, mixed=mixed, s_mix=s_mix, h1=h1, b=b, up=up, act=act))
        h = h2

    dh16, dh, sq = loss_head("loss_head", h, target)
    loss = lax.psum(0.5 * jnp.sum(sq) / h.shape[1], ("x", "y", "c"))

    scatters = []
    small_g = {n: [None] * p[n].shape[0] for n in ('mix_norm', 'ffn_norm', 'ffn_conv_w', 'ffn_conv_b')}
    for i in reversed(range(DEPTH)):
        w, s = weights[i], saved[i]
        name = MIXERS[i][0]
        dact = mm_nt(dh16, w['down'], BF16, f"ffn_dact_{i}")
        dw_down = mm_tn(s['act'], dh16, 1, BF16, f"ffn_dw_down_{i}").reshape(N_DEV, D_FF // N_DEV, -1)
        dug, duv, dcwg, dcwv, dcbg, dcbv = conv_bwd(f"ffn_conv_bwd_{i}", s['up'], conv_w[i], p['ffn_conv_b'][i:i + 1], dact)
        dup = jnp.concatenate([dug, duv], axis=1)
        dw_up = mm_tn(s['b'], dup, N_DEV, BF16, f"ffn_dw_up_{i}")
        handle, token = exchange_start(f"xstart_rs_ffn{i}", "scatter", [dw_up, dw_down])
        scatters.append((handle, ['ffn_w_up', 'ffn_w_down']))
        db = mm_nt(dup, w['up'], F32, f"ffn_db_{i}")
        dh1_16, dh1, dg_ffn = rms_bwd(f"ffn_norm_bwd_{i}", s['h1'], p['ffn_norm'][i:i + 1] + token[0:1, 0:1], db,
                                      h.shape[1], 128, h.shape[1], resid=dh, also_bf16=True)
        dmix = mm_nt(dh1_16, w['out'], BF16, f"mix_dout_{i}")
        dw_out = mm_tn(s['mixed'], dh1_16, 1, BF16, f"mix_dw_out_{i}").reshape(N_DEV, -1, h.shape[1])
        da, big_i, small_i = MIXERS[i][2](dmix, s['s_mix'], w, p)
        big_i[name + '_w_out'] = dw_out
        handle, token = exchange_start(f"xstart_rs_mix{i}", "scatter", list(big_i.values()))
        scatters.append((handle, list(big_i)))
        dh16, dh, dg_mix = rms_bwd(f"mix_norm_bwd_{i}", s['h'], p['mix_norm'][i:i + 1] + token[0:1, 0:1], da, h.shape[1],
                                   128, h.shape[1], resid=dh1, also_bf16=True)
        for k, v in small_i.items():
            small_g[k] = v
        small_g['mix_norm'][i], small_g['ffn_norm'][i] = dg_mix, dg_ffn
        small_g['ffn_conv_w'][i] = jnp.concatenate([dcwg, dcwv], axis=1)[None]
        small_g['ffn_conv_b'][i] = jnp.concatenate([dcbg, dcbv], axis=1)
    for n in ('mix_norm', 'ffn_norm', 'ffn_conv_w', 'ffn_conv_b'):
        small_g[n] = jnp.concatenate(small_g[n], axis=0)
    grad_x = dh[None]

    full_shapes = [tuple(small_g[n].shape) for n in SMALL]
    small_handle, small_token = exchange_start("xstart_small", "gather", [_pack_rows([small_g[n] for n in SMALL])])
    big = {}
    for handle, names in scatters[:-1]:
        for n, landed in zip(names, exchange_wait(handle, small_token)):
            big.setdefault(n, []).insert(0, landed)
    grads, deltas, new_m, new_v = {}, {}, {}, {}
    last_handle, last_names = scatters[-1]
    for n in BIG:
        if n not in last_names:
            grads[n], deltas[n], new_m[n], new_v[n] = adam_sum("adam_" + n, big[n], p[n], p['m_' + n], p['v_' + n])
    for n, landed in zip(last_names, exchange_wait(last_handle, new_v['ffn_w_up'])):
        grads[n], deltas[n], new_m[n], new_v[n] = adam_sum("adam_" + n, [landed], p[n], p['m_' + n], p['v_' + n])
    partials = exchange_wait(small_handle, new_v[last_names[0]])[0]
    summed = _unpack_rows(sum_parts("sum_small_grads", partials), full_shapes)
    mine = []
    for n, g in zip(SMALL, summed):
        if n in SMALL_SHARDED:
            ax = SMALL_SHARDED[n]
            g = lax.dynamic_slice_in_dim(g, my_slot * p[n].shape[ax], p[n].shape[ax], axis=ax)
        mine.append(g)
    shapes = [p[n].shape for n in SMALL]
    packed = [_pack_rows(arrs) for arrs in ([p[n] for n in SMALL], mine, [p['m_' + n] for n in SMALL], [p['v_' + n] for n in SMALL])]
    d_s, m_s, v_s = adam_flat("adam_small", *packed)
    for n, g, d, m, v in zip(SMALL, mine, _unpack_rows(d_s, shapes), _unpack_rows(m_s, shapes), _unpack_rows(v_s, shapes)):
        grads[n], deltas[n], new_m[n], new_v[n] = g, d, m, v

    return (loss, grad_x, *[grads[n] for n in WEIGHTS], *[deltas[n] for n in WEIGHTS], *[new_m[n] for n in WEIGHTS],
            *[new_v[n] for n in WEIGHTS])


def kernel(x, positions, mix_norm, ffn_norm, fox_w_in, fox_b_f, fox_q_gain, fox_k_gain, fox_w_out, mla_w_in, mla_q_a_gain, mla_kv_a_gain, mla_w_q_b, mla_w_kv_b, mla_q_gain, mla_k_gain, mla_w_out, sb_w_in, sb_q_gain, sb_k_gain, sb_w_out, sgu_w_in, sgu_v_gain, sgu_w_s, sgu_b_s, sgu_w_out, ffn_w_up, ffn_conv_w, ffn_conv_b, ffn_w_down, loss_target, m_mix_norm, m_ffn_norm, m_fox_w_in, m_fox_b_f, m_fox_q_gain, m_fox_k_gain, m_fox_w_out, m_mla_w_in, m_mla_q_a_gain, m_mla_kv_a_gain, m_mla_w_q_b, m_mla_w_kv_b, m_mla_q_gain, m_mla_k_gain, m_mla_w_out, m_sb_w_in, m_sb_q_gain, m_sb_k_gain, m_sb_w_out, m_sgu_w_in, m_sgu_v_gain, m_sgu_w_s, m_sgu_b_s, m_sgu_w_out, m_ffn_w_up, m_ffn_conv_w, m_ffn_conv_b, m_ffn_w_down, v_mix_norm, v_ffn_norm, v_fox_w_in, v_fox_b_f, v_fox_q_gain, v_fox_k_gain, v_fox_w_out, v_mla_w_in, v_mla_q_a_gain, v_mla_kv_a_gain, v_mla_w_q_b, v_mla_w_kv_b, v_mla_q_gain, v_mla_k_gain, v_mla_w_out, v_sb_w_in, v_sb_q_gain, v_sb_k_gain, v_sb_w_out, v_sgu_w_in, v_sgu_v_gain, v_sgu_w_s, v_sgu_b_s, v_sgu_w_out, v_ffn_w_up, v_ffn_conv_w, v_ffn_conv_b, v_ffn_w_down):
    args = locals()
    names = ['x', 'positions'] + WEIGHTS + ['loss_target'] + ['m_' + n for n in WEIGHTS] + ['v_' + n for n in WEIGHTS]
    return _train_step({n: args[n] for n in names})
```

```python
import functools
import math

import jax
import jax.numpy as jnp
from jax import lax
from jax.experimental import pallas as pl
from jax.experimental.pallas import tpu as pltpu

F32 = jnp.float32
BF16 = jnp.bfloat16
MESH = pl.DeviceIdType.MESH

N_DEV = 8
N_HEADS = 16
HEAD_DIM = 128
EPS = 1e-6
DEPTH = 4
D_FF = 5632
MLA_ROPE = 64
ROPE_THETA = 10000.0
VMEM_LIMIT = 48 * 1024 * 1024

ADAM_LR, ADAM_B1, ADAM_B2, ADAM_EPS, ADAM_WD, ADAM_STEP = 0.001, 0.9, 0.999, 1e-08, 0.01, 10

WEIGHTS = ['mix_norm', 'ffn_norm', 'fox_w_in', 'fox_b_f', 'fox_q_gain', 'fox_k_gain', 'fox_w_out', 'mla_w_in',
           'mla_q_a_gain', 'mla_kv_a_gain', 'mla_w_q_b', 'mla_w_kv_b', 'mla_q_gain', 'mla_k_gain', 'mla_w_out',
           'sb_w_in', 'sb_q_gain', 'sb_k_gain', 'sb_w_out', 'sgu_w_in', 'sgu_v_gain', 'sgu_w_s', 'sgu_b_s',
           'sgu_w_out', 'ffn_w_up', 'ffn_conv_w', 'ffn_conv_b', 'ffn_w_down']
BIG = ['fox_w_in', 'fox_w_out', 'mla_w_in', 'mla_w_q_b', 'mla_w_kv_b', 'mla_w_out', 'sb_w_in', 'sb_w_out',
       'sgu_w_in', 'sgu_w_out', 'ffn_w_up', 'ffn_w_down']
SMALL = [w for w in WEIGHTS if w not in BIG]
SMALL_SHARDED = {'mla_q_a_gain': 1, 'mla_kv_a_gain': 1, 'sgu_v_gain': 1, 'ffn_conv_w': 2}


def _cparams(n_grid):
    return pltpu.CompilerParams(dimension_semantics=("arbitrary",) * n_grid, vmem_limit_bytes=VMEM_LIMIT)


def _pick(n, cap):
    best = None
    t = 128
    while t <= min(n, cap):
        if n % t == 0:
            best = t
        t += 128
    return best if best is not None else n


def _mm_call(name, a, b, out_shape, a_spec, b_spec, o_spec, grid, dims, acc_shape, add=None):
    nk = grid[2]

    def body(*refs):
        a_ref, b_ref = refs[:2]
        add_ref = refs[2] if add is not None else None
        o_ref = refs[3] if add is not None else refs[2]
        prod = lax.dot_general(a_ref[...].astype(BF16), b_ref[...].astype(BF16), (dims, ((), ())),
                               preferred_element_type=F32)

        def finish(r):
            if add_ref is not None:
                r = r + add_ref[...].astype(F32)
            o_ref[...] = r.astype(o_ref.dtype)

        if nk == 1:
            finish(prod)
            return
        acc = refs[-1]
        k = pl.program_id(2)

        @pl.when(k == 0)
        def _():
            acc[...] = prod

        @pl.when(k > 0)
        def _():
            acc[...] += prod

        @pl.when(k == nk - 1)
        def _():
            finish(acc[...])

    ins = [a, b] + ([] if add is None else [add])
    in_specs = [a_spec, b_spec] + ([] if add is None else [o_spec])
    return pl.pallas_call(body, grid=grid, in_specs=in_specs, out_specs=o_spec, out_shape=out_shape,
                          scratch_shapes=[] if nk == 1 else [pltpu.VMEM(acc_shape, F32)], name=name,
                          compiler_params=_cparams(3))(*ins)


def mm_nn(a, b3, out_dtype, name, add=None, joff=0, nj=None):
    m, kk = a.shape
    _, kb, n = b3.shape
    assert kb == kk
    nj = b3.shape[0] - joff if nj is None else nj
    tn, tk = _pick(n, 1536), _pick(kk, 2048)
    tm = _pick(m, 512 if a.dtype == F32 and tk > 1024 else 1024)
    nb = n // tn
    return _mm_call(
        name, a, b3, jax.ShapeDtypeStruct((m, nj * n), out_dtype),
        pl.BlockSpec((tm, tk), lambda i, c, k: (i, k)),
        pl.BlockSpec((None, tk, tn), lambda i, c, k: (joff + c // nb, k, c % nb)),
        pl.BlockSpec((tm, tn), lambda i, c, k: (i, c)),
        (m // tm, nj * nb, kk // tk), ((1,), (0,)), (tm, tn), add=add)


def mm_nt(a, b3, out_dtype, name, add=None, joff=0, nj=None):
    m, na = a.shape
    _, ko, n = b3.shape
    nj = b3.shape[0] - joff if nj is None else nj
    assert na == nj * n
    to, tn = _pick(ko, 1024), _pick(n, 2048)
    tm = _pick(m, 512 if a.dtype == F32 and tn > 1024 else 1024)
    nb = n // tn
    return _mm_call(
        name, a, b3, jax.ShapeDtypeStruct((m, ko), out_dtype),
        pl.BlockSpec((tm, tn), lambda i, o, c: (i, c)),
        pl.BlockSpec((None, to, tn), lambda i, o, c: (joff + c // nb, o, c % nb)),
        pl.BlockSpec((tm, to), lambda i, o, c: (i, o)),
        (m // tm, ko // to, nj * nb), ((1,), (1,)), (tm, to), add=add)


def mm_tn(a, b, nj, out_dtype, name):
    s, ko = a.shape
    sb, nb_tot = b.shape
    assert sb == s and nb_tot % nj == 0
    n = nb_tot // nj
    to, tn = _pick(ko, 1024), _pick(n, 1536)
    ts = _pick(s, 1024 if F32 in (a.dtype, b.dtype) else 2048)
    nb = n // tn
    return _mm_call(
        name, a, b, jax.ShapeDtypeStruct((nj, ko, n), out_dtype),
        pl.BlockSpec((ts, to), lambda o, c, k: (k, o)),
        pl.BlockSpec((ts, tn), lambda o, c, k: (k, c)),
        pl.BlockSpec((None, to, tn), lambda o, c, k: (c // nb, o, c % nb)),
        (ko // to, nj * nb, s // ts), ((0,), (0,)), (to, tn))


def blockk(name, fn, grid, ins, outs):
    n_in = len(ins)
    accs = [o[2] for o in outs]

    def body(*refs):
        vals = fn(*[r[...] for r in refs[:n_in]])
        if not isinstance(vals, (tuple, list)):
            vals = (vals,)
        i, j = pl.program_id(0), pl.program_id(1)
        for r, v, acc in zip(refs[n_in:], vals, accs):
            if acc is None:
                r[...] = v.astype(r.dtype)
            else:
                first = (j == 0) if acc == 'inner' else jnp.logical_and(i == 0, j == 0)

                @pl.when(first)
                def _(r=r, v=v):
                    r[...] = v.astype(r.dtype)

                @pl.when(jnp.logical_not(first))
                def _(r=r, v=v):
                    r[...] += v.astype(r.dtype)

    res = pl.pallas_call(body, grid=grid, in_specs=[s for _, s in ins], out_specs=[o[1] for o in outs],
                         out_shape=[o[0] for o in outs], name=name, compiler_params=_cparams(2))(*[a for a, _ in ins])
    return res


def _sds(shape, dtype):
    return jax.ShapeDtypeStruct(tuple(shape), dtype)


def _rows(tr, w, col=0):
    if col == 'j':
        return pl.BlockSpec((tr, w), lambda i, j: (i, j))
    if callable(col):
        return pl.BlockSpec((tr, w), lambda i, j: (i, col(j)))
    return pl.BlockSpec((tr, w), lambda i, j: (i, col))


def _whole(shape):
    nd = len(shape)
    return pl.BlockSpec(tuple(shape), lambda i, j: (0,) * nd)


def _rms(x, g, n):
    ms = jnp.sum(x * x, axis=-1, keepdims=True) * (1.0 / n)
    return x * lax.rsqrt(ms + EPS) * g


def _sig(x):
    return 1.0 / (1.0 + jnp.exp(-x))


def _gelu(x):
    return 0.5 * x * (1.0 + jnp.tanh(math.sqrt(2.0 / math.pi) * (x + 0.044715 * (x * x * x))))


def _lane_iota(shape):
    return lax.broadcasted_iota(jnp.int32, shape, len(shape) - 1)


def _rope(x, cos, sin):
    lane = _lane_iota(x.shape)
    half = MLA_ROPE // 2
    swapped = jnp.where(lane < half, pltpu.roll(x, HEAD_DIM - half, 1), pltpu.roll(x, half, 1))
    sign = jnp.where(lane < half, -1.0, 1.0)
    return x * cos + swapped * (sin * sign)


def _rope_t(dy, cos, sin):
    lane = _lane_iota(dy.shape)
    half = MLA_ROPE // 2
    t = dy * sin
    swapped = jnp.where(lane < half, pltpu.roll(t, HEAD_DIM - half, 1), pltpu.roll(t, half, 1))
    sign = jnp.where(lane < half, 1.0, -1.0)
    return dy * cos + swapped * sign


def rms_fwd(name, x, gain, n, tr, width, xcol=0, nh=1, out_dtype=BF16, out_cols=None):
    r = x.shape[0]
    tr = r if nh > 1 else tr
    out_cols = width * nh if out_cols is None else out_cols
    xspec = _rows(tr, width, (lambda j: xcol + j) if nh > 1 else xcol)
    ospec = _rows(tr, width, 'j' if nh > 1 else 0)
    return blockk(name, lambda xb, g: _rms(xb.astype(F32), g, n), (r // tr, nh),
                  [(x, xspec), (gain, _whole(gain.shape))], [(_sds((r, out_cols), out_dtype), ospec, None)])[0]


def rms_bwd(name, x, gain, dy, n, tr, width, xcol=0, nh=1, dycol=0, resid=None, out_dtype=F32, also_bf16=False):
    r = x.shape[0]
    tr = r if nh > 1 else tr
    xspec = _rows(tr, width, (lambda j: xcol + j) if nh > 1 else xcol)
    dyspec = _rows(tr, width, (lambda j: dycol + j) if nh > 1 else dycol)
    ospec = _rows(tr, width, 'j' if nh > 1 else 0)

    def fn(xb, g, dyb, *rest):
        _, vjp = jax.vjp(lambda a, b: _rms(a, b, n), xb.astype(F32), g)
        dx, dg = vjp(dyb.astype(F32))
        if rest:
            dx = dx + rest[0].astype(F32)
        return ((dx,) if also_bf16 else ()) + (dx, dg)

    ins = [(x, xspec), (gain, _whole(gain.shape)), (dy, dyspec)]
    if resid is not None:
        ins.append((resid, ospec))
    outs = [(_sds((r, width * nh), out_dtype), ospec, None), (_sds(gain.shape, F32), _whole(gain.shape), 'all')]
    if also_bf16:
        outs.insert(0, (_sds((r, width * nh), BF16), ospec, None))
    return blockk(name, fn, (r // tr, nh), ins, outs)


def _nt(a, b):
    return lax.dot_general(a, b, (((1,), (1,)), ((), ())), preferred_element_type=F32)


def _tn(a, b):
    return lax.dot_general(a, b, (((0,), (0,)), ((), ())), preferred_element_type=F32)


def _nn(a, b):
    return lax.dot_general(a, b, (((1,), (0,)), ((), ())), preferred_element_type=F32)


ATTN_BLOCK = 512
ATTN_HEADS = 2
ATTN_GRID_HEADS = N_HEADS // ATTN_HEADS


def _head_cols(off):
    assert off % ATTN_HEADS == 0
    return off // ATTN_HEADS


def _hd(ref, hh):
    return ref[:, hh * HEAD_DIM:(hh + 1) * HEAD_DIM]


def _attn_specs(tq, qoff, koff, voff, extra, bias, q2off):
    w = HEAD_DIM * ATTN_HEADS
    qc, kc, vc, q2c = (_head_cols(o) for o in (qoff, koff, voff, q2off))
    specs = [pl.BlockSpec((tq, w), lambda h, i, j: (i, qc + h)),
             pl.BlockSpec((tq, w), lambda h, i, j: (jnp.minimum(i, j), kc + h)),
             pl.BlockSpec((tq, w), lambda h, i, j: (jnp.minimum(i, j), vc + h))]
    if extra:
        specs += [pl.BlockSpec((tq, w), lambda h, i, j: (i, q2c + h)),
                  pl.BlockSpec((tq, HEAD_DIM), lambda h, i, j: (jnp.minimum(i, j), 0))]
    if bias:
        specs += [pl.BlockSpec((ATTN_HEADS, tq, 1), lambda h, i, j: (h, i, 0)),
                  pl.BlockSpec((ATTN_HEADS, 1, tq), lambda h, i, j: (h, 0, jnp.minimum(i, j)))]
    return specs


def _scores(q, k, q2, k2, cc, cr, scale, tq, diagonal):
    s = _nt(q.astype(BF16), k.astype(BF16))
    if q2 is not None:
        s = s + _nt(q2.astype(BF16), k2.astype(BF16))
    s = s * scale
    if cc is not None:
        s = s + (cc - cr)
    if not diagonal:
        return s, None
    return s, lax.broadcasted_iota(jnp.int32, (tq, tq), 1) <= lax.broadcasted_iota(jnp.int32, (tq, tq), 0)


def _on_blocks(qi, kj, step):
    @pl.when(kj < qi)
    def _():
        step(False)

    @pl.when(kj == qi)
    def _():
        step(True)


def attn_fwd(name, q, k, v, scale, *, qoff=0, koff=0, voff=0, q2=None, k2=None, q2off=0, cum=None, tq=ATTN_BLOCK,
             exact_o=False):
    s_len = q.shape[0]
    nq = s_len // tq
    extra, bias = q2 is not None, cum is not None
    n_in = 3 + 2 * extra + 2 * bias

    def body(*refs):
        q_ref, k_ref, v_ref = refs[:3]
        p = 3
        q2_ref = k2_ref = cc_ref = cr_ref = None
        if extra:
            q2_ref, k2_ref = refs[p:p + 2]
            p += 2
        if bias:
            cc_ref, cr_ref = refs[p:p + 2]
            p += 2
        o_ref, lse_ref, m_s, l_s, acc_s = refs[p:]
        qi, kj = pl.program_id(1), pl.program_id(2)

        @pl.when(kj == 0)
        def _():
            m_s[...] = jnp.full_like(m_s, -jnp.inf)
            l_s[...] = jnp.zeros_like(l_s)
            acc_s[...] = jnp.zeros_like(acc_s)

        def step(diagonal):
            for hh in range(ATTN_HEADS):
                s, allowed = _scores(_hd(q_ref, hh), _hd(k_ref, hh), _hd(q2_ref, hh) if extra else None,
                                     k2_ref[...] if extra else None, cc_ref[hh] if bias else None,
                                     cr_ref[hh] if bias else None, scale, tq, diagonal)
                if diagonal:
                    s = jnp.where(allowed, s, -jnp.inf)
                m_old = m_s[hh]
                m_new = jnp.maximum(m_old, jnp.max(s, axis=-1, keepdims=True))
                alpha = jnp.exp(m_old - m_new)
                pr = jnp.exp(s - m_new)
                l_s[hh] = alpha * l_s[hh] + jnp.sum(pr, axis=-1, keepdims=True)
                vb = _hd(v_ref, hh).astype(BF16)
                pv = _nn(pr.astype(BF16), vb)
                if exact_o:
                    pv = pv + _nn((pr - pr.astype(BF16).astype(F32)).astype(BF16), vb)
                acc_s[hh] = alpha * acc_s[hh] + pv
                m_s[hh] = m_new

        _on_blocks(qi, kj, step)

        @pl.when(kj == qi)
        def _():
            for hh in range(ATTN_HEADS):
                o_ref[:, hh * HEAD_DIM:(hh + 1) * HEAD_DIM] = (acc_s[hh] / l_s[hh]).astype(o_ref.dtype)
                lse_ref[hh] = m_s[hh] + jnp.log(l_s[hh])

    ins = [q, k, v] + ([q2, k2] if extra else []) + (list(cum) if bias else [])
    d, w = HEAD_DIM, HEAD_DIM * ATTN_HEADS
    return pl.pallas_call(
        body, grid=(ATTN_GRID_HEADS, nq, nq), in_specs=_attn_specs(tq, qoff, koff, voff, extra, bias, q2off),
        out_specs=[pl.BlockSpec((tq, w), lambda h, i, j: (i, h)),
                   pl.BlockSpec((ATTN_HEADS, tq, 1), lambda h, i, j: (h, i, 0))],
        out_shape=[_sds((s_len, N_HEADS * d), F32 if exact_o else BF16), _sds((N_HEADS, s_len, 1), F32)],
        scratch_shapes=[pltpu.VMEM((ATTN_HEADS, tq, 1), F32), pltpu.VMEM((ATTN_HEADS, tq, 1), F32),
                        pltpu.VMEM((ATTN_HEADS, tq, d), F32)],
        name=name, compiler_params=_cparams(3))(*ins)


def attn_bwd(name, q, k, v, o, do, lse, scale, *, qoff=0, koff=0, voff=0, q2=None, k2=None, q2off=0, cum=None,
             tq=ATTN_BLOCK):
    s_len = q.shape[0]
    nq = s_len // tq
    extra, bias = q2 is not None, cum is not None
    d = HEAD_DIM
    n_in = 6 + 2 * extra + 2 * bias

    def body(*refs):
        q_ref, k_ref, v_ref = refs[:3]
        p = 3
        q2_ref = k2_ref = cc_ref = cr_ref = None
        if extra:
            q2_ref, k2_ref = refs[p:p + 2]
            p += 2
        if bias:
            cc_ref, cr_ref = refs[p:p + 2]
            p += 2
        o_ref, do_ref, lse_ref = refs[p:p + 3]
        p += 3
        dq_ref, dk_ref, dv_ref = refs[p:p + 3]
        p += 3
        dq2_ref = dk2_ref = dcs_ref = None
        if extra:
            dq2_ref, dk2_ref = refs[p:p + 2]
            p += 2
        if bias:
            dcs_ref = refs[p]
            p += 1
        dq_s, delta_s = refs[p:p + 2]
        dq2_s = refs[p + 2] if extra else None
        h, qi, kj = pl.program_id(0), pl.program_id(1), pl.program_id(2)

        @pl.when(jnp.logical_and(qi == 0, kj == 0))
        def _():
            dk_ref[...] = jnp.zeros_like(dk_ref)
            dv_ref[...] = jnp.zeros_like(dv_ref)

        if extra:
            @pl.when(jnp.logical_and(h == 0, jnp.logical_and(qi == 0, kj == 0)))
            def _():
                dk2_ref[...] = jnp.zeros_like(dk2_ref)

        @pl.when(kj == 0)
        def _():
            dq_s[...] = jnp.zeros_like(dq_s)
            if extra:
                dq2_s[...] = jnp.zeros_like(dq2_s)
            for hh in range(ATTN_HEADS):
                delta_s[hh] = jnp.sum(_hd(do_ref, hh).astype(F32) * _hd(o_ref, hh).astype(F32), axis=-1, keepdims=True)

        if bias:
            @pl.when(kj > qi)
            def _():
                dcs_ref[...] = jnp.zeros_like(dcs_ref)

        def step(diagonal):
            ks = pl.ds(pl.multiple_of(kj * tq, tq), tq)
            for hh in range(ATTN_HEADS):
                cols = slice(hh * HEAD_DIM, (hh + 1) * HEAD_DIM)
                qh, kh = _hd(q_ref, hh).astype(BF16), _hd(k_ref, hh).astype(BF16)
                q2h = _hd(q2_ref, hh).astype(BF16) if extra else None
                k2h = k2_ref[...].astype(BF16) if extra else None
                s, allowed = _scores(qh, kh, q2h, k2h, cc_ref[hh] if bias else None, cr_ref[hh] if bias else None,
                                     scale, tq, diagonal)
                pr = jnp.exp(s - lse_ref[hh])
                if diagonal:
                    pr = jnp.where(allowed, pr, 0.0)
                dob = _hd(do_ref, hh).astype(BF16)
                dp = _nt(dob, _hd(v_ref, hh).astype(BF16))
                ds = pr * (dp - delta_s[hh])
                dsb = (ds * scale).astype(BF16)
                dq_s[hh] += _nn(dsb, kh)
                dk_ref[ks, cols] += _tn(dsb, qh)
                dv_ref[ks, cols] += _tn(pr.astype(BF16), dob)
                if extra:
                    dq2_s[hh] += _nn(dsb, k2h)
                    dk2_ref[ks, :] += _tn(dsb, q2h)
                if bias:
                    dcs_ref[hh] = jnp.sum(ds, axis=0, keepdims=True)

        _on_blocks(qi, kj, step)

        @pl.when(kj == qi)
        def _():
            for hh in range(ATTN_HEADS):
                cols = slice(hh * HEAD_DIM, (hh + 1) * HEAD_DIM)
                dq_ref[:, cols] = dq_s[hh]
                if extra:
                    dq2_ref[:, cols] = dq2_s[hh]

    w = HEAD_DIM * ATTN_HEADS
    ins = [q, k, v] + ([q2, k2] if extra else []) + (list(cum) if bias else []) + [o, do, lse]
    in_specs = _attn_specs(tq, qoff, koff, voff, extra, bias, q2off) + [
        pl.BlockSpec((tq, w), lambda h, i, j: (i, h)), pl.BlockSpec((tq, w), lambda h, i, j: (i, h)),
        pl.BlockSpec((ATTN_HEADS, tq, 1), lambda h, i, j: (h, i, 0))]
    full = _sds((s_len, N_HEADS * d), F32)
    out_shape = [full, full, full]
    out_specs = [pl.BlockSpec((tq, w), lambda h, i, j: (i, h)), pl.BlockSpec((s_len, w), lambda h, i, j: (0, h)),
                 pl.BlockSpec((s_len, w), lambda h, i, j: (0, h))]
    scratch = [pltpu.VMEM((ATTN_HEADS, tq, d), F32), pltpu.VMEM((ATTN_HEADS, tq, 1), F32)]
    if extra:
        out_shape += [full, _sds((s_len, d), F32)]
        out_specs += [pl.BlockSpec((tq, w), lambda h, i, j: (i, h)), pl.BlockSpec((s_len, d), lambda h, i, j: (0, 0))]
        scratch.append(pltpu.VMEM((ATTN_HEADS, tq, d), F32))
    if bias:
        out_shape.append(_sds((N_HEADS, nq, 1, s_len), F32))
        out_specs.append(pl.BlockSpec((ATTN_HEADS, None, 1, tq), lambda h, i, j: (h, i, 0, j)))
    return pl.pallas_call(body, grid=(ATTN_GRID_HEADS, nq, nq), in_specs=in_specs, out_specs=out_specs,
                          out_shape=out_shape, scratch_shapes=scratch, name=name, compiler_params=_cparams(3))(*ins)


def _sb_terms(q, k, scale, tq, diagonal):
    z = _nt(q.astype(BF16), k.astype(BF16)) * scale
    lg = jnp.log(1.0 + jnp.exp(-jnp.abs(z)))
    log_keep = -(jnp.maximum(z, 0.0) + lg)
    log_beta = jnp.minimum(z, 0.0) - lg
    if not diagonal:
        return None, log_keep, log_beta
    strict = lax.broadcasted_iota(jnp.int32, (tq, tq), 1) < lax.broadcasted_iota(jnp.int32, (tq, tq), 0)
    return strict, jnp.where(strict, log_keep, 0.0), log_beta


def _tri(tq, pred):
    a = lax.broadcasted_iota(jnp.int32, (tq, tq), 0)
    b = lax.broadcasted_iota(jnp.int32, (tq, tq), 1)
    return jnp.where(pred(a, b), 1.0, 0.0).astype(BF16)


SUM_CHUNK = 256


def _lane_sums(x, later):
    n = x.shape[1]
    chunk = min(SUM_CHUNK, n)
    tri = _tri(chunk, (lambda m, j: m > j) if later else (lambda m, j: m < j))
    order = range(n // chunk - 1, -1, -1) if later else range(n // chunk)
    pieces, carry = [None] * (n // chunk), None
    for cidx in order:
        xc = x[:, cidx * chunk:(cidx + 1) * chunk]
        hi = xc.astype(BF16)
        local = _nn(hi, tri) + _nn((xc - hi.astype(F32)).astype(BF16), tri)
        total = jnp.sum(xc, axis=-1, keepdims=True)
        pieces[cidx] = local if carry is None else local + carry
        carry = total if carry is None else carry + total
    return jnp.concatenate(pieces, axis=1), carry


def sb_fwd(name, qn, kn, qkv, scale, voff, tq=ATTN_BLOCK):
    s_len = qn.shape[0]
    nq = s_len // tq
    d = HEAD_DIM

    def body(q_ref, k_ref, v_ref, o_ref, car_ref, ca_s, acc_s):
        qi, kj = pl.program_id(1), pl.program_id(2)

        @pl.when(kj == 0)
        def _():
            ca_s[...] = jnp.zeros_like(ca_s)
            acc_s[...] = jnp.zeros_like(acc_s)

        def step(diagonal):
            for hh in range(ATTN_HEADS):
                strict, log_keep, log_beta = _sb_terms(_hd(q_ref, hh), _hd(k_ref, hh), scale, tq, diagonal)
                ca = ca_s[hh]
                car_ref[hh] = ca
                after, total = _lane_sums(log_keep, True)
                a = jnp.exp(log_beta + (after + ca))
                if diagonal:
                    a = jnp.where(strict, a, 0.0)
                acc_s[hh] += _nn(a.astype(BF16), _hd(v_ref, hh).astype(BF16))
                ca_s[hh] = ca + total

        @pl.when(kj == 0)
        def _():
            step(True)

        @pl.when(jnp.logical_and(kj > 0, kj <= qi))
        def _():
            step(False)

        @pl.when(kj == qi)
        def _():
            for hh in range(ATTN_HEADS):
                o_ref[:, hh * HEAD_DIM:(hh + 1) * HEAD_DIM] = acc_s[hh].astype(o_ref.dtype)

    kblk = lambda i, j: jnp.maximum(i - j, 0)
    w, vc = HEAD_DIM * ATTN_HEADS, _head_cols(voff)
    return pl.pallas_call(
        body, grid=(ATTN_GRID_HEADS, nq, nq),
        in_specs=[pl.BlockSpec((tq, w), lambda h, i, j: (i, h)), pl.BlockSpec((tq, w), lambda h, i, j: (kblk(i, j), h)),
                  pl.BlockSpec((tq, w), lambda h, i, j: (kblk(i, j), vc + h))],
        out_specs=[pl.BlockSpec((tq, w), lambda h, i, j: (i, h)),
                   pl.BlockSpec((ATTN_HEADS, None, tq, 1), lambda h, i, j: (h, kblk(i, j), i, 0))],
        out_shape=[_sds((s_len, N_HEADS * d), BF16), _sds((N_HEADS, nq, s_len, 1), F32)],
        scratch_shapes=[pltpu.VMEM((ATTN_HEADS, tq, 1), F32), pltpu.VMEM((ATTN_HEADS, tq, d), F32)],
        name=name, compiler_params=_cparams(3))(qn, kn, qkv)


def sb_bwd(name, qn, kn, qkv, do, carries, scale, voff, tq=ATTN_BLOCK):
    s_len = qn.shape[0]
    nq = s_len // tq
    d = HEAD_DIM

    def body(q_ref, k_ref, v_ref, do_ref, car_ref, dq_ref, dk_ref, dv_ref, dq_s, cg_s):
        qi, kj = pl.program_id(1), pl.program_id(2)

        @pl.when(jnp.logical_and(qi == 0, kj == 0))
        def _():
            dk_ref[...] = jnp.zeros_like(dk_ref)
            dv_ref[...] = jnp.zeros_like(dv_ref)

        @pl.when(kj == 0)
        def _():
            dq_s[...] = jnp.zeros_like(dq_s)
            cg_s[...] = jnp.zeros_like(cg_s)

        def step(diagonal):
            ks = pl.ds(pl.multiple_of(kj * tq, tq), tq)
            for hh in range(ATTN_HEADS):
                cols = slice(hh * HEAD_DIM, (hh + 1) * HEAD_DIM)
                qh, kh = _hd(q_ref, hh).astype(BF16), _hd(k_ref, hh).astype(BF16)
                strict, log_keep, log_beta = _sb_terms(qh, kh, scale, tq, diagonal)
                after, _ = _lane_sums(log_keep, True)
                a = jnp.exp(log_beta + (after + car_ref[hh]))
                if diagonal:
                    a = jnp.where(strict, a, 0.0)
                dob = _hd(do_ref, hh).astype(BF16)
                g = a * _nt(dob, _hd(v_ref, hh).astype(BF16))
                cg = cg_s[hh]
                before, total = _lane_sums(g, False)
                big_g = before + cg
                cg_s[hh] = cg + total
                beta = jnp.exp(log_beta)
                dz = g * (1.0 - beta) - big_g * beta
                if diagonal:
                    dz = jnp.where(strict, dz, 0.0)
                dzb = (dz * scale).astype(BF16)
                dq_s[hh] += _nn(dzb, kh)
                dk_ref[ks, cols] += _tn(dzb, qh)
                dv_ref[ks, cols] += _tn(a.astype(BF16), dob)

        _on_blocks(qi, kj, step)

        @pl.when(kj == qi)
        def _():
            for hh in range(ATTN_HEADS):
                dq_ref[:, hh * HEAD_DIM:(hh + 1) * HEAD_DIM] = dq_s[hh]

    kblk = lambda i, j: jnp.minimum(i, j)
    full = _sds((s_len, N_HEADS * d), F32)
    w, vc = HEAD_DIM * ATTN_HEADS, _head_cols(voff)
    return pl.pallas_call(
        body, grid=(ATTN_GRID_HEADS, nq, nq),
        in_specs=[pl.BlockSpec((tq, w), lambda h, i, j: (i, h)), pl.BlockSpec((tq, w), lambda h, i, j: (kblk(i, j), h)),
                  pl.BlockSpec((tq, w), lambda h, i, j: (kblk(i, j), vc + h)),
                  pl.BlockSpec((tq, w), lambda h, i, j: (i, h)),
                  pl.BlockSpec((ATTN_HEADS, None, tq, 1), lambda h, i, j: (h, kblk(i, j), i, 0))],
        out_specs=[pl.BlockSpec((tq, w), lambda h, i, j: (i, h)), pl.BlockSpec((s_len, w), lambda h, i, j: (0, h)),
                   pl.BlockSpec((s_len, w), lambda h, i, j: (0, h))],
        out_shape=[full, full, full],
        scratch_shapes=[pltpu.VMEM((ATTN_HEADS, tq, d), F32), pltpu.VMEM((ATTN_HEADS, tq, 1), F32)],
        name=name, compiler_params=_cparams(3))(qn, kn, qkv, do, carries)


def _cumsum_rows(x, reverse):
    n = x.shape[0] // HEAD_DIM
    tri = _tri(HEAD_DIM, (lambda a, b: b >= a) if reverse else (lambda a, b: b <= a))
    pieces = [None] * n
    carry = jnp.zeros((1, HEAD_DIM), F32)
    order = range(n - 1, -1, -1) if reverse else range(n)
    for blk in order:
        xb = x[blk * HEAD_DIM:(blk + 1) * HEAD_DIM, :]
        x1 = xb.astype(BF16)
        r1 = xb - x1.astype(F32)
        x2 = r1.astype(BF16)
        x3 = (r1 - x2.astype(F32)).astype(BF16)
        c = _nn(tri, x1) + _nn(tri, x2) + _nn(tri, x3) + carry
        pieces[blk] = c
        carry = c[0:1, :] if reverse else c[HEAD_DIM - 1:HEAD_DIM, :]
    return jnp.concatenate(pieces, axis=0)


def _log_sigmoid(x):
    return jnp.minimum(x, 0.0) - jnp.log(1.0 + jnp.exp(-jnp.abs(x)))


def fox_gate_fwd(name, fl, bf):
    return blockk(name, lambda f, b: _cumsum_rows(_log_sigmoid(f + b), False), (1, 1),
                  [(fl, _whole(fl.shape)), (bf, _whole(bf.shape))], [(_sds(fl.shape, F32), _whole(fl.shape), None)])[0]


def fox_gate_bwd(name, fl, bf, dcum):
    def fn(f, b, dc):
        dlogf = _cumsum_rows(dc, True)
        dfl = dlogf * _sig(-(f + b))
        return dfl, jnp.sum(dfl, axis=0, keepdims=True)

    return blockk(name, fn, (1, 1), [(fl, _whole(fl.shape)), (bf, _whole(bf.shape)), (dcum, _whole(dcum.shape))],
                  [(_sds(fl.shape, F32), _whole(fl.shape), None), (_sds(bf.shape, F32), _whole(bf.shape), None)])


def rope_fwd(name, x, gain, cos, sin, tr, xcol, nh):
    r = x.shape[0]
    tr = r if nh > 1 else tr
    xspec = _rows(tr, HEAD_DIM, lambda j: xcol + j)
    tspec = _rows(tr, HEAD_DIM, 0)
    return blockk(name, lambda xb, g, c, s: _rope(_rms(xb.astype(F32), g, MLA_ROPE), c, s), (r // tr, nh),
                  [(x, xspec), (gain, _whole(gain.shape)), (cos, tspec), (sin, tspec)],
                  [(_sds((r, HEAD_DIM * nh), BF16), _rows(tr, HEAD_DIM, 'j'), None)])[0]


def rope_bwd(name, x, gain, cos, sin, dy, tr, xcol, nh):
    r = x.shape[0]
    tr = r if nh > 1 else tr
    xspec = _rows(tr, HEAD_DIM, lambda j: xcol + j)
    tspec = _rows(tr, HEAD_DIM, 0)
    ospec = _rows(tr, HEAD_DIM, 'j')

    def fn(xb, g, c, s, dyb):
        _, vjp = jax.vjp(lambda a, b: _rms(a, b, MLA_ROPE), xb.astype(F32), g)
        return vjp(_rope_t(dyb.astype(F32), c, s))

    return blockk(name, fn, (r // tr, nh),
                  [(x, xspec), (gain, _whole(gain.shape)), (cos, tspec), (sin, tspec), (dy, ospec)],
                  [(_sds((r, HEAD_DIM * nh), F32), ospec, None), (_sds(gain.shape, F32), _whole(gain.shape), 'all')])


def sgu_pre_fwd(name, pre, gain, tr):
    s_len, w2 = pre.shape
    w = w2 // 2
    return blockk(name, lambda pu, pv, g: (_gelu(pu.astype(F32)), _rms(_gelu(pv.astype(F32)), g, w)), (s_len // tr, 1),
                  [(pre, _rows(tr, w, 0)), (pre, _rows(tr, w, 1)), (gain, _whole(gain.shape))],
                  [(_sds((s_len, w), BF16), _rows(tr, w, 0), None), (_sds((s_len, w), BF16), _rows(tr, w, 0), None)])


def sgu_pre_bwd(name, pre, gain, du, dvn, tr):
    s_len, w2 = pre.shape
    w = w2 // 2

    def fn(pu, pv, g, dub, dvb):
        _, vjp_u = jax.vjp(_gelu, pu.astype(F32))
        _, vjp_v = jax.vjp(lambda a, b: _rms(_gelu(a), b, w), pv.astype(F32), g)
        dpv, dg = vjp_v(dvb.astype(F32))
        return vjp_u(dub.astype(F32))[0], dpv, dg

    spec = _rows(tr, w, 0)
    return blockk(name, fn, (s_len // tr, 1),
                  [(pre, spec), (pre, _rows(tr, w, 1)), (gain, _whole(gain.shape)), (du, spec), (dvn, spec)],
                  [(_sds((s_len, w), BF16), spec, None), (_sds((s_len, w), BF16), spec, None),
                   (_sds(gain.shape, F32), _whole(gain.shape), 'all')])


def _ws_masked(ws):
    t = ws.shape[0]
    a = lax.broadcasted_iota(jnp.int32, (t, t), 0)
    b = lax.broadcasted_iota(jnp.int32, (t, t), 1)
    return jnp.where(b <= a, ws, 0.0)


def sgu_mix_fwd(name, vn, u, ws, bs3):
    s_len, w = vn.shape
    t = ws.shape[1]

    n_groups = w // t

    def fn(vb, ub, wsb, bsb):
        pieces = []
        for g in range(n_groups):
            cols = slice(g * t, (g + 1) * t)
            mixed = _nn(_ws_masked(wsb[g]).astype(BF16), vb[:, cols].astype(BF16)) + bsb[g]
            pieces.append(ub[:, cols].astype(F32) * mixed)
        return jnp.concatenate(pieces, axis=1)

    blk = pl.BlockSpec((t, w), lambda i, j: (i, 0))
    return blockk(name, fn, (s_len // t, 1), [(vn, blk), (u, blk), (ws, _whole(ws.shape)), (bs3, _whole(bs3.shape))],
                  [(_sds((s_len, w), BF16), blk, None)])[0]


def sgu_mix_bwd(name, vn, u, ws, bs3, dprod):
    s_len, w = vn.shape
    t = ws.shape[1]

    n_groups = w // t

    def fn(vb, ub, wsb, bsb, dpb):
        dus, dvns, dwss, dbss = [], [], [], []
        for g in range(n_groups):
            cols = slice(g * t, (g + 1) * t)
            wm = _ws_masked(wsb[g]).astype(BF16)
            vb16 = vb[:, cols].astype(BF16)
            mixed = _nn(wm, vb16) + bsb[g]
            dp = dpb[:, cols].astype(F32)
            dus.append(dp * mixed)
            dm = dp * ub[:, cols].astype(F32)
            dmb = dm.astype(BF16)
            dvns.append(_tn(wm, dmb))
            dwss.append(_ws_masked(_nt(dmb, vb16)))
            dbss.append(jnp.sum(dm, axis=-1, keepdims=True))
        return jnp.concatenate(dus, axis=1), jnp.concatenate(dvns, axis=1), jnp.stack(dwss), jnp.stack(dbss)

    blk = pl.BlockSpec((t, w), lambda i, j: (j, 0))
    return blockk(name, fn, (1, s_len // t),
                  [(vn, blk), (u, blk), (ws, _whole(ws.shape)), (bs3, _whole(bs3.shape)), (dprod, blk)],
                  [(_sds((s_len, w), BF16), blk, None), (_sds((s_len, w), BF16), blk, None),
                   (_sds(ws.shape, F32), _whole(ws.shape), 'all'), (_sds(bs3.shape, F32), _whole(bs3.shape), 'all')])


def _shift_down(x, k):
    row = lax.broadcasted_iota(jnp.int32, x.shape, 0)
    return jnp.where(row >= k, pltpu.roll(x, k, 0), 0.0)


def _shift_up(x, k):
    n = x.shape[0]
    row = lax.broadcasted_iota(jnp.int32, x.shape, 0)
    return jnp.where(row < n - k, pltpu.roll(x, n - k, 0), 0.0)


def _conv(up, cw, cb):
    return cb + cw[0:1, :] * _shift_down(up, 2) + cw[1:2, :] * _shift_down(up, 1) + cw[2:3, :] * up


def _conv_specs(s_len, tc, nf):
    g = pl.BlockSpec((s_len, tc), lambda i, j: (0, j))
    v = pl.BlockSpec((s_len, tc), lambda i, j: (0, j + nf))
    wg = pl.BlockSpec((3, tc), lambda i, j: (0, j))
    wv = pl.BlockSpec((3, tc), lambda i, j: (0, j + nf))
    bg = pl.BlockSpec((1, tc), lambda i, j: (0, j))
    bv = pl.BlockSpec((1, tc), lambda i, j: (0, j + nf))
    return g, v, wg, wv, bg, bv


def conv_fwd(name, up, cw, cb, tc=256):
    s_len, f2 = up.shape
    f = f2 // 2
    nf = f // tc
    g, v, wg, wv, bg, bv = _conv_specs(s_len, tc, nf)

    def fn(ug, uv, cwg, cwv, cbg, cbv):
        yg = _conv(ug.astype(F32), cwg, cbg)
        yv = _conv(uv.astype(F32), cwv, cbv)
        return yg * _sig(yg) * yv

    return blockk(name, fn, (1, nf), [(up, g), (up, v), (cw, wg), (cw, wv), (cb, bg), (cb, bv)],
                  [(_sds((s_len, f), BF16), g, None)])[0]


def conv_bwd(name, up, cw, cb, dact, tc=128):
    s_len, f2 = up.shape
    f = f2 // 2
    nf = f // tc
    g, v, wg, wv, bg, bv = _conv_specs(s_len, tc, nf)

    def conv(u, cwh, cbh):
        u1, u2 = _shift_down(u, 1), _shift_down(u, 2)
        return cbh + cwh[0:1, :] * u2 + cwh[1:2, :] * u1 + cwh[2:3, :] * u, u1, u2

    def half(dy, u, u1, u2, cwh):
        dup = cwh[2:3, :] * dy + _shift_up(cwh[1:2, :] * dy + _shift_up(cwh[0:1, :] * dy, 1), 1)
        dcw = jnp.concatenate([jnp.sum(dy * u2, axis=0, keepdims=True), jnp.sum(dy * u1, axis=0, keepdims=True),
                               jnp.sum(dy * u, axis=0, keepdims=True)], axis=0)
        return dup, dcw, jnp.sum(dy, axis=0, keepdims=True)

    def fn(ug, uv, cwg, cwv, cbg, cbv, da):
        ug, uv, da = ug.astype(F32), uv.astype(F32), da.astype(F32)
        yg, ug1, ug2 = conv(ug, cwg, cbg)
        yv, uv1, uv2 = conv(uv, cwv, cbv)
        sg = _sig(yg)
        silu = yg * sg
        dyv = da * silu
        dyg = da * yv * (sg + silu * (1.0 - sg))
        dug, dcwg, dcbg = half(dyg, ug, ug1, ug2, cwg)
        duv, dcwv, dcbv = half(dyv, uv, uv1, uv2, cwv)
        return dug, duv, dcwg, dcwv, dcbg, dcbv

    return blockk(name, fn, (1, nf), [(up, g), (up, v), (cw, wg), (cw, wv), (cb, bg), (cb, bv), (dact, g)],
                  [(_sds((s_len, f), BF16), g, None), (_sds((s_len, f), BF16), g, None),
                   (_sds((3, f), F32), wg, None), (_sds((3, f), F32), wg, None),
                   (_sds((1, f), F32), bg, None), (_sds((1, f), F32), bg, None)])


def loss_head(name, y, target, tr=256):
    s_len, d = y.shape

    def fn(yb, tb):
        e = yb - tb
        dy = e * (1.0 / d)
        return dy, dy, jnp.sum(e * e, axis=0, keepdims=True)

    spec = _rows(tr, d, 0)
    return blockk(name, fn, (s_len // tr, 1), [(y, spec), (target, spec)],
                  [(_sds((s_len, d), BF16), spec, None), (_sds((s_len, d), F32), spec, None),
                   (_sds((1, d), F32), _whole((1, d)), 'all')])


def _adam(w, g, m, v):
    m = ADAM_B1 * m + (1.0 - ADAM_B1) * g
    v = ADAM_B2 * v + (1.0 - ADAM_B2) * (g * g)
    m_hat = m / (1.0 - ADAM_B1 ** ADAM_STEP)
    v_hat = v / (1.0 - ADAM_B2 ** ADAM_STEP)
    delta = -ADAM_LR * (m_hat / (jnp.sqrt(v_hat) + ADAM_EPS) + ADAM_WD * w)
    return delta, m, v


def adam_sum(name, parts, w, m, v):
    n_layers, r, c = w.shape
    n_parts = parts[0].shape[0]
    assert len(parts) == n_layers
    tr = r
    for cand in (512, 256, 128, 64, 32, 16):
        if r % cand == 0 and cand * c * 4 <= 1024 * 1024:
            tr = cand
            break

    def body(*refs):
        part_refs = refs[:n_layers]
        w_ref, m_ref, v_ref, g_out, d_out, m_out, v_out = refs[n_layers:]
        layer = pl.program_id(0)
        for ll in range(n_layers):
            @pl.when(layer == ll)
            def _(ll=ll):
                g = part_refs[ll][0].astype(F32)
                for k in range(1, n_parts):
                    g = g + part_refs[ll][k].astype(F32)
                delta, m_new, v_new = _adam(w_ref[...], g, m_ref[...], v_ref[...])
                g_out[...] = g
                d_out[...] = delta
                m_out[...] = m_new
                v_out[...] = v_new

    spec = pl.BlockSpec((None, tr, c), lambda l, i: (l, i, 0))
    part_specs = [pl.BlockSpec((n_parts, tr, c), lambda l, i, ll=ll: (0, jnp.where(l == ll, i, 0), 0))
                  for ll in range(n_layers)]
    out = _sds((n_layers, r, c), F32)
    return pl.pallas_call(body, grid=(n_layers, r // tr), in_specs=part_specs + [spec] * 3, out_specs=[spec] * 4,
                          out_shape=[out] * 4, name=name, compiler_params=_cparams(2))(*parts, w, m, v)


def sum_parts(name, parts, tr=256):
    n_parts, r, c = parts.shape

    def fn(pb):
        g = pb[0].astype(F32)
        for k in range(1, n_parts):
            g = g + pb[k].astype(F32)
        return g

    return blockk(name, fn, (r // tr, 1), [(parts, pl.BlockSpec((n_parts, tr, c), lambda i, j: (0, i, 0)))],
                  [(_sds((r, c), F32), _rows(tr, c, 0), None)])[0]


def adam_flat(name, w, g, m, v, tr=256):
    r, c = w.shape
    spec = _rows(tr, c, 0)
    return blockk(name, lambda wb, gb, mb, vb: _adam(wb, gb, mb, vb), (r // tr, 1),
                  [(w, spec), (g, spec), (m, spec), (v, spec)], [(_sds((r, c), F32), spec, None)] * 3)


_ANY = pl.BlockSpec(memory_space=pl.ANY)


def _place():
    return lax.axis_index("x"), lax.axis_index("y"), lax.axis_index("c")


def _slot(px, py, pc):
    return 4 * px + 2 * py + pc


def all_gather(name, items):
    n = len(items)

    def body(*refs):
        xs, outs = refs[:n], refs[n:2 * n]
        send_sems, recv_sems, local_sems = refs[2 * n:]
        x, y, c = _place()
        me, sibling = (x, y, c), (x, y, 1 - c)
        chips = [(1 - x, y), (x, 1 - y), (1 - x, 1 - y)]

        def copy(t, k, block, to, src=None):
            dst = outs[t].at[_slot(*block)]
            return pltpu.make_async_remote_copy(src_ref=dst if src is None else src, dst_ref=dst,
                                                send_sem=send_sems.at[7 * t + k], recv_sem=recv_sems.at[7 * t + k],
                                                device_id=to, device_id_type=MESH)

        mine = [pltpu.make_async_copy(xs[t], outs[t].at[_slot(*me)], local_sems.at[t]) for t in range(n)]
        for cp in mine:
            cp.start()
        started = []
        for t in range(n):
            started.append(copy(t, 0, me, sibling, src=xs[t]))
            started += [copy(t, 1 + j, me, (*chip, c), src=xs[t]) for j, chip in enumerate(chips)]
        for cp in started:
            cp.start()
        for j, chip in enumerate(chips):
            for t in range(n):
                copy(t, 1 + j, (*chip, c), me).wait_recv()
                passed = copy(t, 4 + j, (*chip, c), sibling)
                passed.start()
                started.append(passed)
        for t in range(n):
            copy(t, 0, sibling, me).wait_recv()
            for j, chip in enumerate(chips):
                copy(t, 4 + j, (*chip, 1 - c), me).wait_recv()
        for cp in started:
            cp.wait_send()
        for cp in mine:
            cp.wait()

    return pl.pallas_call(
        body, in_specs=[_ANY] * n, out_specs=[_ANY] * n,
        out_shape=[_sds((N_DEV,) + a.shape, a.dtype) for a in items],
        scratch_shapes=[pltpu.SemaphoreType.DMA((7 * n,)), pltpu.SemaphoreType.DMA((7 * n,)), pltpu.SemaphoreType.DMA((n,))],
        name=name)(*items)


_HBM = pl.BlockSpec(memory_space=pltpu.HBM)
_SEM = pl.BlockSpec(memory_space=pltpu.SEMAPHORE)
_EFFECT = pltpu.SideEffectType.DATAFLOW_SIDE_EFFECTING


def _peer(k, x, y, c):
    return ((1 - x) if k & 4 else x, (1 - y) if k & 2 else y, (1 - c) if k & 1 else c)


PEERS_ALL = (1, 2, 3, 4, 5, 6, 7)
PEERS_CHIPWISE = (1, 2, 4, 6)


def _exchange_copies(mode, refs, send_sems, recv_sems, landing):
    x, y, c = _place()
    my_slot = _slot(x, y, c)
    copies = []
    if mode == "pass_on":
        for t, land in enumerate(refs):
            for j, chip in enumerate([(1 - x, y), (x, 1 - y), (1 - x, 1 - y)]):
                slot = _slot(*chip, (1 - c) if landing else c)
                copies.append(pltpu.make_async_remote_copy(
                    src_ref=land.at[slot], dst_ref=land.at[slot], send_sem=send_sems.at[3 * t + j],
                    recv_sem=recv_sems.at[3 * t + j], device_id=(x, y, 1 - c), device_id_type=MESH))
        return copies
    n = len(refs) // 2
    for t, (src, land) in enumerate(zip(refs[:n], refs[n:])):
        for k in (PEERS_CHIPWISE if mode == "gather_chipwise" else PEERS_ALL):
            peer = _peer(k, x, y, c)
            copies.append(pltpu.make_async_remote_copy(
                src_ref=src.at[_slot(*peer)] if mode == "scatter" else src,
                dst_ref=land.at[_slot(*peer) if landing else my_slot],
                send_sem=send_sems.at[7 * t + k - 1], recv_sem=recv_sems.at[7 * t + k - 1],
                device_id=peer, device_id_type=MESH))
    return copies


def exchange_start(name, mode, arrays):
    n = len(arrays)
    passing = mode == "pass_on"
    n_sems = (3 if passing else 7) * n
    lands = [] if passing else [pltpu.HBM(a.shape if mode == "scatter" else (N_DEV,) + a.shape, a.dtype) for a in arrays]

    def body(*refs):
        srcs, outs = refs[:n], refs[n + 2:]
        bufs = list(srcs) if passing else list(srcs) + list(outs[n:2 * n])
        for send in _exchange_copies(mode, bufs, refs[n], refs[n + 1], False):
            send.start()
        refs[-1][...] = jnp.zeros_like(refs[-1])

    arrays = [pltpu.with_memory_space_constraint(a, pltpu.HBM) for a in arrays]
    res = pl.pallas_call(
        body, name=name, in_specs=[_HBM] * n,
        out_specs=(_SEM, _SEM) + (_HBM,) * (n + len(lands)) + (pl.BlockSpec(memory_space=pltpu.VMEM),),
        out_shape=(pltpu.SemaphoreType.DMA((n_sems,)), pltpu.SemaphoreType.DMA((n_sems,)))
        + tuple(pltpu.HBM(a.shape, a.dtype) for a in arrays) + tuple(lands) + (_sds((8, HEAD_DIM), F32),),
        input_output_aliases={i: 2 + i for i in range(n)},
        compiler_params=pltpu.CompilerParams(has_side_effects=_EFFECT))(*arrays)
    return dict(name=name, mode=mode, sems=res[:2], bufs=res[2:-1]), res[-1]


def exchange_wait(handle, after):
    mode, bufs = handle['mode'], handle['bufs']
    nb = len(bufs)

    def body(*refs):
        for landed in _exchange_copies(mode, refs[:nb], refs[nb], refs[nb + 1], True):
            landed.wait_send()
            landed.wait_recv()

    res = pl.pallas_call(
        body, name=handle['name'].replace("start", "wait"), in_specs=[_HBM] * nb + [_SEM, _SEM, _ANY],
        out_specs=(_HBM,) * nb, out_shape=tuple(pltpu.HBM(a.shape, a.dtype) for a in bufs),
        input_output_aliases={i: i for i in range(nb)},
        compiler_params=pltpu.CompilerParams(has_side_effects=_EFFECT))(*bufs, *handle['sems'], after)
    if mode == "pass_on":
        return list(res)
    my_slot = _slot(*_place())
    filled = []
    for src, land in zip(res[:nb // 2], res[nb // 2:]):
        own = lax.dynamic_index_in_dim(src, my_slot, 0, keepdims=True) if mode == "scatter" else src[None]
        filled.append(lax.dynamic_update_slice(land, own, (my_slot,) + (0,) * (land.ndim - 1)))
    return filled


def _pad_lanes(a, width=HEAD_DIM):
    return jnp.pad(a, [(0, 0)] * (a.ndim - 1) + [(0, width - a.shape[-1])])


def _slabs(full, n):
    k = full.shape[0]
    return full.reshape(k, N_DEV, n).transpose(1, 0, 2)


def fox_fwd(a, w, p):
    qkv = mm_nn(a, w['qkv'], BF16, "fox_qkv")
    fl = mm_nn(a, w['f'], F32, "fox_flogit")
    bf = _pad_lanes(p['fox_b_f'])
    cum = fox_gate_fwd("fox_gate_fwd", fl, bf)
    cum_t = cum[:, :N_HEADS].T
    cums = (cum_t[:, :, None], cum_t[:, None, :])
    qn = rms_fwd("fox_qnorm", qkv, p['fox_q_gain'], HEAD_DIM, 512, HEAD_DIM, xcol=0, nh=N_HEADS)
    kn = rms_fwd("fox_knorm", qkv, p['fox_k_gain'], HEAD_DIM, 512, HEAD_DIM, xcol=N_HEADS, nh=N_HEADS)
    o, lse = attn_fwd("fox_attn_fwd", qn, kn, qkv, HEAD_DIM ** -0.5, voff=2 * N_HEADS, cum=cums, exact_o=True)
    return o, dict(a=a, qkv=qkv, fl=fl, bf=bf, cums=cums, qn=qn, kn=kn, o=o, lse=lse)


def fox_bwd(do, s, w, p):
    dqn, dkn, dv, dcs = attn_bwd("fox_attn_bwd", s['qn'], s['kn'], s['qkv'], s['o'], do, s['lse'], HEAD_DIM ** -0.5,
                                 voff=2 * N_HEADS, cum=s['cums'])
    dq, dgq = rms_bwd("fox_qnorm_bwd", s['qkv'], p['fox_q_gain'], dqn, HEAD_DIM, 512, HEAD_DIM, xcol=0, nh=N_HEADS,
                      out_dtype=BF16)
    dk, dgk = rms_bwd("fox_knorm_bwd", s['qkv'], p['fox_k_gain'], dkn, HEAD_DIM, 512, HEAD_DIM, xcol=N_HEADS,
                      nh=N_HEADS, out_dtype=BF16)
    dqkv = jnp.concatenate([dq, dk, dv.astype(BF16)], axis=1)
    dcum = _pad_lanes(-jnp.sum(dcs[:, :, 0, :], axis=1).T)
    dfl, dbf = fox_gate_bwd("fox_gate_bwd", s['fl'], s['bf'], dcum)
    da = mm_nt(dqkv, w['qkv'], F32, "fox_da_qkv")
    da = mm_nt(dfl, w['f'], F32, "fox_da_f", add=da)
    dw_qkv = mm_tn(s['a'], dqkv, 1, BF16, "fox_dw_qkv")[0]
    dw_f = mm_tn(s['a'], dfl, 1, BF16, "fox_dw_f")[0][:, :N_HEADS]
    dw = _slabs(jnp.concatenate([dw_qkv, dw_f], axis=1), 770)
    return da, dict(fox_w_in=dw), dict(fox_b_f=dbf[:, :N_HEADS], fox_q_gain=dgq, fox_k_gain=dgk)


def _rope_tables(positions):
    inv_freq = ROPE_THETA ** (-jnp.arange(0, MLA_ROPE, 2, dtype=F32) / MLA_ROPE)
    ang = positions.astype(F32)[:, None] * inv_freq
    cos, sin = jnp.cos(ang), jnp.sin(ang)
    return _pad_lanes(jnp.concatenate([cos, cos], axis=1)), _pad_lanes(jnp.concatenate([sin, sin], axis=1))


def mla_fwd(a, w, p):
    qg, kg = p['mla_q_gain'], p['mla_k_gain']
    gains = dict(qn=qg[:, :HEAD_DIM], qr=_pad_lanes(qg[:, HEAD_DIM:]), kn=kg[:, :HEAD_DIM], kr=_pad_lanes(kg[:, HEAD_DIM:]))
    cos, sin = _rope_tables(p['positions'])
    ccr = mm_nn(a, w['in'], F32, "mla_in")
    cqn = rms_fwd("mla_cq_norm", ccr, w['q_a_gain'], 512, 256, 512, xcol=0)
    ckvn = rms_fwd("mla_ckv_norm", ccr, w['kv_a_gain'], 512, 256, 512, xcol=1)
    qf = mm_nn(cqn, w['q_b'], F32, "mla_q_b")
    kvf = mm_nn(ckvn, w['kv_b'], F32, "mla_kv_b")
    q_nope = rms_fwd("mla_qnope_norm", qf, gains['qn'], HEAD_DIM, 512, HEAD_DIM, xcol=0, nh=N_HEADS)
    k_nope = rms_fwd("mla_knope_norm", kvf, gains['kn'], HEAD_DIM, 512, HEAD_DIM, xcol=0, nh=N_HEADS)
    q_rope = rope_fwd("mla_qrope", qf, gains['qr'], cos, sin, 512, N_HEADS, N_HEADS)
    k_rope = rope_fwd("mla_krope", ccr, gains['kr'], cos, sin, 512, 8, 1)
    scale = (HEAD_DIM + MLA_ROPE) ** -0.5
    o, lse = attn_fwd("mla_attn_fwd", q_nope, k_nope, kvf, scale, voff=N_HEADS, q2=q_rope, k2=k_rope)
    return o, dict(a=a, gains=gains, cos=cos, sin=sin, ccr=ccr, cqn=cqn, ckvn=ckvn, qf=qf, kvf=kvf, q_nope=q_nope,
                   k_nope=k_nope, q_rope=q_rope, k_rope=k_rope, o=o, lse=lse, scale=scale)


def mla_bwd(do, s, w, p):
    g = s['gains']
    dqn, dkn, dv, dq2, dk2 = attn_bwd("mla_attn_bwd", s['q_nope'], s['k_nope'], s['kvf'], s['o'], do, s['lse'], s['scale'],
                                      voff=N_HEADS, q2=s['q_rope'], k2=s['k_rope'])
    dqf_n, dg_qn = rms_bwd("mla_qnope_bwd", s['qf'], g['qn'], dqn, HEAD_DIM, 512, HEAD_DIM, xcol=0, nh=N_HEADS)
    dkf, dg_kn = rms_bwd("mla_knope_bwd", s['kvf'], g['kn'], dkn, HEAD_DIM, 512, HEAD_DIM, xcol=0, nh=N_HEADS)
    dqf_r, dg_qr = rope_bwd("mla_qrope_bwd", s['qf'], g['qr'], s['cos'], s['sin'], dq2, 512, N_HEADS, N_HEADS)
    dkr, dg_kr = rope_bwd("mla_krope_bwd", s['ccr'], g['kr'], s['cos'], s['sin'], dk2, 512, 8, 1)
    dqf = jnp.concatenate([dqf_n, dqf_r], axis=1)
    dkvf = jnp.concatenate([dkf, dv], axis=1)
    dcqn = mm_nt(dqf, w['q_b'], F32, "mla_dcq")
    dckvn = mm_nt(dkvf, w['kv_b'], F32, "mla_dckv")
    dw_qb = mm_tn(s['cqn'], dqf, 1, BF16, "mla_dw_qb")[0]
    dw_kvb = mm_tn(s['ckvn'], dkvf, 1, BF16, "mla_dw_kvb")[0]
    dcq, dg_qa = rms_bwd("mla_cq_bwd", s['ccr'], w['q_a_gain'], dcqn, 512, 256, 512, xcol=0)
    dckv, dg_kva = rms_bwd("mla_ckv_bwd", s['ccr'], w['kv_a_gain'], dckvn, 512, 256, 512, xcol=1)
    dccr = jnp.concatenate([dcq, dckv, dkr], axis=1)
    da = mm_nt(dccr, w['in'], F32, "mla_da")
    dw_in = mm_tn(s['a'], dccr, 1, BF16, "mla_dw_in")[0][:, :1088].reshape(N_DEV, 256, 1088)
    hp = 2
    nope = dw_qb[:, :2048].reshape(512, N_DEV, hp, HEAD_DIM)
    rope = dw_qb[:, 2048:].reshape(512, N_DEV, hp, HEAD_DIM)[..., :MLA_ROPE]
    dw_qb_s = jnp.concatenate([nope, rope], axis=-1).transpose(1, 0, 2, 3).reshape(N_DEV, 512, hp * 192)
    kk = dw_kvb[:, :2048].reshape(512, N_DEV, hp, HEAD_DIM)
    vv = dw_kvb[:, 2048:].reshape(512, N_DEV, hp, HEAD_DIM)
    dw_kvb_s = jnp.concatenate([kk, vv], axis=-1).transpose(1, 0, 2, 3).reshape(N_DEV, 512, hp * 256)
    small = dict(mla_q_a_gain=dg_qa, mla_kv_a_gain=dg_kva,
                 mla_q_gain=jnp.concatenate([dg_qn, dg_qr[:, :MLA_ROPE]], axis=1),
                 mla_k_gain=jnp.concatenate([dg_kn, dg_kr[:, :MLA_ROPE]], axis=1))
    return da, dict(mla_w_in=dw_in, mla_w_q_b=dw_qb_s, mla_w_kv_b=dw_kvb_s), small


def sb_fwd_layer(a, w, p):
    qkv = mm_nn(a, w['in'], BF16, "sb_qkv")
    qn = rms_fwd("sb_qnorm", qkv, p['sb_q_gain'], HEAD_DIM, 512, HEAD_DIM, xcol=0, nh=N_HEADS)
    kn = rms_fwd("sb_knorm", qkv, p['sb_k_gain'], HEAD_DIM, 512, HEAD_DIM, xcol=N_HEADS, nh=N_HEADS)
    o, carries = sb_fwd("sb_attn_fwd", qn, kn, qkv, HEAD_DIM ** -0.5, 2 * N_HEADS)
    return o, dict(a=a, qkv=qkv, qn=qn, kn=kn, carries=carries)


def sb_bwd_layer(do, s, w, p):
    dqn, dkn, dv = sb_bwd("sb_attn_bwd", s['qn'], s['kn'], s['qkv'], do, s['carries'], HEAD_DIM ** -0.5, 2 * N_HEADS)
    dq, dgq = rms_bwd("sb_qnorm_bwd", s['qkv'], p['sb_q_gain'], dqn, HEAD_DIM, 512, HEAD_DIM, xcol=0, nh=N_HEADS,
                      out_dtype=BF16)
    dk, dgk = rms_bwd("sb_knorm_bwd", s['qkv'], p['sb_k_gain'], dkn, HEAD_DIM, 512, HEAD_DIM, xcol=N_HEADS, nh=N_HEADS,
                      out_dtype=BF16)
    dqkv = jnp.concatenate([dq, dk, dv.astype(BF16)], axis=1)
    da = mm_nt(dqkv, w['in'], F32, "sb_da")
    dw = mm_tn(s['a'], dqkv, N_DEV, BF16, "sb_dw_in")
    return da, dict(sb_w_in=dw), dict(sb_q_gain=dgq, sb_k_gain=dgk)


def sgu_fwd(a, w, p):
    pre = mm_nn(a, w['in'], BF16, "sgu_in")
    u, vn = sgu_pre_fwd("sgu_pre_fwd", pre, w['v_gain'], 256)
    ws = p['sgu_w_s'][0]
    bs3 = p['sgu_b_s'][0][:, :, None]
    prod = sgu_mix_fwd("sgu_mix_fwd", vn, u, ws, bs3)
    return prod, dict(a=a, pre=pre, u=u, vn=vn, ws=ws, bs3=bs3)


def sgu_bwd(dprod, s, w, p):
    du, dvn, dws, dbs3 = sgu_mix_bwd("sgu_mix_bwd", s['vn'], s['u'], s['ws'], s['bs3'], dprod)
    dpu, dpv, dgv = sgu_pre_bwd("sgu_pre_bwd", s['pre'], w['v_gain'], du, dvn, 128)
    dpre = jnp.concatenate([dpu, dpv], axis=1)
    da = mm_nt(dpre, w['in'], F32, "sgu_da")
    dw = mm_tn(s['a'], dpre, N_DEV, BF16, "sgu_dw_in")
    return da, dict(sgu_w_in=dw), dict(sgu_v_gain=dgv, sgu_w_s=dws[None], sgu_b_s=dbs3[None, :, :, 0])


MIXERS = [("fox", fox_fwd, fox_bwd), ("mla", mla_fwd, mla_bwd), ("sb", sb_fwd_layer, sb_bwd_layer),
          ("sgu", sgu_fwd, sgu_bwd)]


def _pack_rows(arrays, row_mult=256):
    flat = jnp.concatenate([a.reshape(-1).astype(F32) for a in arrays])
    per = row_mult * HEAD_DIM
    total = -(-flat.shape[0] // per) * per
    return jnp.pad(flat, (0, total - flat.shape[0])).reshape(total // HEAD_DIM, HEAD_DIM)


def _unpack_rows(packed, shapes):
    flat = packed.reshape(-1)
    out, off = [], 0
    for shp in shapes:
        n = math.prod(shp)
        out.append(flat[off:off + n].reshape(shp))
        off += n
    return out


def _mixer_shards(i, p):
    name = MIXERS[i][0]
    bf = lambda a: a.astype(BF16)
    items = {'out': bf(p[name + '_w_out'][0])}
    if name == "fox":
        items['in'] = bf(p['fox_w_in'][0])
        items['small'] = _pack_rows([p[n] for n in SMALL_SHARDED], 8)
    elif name == "mla":
        items.update({'in': bf(p['mla_w_in'][0]), 'q_b': bf(p['mla_w_q_b'][0]), 'kv_b': bf(p['mla_w_kv_b'][0])})
    else:
        items['in'] = bf(p[name + '_w_in'][0])
    return items


def _ffn_shards(i, p):
    return {'up': p['ffn_w_up'][i].astype(BF16), 'down': p['ffn_w_down'][i].astype(BF16)}


def _assemble_ffn(got):
    return {'up': got['up'], 'down': got['down'].reshape(1, D_FF, -1)}


def _assemble_mixer(i, got, p):
    name = MIXERS[i][0]
    w = {'out': got['out'].reshape(1, -1, got['out'].shape[-1])} if 'out' in got else {}
    small = None
    if name == "fox":
        if 'in' in got:
            full = got['in'].transpose(1, 0, 2).reshape(got['in'].shape[1], -1)
            w['qkv'] = full[None, :, :3 * N_HEADS * HEAD_DIM]
            w['f'] = _pad_lanes(full[:, 3 * N_HEADS * HEAD_DIM:])[None]
        if 'small' in got:
            small_shapes = [p[n].shape for n in SMALL_SHARDED]
            parts = [_unpack_rows(got['small'][d], small_shapes) for d in range(N_DEV)]
            small = {n: jnp.concatenate([parts[d][k] for d in range(N_DEV)], axis=ax)
                     for k, (n, ax) in enumerate(SMALL_SHARDED.items())}
    elif name == "mla":
        w['in'] = _pad_lanes(got['in'].reshape(-1, 1088), 1152)[None]
        hp = 2
        qb = got['q_b'].reshape(N_DEV, 512, hp, 192).transpose(1, 0, 2, 3)
        nope = qb[..., :HEAD_DIM].reshape(512, -1)
        rope = _pad_lanes(qb[..., HEAD_DIM:]).reshape(512, -1)
        w['q_b'] = jnp.concatenate([nope, rope], axis=1)[None]
        kvb = got['kv_b'].reshape(N_DEV, 512, hp, 256).transpose(1, 0, 2, 3)
        w['kv_b'] = jnp.concatenate([kvb[..., :HEAD_DIM].reshape(512, -1), kvb[..., HEAD_DIM:].reshape(512, -1)], axis=1)[None]
    else:
        w['in'] = got['in']
    return w, small


def _train_step(p):
    x, target = p['x'][0], p['loss_target'][0]
    p = dict(p, positions=p['positions'][0])
    xi, yi, ci = _place()
    my_slot = _slot(xi, yi, ci)

    shards0 = _mixer_shards(0, p)
    got0 = {'in': all_gather("gather_first", [shards0.pop('in')])[0]}
    pending, order_token = {}, jnp.zeros((1, 1), F32)
    for i in range(DEPTH):
        for kind, shards in (("mix", _mixer_shards(i, p) if i else shards0), ("ffn", _ffn_shards(i, p))):
            handle, token = exchange_start(f"xstart_ag_{kind}{i}", "gather_chipwise", list(shards.values()))
            pending[kind, i] = (handle, list(shards))
            order_token = order_token + token[0:1, 0:1]

    def pass_on(kind, i, after):
        handle, keys = pending[kind, i]
        handle, token = exchange_start(f"xstart_pass_{kind}{i}", "pass_on", exchange_wait(handle, after))
        pending[kind, i] = (handle, keys)
        return token[0:1, 0:1]

    def gathered(kind, i, after):
        handle, keys = pending[kind, i]
        return dict(zip(keys, exchange_wait(handle, after)))

    h = x
    saved, weights = [], []
    small_full = None
    for i in range(DEPTH):
        if i == 0:
            w, _ = _assemble_mixer(0, got0, p)
            gain = p['mix_norm'][0:1] + order_token
        else:
            w, _ = _assemble_mixer(i, gathered("mix", i, h), p)
            gain = p['mix_norm'][i:i + 1] + pass_on("ffn", i, h)
            w.update(q_a_gain=small_full['mla_q_a_gain'], kv_a_gain=small_full['mla_kv_a_gain'],
                     v_gain=small_full['sgu_v_gain'])
        a = rms_fwd(f"mix_norm_{i}", h, gain, h.shape[1], 512, h.shape[1])
        mixed, s_mix = MIXERS[i][1](a, w, p)
        if i == 0:
            pass_on("mix", 0, mixed)
            pass_on("ffn", 0, mixed)
            rest, small_full = _assemble_mixer(0, gathered("mix", 0, mixed), p)
            w.update(rest)
            conv_w = small_full['ffn_conv_w']
        w.update(_assemble_ffn(gathered("ffn", i, mixed)))
        weights.append(w)
        h1 = mm_nn(mixed, w['out'], F32, f"mix_out_{i}", add=h)
        b = rms_fwd(f"ffn_norm_{i}", h1, p['ffn_norm'][i:i + 1], h.shape[1], 512, h.shape[1])
        up = mm_nn(b, w['up'], BF16, f"ffn_up_{i}")
        conv_b = p['ffn_conv_b'][i:i + 1]
        if i + 1 < DEPTH:
            conv_b = conv_b + pass_on("mix", i + 1, up)
        act = conv_fwd(f"ffn_conv_{i}", up, conv_w[i], conv_b)
        h2 = mm_nn(act, w['down'], F32, f"ffn_down_{i}", add=h1)
        saved.append(dict(h=h, mixed=mixed, s_mix=s_mix, h1=h1, b=b, up=up, act=act))
        h = h2

    dh16, dh, sq = loss_head("loss_head", h, target)
    loss = lax.psum(0.5 * jnp.sum(sq) / h.shape[1], ("x", "y", "c"))

    scatters = []
    small_g = {n: [None] * p[n].shape[0] for n in ('mix_norm', 'ffn_norm', 'ffn_conv_w', 'ffn_conv_b')}
    for i in reversed(range(DEPTH)):
        w, s = weights[i], saved[i]
        name = MIXERS[i][0]
        dact = mm_nt(dh16, w['down'], BF16, f"ffn_dact_{i}")
        dw_down = mm_tn(s['act'], dh16, 1, BF16, f"ffn_dw_down_{i}").reshape(N_DEV, D_FF // N_DEV, -1)
        dug, duv, dcwg, dcwv, dcbg, dcbv = conv_bwd(f"ffn_conv_bwd_{i}", s['up'], conv_w[i], p['ffn_conv_b'][i:i + 1], dact)
        dup = jnp.concatenate([dug, duv], axis=1)
        dw_up = mm_tn(s['b'], dup, N_DEV, BF16, f"ffn_dw_up_{i}")
        handle, token = exchange_start(f"xstart_rs_ffn{i}", "scatter", [dw_up, dw_down])
        scatters.append((handle, ['ffn_w_up', 'ffn_w_down']))
        db = mm_nt(dup, w['up'], F32, f"ffn_db_{i}")
        dh1_16, dh1, dg_ffn = rms_bwd(f"ffn_norm_bwd_{i}", s['h1'], p['ffn_norm'][i:i + 1] + token[0:1, 0:1], db,
                                      h.shape[1], 256, h.shape[1], resid=dh, also_bf16=True)
        dw_out = mm_tn(s['mixed'], dh1_16, 1, BF16, f"mix_dw_out_{i}").reshape(N_DEV, -1, h.shape[1])
        if i == 0:
            handle, token = exchange_start("xstart_rs_out0", "scatter", [dw_out])
            scatters.append((handle, [name + '_w_out']))
            dh1_16 = dh1_16 + token[0:1, 0:1].astype(BF16)
        dmix = mm_nt(dh1_16, w['out'], BF16, f"mix_dout_{i}")
        da, big_i, small_i = MIXERS[i][2](dmix, s['s_mix'], w, p)
        if i > 0:
            big_i[name + '_w_out'] = dw_out
        handle, token = exchange_start(f"xstart_rs_mix{i}", "scatter", list(big_i.values()))
        scatters.append((handle, list(big_i)))
        dh16, dh, dg_mix = rms_bwd(f"mix_norm_bwd_{i}", s['h'], p['mix_norm'][i:i + 1] + token[0:1, 0:1], da, h.shape[1],
                                   256, h.shape[1], resid=dh1, also_bf16=True)
        for k, v in small_i.items():
            small_g[k] = v
        small_g['mix_norm'][i], small_g['ffn_norm'][i] = dg_mix, dg_ffn
        small_g['ffn_conv_w'][i] = jnp.concatenate([dcwg, dcwv], axis=1)[None]
        small_g['ffn_conv_b'][i] = jnp.concatenate([dcbg, dcbv], axis=1)
    for n in ('mix_norm', 'ffn_norm', 'ffn_conv_w', 'ffn_conv_b'):
        small_g[n] = jnp.concatenate(small_g[n], axis=0)
    grad_x = dh[None]

    full_shapes = [tuple(small_g[n].shape) for n in SMALL]
    small_handle, small_token = exchange_start("xstart_small", "gather", [_pack_rows([small_g[n] for n in SMALL])])
    big = {}
    for handle, names in scatters[:-1]:
        for n, landed in zip(names, exchange_wait(handle, small_token)):
            big.setdefault(n, []).insert(0, landed)
    grads, deltas, new_m, new_v = {}, {}, {}, {}
    last_handle, last_names = scatters[-1]
    for n in BIG:
        if n not in last_names:
            grads[n], deltas[n], new_m[n], new_v[n] = adam_sum("adam_" + n, big[n], p[n], p['m_' + n], p['v_' + n])
    for n, landed in zip(last_names, exchange_wait(last_handle, new_v['ffn_w_up'])):
        grads[n], deltas[n], new_m[n], new_v[n] = adam_sum("adam_" + n, [landed], p[n], p['m_' + n], p['v_' + n])
    partials = exchange_wait(small_handle, new_v[last_names[0]])[0]
    summed = _unpack_rows(sum_parts("sum_small_grads", partials), full_shapes)
    mine = []
    for n, g in zip(SMALL, summed):
        if n in SMALL_SHARDED:
            ax = SMALL_SHARDED[n]
            g = lax.dynamic_slice_in_dim(g, my_slot * p[n].shape[ax], p[n].shape[ax], axis=ax)
        mine.append(g)
    shapes = [p[n].shape for n in SMALL]
    packed = [_pack_rows(arrs) for arrs in ([p[n] for n in SMALL], mine, [p['m_' + n] for n in SMALL], [p['v_' + n] for n in SMALL])]
    d_s, m_s, v_s = adam_flat("adam_small", *packed)
    for n, g, d, m, v in zip(SMALL, mine, _unpack_rows(d_s, shapes), _unpack_rows(m_s, shapes), _unpack_rows(v_s, shapes)):
        grads[n], deltas[n], new_m[n], new_v[n] = g, d, m, v

    return (loss, grad_x, *[grads[n] for n in WEIGHTS], *[deltas[n] for n in WEIGHTS], *[new_m[n] for n in WEIGHTS],
            *[new_v[n] for n in WEIGHTS])


def kernel(x, positions, mix_norm, ffn_norm, fox_w_in, fox_b_f, fox_q_gain, fox_k_gain, fox_w_out, mla_w_in, mla_q_a_gain, mla_kv_a_gain, mla_w_q_b, mla_w_kv_b, mla_q_gain, mla_k_gain, mla_w_out, sb_w_in, sb_q_gain, sb_k_gain, sb_w_out, sgu_w_in, sgu_v_gain, sgu_w_s, sgu_b_s, sgu_w_out, ffn_w_up, ffn_conv_w, ffn_conv_b, ffn_w_down, loss_target, m_mix_norm, m_ffn_norm, m_fox_w_in, m_fox_b_f, m_fox_q_gain, m_fox_k_gain, m_fox_w_out, m_mla_w_in, m_mla_q_a_gain, m_mla_kv_a_gain, m_mla_w_q_b, m_mla_w_kv_b, m_mla_q_gain, m_mla_k_gain, m_mla_w_out, m_sb_w_in, m_sb_q_gain, m_sb_k_gain, m_sb_w_out, m_sgu_w_in, m_sgu_v_gain, m_sgu_w_s, m_sgu_b_s, m_sgu_w_out, m_ffn_w_up, m_ffn_conv_w, m_ffn_conv_b, m_ffn_w_down, v_mix_norm, v_ffn_norm, v_fox_w_in, v_fox_b_f, v_fox_q_gain, v_fox_k_gain, v_fox_w_out, v_mla_w_in, v_mla_q_a_gain, v_mla_kv_a_gain, v_mla_w_q_b, v_mla_w_kv_b, v_mla_q_gain, v_mla_k_gain, v_mla_w_out, v_sb_w_in, v_sb_q_gain, v_sb_k_gain, v_sb_w_out, v_sgu_w_in, v_sgu_v_gain, v_sgu_w_s, v_sgu_b_s, v_sgu_w_out, v_ffn_w_up, v_ffn_conv_w, v_ffn_conv_b, v_ffn_w_down):
    args = locals()
    names = ['x', 'positions'] + WEIGHTS + ['loss_target'] + ['m_' + n for n in WEIGHTS] + ['v_' + n for n in WEIGHTS]
    return _train_step({n: args[n] for n in names})
```

```python
import functools
import math

import jax
import jax.numpy as jnp
from jax import lax
from jax.experimental import pallas as pl
from jax.experimental.pallas import tpu as pltpu

F32 = jnp.float32
BF16 = jnp.bfloat16
MESH = pl.DeviceIdType.MESH

N_DEV = 8
N_HEADS = 16
HEAD_DIM = 128
EPS = 1e-6
DEPTH = 4
D_FF = 5632
MLA_ROPE = 64
ROPE_THETA = 10000.0
VMEM_LIMIT = 48 * 1024 * 1024

ADAM_LR, ADAM_B1, ADAM_B2, ADAM_EPS, ADAM_WD, ADAM_STEP = 0.001, 0.9, 0.999, 1e-08, 0.01, 10

WEIGHTS = ['mix_norm', 'ffn_norm', 'fox_w_in', 'fox_b_f', 'fox_q_gain', 'fox_k_gain', 'fox_w_out', 'mla_w_in',
           'mla_q_a_gain', 'mla_kv_a_gain', 'mla_w_q_b', 'mla_w_kv_b', 'mla_q_gain', 'mla_k_gain', 'mla_w_out',
           'sb_w_in', 'sb_q_gain', 'sb_k_gain', 'sb_w_out', 'sgu_w_in', 'sgu_v_gain', 'sgu_w_s', 'sgu_b_s',
           'sgu_w_out', 'ffn_w_up', 'ffn_conv_w', 'ffn_conv_b', 'ffn_w_down']
BIG = ['fox_w_in', 'fox_w_out', 'mla_w_in', 'mla_w_q_b', 'mla_w_kv_b', 'mla_w_out', 'sb_w_in', 'sb_w_out',
       'sgu_w_in', 'sgu_w_out', 'ffn_w_up', 'ffn_w_down']
SMALL = [w for w in WEIGHTS if w not in BIG]
SMALL_SHARDED = {'mla_q_a_gain': 1, 'mla_kv_a_gain': 1, 'sgu_v_gain': 1, 'ffn_conv_w': 2}


def _cparams(n_grid):
    return pltpu.CompilerParams(dimension_semantics=("arbitrary",) * n_grid, vmem_limit_bytes=VMEM_LIMIT)


def _pick(n, cap):
    best = None
    t = 128
    while t <= min(n, cap):
        if n % t == 0:
            best = t
        t += 128
    return best if best is not None else n


def _mm_call(name, a, b, out_shape, a_spec, b_spec, o_spec, grid, dims, acc_shape, add=None):
    nk = grid[2]

    def body(*refs):
        a_ref, b_ref = refs[:2]
        add_ref = refs[2] if add is not None else None
        o_ref = refs[3] if add is not None else refs[2]
        prod = lax.dot_general(a_ref[...].astype(BF16), b_ref[...].astype(BF16), (dims, ((), ())),
                               preferred_element_type=F32)

        def finish(r):
            if add_ref is not None:
                r = r + add_ref[...].astype(F32)
            o_ref[...] = r.astype(o_ref.dtype)

        if nk == 1:
            finish(prod)
            return
        acc = refs[-1]
        k = pl.program_id(2)

        @pl.when(k == 0)
        def _():
            acc[...] = prod

        @pl.when(k > 0)
        def _():
            acc[...] += prod

        @pl.when(k == nk - 1)
        def _():
            finish(acc[...])

    ins = [a, b] + ([] if add is None else [add])
    in_specs = [a_spec, b_spec] + ([] if add is None else [o_spec])
    return pl.pallas_call(body, grid=grid, in_specs=in_specs, out_specs=o_spec, out_shape=out_shape,
                          scratch_shapes=[] if nk == 1 else [pltpu.VMEM(acc_shape, F32)], name=name,
                          compiler_params=_cparams(3))(*ins)


def mm_nn(a, b3, out_dtype, name, add=None, joff=0, nj=None):
    m, kk = a.shape
    _, kb, n = b3.shape
    assert kb == kk
    nj = b3.shape[0] - joff if nj is None else nj
    tn, tk = _pick(n, 1536), _pick(kk, 2048)
    tm = _pick(m, 512 if a.dtype == F32 and tk > 1024 else 1024)
    nb = n // tn
    return _mm_call(
        name, a, b3, jax.ShapeDtypeStruct((m, nj * n), out_dtype),
        pl.BlockSpec((tm, tk), lambda i, c, k: (i, k)),
        pl.BlockSpec((None, tk, tn), lambda i, c, k: (joff + c // nb, k, c % nb)),
        pl.BlockSpec((tm, tn), lambda i, c, k: (i, c)),
        (m // tm, nj * nb, kk // tk), ((1,), (0,)), (tm, tn), add=add)


def _grouped_cols(x, tile, rows_tile, rows_arg):
    pick = {'i': lambda i, o, c: (i, c), 'k': lambda o, c, k: (k, c)}[rows_arg]
    if x.ndim == 2:
        return pl.BlockSpec((rows_tile, tile), pick)
    per_group = x.shape[2] // tile
    assert x.shape[2] % tile == 0

    def index(*g):
        r, c = pick(*g)
        return c // per_group, r, c % per_group

    return pl.BlockSpec((None, rows_tile, tile), index)


def mm_nt(a, b3, out_dtype, name, add=None, joff=0, nj=None):
    m, na = (a.shape[1], a.shape[0] * a.shape[2]) if a.ndim == 3 else a.shape
    _, ko, n = b3.shape
    nj = b3.shape[0] - joff if nj is None else nj
    assert na == nj * n
    to, tn = _pick(ko, 1024), _pick(n, 2048)
    tm = _pick(m, 512 if a.dtype == F32 and tn > 1024 else 1024)
    nb = n // tn
    return _mm_call(
        name, a, b3, jax.ShapeDtypeStruct((m, ko), out_dtype),
        _grouped_cols(a, tn, tm, 'i'),
        pl.BlockSpec((None, to, tn), lambda i, o, c: (joff + c // nb, o, c % nb)),
        pl.BlockSpec((tm, to), lambda i, o, c: (i, o)),
        (m // tm, ko // to, nj * nb), ((1,), (1,)), (tm, to), add=add)


def mm_tn(a, b, nj, out_dtype, name):
    s, ko = a.shape
    sb, nb_tot = (b.shape[1], b.shape[0] * b.shape[2]) if b.ndim == 3 else b.shape
    assert sb == s and nb_tot % nj == 0
    n = nb_tot // nj
    to, tn = _pick(ko, 1024), _pick(n, 1536)
    ts = _pick(s, 1024 if F32 in (a.dtype, b.dtype) else 2048)
    nb = n // tn
    return _mm_call(
        name, a, b, jax.ShapeDtypeStruct((nj, ko, n), out_dtype),
        pl.BlockSpec((ts, to), lambda o, c, k: (k, o)),
        _grouped_cols(b, tn, ts, 'k'),
        pl.BlockSpec((None, to, tn), lambda o, c, k: (c // nb, o, c % nb)),
        (ko // to, nj * nb, s // ts), ((0,), (0,)), (to, tn))


def blockk(name, fn, grid, ins, outs):
    n_in = len(ins)
    accs = [o[2] for o in outs]

    def body(*refs):
        vals = fn(*[r[...] for r in refs[:n_in]])
        if not isinstance(vals, (tuple, list)):
            vals = (vals,)
        i, j = pl.program_id(0), pl.program_id(1)
        for r, v, acc in zip(refs[n_in:], vals, accs):
            if acc is None:
                r[...] = v.astype(r.dtype)
            else:
                first = (j == 0) if acc == 'inner' else jnp.logical_and(i == 0, j == 0)

                @pl.when(first)
                def _(r=r, v=v):
                    r[...] = v.astype(r.dtype)

                @pl.when(jnp.logical_not(first))
                def _(r=r, v=v):
                    r[...] += v.astype(r.dtype)

    res = pl.pallas_call(body, grid=grid, in_specs=[s for _, s in ins], out_specs=[o[1] for o in outs],
                         out_shape=[o[0] for o in outs], name=name, compiler_params=_cparams(2))(*[a for a, _ in ins])
    return res


def _sds(shape, dtype):
    return jax.ShapeDtypeStruct(tuple(shape), dtype)


def _rows(tr, w, col=0):
    if col == 'j':
        return pl.BlockSpec((tr, w), lambda i, j: (i, j))
    if callable(col):
        return pl.BlockSpec((tr, w), lambda i, j: (i, col(j)))
    return pl.BlockSpec((tr, w), lambda i, j: (i, col))


def _whole(shape):
    nd = len(shape)
    return pl.BlockSpec(tuple(shape), lambda i, j: (0,) * nd)


def _rms(x, g, n):
    ms = jnp.sum(x * x, axis=-1, keepdims=True) * (1.0 / n)
    return x * lax.rsqrt(ms + EPS) * g


def _sig(x):
    return 1.0 / (1.0 + jnp.exp(-x))


def _gelu(x):
    return 0.5 * x * (1.0 + jnp.tanh(math.sqrt(2.0 / math.pi) * (x + 0.044715 * (x * x * x))))


def _lane_iota(shape):
    return lax.broadcasted_iota(jnp.int32, shape, len(shape) - 1)


def _rope(x, cos, sin):
    lane = _lane_iota(x.shape)
    half = MLA_ROPE // 2
    swapped = jnp.where(lane < half, pltpu.roll(x, HEAD_DIM - half, 1), pltpu.roll(x, half, 1))
    sign = jnp.where(lane < half, -1.0, 1.0)
    return x * cos + swapped * (sin * sign)


def _rope_t(dy, cos, sin):
    lane = _lane_iota(dy.shape)
    half = MLA_ROPE // 2
    t = dy * sin
    swapped = jnp.where(lane < half, pltpu.roll(t, HEAD_DIM - half, 1), pltpu.roll(t, half, 1))
    sign = jnp.where(lane < half, 1.0, -1.0)
    return dy * cos + swapped * sign


def rms_fwd(name, x, gain, n, tr, width, xcol=0, nh=1, out_dtype=BF16, out_cols=None):
    r = x.shape[0]
    tr = r if nh > 1 else tr
    out_cols = width * nh if out_cols is None else out_cols
    xspec = _rows(tr, width, (lambda j: xcol + j) if nh > 1 else xcol)
    ospec = _rows(tr, width, 'j' if nh > 1 else 0)
    return blockk(name, lambda xb, g: _rms(xb.astype(F32), g, n), (r // tr, nh),
                  [(x, xspec), (gain, _whole(gain.shape))], [(_sds((r, out_cols), out_dtype), ospec, None)])[0]


def rms_bwd(name, x, gain, dy, n, tr, width, xcol=0, nh=1, dycol=0, resid=None, out_dtype=F32, also_bf16=False):
    r = x.shape[0]
    tr = r if nh > 1 else tr
    xspec = _rows(tr, width, (lambda j: xcol + j) if nh > 1 else xcol)
    dyspec = _rows(tr, width, (lambda j: dycol + j) if nh > 1 else dycol)
    ospec = _rows(tr, width, 'j' if nh > 1 else 0)

    def fn(xb, g, dyb, *rest):
        _, vjp = jax.vjp(lambda a, b: _rms(a, b, n), xb.astype(F32), g)
        dx, dg = vjp(dyb.astype(F32))
        if rest:
            dx = dx + rest[0].astype(F32)
        return ((dx,) if also_bf16 else ()) + (dx, dg)

    ins = [(x, xspec), (gain, _whole(gain.shape)), (dy, dyspec)]
    if resid is not None:
        ins.append((resid, ospec))
    outs = [(_sds((r, width * nh), out_dtype), ospec, None), (_sds(gain.shape, F32), _whole(gain.shape), 'all')]
    if also_bf16:
        outs.insert(0, (_sds((r, width * nh), BF16), ospec, None))
    return blockk(name, fn, (r // tr, nh), ins, outs)


def _nt(a, b):
    return lax.dot_general(a, b, (((1,), (1,)), ((), ())), preferred_element_type=F32)


def _tn(a, b):
    return lax.dot_general(a, b, (((0,), (0,)), ((), ())), preferred_element_type=F32)


def _nn(a, b):
    return lax.dot_general(a, b, (((1,), (0,)), ((), ())), preferred_element_type=F32)


ATTN_BLOCK = 512
ATTN_HEADS = 2
ATTN_GRID_HEADS = N_HEADS // ATTN_HEADS


def _head_cols(off):
    assert off % ATTN_HEADS == 0
    return off // ATTN_HEADS


def _hd(ref, hh):
    return ref[:, hh * HEAD_DIM:(hh + 1) * HEAD_DIM]


def _attn_specs(tq, qoff, koff, voff, extra, bias, q2off):
    w = HEAD_DIM * ATTN_HEADS
    qc, kc, vc, q2c = (_head_cols(o) for o in (qoff, koff, voff, q2off))
    specs = [pl.BlockSpec((tq, w), lambda h, i, j: (i, qc + h)),
             pl.BlockSpec((tq, w), lambda h, i, j: (jnp.minimum(i, j), kc + h)),
             pl.BlockSpec((tq, w), lambda h, i, j: (jnp.minimum(i, j), vc + h))]
    if extra:
        specs += [pl.BlockSpec((tq, w), lambda h, i, j: (i, q2c + h)),
                  pl.BlockSpec((tq, HEAD_DIM), lambda h, i, j: (jnp.minimum(i, j), 0))]
    if bias:
        specs += [pl.BlockSpec((ATTN_HEADS, tq, 1), lambda h, i, j: (h, i, 0)),
                  pl.BlockSpec((ATTN_HEADS, 1, tq), lambda h, i, j: (h, 0, jnp.minimum(i, j)))]
    return specs


def _scores(q, k, q2, k2, cc, cr, scale, tq, diagonal):
    s = _nt(q.astype(BF16), k.astype(BF16))
    if q2 is not None:
        s = s + _nt(q2.astype(BF16), k2.astype(BF16))
    s = s * scale
    if cc is not None:
        s = s + (cc - cr)
    if not diagonal:
        return s, None
    return s, lax.broadcasted_iota(jnp.int32, (tq, tq), 1) <= lax.broadcasted_iota(jnp.int32, (tq, tq), 0)


def _on_blocks(qi, kj, step):
    @pl.when(kj < qi)
    def _():
        step(False)

    @pl.when(kj == qi)
    def _():
        step(True)


def attn_fwd(name, q, k, v, scale, *, qoff=0, koff=0, voff=0, q2=None, k2=None, q2off=0, cum=None, tq=ATTN_BLOCK,
             exact_o=False):
    s_len = q.shape[0]
    nq = s_len // tq
    extra, bias = q2 is not None, cum is not None
    n_in = 3 + 2 * extra + 2 * bias

    def body(*refs):
        q_ref, k_ref, v_ref = refs[:3]
        p = 3
        q2_ref = k2_ref = cc_ref = cr_ref = None
        if extra:
            q2_ref, k2_ref = refs[p:p + 2]
            p += 2
        if bias:
            cc_ref, cr_ref = refs[p:p + 2]
            p += 2
        o_ref, lse_ref, m_s, l_s, acc_s = refs[p:]
        qi, kj = pl.program_id(1), pl.program_id(2)

        @pl.when(kj == 0)
        def _():
            m_s[...] = jnp.full_like(m_s, -jnp.inf)
            l_s[...] = jnp.zeros_like(l_s)
            acc_s[...] = jnp.zeros_like(acc_s)

        def step(diagonal):
            for hh in range(ATTN_HEADS):
                s, allowed = _scores(_hd(q_ref, hh), _hd(k_ref, hh), _hd(q2_ref, hh) if extra else None,
                                     k2_ref[...] if extra else None, cc_ref[hh] if bias else None,
                                     cr_ref[hh] if bias else None, scale, tq, diagonal)
                if diagonal:
                    s = jnp.where(allowed, s, -jnp.inf)
                m_old = m_s[hh]
                m_new = jnp.maximum(m_old, jnp.max(s, axis=-1, keepdims=True))
                alpha = jnp.exp(m_old - m_new)
                pr = jnp.exp(s - m_new)
                l_s[hh] = alpha * l_s[hh] + jnp.sum(pr, axis=-1, keepdims=True)
                vb = _hd(v_ref, hh).astype(BF16)
                pv = _nn(pr.astype(BF16), vb)
                if exact_o:
                    pv = pv + _nn((pr - pr.astype(BF16).astype(F32)).astype(BF16), vb)
                acc_s[hh] = alpha * acc_s[hh] + pv
                m_s[hh] = m_new

        _on_blocks(qi, kj, step)

        @pl.when(kj == qi)
        def _():
            for hh in range(ATTN_HEADS):
                o_ref[:, hh * HEAD_DIM:(hh + 1) * HEAD_DIM] = (acc_s[hh] / l_s[hh]).astype(o_ref.dtype)
                lse_ref[hh] = m_s[hh] + jnp.log(l_s[hh])

    ins = [q, k, v] + ([q2, k2] if extra else []) + (list(cum) if bias else [])
    d, w = HEAD_DIM, HEAD_DIM * ATTN_HEADS
    return pl.pallas_call(
        body, grid=(ATTN_GRID_HEADS, nq, nq), in_specs=_attn_specs(tq, qoff, koff, voff, extra, bias, q2off),
        out_specs=[pl.BlockSpec((tq, w), lambda h, i, j: (i, h)),
                   pl.BlockSpec((ATTN_HEADS, tq, 1), lambda h, i, j: (h, i, 0))],
        out_shape=[_sds((s_len, N_HEADS * d), F32 if exact_o else BF16), _sds((N_HEADS, s_len, 1), F32)],
        scratch_shapes=[pltpu.VMEM((ATTN_HEADS, tq, 1), F32), pltpu.VMEM((ATTN_HEADS, tq, 1), F32),
                        pltpu.VMEM((ATTN_HEADS, tq, d), F32)],
        name=name, compiler_params=_cparams(3))(*ins)


def attn_bwd(name, q, k, v, o, do, lse, scale, *, qoff=0, koff=0, voff=0, q2=None, k2=None, q2off=0, cum=None,
             tq=ATTN_BLOCK):
    s_len = q.shape[0]
    nq = s_len // tq
    extra, bias = q2 is not None, cum is not None
    d = HEAD_DIM
    n_in = 6 + 2 * extra + 2 * bias

    def body(*refs):
        q_ref, k_ref, v_ref = refs[:3]
        p = 3
        q2_ref = k2_ref = cc_ref = cr_ref = None
        if extra:
            q2_ref, k2_ref = refs[p:p + 2]
            p += 2
        if bias:
            cc_ref, cr_ref = refs[p:p + 2]
            p += 2
        o_ref, do_ref, lse_ref = refs[p:p + 3]
        p += 3
        dq_ref, dk_ref, dv_ref = refs[p:p + 3]
        p += 3
        dq2_ref = dk2_ref = dcs_ref = None
        if extra:
            dq2_ref, dk2_ref = refs[p:p + 2]
            p += 2
        if bias:
            dcs_ref = refs[p]
            p += 1
        dq_s, delta_s = refs[p:p + 2]
        dq2_s = refs[p + 2] if extra else None
        h, qi, kj = pl.program_id(0), pl.program_id(1), pl.program_id(2)

        @pl.when(jnp.logical_and(qi == 0, kj == 0))
        def _():
            dk_ref[...] = jnp.zeros_like(dk_ref)
            dv_ref[...] = jnp.zeros_like(dv_ref)

        if extra:
            @pl.when(jnp.logical_and(h == 0, jnp.logical_and(qi == 0, kj == 0)))
            def _():
                dk2_ref[...] = jnp.zeros_like(dk2_ref)

        @pl.when(kj == 0)
        def _():
            dq_s[...] = jnp.zeros_like(dq_s)
            if extra:
                dq2_s[...] = jnp.zeros_like(dq2_s)
            for hh in range(ATTN_HEADS):
                delta_s[hh] = jnp.sum(_hd(do_ref, hh).astype(F32) * _hd(o_ref, hh).astype(F32), axis=-1, keepdims=True)

        if bias:
            @pl.when(kj > qi)
            def _():
                dcs_ref[...] = jnp.zeros_like(dcs_ref)

        def step(diagonal):
            ks = pl.ds(pl.multiple_of(kj * tq, tq), tq)
            for hh in range(ATTN_HEADS):
                cols = slice(hh * HEAD_DIM, (hh + 1) * HEAD_DIM)
                qh, kh = _hd(q_ref, hh).astype(BF16), _hd(k_ref, hh).astype(BF16)
                q2h = _hd(q2_ref, hh).astype(BF16) if extra else None
                k2h = k2_ref[...].astype(BF16) if extra else None
                s, allowed = _scores(qh, kh, q2h, k2h, cc_ref[hh] if bias else None, cr_ref[hh] if bias else None,
                                     scale, tq, diagonal)
                pr = jnp.exp(s - lse_ref[hh])
                if diagonal:
                    pr = jnp.where(allowed, pr, 0.0)
                dob = _hd(do_ref, hh).astype(BF16)
                dp = _nt(dob, _hd(v_ref, hh).astype(BF16))
                ds = pr * (dp - delta_s[hh])
                dsb = (ds * scale).astype(BF16)
                dq_s[hh] += _nn(dsb, kh)
                dk_ref[ks, cols] += _tn(dsb, qh)
                dv_ref[ks, cols] += _tn(pr.astype(BF16), dob)
                if extra:
                    dq2_s[hh] += _nn(dsb, k2h)
                    dk2_ref[ks, :] += _tn(dsb, q2h)
                if bias:
                    dcs_ref[hh] = jnp.sum(ds, axis=0, keepdims=True)

        _on_blocks(qi, kj, step)

        @pl.when(kj == qi)
        def _():
            for hh in range(ATTN_HEADS):
                cols = slice(hh * HEAD_DIM, (hh + 1) * HEAD_DIM)
                dq_ref[:, cols] = dq_s[hh]
                if extra:
                    dq2_ref[:, cols] = dq2_s[hh]

    w = HEAD_DIM * ATTN_HEADS
    ins = [q, k, v] + ([q2, k2] if extra else []) + (list(cum) if bias else []) + [o, do, lse]
    in_specs = _attn_specs(tq, qoff, koff, voff, extra, bias, q2off) + [
        pl.BlockSpec((tq, w), lambda h, i, j: (i, h)), pl.BlockSpec((tq, w), lambda h, i, j: (i, h)),
        pl.BlockSpec((ATTN_HEADS, tq, 1), lambda h, i, j: (h, i, 0))]
    full = _sds((s_len, N_HEADS * d), F32)
    out_shape = [full, full, full]
    out_specs = [pl.BlockSpec((tq, w), lambda h, i, j: (i, h)), pl.BlockSpec((s_len, w), lambda h, i, j: (0, h)),
                 pl.BlockSpec((s_len, w), lambda h, i, j: (0, h))]
    scratch = [pltpu.VMEM((ATTN_HEADS, tq, d), F32), pltpu.VMEM((ATTN_HEADS, tq, 1), F32)]
    if extra:
        out_shape += [full, _sds((s_len, d), F32)]
        out_specs += [pl.BlockSpec((tq, w), lambda h, i, j: (i, h)), pl.BlockSpec((s_len, d), lambda h, i, j: (0, 0))]
        scratch.append(pltpu.VMEM((ATTN_HEADS, tq, d), F32))
    if bias:
        out_shape.append(_sds((N_HEADS, nq, 1, s_len), F32))
        out_specs.append(pl.BlockSpec((ATTN_HEADS, None, 1, tq), lambda h, i, j: (h, i, 0, j)))
    return pl.pallas_call(body, grid=(ATTN_GRID_HEADS, nq, nq), in_specs=in_specs, out_specs=out_specs,
                          out_shape=out_shape, scratch_shapes=scratch, name=name, compiler_params=_cparams(3))(*ins)


def _sb_terms(q, k, scale, tq, diagonal):
    z = _nt(q.astype(BF16), k.astype(BF16)) * scale
    lg = jnp.log(1.0 + jnp.exp(-jnp.abs(z)))
    log_keep = -(jnp.maximum(z, 0.0) + lg)
    log_beta = jnp.minimum(z, 0.0) - lg
    if not diagonal:
        return None, log_keep, log_beta
    strict = lax.broadcasted_iota(jnp.int32, (tq, tq), 1) < lax.broadcasted_iota(jnp.int32, (tq, tq), 0)
    return strict, jnp.where(strict, log_keep, 0.0), log_beta


def _tri(tq, pred):
    a = lax.broadcasted_iota(jnp.int32, (tq, tq), 0)
    b = lax.broadcasted_iota(jnp.int32, (tq, tq), 1)
    return jnp.where(pred(a, b), 1.0, 0.0).astype(BF16)


SUM_CHUNK = 256


def _lane_sums(x, later):
    n = x.shape[1]
    chunk = min(SUM_CHUNK, n)
    tri = _tri(chunk, (lambda m, j: m > j) if later else (lambda m, j: m < j))
    order = range(n // chunk - 1, -1, -1) if later else range(n // chunk)
    pieces, carry = [None] * (n // chunk), None
    for cidx in order:
        xc = x[:, cidx * chunk:(cidx + 1) * chunk]
        hi = xc.astype(BF16)
        local = _nn(hi, tri) + _nn((xc - hi.astype(F32)).astype(BF16), tri)
        total = jnp.sum(xc, axis=-1, keepdims=True)
        pieces[cidx] = local if carry is None else local + carry
        carry = total if carry is None else carry + total
    return jnp.concatenate(pieces, axis=1), carry


def sb_fwd(name, qn, kn, qkv, scale, voff, tq=ATTN_BLOCK):
    s_len = qn.shape[0]
    nq = s_len // tq
    d = HEAD_DIM

    def body(q_ref, k_ref, v_ref, o_ref, car_ref, ca_s, acc_s):
        qi, kj = pl.program_id(1), pl.program_id(2)

        @pl.when(kj == 0)
        def _():
            ca_s[...] = jnp.zeros_like(ca_s)
            acc_s[...] = jnp.zeros_like(acc_s)

        def step(diagonal):
            for hh in range(ATTN_HEADS):
                strict, log_keep, log_beta = _sb_terms(_hd(q_ref, hh), _hd(k_ref, hh), scale, tq, diagonal)
                ca = ca_s[hh]
                car_ref[hh] = ca
                after, total = _lane_sums(log_keep, True)
                a = jnp.exp(log_beta + (after + ca))
                if diagonal:
                    a = jnp.where(strict, a, 0.0)
                acc_s[hh] += _nn(a.astype(BF16), _hd(v_ref, hh).astype(BF16))
                ca_s[hh] = ca + total

        @pl.when(kj == 0)
        def _():
            step(True)

        @pl.when(jnp.logical_and(kj > 0, kj <= qi))
        def _():
            step(False)

        @pl.when(kj == qi)
        def _():
            for hh in range(ATTN_HEADS):
                o_ref[:, hh * HEAD_DIM:(hh + 1) * HEAD_DIM] = acc_s[hh].astype(o_ref.dtype)

    kblk = lambda i, j: jnp.maximum(i - j, 0)
    w, vc = HEAD_DIM * ATTN_HEADS, _head_cols(voff)
    return pl.pallas_call(
        body, grid=(ATTN_GRID_HEADS, nq, nq),
        in_specs=[pl.BlockSpec((tq, w), lambda h, i, j: (i, h)), pl.BlockSpec((tq, w), lambda h, i, j: (kblk(i, j), h)),
                  pl.BlockSpec((tq, w), lambda h, i, j: (kblk(i, j), vc + h))],
        out_specs=[pl.BlockSpec((tq, w), lambda h, i, j: (i, h)),
                   pl.BlockSpec((ATTN_HEADS, None, tq, 1), lambda h, i, j: (h, kblk(i, j), i, 0))],
        out_shape=[_sds((s_len, N_HEADS * d), BF16), _sds((N_HEADS, nq, s_len, 1), F32)],
        scratch_shapes=[pltpu.VMEM((ATTN_HEADS, tq, 1), F32), pltpu.VMEM((ATTN_HEADS, tq, d), F32)],
        name=name, compiler_params=_cparams(3))(qn, kn, qkv)


def sb_bwd(name, qn, kn, qkv, do, carries, scale, voff, tq=ATTN_BLOCK):
    s_len = qn.shape[0]
    nq = s_len // tq
    d = HEAD_DIM

    def body(q_ref, k_ref, v_ref, do_ref, car_ref, dq_ref, dk_ref, dv_ref, dq_s, cg_s):
        qi, kj = pl.program_id(1), pl.program_id(2)

        @pl.when(jnp.logical_and(qi == 0, kj == 0))
        def _():
            dk_ref[...] = jnp.zeros_like(dk_ref)
            dv_ref[...] = jnp.zeros_like(dv_ref)

        @pl.when(kj == 0)
        def _():
            dq_s[...] = jnp.zeros_like(dq_s)
            cg_s[...] = jnp.zeros_like(cg_s)

        def step(diagonal):
            ks = pl.ds(pl.multiple_of(kj * tq, tq), tq)
            for hh in range(ATTN_HEADS):
                cols = slice(hh * HEAD_DIM, (hh + 1) * HEAD_DIM)
                qh, kh = _hd(q_ref, hh).astype(BF16), _hd(k_ref, hh).astype(BF16)
                strict, log_keep, log_beta = _sb_terms(qh, kh, scale, tq, diagonal)
                after, _ = _lane_sums(log_keep, True)
                a = jnp.exp(log_beta + (after + car_ref[hh]))
                if diagonal:
                    a = jnp.where(strict, a, 0.0)
                dob = _hd(do_ref, hh).astype(BF16)
                g = a * _nt(dob, _hd(v_ref, hh).astype(BF16))
                cg = cg_s[hh]
                before, total = _lane_sums(g, False)
                big_g = before + cg
                cg_s[hh] = cg + total
                beta = jnp.exp(log_beta)
                dz = g * (1.0 - beta) - big_g * beta
                if diagonal:
                    dz = jnp.where(strict, dz, 0.0)
                dzb = (dz * scale).astype(BF16)
                dq_s[hh] += _nn(dzb, kh)
                dk_ref[ks, cols] += _tn(dzb, qh)
                dv_ref[ks, cols] += _tn(a.astype(BF16), dob)

        _on_blocks(qi, kj, step)

        @pl.when(kj == qi)
        def _():
            for hh in range(ATTN_HEADS):
                dq_ref[:, hh * HEAD_DIM:(hh + 1) * HEAD_DIM] = dq_s[hh]

    kblk = lambda i, j: jnp.minimum(i, j)
    full = _sds((s_len, N_HEADS * d), F32)
    w, vc = HEAD_DIM * ATTN_HEADS, _head_cols(voff)
    return pl.pallas_call(
        body, grid=(ATTN_GRID_HEADS, nq, nq),
        in_specs=[pl.BlockSpec((tq, w), lambda h, i, j: (i, h)), pl.BlockSpec((tq, w), lambda h, i, j: (kblk(i, j), h)),
                  pl.BlockSpec((tq, w), lambda h, i, j: (kblk(i, j), vc + h)),
                  pl.BlockSpec((tq, w), lambda h, i, j: (i, h)),
                  pl.BlockSpec((ATTN_HEADS, None, tq, 1), lambda h, i, j: (h, kblk(i, j), i, 0))],
        out_specs=[pl.BlockSpec((tq, w), lambda h, i, j: (i, h)), pl.BlockSpec((s_len, w), lambda h, i, j: (0, h)),
                   pl.BlockSpec((s_len, w), lambda h, i, j: (0, h))],
        out_shape=[full, full, full],
        scratch_shapes=[pltpu.VMEM((ATTN_HEADS, tq, d), F32), pltpu.VMEM((ATTN_HEADS, tq, 1), F32)],
        name=name, compiler_params=_cparams(3))(qn, kn, qkv, do, carries)


def _cumsum_rows(x, reverse):
    n = x.shape[0] // HEAD_DIM
    tri = _tri(HEAD_DIM, (lambda a, b: b >= a) if reverse else (lambda a, b: b <= a))
    pieces = [None] * n
    carry = jnp.zeros((1, HEAD_DIM), F32)
    order = range(n - 1, -1, -1) if reverse else range(n)
    for blk in order:
        xb = x[blk * HEAD_DIM:(blk + 1) * HEAD_DIM, :]
        x1 = xb.astype(BF16)
        r1 = xb - x1.astype(F32)
        x2 = r1.astype(BF16)
        x3 = (r1 - x2.astype(F32)).astype(BF16)
        c = _nn(tri, x1) + _nn(tri, x2) + _nn(tri, x3) + carry
        pieces[blk] = c
        carry = c[0:1, :] if reverse else c[HEAD_DIM - 1:HEAD_DIM, :]
    return jnp.concatenate(pieces, axis=0)


def _log_sigmoid(x):
    return jnp.minimum(x, 0.0) - jnp.log(1.0 + jnp.exp(-jnp.abs(x)))


def fox_gate_fwd(name, fl, bf):
    return blockk(name, lambda f, b: _cumsum_rows(_log_sigmoid(f + b), False), (1, 1),
                  [(fl, _whole(fl.shape)), (bf, _whole(bf.shape))], [(_sds(fl.shape, F32), _whole(fl.shape), None)])[0]


def fox_gate_bwd(name, fl, bf, dcum):
    def fn(f, b, dc):
        dlogf = _cumsum_rows(dc, True)
        dfl = dlogf * _sig(-(f + b))
        return dfl, jnp.sum(dfl, axis=0, keepdims=True)

    return blockk(name, fn, (1, 1), [(fl, _whole(fl.shape)), (bf, _whole(bf.shape)), (dcum, _whole(dcum.shape))],
                  [(_sds(fl.shape, F32), _whole(fl.shape), None), (_sds(bf.shape, F32), _whole(bf.shape), None)])


def rope_fwd(name, x, gain, cos, sin, tr, xcol, nh):
    r = x.shape[0]
    tr = r if nh > 1 else tr
    xspec = _rows(tr, HEAD_DIM, lambda j: xcol + j)
    tspec = _rows(tr, HEAD_DIM, 0)
    return blockk(name, lambda xb, g, c, s: _rope(_rms(xb.astype(F32), g, MLA_ROPE), c, s), (r // tr, nh),
                  [(x, xspec), (gain, _whole(gain.shape)), (cos, tspec), (sin, tspec)],
                  [(_sds((r, HEAD_DIM * nh), BF16), _rows(tr, HEAD_DIM, 'j'), None)])[0]


def rope_bwd(name, x, gain, cos, sin, dy, tr, xcol, nh):
    r = x.shape[0]
    tr = r if nh > 1 else tr
    xspec = _rows(tr, HEAD_DIM, lambda j: xcol + j)
    tspec = _rows(tr, HEAD_DIM, 0)
    ospec = _rows(tr, HEAD_DIM, 'j')

    def fn(xb, g, c, s, dyb):
        _, vjp = jax.vjp(lambda a, b: _rms(a, b, MLA_ROPE), xb.astype(F32), g)
        return vjp(_rope_t(dyb.astype(F32), c, s))

    return blockk(name, fn, (r // tr, nh),
                  [(x, xspec), (gain, _whole(gain.shape)), (cos, tspec), (sin, tspec), (dy, ospec)],
                  [(_sds((r, HEAD_DIM * nh), F32), ospec, None), (_sds(gain.shape, F32), _whole(gain.shape), 'all')])


def sgu_pre_fwd(name, pre, gain, tr):
    s_len, w2 = pre.shape
    w = w2 // 2
    return blockk(name, lambda pu, pv, g: (_gelu(pu.astype(F32)), _rms(_gelu(pv.astype(F32)), g, w)), (s_len // tr, 1),
                  [(pre, _rows(tr, w, 0)), (pre, _rows(tr, w, 1)), (gain, _whole(gain.shape))],
                  [(_sds((s_len, w), BF16), _rows(tr, w, 0), None), (_sds((s_len, w), BF16), _rows(tr, w, 0), None)])


def sgu_pre_bwd(name, pre, gain, du, dvn, tr):
    s_len, w2 = pre.shape
    w = w2 // 2

    def fn(pu, pv, g, dub, dvb):
        _, vjp_u = jax.vjp(_gelu, pu.astype(F32))
        _, vjp_v = jax.vjp(lambda a, b: _rms(_gelu(a), b, w), pv.astype(F32), g)
        dpv, dg = vjp_v(dvb.astype(F32))
        return jnp.stack([vjp_u(dub.astype(F32))[0], dpv]), dg

    spec = _rows(tr, w, 0)
    return blockk(name, fn, (s_len // tr, 1),
                  [(pre, spec), (pre, _rows(tr, w, 1)), (gain, _whole(gain.shape)), (du, spec), (dvn, spec)],
                  [(_sds((2, s_len, w), BF16), pl.BlockSpec((2, tr, w), lambda i, j: (0, i, 0)), None),
                   (_sds(gain.shape, F32), _whole(gain.shape), 'all')])


def _ws_masked(ws):
    t = ws.shape[0]
    a = lax.broadcasted_iota(jnp.int32, (t, t), 0)
    b = lax.broadcasted_iota(jnp.int32, (t, t), 1)
    return jnp.where(b <= a, ws, 0.0)


def sgu_mix_fwd(name, vn, u, ws, bs3):
    s_len, w = vn.shape
    t = ws.shape[1]

    n_groups = w // t

    def fn(vb, ub, wsb, bsb):
        pieces = []
        for g in range(n_groups):
            cols = slice(g * t, (g + 1) * t)
            mixed = _nn(_ws_masked(wsb[g]).astype(BF16), vb[:, cols].astype(BF16)) + bsb[g]
            pieces.append(ub[:, cols].astype(F32) * mixed)
        return jnp.concatenate(pieces, axis=1)

    blk = pl.BlockSpec((t, w), lambda i, j: (i, 0))
    return blockk(name, fn, (s_len // t, 1), [(vn, blk), (u, blk), (ws, _whole(ws.shape)), (bs3, _whole(bs3.shape))],
                  [(_sds((s_len, w), BF16), blk, None)])[0]


def sgu_mix_bwd(name, vn, u, ws, bs3, dprod):
    s_len, w = vn.shape
    t = ws.shape[1]

    n_groups = w // t

    def fn(vb, ub, wsb, bsb, dpb):
        dus, dvns, dwss, dbss = [], [], [], []
        for g in range(n_groups):
            cols = slice(g * t, (g + 1) * t)
            wm = _ws_masked(wsb[g]).astype(BF16)
            vb16 = vb[:, cols].astype(BF16)
            mixed = _nn(wm, vb16) + bsb[g]
            dp = dpb[:, cols].astype(F32)
            dus.append(dp * mixed)
            dm = dp * ub[:, cols].astype(F32)
            dmb = dm.astype(BF16)
            dvns.append(_tn(wm, dmb))
            dwss.append(_ws_masked(_nt(dmb, vb16)))
            dbss.append(jnp.sum(dm, axis=-1, keepdims=True))
        return jnp.concatenate(dus, axis=1), jnp.concatenate(dvns, axis=1), jnp.stack(dwss), jnp.stack(dbss)

    blk = pl.BlockSpec((t, w), lambda i, j: (j, 0))
    return blockk(name, fn, (1, s_len // t),
                  [(vn, blk), (u, blk), (ws, _whole(ws.shape)), (bs3, _whole(bs3.shape)), (dprod, blk)],
                  [(_sds((s_len, w), BF16), blk, None), (_sds((s_len, w), BF16), blk, None),
                   (_sds(ws.shape, F32), _whole(ws.shape), 'all'), (_sds(bs3.shape, F32), _whole(bs3.shape), 'all')])


def _shift_down(x, k):
    row = lax.broadcasted_iota(jnp.int32, x.shape, 0)
    return jnp.where(row >= k, pltpu.roll(x, k, 0), 0.0)


def _shift_up(x, k):
    n = x.shape[0]
    row = lax.broadcasted_iota(jnp.int32, x.shape, 0)
    return jnp.where(row < n - k, pltpu.roll(x, n - k, 0), 0.0)


def _conv(up, cw, cb):
    return cb + cw[0:1, :] * _shift_down(up, 2) + cw[1:2, :] * _shift_down(up, 1) + cw[2:3, :] * up


def _conv_specs(s_len, tc, nf):
    g = pl.BlockSpec((s_len, tc), lambda i, j: (0, j))
    v = pl.BlockSpec((s_len, tc), lambda i, j: (0, j + nf))
    wg = pl.BlockSpec((3, tc), lambda i, j: (0, j))
    wv = pl.BlockSpec((3, tc), lambda i, j: (0, j + nf))
    bg = pl.BlockSpec((1, tc), lambda i, j: (0, j))
    bv = pl.BlockSpec((1, tc), lambda i, j: (0, j + nf))
    return g, v, wg, wv, bg, bv


def conv_fwd(name, up, cw, cb, tc=256):
    s_len, f2 = up.shape
    f = f2 // 2
    nf = f // tc
    g, v, wg, wv, bg, bv = _conv_specs(s_len, tc, nf)

    def fn(ug, uv, cwg, cwv, cbg, cbv):
        yg = _conv(ug.astype(F32), cwg, cbg)
        yv = _conv(uv.astype(F32), cwv, cbv)
        return yg * _sig(yg) * yv

    return blockk(name, fn, (1, nf), [(up, g), (up, v), (cw, wg), (cw, wv), (cb, bg), (cb, bv)],
                  [(_sds((s_len, f), BF16), g, None)])[0]


def conv_bwd(name, up, cw, cb, dact, tc=128):
    s_len, f2 = up.shape
    f = f2 // 2
    nf = f // tc
    g, v, wg, wv, bg, bv = _conv_specs(s_len, tc, nf)

    def conv(u, cwh, cbh):
        u1, u2 = _shift_down(u, 1), _shift_down(u, 2)
        return cbh + cwh[0:1, :] * u2 + cwh[1:2, :] * u1 + cwh[2:3, :] * u, u1, u2

    def half(dy, u, u1, u2, cwh):
        dup = cwh[2:3, :] * dy + _shift_up(cwh[1:2, :] * dy + _shift_up(cwh[0:1, :] * dy, 1), 1)
        dcw = jnp.concatenate([jnp.sum(dy * u2, axis=0, keepdims=True), jnp.sum(dy * u1, axis=0, keepdims=True),
                               jnp.sum(dy * u, axis=0, keepdims=True)], axis=0)
        return dup, dcw, jnp.sum(dy, axis=0, keepdims=True)

    def fn(ug, uv, cwg, cwv, cbg, cbv, da):
        ug, uv, da = ug.astype(F32), uv.astype(F32), da.astype(F32)
        yg, ug1, ug2 = conv(ug, cwg, cbg)
        yv, uv1, uv2 = conv(uv, cwv, cbv)
        sg = _sig(yg)
        silu = yg * sg
        dyv = da * silu
        dyg = da * yv * (sg + silu * (1.0 - sg))
        dug, dcwg, dcbg = half(dyg, ug, ug1, ug2, cwg)
        duv, dcwv, dcbv = half(dyv, uv, uv1, uv2, cwv)
        return jnp.stack([dug, duv]), dcwg, dcwv, dcbg, dcbv

    both = pl.BlockSpec((2, s_len, tc), lambda i, j: (0, 0, j))
    return blockk(name, fn, (1, nf), [(up, g), (up, v), (cw, wg), (cw, wv), (cb, bg), (cb, bv), (dact, g)],
                  [(_sds((2, s_len, f), BF16), both, None),
                   (_sds((3, f), F32), wg, None), (_sds((3, f), F32), wg, None),
                   (_sds((1, f), F32), bg, None), (_sds((1, f), F32), bg, None)])


def loss_head(name, y, target, tr=256):
    s_len, d = y.shape

    def fn(yb, tb):
        e = yb - tb
        dy = e * (1.0 / d)
        return dy, dy, jnp.sum(e * e, axis=0, keepdims=True)

    spec = _rows(tr, d, 0)
    return blockk(name, fn, (s_len // tr, 1), [(y, spec), (target, spec)],
                  [(_sds((s_len, d), BF16), spec, None), (_sds((s_len, d), F32), spec, None),
                   (_sds((1, d), F32), _whole((1, d)), 'all')])


def _adam(w, g, m, v):
    m = ADAM_B1 * m + (1.0 - ADAM_B1) * g
    v = ADAM_B2 * v + (1.0 - ADAM_B2) * (g * g)
    m_hat = m / (1.0 - ADAM_B1 ** ADAM_STEP)
    v_hat = v / (1.0 - ADAM_B2 ** ADAM_STEP)
    delta = -ADAM_LR * (m_hat / (jnp.sqrt(v_hat) + ADAM_EPS) + ADAM_WD * w)
    return delta, m, v


def adam_sum(name, parts, w, m, v):
    n_layers, r, c = w.shape
    n_parts = parts[0].shape[0]
    assert len(parts) == n_layers
    tr = r
    for cand in (512, 256, 128, 64, 32, 16):
        if r % cand == 0 and cand * c * 4 <= 1024 * 1024:
            tr = cand
            break

    def body(*refs):
        part_refs = refs[:n_layers]
        w_ref, m_ref, v_ref, g_out, d_out, m_out, v_out = refs[n_layers:]
        layer = pl.program_id(0)
        for ll in range(n_layers):
            @pl.when(layer == ll)
            def _(ll=ll):
                g = part_refs[ll][0].astype(F32)
                for k in range(1, n_parts):
                    g = g + part_refs[ll][k].astype(F32)
                delta, m_new, v_new = _adam(w_ref[...], g, m_ref[...], v_ref[...])
                g_out[...] = g
                d_out[...] = delta
                m_out[...] = m_new
                v_out[...] = v_new

    spec = pl.BlockSpec((None, tr, c), lambda l, i: (l, i, 0))
    part_specs = [pl.BlockSpec((n_parts, tr, c), lambda l, i, ll=ll: (0, jnp.where(l == ll, i, 0), 0))
                  for ll in range(n_layers)]
    out = _sds((n_layers, r, c), F32)
    return pl.pallas_call(body, grid=(n_layers, r // tr), in_specs=part_specs + [spec] * 3, out_specs=[spec] * 4,
                          out_shape=[out] * 4, name=name, compiler_params=_cparams(2))(*parts, w, m, v)


def sum_parts(name, parts, tr=256):
    n_parts, r, c = parts.shape

    def fn(pb):
        g = pb[0].astype(F32)
        for k in range(1, n_parts):
            g = g + pb[k].astype(F32)
        return g

    return blockk(name, fn, (r // tr, 1), [(parts, pl.BlockSpec((n_parts, tr, c), lambda i, j: (0, i, 0)))],
                  [(_sds((r, c), F32), _rows(tr, c, 0), None)])[0]


def adam_flat(name, w, g, m, v, tr=256):
    r, c = w.shape
    spec = _rows(tr, c, 0)
    return blockk(name, lambda wb, gb, mb, vb: _adam(wb, gb, mb, vb), (r // tr, 1),
                  [(w, spec), (g, spec), (m, spec), (v, spec)], [(_sds((r, c), F32), spec, None)] * 3)


_ANY = pl.BlockSpec(memory_space=pl.ANY)


def _place():
    return lax.axis_index("x"), lax.axis_index("y"), lax.axis_index("c")


def _slot(px, py, pc):
    return 4 * px + 2 * py + pc


def all_gather(name, items):
    n = len(items)

    def body(*refs):
        xs, outs = refs[:n], refs[n:2 * n]
        send_sems, recv_sems, local_sems = refs[2 * n:]
        x, y, c = _place()
        me, sibling = (x, y, c), (x, y, 1 - c)
        chips = [(1 - x, y), (x, 1 - y), (1 - x, 1 - y)]

        def copy(t, k, block, to, src=None):
            dst = outs[t].at[_slot(*block)]
            return pltpu.make_async_remote_copy(src_ref=dst if src is None else src, dst_ref=dst,
                                                send_sem=send_sems.at[7 * t + k], recv_sem=recv_sems.at[7 * t + k],
                                                device_id=to, device_id_type=MESH)

        mine = [pltpu.make_async_copy(xs[t], outs[t].at[_slot(*me)], local_sems.at[t]) for t in range(n)]
        for cp in mine:
            cp.start()
        started = []
        for t in range(n):
            started.append(copy(t, 0, me, sibling, src=xs[t]))
            started += [copy(t, 1 + j, me, (*chip, c), src=xs[t]) for j, chip in enumerate(chips)]
        for cp in started:
            cp.start()
        for j, chip in enumerate(chips):
            for t in range(n):
                copy(t, 1 + j, (*chip, c), me).wait_recv()
                passed = copy(t, 4 + j, (*chip, c), sibling)
                passed.start()
                started.append(passed)
        for t in range(n):
            copy(t, 0, sibling, me).wait_recv()
            for j, chip in enumerate(chips):
                copy(t, 4 + j, (*chip, 1 - c), me).wait_recv()
        for cp in started:
            cp.wait_send()
        for cp in mine:
            cp.wait()

    return pl.pallas_call(
        body, in_specs=[_ANY] * n, out_specs=[_ANY] * n,
        out_shape=[_sds((N_DEV,) + a.shape, a.dtype) for a in items],
        scratch_shapes=[pltpu.SemaphoreType.DMA((7 * n,)), pltpu.SemaphoreType.DMA((7 * n,)), pltpu.SemaphoreType.DMA((n,))],
        name=name)(*items)


_HBM = pl.BlockSpec(memory_space=pltpu.HBM)
_SEM = pl.BlockSpec(memory_space=pltpu.SEMAPHORE)
_EFFECT = pltpu.SideEffectType.DATAFLOW_SIDE_EFFECTING


def _peer(k, x, y, c):
    return ((1 - x) if k & 4 else x, (1 - y) if k & 2 else y, (1 - c) if k & 1 else c)


PEERS_ALL = (1, 2, 3, 4, 5, 6, 7)
PEERS_CHIPWISE = (1, 2, 4, 6)


def _exchange_copies(mode, refs, send_sems, recv_sems, landing):
    x, y, c = _place()
    my_slot = _slot(x, y, c)
    copies = []
    if mode == "pass_on":
        for t, land in enumerate(refs):
            for j, chip in enumerate([(1 - x, y), (x, 1 - y), (1 - x, 1 - y)]):
                slot = _slot(*chip, (1 - c) if landing else c)
                copies.append(pltpu.make_async_remote_copy(
                    src_ref=land.at[slot], dst_ref=land.at[slot], send_sem=send_sems.at[3 * t + j],
                    recv_sem=recv_sems.at[3 * t + j], device_id=(x, y, 1 - c), device_id_type=MESH))
        return copies
    n = len(refs) // 2
    for t, (src, land) in enumerate(zip(refs[:n], refs[n:])):
        for k in (PEERS_CHIPWISE if mode == "gather_chipwise" else PEERS_ALL):
            peer = _peer(k, x, y, c)
            copies.append(pltpu.make_async_remote_copy(
                src_ref=src.at[_slot(*peer)] if mode == "scatter" else src,
                dst_ref=land.at[_slot(*peer) if landing else my_slot],
                send_sem=send_sems.at[7 * t + k - 1], recv_sem=recv_sems.at[7 * t + k - 1],
                device_id=peer, device_id_type=MESH))
    return copies


def exchange_start(name, mode, arrays):
    n = len(arrays)
    passing = mode == "pass_on"
    n_sems = (3 if passing else 7) * n
    lands = [] if passing else [pltpu.HBM(a.shape if mode == "scatter" else (N_DEV,) + a.shape, a.dtype) for a in arrays]

    def body(*refs):
        srcs, outs = refs[:n], refs[n + 2:]
        bufs = list(srcs) if passing else list(srcs) + list(outs[n:2 * n])
        for send in _exchange_copies(mode, bufs, refs[n], refs[n + 1], False):
            send.start()
        refs[-1][...] = jnp.zeros_like(refs[-1])

    arrays = [pltpu.with_memory_space_constraint(a, pltpu.HBM) for a in arrays]
    res = pl.pallas_call(
        body, name=name, in_specs=[_HBM] * n,
        out_specs=(_SEM, _SEM) + (_HBM,) * (n + len(lands)) + (pl.BlockSpec(memory_space=pltpu.VMEM),),
        out_shape=(pltpu.SemaphoreType.DMA((n_sems,)), pltpu.SemaphoreType.DMA((n_sems,)))
        + tuple(pltpu.HBM(a.shape, a.dtype) for a in arrays) + tuple(lands) + (_sds((8, HEAD_DIM), F32),),
        input_output_aliases={i: 2 + i for i in range(n)},
        compiler_params=pltpu.CompilerParams(has_side_effects=_EFFECT))(*arrays)
    return dict(name=name, mode=mode, sems=res[:2], bufs=res[2:-1]), res[-1]


def exchange_wait(handle, after):
    mode, bufs = handle['mode'], handle['bufs']
    nb = len(bufs)

    def body(*refs):
        for landed in _exchange_copies(mode, refs[:nb], refs[nb], refs[nb + 1], True):
            landed.wait_send()
            landed.wait_recv()

    res = pl.pallas_call(
        body, name=handle['name'].replace("start", "wait"), in_specs=[_HBM] * nb + [_SEM, _SEM, _ANY],
        out_specs=(_HBM,) * nb, out_shape=tuple(pltpu.HBM(a.shape, a.dtype) for a in bufs),
        input_output_aliases={i: i for i in range(nb)},
        compiler_params=pltpu.CompilerParams(has_side_effects=_EFFECT))(*bufs, *handle['sems'], after)
    if mode == "pass_on":
        return list(res)
    my_slot = _slot(*_place())
    filled = []
    for src, land in zip(res[:nb // 2], res[nb // 2:]):
        own = lax.dynamic_index_in_dim(src, my_slot, 0, keepdims=True) if mode == "scatter" else src[None]
        filled.append(lax.dynamic_update_slice(land, own, (my_slot,) + (0,) * (land.ndim - 1)))
    return filled


def _pad_lanes(a, width=HEAD_DIM):
    return jnp.pad(a, [(0, 0)] * (a.ndim - 1) + [(0, width - a.shape[-1])])


def _slabs(full, n):
    k = full.shape[0]
    return full.reshape(k, N_DEV, n).transpose(1, 0, 2)


def fox_fwd(a, w, p):
    qkv = mm_nn(a, w['qkv'], BF16, "fox_qkv")
    fl = mm_nn(a, w['f'], F32, "fox_flogit")
    bf = _pad_lanes(p['fox_b_f'])
    cum = fox_gate_fwd("fox_gate_fwd", fl, bf)
    cum_t = cum[:, :N_HEADS].T
    cums = (cum_t[:, :, None], cum_t[:, None, :])
    qn = rms_fwd("fox_qnorm", qkv, p['fox_q_gain'], HEAD_DIM, 512, HEAD_DIM, xcol=0, nh=N_HEADS)
    kn = rms_fwd("fox_knorm", qkv, p['fox_k_gain'], HEAD_DIM, 512, HEAD_DIM, xcol=N_HEADS, nh=N_HEADS)
    o, lse = attn_fwd("fox_attn_fwd", qn, kn, qkv, HEAD_DIM ** -0.5, voff=2 * N_HEADS, cum=cums, exact_o=True)
    return o, dict(a=a, qkv=qkv, fl=fl, bf=bf, cums=cums, qn=qn, kn=kn, o=o, lse=lse)


def fox_bwd(do, s, w, p):
    dqn, dkn, dv, dcs = attn_bwd("fox_attn_bwd", s['qn'], s['kn'], s['qkv'], s['o'], do, s['lse'], HEAD_DIM ** -0.5,
                                 voff=2 * N_HEADS, cum=s['cums'])
    dq, dgq = rms_bwd("fox_qnorm_bwd", s['qkv'], p['fox_q_gain'], dqn, HEAD_DIM, 512, HEAD_DIM, xcol=0, nh=N_HEADS,
                      out_dtype=BF16)
    dk, dgk = rms_bwd("fox_knorm_bwd", s['qkv'], p['fox_k_gain'], dkn, HEAD_DIM, 512, HEAD_DIM, xcol=N_HEADS,
                      nh=N_HEADS, out_dtype=BF16)
    dqkv = jnp.concatenate([dq, dk, dv.astype(BF16)], axis=1)
    dcum = _pad_lanes(-jnp.sum(dcs[:, :, 0, :], axis=1).T)
    dfl, dbf = fox_gate_bwd("fox_gate_bwd", s['fl'], s['bf'], dcum)
    da = mm_nt(dqkv, w['qkv'], F32, "fox_da_qkv")
    da = mm_nt(dfl, w['f'], F32, "fox_da_f", add=da)
    dw_qkv = mm_tn(s['a'], dqkv, 1, BF16, "fox_dw_qkv")[0]
    dw_f = mm_tn(s['a'], dfl, 1, BF16, "fox_dw_f")[0][:, :N_HEADS]
    dw = _slabs(jnp.concatenate([dw_qkv, dw_f], axis=1), 770)
    return da, dict(fox_w_in=dw), dict(fox_b_f=dbf[:, :N_HEADS], fox_q_gain=dgq, fox_k_gain=dgk)


def _rope_tables(positions):
    inv_freq = ROPE_THETA ** (-jnp.arange(0, MLA_ROPE, 2, dtype=F32) / MLA_ROPE)
    ang = positions.astype(F32)[:, None] * inv_freq
    cos, sin = jnp.cos(ang), jnp.sin(ang)
    return _pad_lanes(jnp.concatenate([cos, cos], axis=1)), _pad_lanes(jnp.concatenate([sin, sin], axis=1))


def mla_fwd(a, w, p):
    qg, kg = p['mla_q_gain'], p['mla_k_gain']
    gains = dict(qn=qg[:, :HEAD_DIM], qr=_pad_lanes(qg[:, HEAD_DIM:]), kn=kg[:, :HEAD_DIM], kr=_pad_lanes(kg[:, HEAD_DIM:]))
    cos, sin = _rope_tables(p['positions'])
    ccr = mm_nn(a, w['in'], F32, "mla_in")
    cqn = rms_fwd("mla_cq_norm", ccr, w['q_a_gain'], 512, 256, 512, xcol=0)
    ckvn = rms_fwd("mla_ckv_norm", ccr, w['kv_a_gain'], 512, 256, 512, xcol=1)
    qf = mm_nn(cqn, w['q_b'], F32, "mla_q_b")
    kvf = mm_nn(ckvn, w['kv_b'], F32, "mla_kv_b")
    q_nope = rms_fwd("mla_qnope_norm", qf, gains['qn'], HEAD_DIM, 512, HEAD_DIM, xcol=0, nh=N_HEADS)
    k_nope = rms_fwd("mla_knope_norm", kvf, gains['kn'], HEAD_DIM, 512, HEAD_DIM, xcol=0, nh=N_HEADS)
    q_rope = rope_fwd("mla_qrope", qf, gains['qr'], cos, sin, 512, N_HEADS, N_HEADS)
    k_rope = rope_fwd("mla_krope", ccr, gains['kr'], cos, sin, 512, 8, 1)
    scale = (HEAD_DIM + MLA_ROPE) ** -0.5
    o, lse = attn_fwd("mla_attn_fwd", q_nope, k_nope, kvf, scale, voff=N_HEADS, q2=q_rope, k2=k_rope)
    return o, dict(a=a, gains=gains, cos=cos, sin=sin, ccr=ccr, cqn=cqn, ckvn=ckvn, qf=qf, kvf=kvf, q_nope=q_nope,
                   k_nope=k_nope, q_rope=q_rope, k_rope=k_rope, o=o, lse=lse, scale=scale)


def mla_bwd(do, s, w, p):
    g = s['gains']
    dqn, dkn, dv, dq2, dk2 = attn_bwd("mla_attn_bwd", s['q_nope'], s['k_nope'], s['kvf'], s['o'], do, s['lse'], s['scale'],
                                      voff=N_HEADS, q2=s['q_rope'], k2=s['k_rope'])
    dqf_n, dg_qn = rms_bwd("mla_qnope_bwd", s['qf'], g['qn'], dqn, HEAD_DIM, 512, HEAD_DIM, xcol=0, nh=N_HEADS)
    dkf, dg_kn = rms_bwd("mla_knope_bwd", s['kvf'], g['kn'], dkn, HEAD_DIM, 512, HEAD_DIM, xcol=0, nh=N_HEADS)
    dqf_r, dg_qr = rope_bwd("mla_qrope_bwd", s['qf'], g['qr'], s['cos'], s['sin'], dq2, 512, N_HEADS, N_HEADS)
    dkr, dg_kr = rope_bwd("mla_krope_bwd", s['ccr'], g['kr'], s['cos'], s['sin'], dk2, 512, 8, 1)
    dqf = jnp.concatenate([dqf_n, dqf_r], axis=1)
    dkvf = jnp.concatenate([dkf, dv], axis=1)
    dcqn = mm_nt(dqf, w['q_b'], F32, "mla_dcq")
    dckvn = mm_nt(dkvf, w['kv_b'], F32, "mla_dckv")
    dw_qb = mm_tn(s['cqn'], dqf, 1, BF16, "mla_dw_qb")[0]
    dw_kvb = mm_tn(s['ckvn'], dkvf, 1, BF16, "mla_dw_kvb")[0]
    dcq, dg_qa = rms_bwd("mla_cq_bwd", s['ccr'], w['q_a_gain'], dcqn, 512, 256, 512, xcol=0)
    dckv, dg_kva = rms_bwd("mla_ckv_bwd", s['ccr'], w['kv_a_gain'], dckvn, 512, 256, 512, xcol=1)
    dccr = jnp.concatenate([dcq, dckv, dkr], axis=1)
    da = mm_nt(dccr, w['in'], F32, "mla_da")
    dw_in = mm_tn(s['a'], dccr, 1, BF16, "mla_dw_in")[0][:, :1088].reshape(N_DEV, 256, 1088)
    hp = 2
    nope = dw_qb[:, :2048].reshape(512, N_DEV, hp, HEAD_DIM)
    rope = dw_qb[:, 2048:].reshape(512, N_DEV, hp, HEAD_DIM)[..., :MLA_ROPE]
    dw_qb_s = jnp.concatenate([nope, rope], axis=-1).transpose(1, 0, 2, 3).reshape(N_DEV, 512, hp * 192)
    kk = dw_kvb[:, :2048].reshape(512, N_DEV, hp, HEAD_DIM)
    vv = dw_kvb[:, 2048:].reshape(512, N_DEV, hp, HEAD_DIM)
    dw_kvb_s = jnp.concatenate([kk, vv], axis=-1).transpose(1, 0, 2, 3).reshape(N_DEV, 512, hp * 256)
    small = dict(mla_q_a_gain=dg_qa, mla_kv_a_gain=dg_kva,
                 mla_q_gain=jnp.concatenate([dg_qn, dg_qr[:, :MLA_ROPE]], axis=1),
                 mla_k_gain=jnp.concatenate([dg_kn, dg_kr[:, :MLA_ROPE]], axis=1))
    return da, dict(mla_w_in=dw_in, mla_w_q_b=dw_qb_s, mla_w_kv_b=dw_kvb_s), small


def sb_fwd_layer(a, w, p):
    qkv = mm_nn(a, w['in'], BF16, "sb_qkv")
    qn = rms_fwd("sb_qnorm", qkv, p['sb_q_gain'], HEAD_DIM, 512, HEAD_DIM, xcol=0, nh=N_HEADS)
    kn = rms_fwd("sb_knorm", qkv, p['sb_k_gain'], HEAD_DIM, 512, HEAD_DIM, xcol=N_HEADS, nh=N_HEADS)
    o, carries = sb_fwd("sb_attn_fwd", qn, kn, qkv, HEAD_DIM ** -0.5, 2 * N_HEADS)
    return o, dict(a=a, qkv=qkv, qn=qn, kn=kn, carries=carries)


def sb_bwd_layer(do, s, w, p):
    dqn, dkn, dv = sb_bwd("sb_attn_bwd", s['qn'], s['kn'], s['qkv'], do, s['carries'], HEAD_DIM ** -0.5, 2 * N_HEADS)
    dq, dgq = rms_bwd("sb_qnorm_bwd", s['qkv'], p['sb_q_gain'], dqn, HEAD_DIM, 512, HEAD_DIM, xcol=0, nh=N_HEADS,
                      out_dtype=BF16)
    dk, dgk = rms_bwd("sb_knorm_bwd", s['qkv'], p['sb_k_gain'], dkn, HEAD_DIM, 512, HEAD_DIM, xcol=N_HEADS, nh=N_HEADS,
                      out_dtype=BF16)
    dqkv = jnp.concatenate([dq, dk, dv.astype(BF16)], axis=1)
    da = mm_nt(dqkv, w['in'], F32, "sb_da")
    dw = mm_tn(s['a'], dqkv, N_DEV, BF16, "sb_dw_in")
    return da, dict(sb_w_in=dw), dict(sb_q_gain=dgq, sb_k_gain=dgk)


def sgu_fwd(a, w, p):
    pre = mm_nn(a, w['in'], BF16, "sgu_in")
    u, vn = sgu_pre_fwd("sgu_pre_fwd", pre, w['v_gain'], 256)
    ws = p['sgu_w_s'][0]
    bs3 = p['sgu_b_s'][0][:, :, None]
    prod = sgu_mix_fwd("sgu_mix_fwd", vn, u, ws, bs3)
    return prod, dict(a=a, pre=pre, u=u, vn=vn, ws=ws, bs3=bs3)


def sgu_bwd(dprod, s, w, p):
    du, dvn, dws, dbs3 = sgu_mix_bwd("sgu_mix_bwd", s['vn'], s['u'], s['ws'], s['bs3'], dprod)
    dpre, dgv = sgu_pre_bwd("sgu_pre_bwd", s['pre'], w['v_gain'], du, dvn, 128)
    da = mm_nt(dpre, w['in'], F32, "sgu_da")
    dw = mm_tn(s['a'], dpre, N_DEV, BF16, "sgu_dw_in")
    return da, dict(sgu_w_in=dw), dict(sgu_v_gain=dgv, sgu_w_s=dws[None], sgu_b_s=dbs3[None, :, :, 0])


MIXERS = [("fox", fox_fwd, fox_bwd), ("mla", mla_fwd, mla_bwd), ("sb", sb_fwd_layer, sb_bwd_layer),
          ("sgu", sgu_fwd, sgu_bwd)]


def _pack_rows(arrays, row_mult=256):
    flat = jnp.concatenate([a.reshape(-1).astype(F32) for a in arrays])
    per = row_mult * HEAD_DIM
    total = -(-flat.shape[0] // per) * per
    return jnp.pad(flat, (0, total - flat.shape[0])).reshape(total // HEAD_DIM, HEAD_DIM)


def _unpack_rows(packed, shapes):
    flat = packed.reshape(-1)
    out, off = [], 0
    for shp in shapes:
        n = math.prod(shp)
        out.append(flat[off:off + n].reshape(shp))
        off += n
    return out


def _mixer_shards(i, p):
    name = MIXERS[i][0]
    bf = lambda a: a.astype(BF16)
    items = {'out': bf(p[name + '_w_out'][0])}
    if name == "fox":
        items['in'] = bf(p['fox_w_in'][0])
        items['small'] = _pack_rows([p[n] for n in SMALL_SHARDED], 8)
    elif name == "mla":
        items.update({'in': bf(p['mla_w_in'][0]), 'q_b': bf(p['mla_w_q_b'][0]), 'kv_b': bf(p['mla_w_kv_b'][0])})
    else:
        items['in'] = bf(p[name + '_w_in'][0])
    return items


def _ffn_shards(i, p):
    return {'up': p['ffn_w_up'][i].astype(BF16), 'down': p['ffn_w_down'][i].astype(BF16)}


def _assemble_ffn(got):
    return {'up': got['up'], 'down': got['down'].reshape(1, D_FF, -1)}


def _assemble_mixer(i, got, p):
    name = MIXERS[i][0]
    w = {'out': got['out'].reshape(1, -1, got['out'].shape[-1])} if 'out' in got else {}
    small = None
    if name == "fox":
        if 'in' in got:
            full = got['in'].transpose(1, 0, 2).reshape(got['in'].shape[1], -1)
            w['qkv'] = full[None, :, :3 * N_HEADS * HEAD_DIM]
            w['f'] = _pad_lanes(full[:, 3 * N_HEADS * HEAD_DIM:])[None]
        if 'small' in got:
            small_shapes = [p[n].shape for n in SMALL_SHARDED]
            parts = [_unpack_rows(got['small'][d], small_shapes) for d in range(N_DEV)]
            small = {n: jnp.concatenate([parts[d][k] for d in range(N_DEV)], axis=ax)
                     for k, (n, ax) in enumerate(SMALL_SHARDED.items())}
    elif name == "mla":
        w['in'] = _pad_lanes(got['in'].reshape(-1, 1088), 1152)[None]
        hp = 2
        qb = got['q_b'].reshape(N_DEV, 512, hp, 192).transpose(1, 0, 2, 3)
        nope = qb[..., :HEAD_DIM].reshape(512, -1)
        rope = _pad_lanes(qb[..., HEAD_DIM:]).reshape(512, -1)
        w['q_b'] = jnp.concatenate([nope, rope], axis=1)[None]
        kvb = got['kv_b'].reshape(N_DEV, 512, hp, 256).transpose(1, 0, 2, 3)
        w['kv_b'] = jnp.concatenate([kvb[..., :HEAD_DIM].reshape(512, -1), kvb[..., HEAD_DIM:].reshape(512, -1)], axis=1)[None]
    else:
        w['in'] = got['in']
    return w, small


def _train_step(p):
    x, target = p['x'][0], p['loss_target'][0]
    p = dict(p, positions=p['positions'][0])
    xi, yi, ci = _place()
    my_slot = _slot(xi, yi, ci)

    shards0 = _mixer_shards(0, p)
    got0 = {'in': all_gather("gather_first", [shards0.pop('in')])[0]}
    pending, order_token = {}, jnp.zeros((1, 1), F32)
    for i in range(DEPTH):
        for kind, shards in (("mix", _mixer_shards(i, p) if i else shards0), ("ffn", _ffn_shards(i, p))):
            handle, token = exchange_start(f"xstart_ag_{kind}{i}", "gather_chipwise", list(shards.values()))
            pending[kind, i] = (handle, list(shards))
            order_token = order_token + token[0:1, 0:1]

    def pass_on(kind, i, after):
        handle, keys = pending[kind, i]
        handle, token = exchange_start(f"xstart_pass_{kind}{i}", "pass_on", exchange_wait(handle, after))
        pending[kind, i] = (handle, keys)
        return token[0:1, 0:1]

    def gathered(kind, i, after):
        handle, keys = pending[kind, i]
        return dict(zip(keys, exchange_wait(handle, after)))

    h = x
    saved, weights = [], []
    small_full = None
    for i in range(DEPTH):
        if i == 0:
            w, _ = _assemble_mixer(0, got0, p)
            gain = p['mix_norm'][0:1] + order_token
        else:
            w, _ = _assemble_mixer(i, gathered("mix", i, h), p)
            gain = p['mix_norm'][i:i + 1] + pass_on("ffn", i, h)
            w.update(q_a_gain=small_full['mla_q_a_gain'], kv_a_gain=small_full['mla_kv_a_gain'],
                     v_gain=small_full['sgu_v_gain'])
        a = rms_fwd(f"mix_norm_{i}", h, gain, h.shape[1], 512, h.shape[1])
        mixed, s_mix = MIXERS[i][1](a, w, p)
        if i == 0:
            pass_on("mix", 0, mixed)
            pass_on("ffn", 0, mixed)
            rest, small_full = _assemble_mixer(0, gathered("mix", 0, mixed), p)
            w.update(rest)
            conv_w = small_full['ffn_conv_w']
        w.update(_assemble_ffn(gathered("ffn", i, mixed)))
        weights.append(w)
        h1 = mm_nn(mixed, w['out'], F32, f"mix_out_{i}", add=h)
        b = rms_fwd(f"ffn_norm_{i}", h1, p['ffn_norm'][i:i + 1], h.shape[1], 512, h.shape[1])
        up = mm_nn(b, w['up'], BF16, f"ffn_up_{i}")
        conv_b = p['ffn_conv_b'][i:i + 1]
        if i + 1 < DEPTH:
            conv_b = conv_b + pass_on("mix", i + 1, up)
        act = conv_fwd(f"ffn_conv_{i}", up, conv_w[i], conv_b)
        h2 = mm_nn(act, w['down'], F32, f"ffn_down_{i}", add=h1)
        saved.append(dict(h=h, mixed=mixed, s_mix=s_mix, h1=h1, b=b, up=up, act=act))
        h = h2

    dh16, dh, sq = loss_head("loss_head", h, target)
    loss = lax.psum(0.5 * jnp.sum(sq) / h.shape[1], ("x", "y", "c"))

    scatters = []
    layered = ('mix_norm', 'ffn_norm', 'ffn_conv_w', 'ffn_conv_b')
    early_names = [n for n in SMALL if not n.startswith('fox_')]
    late_names = [n for n in SMALL if n in layered or n.startswith('fox_')]
    small_g = {n: [None] * p[n].shape[0] for n in layered}
    early_token = jnp.zeros((1, 1), F32)
    for i in reversed(range(DEPTH)):
        w, s = weights[i], saved[i]
        name = MIXERS[i][0]
        dact = mm_nt(dh16, w['down'], BF16, f"ffn_dact_{i}")
        dw_down = mm_tn(s['act'], dh16, 1, BF16, f"ffn_dw_down_{i}").reshape(N_DEV, D_FF // N_DEV, -1)
        dup, dcwg, dcwv, dcbg, dcbv = conv_bwd(f"ffn_conv_bwd_{i}", s['up'], conv_w[i],
                                               p['ffn_conv_b'][i:i + 1] + early_token, dact)
        dw_up = mm_tn(s['b'], dup, N_DEV, BF16, f"ffn_dw_up_{i}")
        handle, token = exchange_start(f"xstart_rs_ffn{i}", "scatter", [dw_up, dw_down])
        scatters.append((handle, ['ffn_w_up', 'ffn_w_down']))
        db = mm_nt(dup, w['up'], F32, f"ffn_db_{i}")
        dh1_16, dh1, dg_ffn = rms_bwd(f"ffn_norm_bwd_{i}", s['h1'], p['ffn_norm'][i:i + 1] + token[0:1, 0:1], db,
                                      h.shape[1], 256, h.shape[1], resid=dh, also_bf16=True)
        dw_out = mm_tn(s['mixed'], dh1_16, 1, BF16, f"mix_dw_out_{i}").reshape(N_DEV, -1, h.shape[1])
        if i == 0:
            handle, token = exchange_start("xstart_rs_out0", "scatter", [dw_out])
            scatters.append((handle, [name + '_w_out']))
            dh1_16 = dh1_16 + token[0:1, 0:1].astype(BF16)
        dmix = mm_nt(dh1_16, w['out'], BF16, f"mix_dout_{i}")
        da, big_i, small_i = MIXERS[i][2](dmix, s['s_mix'], w, p)
        if i > 0:
            big_i[name + '_w_out'] = dw_out
        handle, token = exchange_start(f"xstart_rs_mix{i}", "scatter", list(big_i.values()))
        scatters.append((handle, list(big_i)))
        dh16, dh, dg_mix = rms_bwd(f"mix_norm_bwd_{i}", s['h'], p['mix_norm'][i:i + 1] + token[0:1, 0:1], da, h.shape[1],
                                   256, h.shape[1], resid=dh1, also_bf16=True)
        for k, v in small_i.items():
            small_g[k] = v
        small_g['mix_norm'][i], small_g['ffn_norm'][i] = dg_mix, dg_ffn
        small_g['ffn_conv_w'][i] = jnp.concatenate([dcwg, dcwv], axis=1)[None]
        small_g['ffn_conv_b'][i] = jnp.concatenate([dcbg, dcbv], axis=1)
        if i == 1:
            early = [jnp.concatenate(small_g[n][1:], axis=0) if n in layered else small_g[n] for n in early_names]
            early_handle, token = exchange_start("xstart_small_early", "gather", [_pack_rows(early)])
            early_token = token[0:1, 0:1]
    late = [small_g[n][0] if n in layered else small_g[n] for n in late_names]
    grad_x = dh[None]

    small_handle, small_token = exchange_start("xstart_small", "gather", [_pack_rows(late)])
    big = {}
    for handle, names in scatters[:-1]:
        for n, landed in zip(names, exchange_wait(handle, small_token)):
            big.setdefault(n, []).insert(0, landed)
    grads, deltas, new_m, new_v = {}, {}, {}, {}
    last_handle, last_names = scatters[-1]
    for n in BIG:
        if n not in last_names:
            grads[n], deltas[n], new_m[n], new_v[n] = adam_sum("adam_" + n, big[n], p[n], p['m_' + n], p['v_' + n])
    for n, landed in zip(last_names, exchange_wait(last_handle, new_v['ffn_w_up'])):
        grads[n], deltas[n], new_m[n], new_v[n] = adam_sum("adam_" + n, [landed], p[n], p['m_' + n], p['v_' + n])
    early_sum = _unpack_rows(sum_parts("sum_small_early", exchange_wait(early_handle, small_token)[0]),
                             [tuple(a.shape) for a in early])
    late_sum = _unpack_rows(sum_parts("sum_small_late", exchange_wait(small_handle, new_v[last_names[0]])[0]),
                            [tuple(a.shape) for a in late])
    summed = dict(zip(early_names, early_sum))
    for n, g in zip(late_names, late_sum):
        summed[n] = jnp.concatenate([g, summed[n]], axis=0) if n in layered else g
    mine = []
    for n in SMALL:
        g = summed[n]
        if n in SMALL_SHARDED:
            ax = SMALL_SHARDED[n]
            g = lax.dynamic_slice_in_dim(g, my_slot * p[n].shape[ax], p[n].shape[ax], axis=ax)
        mine.append(g)
    shapes = [p[n].shape for n in SMALL]
    packed = [_pack_rows(arrs) for arrs in ([p[n] for n in SMALL], mine, [p['m_' + n] for n in SMALL], [p['v_' + n] for n in SMALL])]
    d_s, m_s, v_s = adam_flat("adam_small", *packed)
    for n, g, d, m, v in zip(SMALL, mine, _unpack_rows(d_s, shapes), _unpack_rows(m_s, shapes), _unpack_rows(v_s, shapes)):
        grads[n], deltas[n], new_m[n], new_v[n] = g, d, m, v

    return (loss, grad_x, *[grads[n] for n in WEIGHTS], *[deltas[n] for n in WEIGHTS], *[new_m[n] for n in WEIGHTS],
            *[new_v[n] for n in WEIGHTS])


def kernel(x, positions, mix_norm, ffn_norm, fox_w_in, fox_b_f, fox_q_gain, fox_k_gain, fox_w_out, mla_w_in, mla_q_a_gain, mla_kv_a_gain, mla_w_q_b, mla_w_kv_b, mla_q_gain, mla_k_gain, mla_w_out, sb_w_in, sb_q_gain, sb_k_gain, sb_w_out, sgu_w_in, sgu_v_gain, sgu_w_s, sgu_b_s, sgu_w_out, ffn_w_up, ffn_conv_w, ffn_conv_b, ffn_w_down, loss_target, m_mix_norm, m_ffn_norm, m_fox_w_in, m_fox_b_f, m_fox_q_gain, m_fox_k_gain, m_fox_w_out, m_mla_w_in, m_mla_q_a_gain, m_mla_kv_a_gain, m_mla_w_q_b, m_mla_w_kv_b, m_mla_q_gain, m_mla_k_gain, m_mla_w_out, m_sb_w_in, m_sb_q_gain, m_sb_k_gain, m_sb_w_out, m_sgu_w_in, m_sgu_v_gain, m_sgu_w_s, m_sgu_b_s, m_sgu_w_out, m_ffn_w_up, m_ffn_conv_w, m_ffn_conv_b, m_ffn_w_down, v_mix_norm, v_ffn_norm, v_fox_w_in, v_fox_b_f, v_fox_q_gain, v_fox_k_gain, v_fox_w_out, v_mla_w_in, v_mla_q_a_gain, v_mla_kv_a_gain, v_mla_w_q_b, v_mla_w_kv_b, v_mla_q_gain, v_mla_k_gain, v_mla_w_out, v_sb_w_in, v_sb_q_gain, v_sb_k_gain, v_sb_w_out, v_sgu_w_in, v_sgu_v_gain, v_sgu_w_s, v_sgu_b_s, v_sgu_w_out, v_ffn_w_up, v_ffn_conv_w, v_ffn_conv_b, v_ffn_w_down):
    args = locals()
    names = ['x', 'positions'] + WEIGHTS + ['loss_target'] + ['m_' + n for n in WEIGHTS] + ['v_' + n for n in WEIGHTS]
    return _train_step({n: args[n] for n in names})
```

```python
import functools
import math

import jax
import jax.numpy as jnp
from jax import lax
from jax.experimental import pallas as pl
from jax.experimental.pallas import tpu as pltpu

F32 = jnp.float32
BF16 = jnp.bfloat16
MESH = pl.DeviceIdType.MESH

N_DEV = 8
N_HEADS = 16
HEAD_DIM = 128
EPS = 1e-6
DEPTH = 4
D_FF = 5632
MLA_ROPE = 64
ROPE_THETA = 10000.0
VMEM_LIMIT = 48 * 1024 * 1024

ADAM_LR, ADAM_B1, ADAM_B2, ADAM_EPS, ADAM_WD, ADAM_STEP = 0.001, 0.9, 0.999, 1e-08, 0.01, 10

WEIGHTS = ['mix_norm', 'ffn_norm', 'fox_w_in', 'fox_b_f', 'fox_q_gain', 'fox_k_gain', 'fox_w_out', 'mla_w_in',
           'mla_q_a_gain', 'mla_kv_a_gain', 'mla_w_q_b', 'mla_w_kv_b', 'mla_q_gain', 'mla_k_gain', 'mla_w_out',
           'sb_w_in', 'sb_q_gain', 'sb_k_gain', 'sb_w_out', 'sgu_w_in', 'sgu_v_gain', 'sgu_w_s', 'sgu_b_s',
           'sgu_w_out', 'ffn_w_up', 'ffn_conv_w', 'ffn_conv_b', 'ffn_w_down']
BIG = ['fox_w_in', 'fox_w_out', 'mla_w_in', 'mla_w_q_b', 'mla_w_kv_b', 'mla_w_out', 'sb_w_in', 'sb_w_out',
       'sgu_w_in', 'sgu_w_out', 'ffn_w_up', 'ffn_w_down']
SMALL = [w for w in WEIGHTS if w not in BIG]
SMALL_SHARDED = {'mla_q_a_gain': 1, 'mla_kv_a_gain': 1, 'sgu_v_gain': 1, 'ffn_conv_w': 2}


def _cparams(n_grid):
    return pltpu.CompilerParams(dimension_semantics=("arbitrary",) * n_grid, vmem_limit_bytes=VMEM_LIMIT)


def _pick(n, cap):
    best = None
    t = 128
    while t <= min(n, cap):
        if n % t == 0:
            best = t
        t += 128
    return best if best is not None else n


def _mm_call(name, a, b, out_shape, a_spec, b_spec, o_spec, grid, dims, acc_shape, add=None):
    nk = grid[2]

    def body(*refs):
        a_ref, b_ref = refs[:2]
        add_ref = refs[2] if add is not None else None
        o_ref = refs[3] if add is not None else refs[2]
        prod = lax.dot_general(a_ref[...].astype(BF16), b_ref[...].astype(BF16), (dims, ((), ())),
                               preferred_element_type=F32)

        def finish(r):
            if add_ref is not None:
                r = r + add_ref[...].astype(F32)
            o_ref[...] = r.astype(o_ref.dtype)

        if nk == 1:
            finish(prod)
            return
        acc = refs[-1]
        k = pl.program_id(2)

        @pl.when(k == 0)
        def _():
            acc[...] = prod

        @pl.when(k > 0)
        def _():
            acc[...] += prod

        @pl.when(k == nk - 1)
        def _():
            finish(acc[...])

    ins = [a, b] + ([] if add is None else [add])
    in_specs = [a_spec, b_spec] + ([] if add is None else [o_spec])
    return pl.pallas_call(body, grid=grid, in_specs=in_specs, out_specs=o_spec, out_shape=out_shape,
                          scratch_shapes=[] if nk == 1 else [pltpu.VMEM(acc_shape, F32)], name=name,
                          compiler_params=_cparams(3))(*ins)


def mm_nn(a, b3, out_dtype, name, add=None, joff=0, nj=None):
    m, kk = a.shape
    _, kb, n = b3.shape
    assert kb == kk
    nj = b3.shape[0] - joff if nj is None else nj
    tn, tk = _pick(n, 1536), _pick(kk, 2048)
    tm = _pick(m, 512 if a.dtype == F32 and tk > 1024 else 1024)
    nb = n // tn
    return _mm_call(
        name, a, b3, jax.ShapeDtypeStruct((m, nj * n), out_dtype),
        pl.BlockSpec((tm, tk), lambda i, c, k: (i, k)),
        pl.BlockSpec((None, tk, tn), lambda i, c, k: (joff + c // nb, k, c % nb)),
        pl.BlockSpec((tm, tn), lambda i, c, k: (i, c)),
        (m // tm, nj * nb, kk // tk), ((1,), (0,)), (tm, tn), add=add)


def _grouped_cols(x, tile, rows_tile, rows_arg):
    pick = {'i': lambda i, o, c: (i, c), 'k': lambda o, c, k: (k, c)}[rows_arg]
    if x.ndim == 2:
        return pl.BlockSpec((rows_tile, tile), pick)
    per_group = x.shape[2] // tile
    assert x.shape[2] % tile == 0

    def index(*g):
        r, c = pick(*g)
        return c // per_group, r, c % per_group

    return pl.BlockSpec((None, rows_tile, tile), index)


def mm_nt(a, b3, out_dtype, name, add=None, joff=0, nj=None):
    m, na = (a.shape[1], a.shape[0] * a.shape[2]) if a.ndim == 3 else a.shape
    _, ko, n = b3.shape
    nj = b3.shape[0] - joff if nj is None else nj
    assert na == nj * n
    to, tn = _pick(ko, 1024), _pick(n, 2048)
    tm = _pick(m, 512 if a.dtype == F32 and tn > 1024 else 1024)
    nb = n // tn
    return _mm_call(
        name, a, b3, jax.ShapeDtypeStruct((m, ko), out_dtype),
        _grouped_cols(a, tn, tm, 'i'),
        pl.BlockSpec((None, to, tn), lambda i, o, c: (joff + c // nb, o, c % nb)),
        pl.BlockSpec((tm, to), lambda i, o, c: (i, o)),
        (m // tm, ko // to, nj * nb), ((1,), (1,)), (tm, to), add=add)


def mm_tn(a, b, nj, out_dtype, name):
    s, ko = a.shape
    sb, nb_tot = (b.shape[1], b.shape[0] * b.shape[2]) if b.ndim == 3 else b.shape
    assert sb == s and nb_tot % nj == 0
    n = nb_tot // nj
    to, tn = _pick(ko, 1024), _pick(n, 1536)
    ts = _pick(s, 1024 if F32 in (a.dtype, b.dtype) else 2048)
    nb = n // tn
    return _mm_call(
        name, a, b, jax.ShapeDtypeStruct((nj, ko, n), out_dtype),
        pl.BlockSpec((ts, to), lambda o, c, k: (k, o)),
        _grouped_cols(b, tn, ts, 'k'),
        pl.BlockSpec((None, to, tn), lambda o, c, k: (c // nb, o, c % nb)),
        (ko // to, nj * nb, s // ts), ((0,), (0,)), (to, tn))


def blockk(name, fn, grid, ins, outs):
    n_in = len(ins)
    accs = [o[2] for o in outs]

    def body(*refs):
        vals = fn(*[r[...] for r in refs[:n_in]])
        if not isinstance(vals, (tuple, list)):
            vals = (vals,)
        i, j = pl.program_id(0), pl.program_id(1)
        for r, v, acc in zip(refs[n_in:], vals, accs):
            if acc is None:
                r[...] = v.astype(r.dtype)
            else:
                first = (j == 0) if acc == 'inner' else jnp.logical_and(i == 0, j == 0)

                @pl.when(first)
                def _(r=r, v=v):
                    r[...] = v.astype(r.dtype)

                @pl.when(jnp.logical_not(first))
                def _(r=r, v=v):
                    r[...] += v.astype(r.dtype)

    res = pl.pallas_call(body, grid=grid, in_specs=[s for _, s in ins], out_specs=[o[1] for o in outs],
                         out_shape=[o[0] for o in outs], name=name, compiler_params=_cparams(2))(*[a for a, _ in ins])
    return res


def _sds(shape, dtype):
    return jax.ShapeDtypeStruct(tuple(shape), dtype)


def _rows(tr, w, col=0):
    if col == 'j':
        return pl.BlockSpec((tr, w), lambda i, j: (i, j))
    if callable(col):
        return pl.BlockSpec((tr, w), lambda i, j: (i, col(j)))
    return pl.BlockSpec((tr, w), lambda i, j: (i, col))


def _whole(shape):
    nd = len(shape)
    return pl.BlockSpec(tuple(shape), lambda i, j: (0,) * nd)


def _rms(x, g, n):
    ms = jnp.sum(x * x, axis=-1, keepdims=True) * (1.0 / n)
    return x * lax.rsqrt(ms + EPS) * g


def _sig(x):
    return 1.0 / (1.0 + jnp.exp(-x))


def _gelu(x):
    return 0.5 * x * (1.0 + jnp.tanh(math.sqrt(2.0 / math.pi) * (x + 0.044715 * (x * x * x))))


def _lane_iota(shape):
    return lax.broadcasted_iota(jnp.int32, shape, len(shape) - 1)


def _rope(x, cos, sin):
    lane = _lane_iota(x.shape)
    half = MLA_ROPE // 2
    swapped = jnp.where(lane < half, pltpu.roll(x, HEAD_DIM - half, 1), pltpu.roll(x, half, 1))
    sign = jnp.where(lane < half, -1.0, 1.0)
    return x * cos + swapped * (sin * sign)


def _rope_t(dy, cos, sin):
    lane = _lane_iota(dy.shape)
    half = MLA_ROPE // 2
    t = dy * sin
    swapped = jnp.where(lane < half, pltpu.roll(t, HEAD_DIM - half, 1), pltpu.roll(t, half, 1))
    sign = jnp.where(lane < half, 1.0, -1.0)
    return dy * cos + swapped * sign


def rms_fwd(name, x, gain, n, tr, width, xcol=0, nh=1, out_dtype=BF16, out_cols=None):
    r = x.shape[0]
    tr = r if nh > 1 else tr
    out_cols = width * nh if out_cols is None else out_cols
    xspec = _rows(tr, width, (lambda j: xcol + j) if nh > 1 else xcol)
    ospec = _rows(tr, width, 'j' if nh > 1 else 0)
    return blockk(name, lambda xb, g: _rms(xb.astype(F32), g, n), (r // tr, nh),
                  [(x, xspec), (gain, _whole(gain.shape))], [(_sds((r, out_cols), out_dtype), ospec, None)])[0]


def rms_bwd(name, x, gain, dy, n, tr, width, xcol=0, nh=1, dycol=0, resid=None, out_dtype=F32, also_bf16=False):
    r = x.shape[0]
    tr = r if nh > 1 else tr
    xspec = _rows(tr, width, (lambda j: xcol + j) if nh > 1 else xcol)
    dyspec = _rows(tr, width, (lambda j: dycol + j) if nh > 1 else dycol)
    ospec = _rows(tr, width, 'j' if nh > 1 else 0)

    def fn(xb, g, dyb, *rest):
        _, vjp = jax.vjp(lambda a, b: _rms(a, b, n), xb.astype(F32), g)
        dx, dg = vjp(dyb.astype(F32))
        if rest:
            dx = dx + rest[0].astype(F32)
        return ((dx,) if also_bf16 else ()) + (dx, dg)

    ins = [(x, xspec), (gain, _whole(gain.shape)), (dy, dyspec)]
    if resid is not None:
        ins.append((resid, ospec))
    outs = [(_sds((r, width * nh), out_dtype), ospec, None), (_sds(gain.shape, F32), _whole(gain.shape), 'all')]
    if also_bf16:
        outs.insert(0, (_sds((r, width * nh), BF16), ospec, None))
    return blockk(name, fn, (r // tr, nh), ins, outs)


def _nt(a, b):
    return lax.dot_general(a, b, (((1,), (1,)), ((), ())), preferred_element_type=F32)


def _tn(a, b):
    return lax.dot_general(a, b, (((0,), (0,)), ((), ())), preferred_element_type=F32)


def _nn(a, b):
    return lax.dot_general(a, b, (((1,), (0,)), ((), ())), preferred_element_type=F32)


ATTN_BLOCK = 512
ATTN_HEADS = 4
ATTN_GRID_HEADS = N_HEADS // ATTN_HEADS


def _head_cols(off):
    assert off % ATTN_HEADS == 0
    return off // ATTN_HEADS


def _hd(ref, hh):
    return ref[:, hh * HEAD_DIM:(hh + 1) * HEAD_DIM]


def _attn_specs(tq, qoff, koff, voff, extra, bias, q2off):
    w = HEAD_DIM * ATTN_HEADS
    qc, kc, vc, q2c = (_head_cols(o) for o in (qoff, koff, voff, q2off))
    specs = [pl.BlockSpec((tq, w), lambda h, i, j: (i, qc + h)),
             pl.BlockSpec((tq, w), lambda h, i, j: (jnp.minimum(i, j), kc + h)),
             pl.BlockSpec((tq, w), lambda h, i, j: (jnp.minimum(i, j), vc + h))]
    if extra:
        specs += [pl.BlockSpec((tq, w), lambda h, i, j: (i, q2c + h)),
                  pl.BlockSpec((tq, HEAD_DIM), lambda h, i, j: (jnp.minimum(i, j), 0))]
    if bias:
        specs += [pl.BlockSpec((ATTN_HEADS, tq, 1), lambda h, i, j: (h, i, 0)),
                  pl.BlockSpec((ATTN_HEADS, 1, tq), lambda h, i, j: (h, 0, jnp.minimum(i, j)))]
    return specs


def _scores(q, k, q2, k2, cc, cr, scale, tq, diagonal):
    s = _nt(q.astype(BF16), k.astype(BF16))
    if q2 is not None:
        s = s + _nt(q2.astype(BF16), k2.astype(BF16))
    s = s * scale
    if cc is not None:
        s = s + (cc - cr)
    if not diagonal:
        return s, None
    return s, lax.broadcasted_iota(jnp.int32, (tq, tq), 1) <= lax.broadcasted_iota(jnp.int32, (tq, tq), 0)


def _on_blocks(qi, kj, step):
    @pl.when(kj < qi)
    def _():
        step(False)

    @pl.when(kj == qi)
    def _():
        step(True)


def attn_fwd(name, q, k, v, scale, *, qoff=0, koff=0, voff=0, q2=None, k2=None, q2off=0, cum=None, tq=ATTN_BLOCK,
             exact_o=False):
    s_len = q.shape[0]
    nq = s_len // tq
    extra, bias = q2 is not None, cum is not None
    n_in = 3 + 2 * extra + 2 * bias

    def body(*refs):
        q_ref, k_ref, v_ref = refs[:3]
        p = 3
        q2_ref = k2_ref = cc_ref = cr_ref = None
        if extra:
            q2_ref, k2_ref = refs[p:p + 2]
            p += 2
        if bias:
            cc_ref, cr_ref = refs[p:p + 2]
            p += 2
        o_ref, lse_ref, m_s, l_s, acc_s = refs[p:]
        qi, kj = pl.program_id(1), pl.program_id(2)

        @pl.when(kj == 0)
        def _():
            m_s[...] = jnp.full_like(m_s, -jnp.inf)
            l_s[...] = jnp.zeros_like(l_s)
            acc_s[...] = jnp.zeros_like(acc_s)

        def step(diagonal):
            for hh in range(ATTN_HEADS):
                s, allowed = _scores(_hd(q_ref, hh), _hd(k_ref, hh), _hd(q2_ref, hh) if extra else None,
                                     k2_ref[...] if extra else None, cc_ref[hh] if bias else None,
                                     cr_ref[hh] if bias else None, scale, tq, diagonal)
                if diagonal:
                    s = jnp.where(allowed, s, -jnp.inf)
                m_old = m_s[hh]
                m_new = jnp.maximum(m_old, jnp.max(s, axis=-1, keepdims=True))
                alpha = jnp.exp(m_old - m_new)
                pr = jnp.exp(s - m_new)
                l_s[hh] = alpha * l_s[hh] + jnp.sum(pr, axis=-1, keepdims=True)
                vb = _hd(v_ref, hh).astype(BF16)
                pv = _nn(pr.astype(BF16), vb)
                if exact_o:
                    pv = pv + _nn((pr - pr.astype(BF16).astype(F32)).astype(BF16), vb)
                acc_s[hh] = alpha * acc_s[hh] + pv
                m_s[hh] = m_new

        _on_blocks(qi, kj, step)

        @pl.when(kj == qi)
        def _():
            for hh in range(ATTN_HEADS):
                o_ref[:, hh * HEAD_DIM:(hh + 1) * HEAD_DIM] = (acc_s[hh] / l_s[hh]).astype(o_ref.dtype)
                lse_ref[hh] = m_s[hh] + jnp.log(l_s[hh])

    ins = [q, k, v] + ([q2, k2] if extra else []) + (list(cum) if bias else [])
    d, w = HEAD_DIM, HEAD_DIM * ATTN_HEADS
    return pl.pallas_call(
        body, grid=(ATTN_GRID_HEADS, nq, nq), in_specs=_attn_specs(tq, qoff, koff, voff, extra, bias, q2off),
        out_specs=[pl.BlockSpec((tq, w), lambda h, i, j: (i, h)),
                   pl.BlockSpec((ATTN_HEADS, tq, 1), lambda h, i, j: (h, i, 0))],
        out_shape=[_sds((s_len, N_HEADS * d), F32 if exact_o else BF16), _sds((N_HEADS, s_len, 1), F32)],
        scratch_shapes=[pltpu.VMEM((ATTN_HEADS, tq, 1), F32), pltpu.VMEM((ATTN_HEADS, tq, 1), F32),
                        pltpu.VMEM((ATTN_HEADS, tq, d), F32)],
        name=name, compiler_params=_cparams(3))(*ins)


def attn_bwd(name, q, k, v, o, do, lse, scale, *, qoff=0, koff=0, voff=0, q2=None, k2=None, q2off=0, cum=None,
             tq=ATTN_BLOCK):
    s_len = q.shape[0]
    nq = s_len // tq
    extra, bias = q2 is not None, cum is not None
    d = HEAD_DIM
    n_in = 6 + 2 * extra + 2 * bias

    def body(*refs):
        q_ref, k_ref, v_ref = refs[:3]
        p = 3
        q2_ref = k2_ref = cc_ref = cr_ref = None
        if extra:
            q2_ref, k2_ref = refs[p:p + 2]
            p += 2
        if bias:
            cc_ref, cr_ref = refs[p:p + 2]
            p += 2
        o_ref, do_ref, lse_ref = refs[p:p + 3]
        p += 3
        dq_ref, dk_ref, dv_ref = refs[p:p + 3]
        p += 3
        dq2_ref = dk2_ref = dcs_ref = None
        if extra:
            dq2_ref, dk2_ref = refs[p:p + 2]
            p += 2
        if bias:
            dcs_ref = refs[p]
            p += 1
        dq_s, delta_s = refs[p:p + 2]
        dq2_s = refs[p + 2] if extra else None
        h, qi, kj = pl.program_id(0), pl.program_id(1), pl.program_id(2)

        @pl.when(jnp.logical_and(qi == 0, kj == 0))
        def _():
            dk_ref[...] = jnp.zeros_like(dk_ref)
            dv_ref[...] = jnp.zeros_like(dv_ref)

        if extra:
            @pl.when(jnp.logical_and(h == 0, jnp.logical_and(qi == 0, kj == 0)))
            def _():
                dk2_ref[...] = jnp.zeros_like(dk2_ref)

        @pl.when(kj == 0)
        def _():
            dq_s[...] = jnp.zeros_like(dq_s)
            if extra:
                dq2_s[...] = jnp.zeros_like(dq2_s)
            for hh in range(ATTN_HEADS):
                delta_s[hh] = jnp.sum(_hd(do_ref, hh).astype(F32) * _hd(o_ref, hh).astype(F32), axis=-1, keepdims=True)

        if bias:
            @pl.when(kj > qi)
            def _():
                dcs_ref[...] = jnp.zeros_like(dcs_ref)

        def step(diagonal):
            ks = pl.ds(pl.multiple_of(kj * tq, tq), tq)
            for hh in range(ATTN_HEADS):
                cols = slice(hh * HEAD_DIM, (hh + 1) * HEAD_DIM)
                qh, kh = _hd(q_ref, hh).astype(BF16), _hd(k_ref, hh).astype(BF16)
                q2h = _hd(q2_ref, hh).astype(BF16) if extra else None
                k2h = k2_ref[...].astype(BF16) if extra else None
                s, allowed = _scores(qh, kh, q2h, k2h, cc_ref[hh] if bias else None, cr_ref[hh] if bias else None,
                                     scale, tq, diagonal)
                pr = jnp.exp(s - lse_ref[hh])
                if diagonal:
                    pr = jnp.where(allowed, pr, 0.0)
                dob = _hd(do_ref, hh).astype(BF16)
                dp = _nt(dob, _hd(v_ref, hh).astype(BF16))
                ds = pr * (dp - delta_s[hh])
                dsb = (ds * scale).astype(BF16)
                dq_s[hh] += _nn(dsb, kh)
                dk_ref[ks, cols] += _tn(dsb, qh)
                dv_ref[ks, cols] += _tn(pr.astype(BF16), dob)
                if extra:
                    dq2_s[hh] += _nn(dsb, k2h)
                    dk2_ref[ks, :] += _tn(dsb, q2h)
                if bias:
                    dcs_ref[hh] = jnp.sum(ds, axis=0, keepdims=True)

        _on_blocks(qi, kj, step)

        @pl.when(kj == qi)
        def _():
            for hh in range(ATTN_HEADS):
                cols = slice(hh * HEAD_DIM, (hh + 1) * HEAD_DIM)
                dq_ref[:, cols] = dq_s[hh]
                if extra:
                    dq2_ref[:, cols] = dq2_s[hh]

    w = HEAD_DIM * ATTN_HEADS
    ins = [q, k, v] + ([q2, k2] if extra else []) + (list(cum) if bias else []) + [o, do, lse]
    in_specs = _attn_specs(tq, qoff, koff, voff, extra, bias, q2off) + [
        pl.BlockSpec((tq, w), lambda h, i, j: (i, h)), pl.BlockSpec((tq, w), lambda h, i, j: (i, h)),
        pl.BlockSpec((ATTN_HEADS, tq, 1), lambda h, i, j: (h, i, 0))]
    full = _sds((s_len, N_HEADS * d), F32)
    out_shape = [full, full, full]
    out_specs = [pl.BlockSpec((tq, w), lambda h, i, j: (i, h)), pl.BlockSpec((s_len, w), lambda h, i, j: (0, h)),
                 pl.BlockSpec((s_len, w), lambda h, i, j: (0, h))]
    scratch = [pltpu.VMEM((ATTN_HEADS, tq, d), F32), pltpu.VMEM((ATTN_HEADS, tq, 1), F32)]
    if extra:
        out_shape += [full, _sds((s_len, d), F32)]
        out_specs += [pl.BlockSpec((tq, w), lambda h, i, j: (i, h)), pl.BlockSpec((s_len, d), lambda h, i, j: (0, 0))]
        scratch.append(pltpu.VMEM((ATTN_HEADS, tq, d), F32))
    if bias:
        out_shape.append(_sds((N_HEADS, nq, 1, s_len), F32))
        out_specs.append(pl.BlockSpec((ATTN_HEADS, None, 1, tq), lambda h, i, j: (h, i, 0, j)))
    return pl.pallas_call(body, grid=(ATTN_GRID_HEADS, nq, nq), in_specs=in_specs, out_specs=out_specs,
                          out_shape=out_shape, scratch_shapes=scratch, name=name, compiler_params=_cparams(3))(*ins)


def _sb_terms(q, k, scale, tq, diagonal):
    z = _nt(q.astype(BF16), k.astype(BF16)) * scale
    lg = jnp.log(1.0 + jnp.exp(-jnp.abs(z)))
    log_keep = -(jnp.maximum(z, 0.0) + lg)
    log_beta = jnp.minimum(z, 0.0) - lg
    if not diagonal:
        return None, log_keep, log_beta
    strict = lax.broadcasted_iota(jnp.int32, (tq, tq), 1) < lax.broadcasted_iota(jnp.int32, (tq, tq), 0)
    return strict, jnp.where(strict, log_keep, 0.0), log_beta


def _tri(tq, pred):
    a = lax.broadcasted_iota(jnp.int32, (tq, tq), 0)
    b = lax.broadcasted_iota(jnp.int32, (tq, tq), 1)
    return jnp.where(pred(a, b), 1.0, 0.0).astype(BF16)


SUM_CHUNK = 256


def _lane_sums(x, later):
    n = x.shape[1]
    chunk = min(SUM_CHUNK, n)
    tri = _tri(chunk, (lambda m, j: m > j) if later else (lambda m, j: m < j))
    order = range(n // chunk - 1, -1, -1) if later else range(n // chunk)
    pieces, carry = [None] * (n // chunk), None
    for cidx in order:
        xc = x[:, cidx * chunk:(cidx + 1) * chunk]
        hi = xc.astype(BF16)
        local = _nn(hi, tri) + _nn((xc - hi.astype(F32)).astype(BF16), tri)
        total = jnp.sum(xc, axis=-1, keepdims=True)
        pieces[cidx] = local if carry is None else local + carry
        carry = total if carry is None else carry + total
    return jnp.concatenate(pieces, axis=1), carry


def sb_fwd(name, qn, kn, qkv, scale, voff, tq=ATTN_BLOCK):
    s_len = qn.shape[0]
    nq = s_len // tq
    d = HEAD_DIM

    def body(q_ref, k_ref, v_ref, o_ref, car_ref, ca_s, acc_s):
        qi, kj = pl.program_id(1), pl.program_id(2)

        @pl.when(kj == 0)
        def _():
            ca_s[...] = jnp.zeros_like(ca_s)
            acc_s[...] = jnp.zeros_like(acc_s)

        def step(diagonal):
            for hh in range(ATTN_HEADS):
                strict, log_keep, log_beta = _sb_terms(_hd(q_ref, hh), _hd(k_ref, hh), scale, tq, diagonal)
                ca = ca_s[hh]
                car_ref[hh] = ca
                after, total = _lane_sums(log_keep, True)
                a = jnp.exp(log_beta + (after + ca))
                if diagonal:
                    a = jnp.where(strict, a, 0.0)
                acc_s[hh] += _nn(a.astype(BF16), _hd(v_ref, hh).astype(BF16))
                ca_s[hh] = ca + total

        @pl.when(kj == 0)
        def _():
            step(True)

        @pl.when(jnp.logical_and(kj > 0, kj <= qi))
        def _():
            step(False)

        @pl.when(kj == qi)
        def _():
            for hh in range(ATTN_HEADS):
                o_ref[:, hh * HEAD_DIM:(hh + 1) * HEAD_DIM] = acc_s[hh].astype(o_ref.dtype)

    kblk = lambda i, j: jnp.maximum(i - j, 0)
    w, vc = HEAD_DIM * ATTN_HEADS, _head_cols(voff)
    return pl.pallas_call(
        body, grid=(ATTN_GRID_HEADS, nq, nq),
        in_specs=[pl.BlockSpec((tq, w), lambda h, i, j: (i, h)), pl.BlockSpec((tq, w), lambda h, i, j: (kblk(i, j), h)),
                  pl.BlockSpec((tq, w), lambda h, i, j: (kblk(i, j), vc + h))],
        out_specs=[pl.BlockSpec((tq, w), lambda h, i, j: (i, h)),
                   pl.BlockSpec((ATTN_HEADS, None, tq, 1), lambda h, i, j: (h, kblk(i, j), i, 0))],
        out_shape=[_sds((s_len, N_HEADS * d), BF16), _sds((N_HEADS, nq, s_len, 1), F32)],
        scratch_shapes=[pltpu.VMEM((ATTN_HEADS, tq, 1), F32), pltpu.VMEM((ATTN_HEADS, tq, d), F32)],
        name=name, compiler_params=_cparams(3))(qn, kn, qkv)


def sb_bwd(name, qn, kn, qkv, do, carries, scale, voff, tq=ATTN_BLOCK):
    s_len = qn.shape[0]
    nq = s_len // tq
    d = HEAD_DIM

    def body(q_ref, k_ref, v_ref, do_ref, car_ref, dq_ref, dk_ref, dv_ref, dq_s, cg_s):
        qi, kj = pl.program_id(1), pl.program_id(2)

        @pl.when(jnp.logical_and(qi == 0, kj == 0))
        def _():
            dk_ref[...] = jnp.zeros_like(dk_ref)
            dv_ref[...] = jnp.zeros_like(dv_ref)

        @pl.when(kj == 0)
        def _():
            dq_s[...] = jnp.zeros_like(dq_s)
            cg_s[...] = jnp.zeros_like(cg_s)

        def step(diagonal):
            ks = pl.ds(pl.multiple_of(kj * tq, tq), tq)
            for hh in range(ATTN_HEADS):
                cols = slice(hh * HEAD_DIM, (hh + 1) * HEAD_DIM)
                qh, kh = _hd(q_ref, hh).astype(BF16), _hd(k_ref, hh).astype(BF16)
                strict, log_keep, log_beta = _sb_terms(qh, kh, scale, tq, diagonal)
                after, _ = _lane_sums(log_keep, True)
                a = jnp.exp(log_beta + (after + car_ref[hh]))
                if diagonal:
                    a = jnp.where(strict, a, 0.0)
                dob = _hd(do_ref, hh).astype(BF16)
                g = a * _nt(dob, _hd(v_ref, hh).astype(BF16))
                cg = cg_s[hh]
                before, total = _lane_sums(g, False)
                big_g = before + cg
                cg_s[hh] = cg + total
                beta = jnp.exp(log_beta)
                dz = g * (1.0 - beta) - big_g * beta
                if diagonal:
                    dz = jnp.where(strict, dz, 0.0)
                dzb = (dz * scale).astype(BF16)
                dq_s[hh] += _nn(dzb, kh)
                dk_ref[ks, cols] += _tn(dzb, qh)
                dv_ref[ks, cols] += _tn(a.astype(BF16), dob)

        _on_blocks(qi, kj, step)

        @pl.when(kj == qi)
        def _():
            for hh in range(ATTN_HEADS):
                dq_ref[:, hh * HEAD_DIM:(hh + 1) * HEAD_DIM] = dq_s[hh]

    kblk = lambda i, j: jnp.minimum(i, j)
    full = _sds((s_len, N_HEADS * d), F32)
    w, vc = HEAD_DIM * ATTN_HEADS, _head_cols(voff)
    return pl.pallas_call(
        body, grid=(ATTN_GRID_HEADS, nq, nq),
        in_specs=[pl.BlockSpec((tq, w), lambda h, i, j: (i, h)), pl.BlockSpec((tq, w), lambda h, i, j: (kblk(i, j), h)),
                  pl.BlockSpec((tq, w), lambda h, i, j: (kblk(i, j), vc + h)),
                  pl.BlockSpec((tq, w), lambda h, i, j: (i, h)),
                  pl.BlockSpec((ATTN_HEADS, None, tq, 1), lambda h, i, j: (h, kblk(i, j), i, 0))],
        out_specs=[pl.BlockSpec((tq, w), lambda h, i, j: (i, h)), pl.BlockSpec((s_len, w), lambda h, i, j: (0, h)),
                   pl.BlockSpec((s_len, w), lambda h, i, j: (0, h))],
        out_shape=[full, full, full],
        scratch_shapes=[pltpu.VMEM((ATTN_HEADS, tq, d), F32), pltpu.VMEM((ATTN_HEADS, tq, 1), F32)],
        name=name, compiler_params=_cparams(3))(qn, kn, qkv, do, carries)


def _cumsum_rows(x, reverse):
    n = x.shape[0] // HEAD_DIM
    tri = _tri(HEAD_DIM, (lambda a, b: b >= a) if reverse else (lambda a, b: b <= a))
    pieces = [None] * n
    carry = jnp.zeros((1, HEAD_DIM), F32)
    order = range(n - 1, -1, -1) if reverse else range(n)
    for blk in order:
        xb = x[blk * HEAD_DIM:(blk + 1) * HEAD_DIM, :]
        x1 = xb.astype(BF16)
        r1 = xb - x1.astype(F32)
        x2 = r1.astype(BF16)
        x3 = (r1 - x2.astype(F32)).astype(BF16)
        c = _nn(tri, x1) + _nn(tri, x2) + _nn(tri, x3) + carry
        pieces[blk] = c
        carry = c[0:1, :] if reverse else c[HEAD_DIM - 1:HEAD_DIM, :]
    return jnp.concatenate(pieces, axis=0)


def _log_sigmoid(x):
    return jnp.minimum(x, 0.0) - jnp.log(1.0 + jnp.exp(-jnp.abs(x)))


def fox_gate_fwd(name, fl, bf):
    return blockk(name, lambda f, b: _cumsum_rows(_log_sigmoid(f + b), False), (1, 1),
                  [(fl, _whole(fl.shape)), (bf, _whole(bf.shape))], [(_sds(fl.shape, F32), _whole(fl.shape), None)])[0]


def fox_gate_bwd(name, fl, bf, dcum):
    def fn(f, b, dc):
        dlogf = _cumsum_rows(dc, True)
        dfl = dlogf * _sig(-(f + b))
        return dfl, jnp.sum(dfl, axis=0, keepdims=True)

    return blockk(name, fn, (1, 1), [(fl, _whole(fl.shape)), (bf, _whole(bf.shape)), (dcum, _whole(dcum.shape))],
                  [(_sds(fl.shape, F32), _whole(fl.shape), None), (_sds(bf.shape, F32), _whole(bf.shape), None)])


def rope_fwd(name, x, gain, cos, sin, tr, xcol, nh):
    r = x.shape[0]
    tr = r if nh > 1 else tr
    xspec = _rows(tr, HEAD_DIM, lambda j: xcol + j)
    tspec = _rows(tr, HEAD_DIM, 0)
    return blockk(name, lambda xb, g, c, s: _rope(_rms(xb.astype(F32), g, MLA_ROPE), c, s), (r // tr, nh),
                  [(x, xspec), (gain, _whole(gain.shape)), (cos, tspec), (sin, tspec)],
                  [(_sds((r, HEAD_DIM * nh), BF16), _rows(tr, HEAD_DIM, 'j'), None)])[0]


def rope_bwd(name, x, gain, cos, sin, dy, tr, xcol, nh):
    r = x.shape[0]
    tr = r if nh > 1 else tr
    xspec = _rows(tr, HEAD_DIM, lambda j: xcol + j)
    tspec = _rows(tr, HEAD_DIM, 0)
    ospec = _rows(tr, HEAD_DIM, 'j')

    def fn(xb, g, c, s, dyb):
        _, vjp = jax.vjp(lambda a, b: _rms(a, b, MLA_ROPE), xb.astype(F32), g)
        return vjp(_rope_t(dyb.astype(F32), c, s))

    return blockk(name, fn, (r // tr, nh),
                  [(x, xspec), (gain, _whole(gain.shape)), (cos, tspec), (sin, tspec), (dy, ospec)],
                  [(_sds((r, HEAD_DIM * nh), F32), ospec, None), (_sds(gain.shape, F32), _whole(gain.shape), 'all')])


def sgu_pre_fwd(name, pre, gain, tr):
    s_len, w2 = pre.shape
    w = w2 // 2
    return blockk(name, lambda pu, pv, g: (_gelu(pu.astype(F32)), _rms(_gelu(pv.astype(F32)), g, w)), (s_len // tr, 1),
                  [(pre, _rows(tr, w, 0)), (pre, _rows(tr, w, 1)), (gain, _whole(gain.shape))],
                  [(_sds((s_len, w), BF16), _rows(tr, w, 0), None), (_sds((s_len, w), BF16), _rows(tr, w, 0), None)])


def sgu_pre_bwd(name, pre, gain, du, dvn, tr):
    s_len, w2 = pre.shape
    w = w2 // 2

    def fn(pu, pv, g, dub, dvb):
        _, vjp_u = jax.vjp(_gelu, pu.astype(F32))
        _, vjp_v = jax.vjp(lambda a, b: _rms(_gelu(a), b, w), pv.astype(F32), g)
        dpv, dg = vjp_v(dvb.astype(F32))
        return jnp.stack([vjp_u(dub.astype(F32))[0], dpv]), dg

    spec = _rows(tr, w, 0)
    return blockk(name, fn, (s_len // tr, 1),
                  [(pre, spec), (pre, _rows(tr, w, 1)), (gain, _whole(gain.shape)), (du, spec), (dvn, spec)],
                  [(_sds((2, s_len, w), BF16), pl.BlockSpec((2, tr, w), lambda i, j: (0, i, 0)), None),
                   (_sds(gain.shape, F32), _whole(gain.shape), 'all')])


def _ws_masked(ws):
    t = ws.shape[0]
    a = lax.broadcasted_iota(jnp.int32, (t, t), 0)
    b = lax.broadcasted_iota(jnp.int32, (t, t), 1)
    return jnp.where(b <= a, ws, 0.0)


def sgu_mix_fwd(name, vn, u, ws, bs3):
    s_len, w = vn.shape
    t = ws.shape[1]

    n_groups = w // t

    def fn(vb, ub, wsb, bsb):
        pieces = []
        for g in range(n_groups):
            cols = slice(g * t, (g + 1) * t)
            mixed = _nn(_ws_masked(wsb[g]).astype(BF16), vb[:, cols].astype(BF16)) + bsb[g]
            pieces.append(ub[:, cols].astype(F32) * mixed)
        return jnp.concatenate(pieces, axis=1)

    blk = pl.BlockSpec((t, w), lambda i, j: (i, 0))
    return blockk(name, fn, (s_len // t, 1), [(vn, blk), (u, blk), (ws, _whole(ws.shape)), (bs3, _whole(bs3.shape))],
                  [(_sds((s_len, w), BF16), blk, None)])[0]


def sgu_mix_bwd(name, vn, u, ws, bs3, dprod):
    s_len, w = vn.shape
    t = ws.shape[1]

    n_groups = w // t

    def fn(vb, ub, wsb, bsb, dpb):
        dus, dvns, dwss, dbss = [], [], [], []
        for g in range(n_groups):
            cols = slice(g * t, (g + 1) * t)
            wm = _ws_masked(wsb[g]).astype(BF16)
            vb16 = vb[:, cols].astype(BF16)
            mixed = _nn(wm, vb16) + bsb[g]
            dp = dpb[:, cols].astype(F32)
            dus.append(dp * mixed)
            dm = dp * ub[:, cols].astype(F32)
            dmb = dm.astype(BF16)
            dvns.append(_tn(wm, dmb))
            dwss.append(_ws_masked(_nt(dmb, vb16)))
            dbss.append(jnp.sum(dm, axis=-1, keepdims=True))
        return jnp.concatenate(dus, axis=1), jnp.concatenate(dvns, axis=1), jnp.stack(dwss), jnp.stack(dbss)

    blk = pl.BlockSpec((t, w), lambda i, j: (j, 0))
    return blockk(name, fn, (1, s_len // t),
                  [(vn, blk), (u, blk), (ws, _whole(ws.shape)), (bs3, _whole(bs3.shape)), (dprod, blk)],
                  [(_sds((s_len, w), BF16), blk, None), (_sds((s_len, w), BF16), blk, None),
                   (_sds(ws.shape, F32), _whole(ws.shape), 'all'), (_sds(bs3.shape, F32), _whole(bs3.shape), 'all')])


def _shift_down(x, k):
    row = lax.broadcasted_iota(jnp.int32, x.shape, 0)
    return jnp.where(row >= k, pltpu.roll(x, k, 0), 0.0)


def _shift_up(x, k):
    n = x.shape[0]
    row = lax.broadcasted_iota(jnp.int32, x.shape, 0)
    return jnp.where(row < n - k, pltpu.roll(x, n - k, 0), 0.0)


def _conv(up, cw, cb):
    return cb + cw[0:1, :] * _shift_down(up, 2) + cw[1:2, :] * _shift_down(up, 1) + cw[2:3, :] * up


def _conv_specs(s_len, tc, nf):
    g = pl.BlockSpec((s_len, tc), lambda i, j: (0, j))
    v = pl.BlockSpec((s_len, tc), lambda i, j: (0, j + nf))
    wg = pl.BlockSpec((3, tc), lambda i, j: (0, j))
    wv = pl.BlockSpec((3, tc), lambda i, j: (0, j + nf))
    bg = pl.BlockSpec((1, tc), lambda i, j: (0, j))
    bv = pl.BlockSpec((1, tc), lambda i, j: (0, j + nf))
    return g, v, wg, wv, bg, bv


def conv_fwd(name, up, cw, cb, tc=256):
    s_len, f2 = up.shape
    f = f2 // 2
    nf = f // tc
    g, v, wg, wv, bg, bv = _conv_specs(s_len, tc, nf)

    def fn(ug, uv, cwg, cwv, cbg, cbv):
        yg = _conv(ug.astype(F32), cwg, cbg)
        yv = _conv(uv.astype(F32), cwv, cbv)
        return yg * _sig(yg) * yv

    return blockk(name, fn, (1, nf), [(up, g), (up, v), (cw, wg), (cw, wv), (cb, bg), (cb, bv)],
                  [(_sds((s_len, f), BF16), g, None)])[0]


def conv_bwd(name, up, cw, cb, dact, tc=128):
    s_len, f2 = up.shape
    f = f2 // 2
    nf = f // tc
    g, v, wg, wv, bg, bv = _conv_specs(s_len, tc, nf)

    def conv(u, cwh, cbh):
        u1, u2 = _shift_down(u, 1), _shift_down(u, 2)
        return cbh + cwh[0:1, :] * u2 + cwh[1:2, :] * u1 + cwh[2:3, :] * u, u1, u2

    def half(dy, u, u1, u2, cwh):
        dup = cwh[2:3, :] * dy + _shift_up(cwh[1:2, :] * dy + _shift_up(cwh[0:1, :] * dy, 1), 1)
        dcw = jnp.concatenate([jnp.sum(dy * u2, axis=0, keepdims=True), jnp.sum(dy * u1, axis=0, keepdims=True),
                               jnp.sum(dy * u, axis=0, keepdims=True)], axis=0)
        return dup, dcw, jnp.sum(dy, axis=0, keepdims=True)

    def fn(ug, uv, cwg, cwv, cbg, cbv, da):
        ug, uv, da = ug.astype(F32), uv.astype(F32), da.astype(F32)
        yg, ug1, ug2 = conv(ug, cwg, cbg)
        yv, uv1, uv2 = conv(uv, cwv, cbv)
        sg = _sig(yg)
        silu = yg * sg
        dyv = da * silu
        dyg = da * yv * (sg + silu * (1.0 - sg))
        dug, dcwg, dcbg = half(dyg, ug, ug1, ug2, cwg)
        duv, dcwv, dcbv = half(dyv, uv, uv1, uv2, cwv)
        return jnp.stack([dug, duv]), dcwg, dcwv, dcbg, dcbv

    both = pl.BlockSpec((2, s_len, tc), lambda i, j: (0, 0, j))
    return blockk(name, fn, (1, nf), [(up, g), (up, v), (cw, wg), (cw, wv), (cb, bg), (cb, bv), (dact, g)],
                  [(_sds((2, s_len, f), BF16), both, None),
                   (_sds((3, f), F32), wg, None), (_sds((3, f), F32), wg, None),
                   (_sds((1, f), F32), bg, None), (_sds((1, f), F32), bg, None)])


def loss_head(name, y, target, tr=256):
    s_len, d = y.shape

    def fn(yb, tb):
        e = yb - tb
        dy = e * (1.0 / d)
        return dy, dy, jnp.sum(e * e, axis=0, keepdims=True)

    spec = _rows(tr, d, 0)
    return blockk(name, fn, (s_len // tr, 1), [(y, spec), (target, spec)],
                  [(_sds((s_len, d), BF16), spec, None), (_sds((s_len, d), F32), spec, None),
                   (_sds((1, d), F32), _whole((1, d)), 'all')])


def _adam(w, g, m, v):
    m = ADAM_B1 * m + (1.0 - ADAM_B1) * g
    v = ADAM_B2 * v + (1.0 - ADAM_B2) * (g * g)
    m_hat = m / (1.0 - ADAM_B1 ** ADAM_STEP)
    v_hat = v / (1.0 - ADAM_B2 ** ADAM_STEP)
    delta = -ADAM_LR * (m_hat / (jnp.sqrt(v_hat) + ADAM_EPS) + ADAM_WD * w)
    return delta, m, v


def adam_sum(name, parts, w, m, v):
    n_layers, r, c = w.shape
    n_parts = parts[0].shape[0]
    assert len(parts) == n_layers
    tr = r
    for cand in (512, 256, 128, 64, 32, 16):
        if r % cand == 0 and cand * c * 4 <= 1024 * 1024:
            tr = cand
            break

    def body(*refs):
        part_refs = refs[:n_layers]
        w_ref, m_ref, v_ref, g_out, d_out, m_out, v_out = refs[n_layers:]
        layer = pl.program_id(0)
        for ll in range(n_layers):
            @pl.when(layer == ll)
            def _(ll=ll):
                g = part_refs[ll][0].astype(F32)
                for k in range(1, n_parts):
                    g = g + part_refs[ll][k].astype(F32)
                delta, m_new, v_new = _adam(w_ref[...], g, m_ref[...], v_ref[...])
                g_out[...] = g
                d_out[...] = delta
                m_out[...] = m_new
                v_out[...] = v_new

    spec = pl.BlockSpec((None, tr, c), lambda l, i: (l, i, 0))
    part_specs = [pl.BlockSpec((n_parts, tr, c), lambda l, i, ll=ll: (0, jnp.where(l == ll, i, 0), 0))
                  for ll in range(n_layers)]
    out = _sds((n_layers, r, c), F32)
    return pl.pallas_call(body, grid=(n_layers, r // tr), in_specs=part_specs + [spec] * 3, out_specs=[spec] * 4,
                          out_shape=[out] * 4, name=name, compiler_params=_cparams(2))(*parts, w, m, v)


def sum_parts(name, parts, tr=256):
    n_parts, r, c = parts.shape

    def fn(pb):
        g = pb[0].astype(F32)
        for k in range(1, n_parts):
            g = g + pb[k].astype(F32)
        return g

    return blockk(name, fn, (r // tr, 1), [(parts, pl.BlockSpec((n_parts, tr, c), lambda i, j: (0, i, 0)))],
                  [(_sds((r, c), F32), _rows(tr, c, 0), None)])[0]


def adam_flat(name, w, g, m, v, tr=256):
    r, c = w.shape
    spec = _rows(tr, c, 0)
    return blockk(name, lambda wb, gb, mb, vb: _adam(wb, gb, mb, vb), (r // tr, 1),
                  [(w, spec), (g, spec), (m, spec), (v, spec)], [(_sds((r, c), F32), spec, None)] * 3)


_ANY = pl.BlockSpec(memory_space=pl.ANY)


def _place():
    return lax.axis_index("x"), lax.axis_index("y"), lax.axis_index("c")


def _slot(px, py, pc):
    return 4 * px + 2 * py + pc


def all_gather(name, items):
    n = len(items)

    def body(*refs):
        xs, outs = refs[:n], refs[n:2 * n]
        send_sems, recv_sems, local_sems = refs[2 * n:]
        x, y, c = _place()
        me, sibling = (x, y, c), (x, y, 1 - c)
        chips = [(1 - x, y), (x, 1 - y), (1 - x, 1 - y)]

        def copy(t, k, block, to, src=None):
            dst = outs[t].at[_slot(*block)]
            return pltpu.make_async_remote_copy(src_ref=dst if src is None else src, dst_ref=dst,
                                                send_sem=send_sems.at[7 * t + k], recv_sem=recv_sems.at[7 * t + k],
                                                device_id=to, device_id_type=MESH)

        mine = [pltpu.make_async_copy(xs[t], outs[t].at[_slot(*me)], local_sems.at[t]) for t in range(n)]
        for cp in mine:
            cp.start()
        started = []
        for t in range(n):
            started.append(copy(t, 0, me, sibling, src=xs[t]))
            started += [copy(t, 1 + j, me, (*chip, c), src=xs[t]) for j, chip in enumerate(chips)]
        for cp in started:
            cp.start()
        for j, chip in enumerate(chips):
            for t in range(n):
                copy(t, 1 + j, (*chip, c), me).wait_recv()
                passed = copy(t, 4 + j, (*chip, c), sibling)
                passed.start()
                started.append(passed)
        for t in range(n):
            copy(t, 0, sibling, me).wait_recv()
            for j, chip in enumerate(chips):
                copy(t, 4 + j, (*chip, 1 - c), me).wait_recv()
        for cp in started:
            cp.wait_send()
        for cp in mine:
            cp.wait()

    return pl.pallas_call(
        body, in_specs=[_ANY] * n, out_specs=[_ANY] * n,
        out_shape=[_sds((N_DEV,) + a.shape, a.dtype) for a in items],
        scratch_shapes=[pltpu.SemaphoreType.DMA((7 * n,)), pltpu.SemaphoreType.DMA((7 * n,)), pltpu.SemaphoreType.DMA((n,))],
        name=name)(*items)


_HBM = pl.BlockSpec(memory_space=pltpu.HBM)
_SEM = pl.BlockSpec(memory_space=pltpu.SEMAPHORE)
_EFFECT = pltpu.SideEffectType.DATAFLOW_SIDE_EFFECTING


def _peer(k, x, y, c):
    return ((1 - x) if k & 4 else x, (1 - y) if k & 2 else y, (1 - c) if k & 1 else c)


PEERS_ALL = (1, 2, 3, 4, 5, 6, 7)
PEERS_CHIPWISE = (1, 2, 4, 6)


def _exchange_copies(mode, refs, send_sems, recv_sems, landing):
    x, y, c = _place()
    my_slot = _slot(x, y, c)
    copies = []
    if mode == "pass_on":
        for t, land in enumerate(refs):
            for j, chip in enumerate([(1 - x, y), (x, 1 - y), (1 - x, 1 - y)]):
                slot = _slot(*chip, (1 - c) if landing else c)
                copies.append(pltpu.make_async_remote_copy(
                    src_ref=land.at[slot], dst_ref=land.at[slot], send_sem=send_sems.at[3 * t + j],
                    recv_sem=recv_sems.at[3 * t + j], device_id=(x, y, 1 - c), device_id_type=MESH))
        return copies
    n = len(refs) // 2
    for t, (src, land) in enumerate(zip(refs[:n], refs[n:])):
        for k in (PEERS_CHIPWISE if mode == "gather_chipwise" else PEERS_ALL):
            peer = _peer(k, x, y, c)
            copies.append(pltpu.make_async_remote_copy(
                src_ref=src.at[_slot(*peer)] if mode == "scatter" else src,
                dst_ref=land.at[_slot(*peer) if landing else my_slot],
                send_sem=send_sems.at[7 * t + k - 1], recv_sem=recv_sems.at[7 * t + k - 1],
                device_id=peer, device_id_type=MESH))
    return copies


def exchange_start(name, mode, arrays):
    n = len(arrays)
    passing = mode == "pass_on"
    n_sems = (3 if passing else 7) * n
    lands = [] if passing else [pltpu.HBM(a.shape if mode == "scatter" else (N_DEV,) + a.shape, a.dtype) for a in arrays]

    def body(*refs):
        srcs, outs = refs[:n], refs[n + 2:]
        bufs = list(srcs) if passing else list(srcs) + list(outs[n:2 * n])
        for send in _exchange_copies(mode, bufs, refs[n], refs[n + 1], False):
            send.start()
        refs[-1][...] = jnp.zeros_like(refs[-1])

    arrays = [pltpu.with_memory_space_constraint(a, pltpu.HBM) for a in arrays]
    res = pl.pallas_call(
        body, name=name, in_specs=[_HBM] * n,
        out_specs=(_SEM, _SEM) + (_HBM,) * (n + len(lands)) + (pl.BlockSpec(memory_space=pltpu.VMEM),),
        out_shape=(pltpu.SemaphoreType.DMA((n_sems,)), pltpu.SemaphoreType.DMA((n_sems,)))
        + tuple(pltpu.HBM(a.shape, a.dtype) for a in arrays) + tuple(lands) + (_sds((8, HEAD_DIM), F32),),
        input_output_aliases={i: 2 + i for i in range(n)},
        compiler_params=pltpu.CompilerParams(has_side_effects=_EFFECT))(*arrays)
    return dict(name=name, mode=mode, sems=res[:2], bufs=res[2:-1]), res[-1]


def exchange_wait(handle, after):
    mode, bufs = handle['mode'], handle['bufs']
    nb = len(bufs)

    def body(*refs):
        for landed in _exchange_copies(mode, refs[:nb], refs[nb], refs[nb + 1], True):
            landed.wait_send()
            landed.wait_recv()

    res = pl.pallas_call(
        body, name=handle['name'].replace("start", "wait"), in_specs=[_HBM] * nb + [_SEM, _SEM, _ANY],
        out_specs=(_HBM,) * nb, out_shape=tuple(pltpu.HBM(a.shape, a.dtype) for a in bufs),
        input_output_aliases={i: i for i in range(nb)},
        compiler_params=pltpu.CompilerParams(has_side_effects=_EFFECT))(*bufs, *handle['sems'], after)
    if mode == "pass_on":
        return list(res)
    my_slot = _slot(*_place())
    filled = []
    for src, land in zip(res[:nb // 2], res[nb // 2:]):
        own = lax.dynamic_index_in_dim(src, my_slot, 0, keepdims=True) if mode == "scatter" else src[None]
        filled.append(lax.dynamic_update_slice(land, own, (my_slot,) + (0,) * (land.ndim - 1)))
    return filled


def _pad_lanes(a, width=HEAD_DIM):
    return jnp.pad(a, [(0, 0)] * (a.ndim - 1) + [(0, width - a.shape[-1])])


def _slabs(full, n):
    k = full.shape[0]
    return full.reshape(k, N_DEV, n).transpose(1, 0, 2)


def fox_fwd(a, w, p):
    qkv = mm_nn(a, w['qkv'], BF16, "fox_qkv")
    fl = mm_nn(a, w['f'], F32, "fox_flogit")
    bf = _pad_lanes(p['fox_b_f'])
    cum = fox_gate_fwd("fox_gate_fwd", fl, bf)
    cum_t = cum[:, :N_HEADS].T
    cums = (cum_t[:, :, None], cum_t[:, None, :])
    qn = rms_fwd("fox_qnorm", qkv, p['fox_q_gain'], HEAD_DIM, 512, HEAD_DIM, xcol=0, nh=N_HEADS)
    kn = rms_fwd("fox_knorm", qkv, p['fox_k_gain'], HEAD_DIM, 512, HEAD_DIM, xcol=N_HEADS, nh=N_HEADS)
    o, lse = attn_fwd("fox_attn_fwd", qn, kn, qkv, HEAD_DIM ** -0.5, voff=2 * N_HEADS, cum=cums, exact_o=True)
    return o, dict(a=a, qkv=qkv, fl=fl, bf=bf, cums=cums, qn=qn, kn=kn, o=o, lse=lse)


def fox_bwd(do, s, w, p):
    dqn, dkn, dv, dcs = attn_bwd("fox_attn_bwd", s['qn'], s['kn'], s['qkv'], s['o'], do, s['lse'], HEAD_DIM ** -0.5,
                                 voff=2 * N_HEADS, cum=s['cums'])
    dq, dgq = rms_bwd("fox_qnorm_bwd", s['qkv'], p['fox_q_gain'], dqn, HEAD_DIM, 512, HEAD_DIM, xcol=0, nh=N_HEADS,
                      out_dtype=BF16)
    dk, dgk = rms_bwd("fox_knorm_bwd", s['qkv'], p['fox_k_gain'], dkn, HEAD_DIM, 512, HEAD_DIM, xcol=N_HEADS,
                      nh=N_HEADS, out_dtype=BF16)
    dqkv = jnp.concatenate([dq, dk, dv.astype(BF16)], axis=1)
    dcum = _pad_lanes(-jnp.sum(dcs[:, :, 0, :], axis=1).T)
    dfl, dbf = fox_gate_bwd("fox_gate_bwd", s['fl'], s['bf'], dcum)
    da = mm_nt(dqkv, w['qkv'], F32, "fox_da_qkv")
    da = mm_nt(dfl, w['f'], F32, "fox_da_f", add=da)
    dw_qkv = mm_tn(s['a'], dqkv, 1, BF16, "fox_dw_qkv")[0]
    dw_f = mm_tn(s['a'], dfl, 1, BF16, "fox_dw_f")[0][:, :N_HEADS]
    dw = _slabs(jnp.concatenate([dw_qkv, dw_f], axis=1), 770)
    return da, dict(fox_w_in=dw), dict(fox_b_f=dbf[:, :N_HEADS], fox_q_gain=dgq, fox_k_gain=dgk)


def _rope_tables(positions):
    inv_freq = ROPE_THETA ** (-jnp.arange(0, MLA_ROPE, 2, dtype=F32) / MLA_ROPE)
    ang = positions.astype(F32)[:, None] * inv_freq
    cos, sin = jnp.cos(ang), jnp.sin(ang)
    return _pad_lanes(jnp.concatenate([cos, cos], axis=1)), _pad_lanes(jnp.concatenate([sin, sin], axis=1))


def mla_fwd(a, w, p):
    qg, kg = p['mla_q_gain'], p['mla_k_gain']
    gains = dict(qn=qg[:, :HEAD_DIM], qr=_pad_lanes(qg[:, HEAD_DIM:]), kn=kg[:, :HEAD_DIM], kr=_pad_lanes(kg[:, HEAD_DIM:]))
    cos, sin = _rope_tables(p['positions'])
    ccr = mm_nn(a, w['in'], F32, "mla_in")
    cqn = rms_fwd("mla_cq_norm", ccr, w['q_a_gain'], 512, 256, 512, xcol=0)
    ckvn = rms_fwd("mla_ckv_norm", ccr, w['kv_a_gain'], 512, 256, 512, xcol=1)
    qf = mm_nn(cqn, w['q_b'], F32, "mla_q_b")
    kvf = mm_nn(ckvn, w['kv_b'], F32, "mla_kv_b")
    q_nope = rms_fwd("mla_qnope_norm", qf, gains['qn'], HEAD_DIM, 512, HEAD_DIM, xcol=0, nh=N_HEADS)
    k_nope = rms_fwd("mla_knope_norm", kvf, gains['kn'], HEAD_DIM, 512, HEAD_DIM, xcol=0, nh=N_HEADS)
    q_rope = rope_fwd("mla_qrope", qf, gains['qr'], cos, sin, 512, N_HEADS, N_HEADS)
    k_rope = rope_fwd("mla_krope", ccr, gains['kr'], cos, sin, 512, 8, 1)
    scale = (HEAD_DIM + MLA_ROPE) ** -0.5
    o, lse = attn_fwd("mla_attn_fwd", q_nope, k_nope, kvf, scale, voff=N_HEADS, q2=q_rope, k2=k_rope)
    return o, dict(a=a, gains=gains, cos=cos, sin=sin, ccr=ccr, cqn=cqn, ckvn=ckvn, qf=qf, kvf=kvf, q_nope=q_nope,
                   k_nope=k_nope, q_rope=q_rope, k_rope=k_rope, o=o, lse=lse, scale=scale)


def mla_bwd(do, s, w, p):
    g = s['gains']
    dqn, dkn, dv, dq2, dk2 = attn_bwd("mla_attn_bwd", s['q_nope'], s['k_nope'], s['kvf'], s['o'], do, s['lse'], s['scale'],
                                      voff=N_HEADS, q2=s['q_rope'], k2=s['k_rope'])
    dqf_n, dg_qn = rms_bwd("mla_qnope_bwd", s['qf'], g['qn'], dqn, HEAD_DIM, 512, HEAD_DIM, xcol=0, nh=N_HEADS)
    dkf, dg_kn = rms_bwd("mla_knope_bwd", s['kvf'], g['kn'], dkn, HEAD_DIM, 512, HEAD_DIM, xcol=0, nh=N_HEADS)
    dqf_r, dg_qr = rope_bwd("mla_qrope_bwd", s['qf'], g['qr'], s['cos'], s['sin'], dq2, 512, N_HEADS, N_HEADS)
    dkr, dg_kr = rope_bwd("mla_krope_bwd", s['ccr'], g['kr'], s['cos'], s['sin'], dk2, 512, 8, 1)
    dqf = jnp.concatenate([dqf_n, dqf_r], axis=1)
    dkvf = jnp.concatenate([dkf, dv], axis=1)
    dcqn = mm_nt(dqf, w['q_b'], F32, "mla_dcq")
    dckvn = mm_nt(dkvf, w['kv_b'], F32, "mla_dckv")
    dw_qb = mm_tn(s['cqn'], dqf, 1, BF16, "mla_dw_qb")[0]
    dw_kvb = mm_tn(s['ckvn'], dkvf, 1, BF16, "mla_dw_kvb")[0]
    dcq, dg_qa = rms_bwd("mla_cq_bwd", s['ccr'], w['q_a_gain'], dcqn, 512, 256, 512, xcol=0)
    dckv, dg_kva = rms_bwd("mla_ckv_bwd", s['ccr'], w['kv_a_gain'], dckvn, 512, 256, 512, xcol=1)
    dccr = jnp.concatenate([dcq, dckv, dkr], axis=1)
    da = mm_nt(dccr, w['in'], F32, "mla_da")
    dw_in = mm_tn(s['a'], dccr, 1, BF16, "mla_dw_in")[0][:, :1088].reshape(N_DEV, 256, 1088)
    hp = 2
    nope = dw_qb[:, :2048].reshape(512, N_DEV, hp, HEAD_DIM)
    rope = dw_qb[:, 2048:].reshape(512, N_DEV, hp, HEAD_DIM)[..., :MLA_ROPE]
    dw_qb_s = jnp.concatenate([nope, rope], axis=-1).transpose(1, 0, 2, 3).reshape(N_DEV, 512, hp * 192)
    kk = dw_kvb[:, :2048].reshape(512, N_DEV, hp, HEAD_DIM)
    vv = dw_kvb[:, 2048:].reshape(512, N_DEV, hp, HEAD_DIM)
    dw_kvb_s = jnp.concatenate([kk, vv], axis=-1).transpose(1, 0, 2, 3).reshape(N_DEV, 512, hp * 256)
    small = dict(mla_q_a_gain=dg_qa, mla_kv_a_gain=dg_kva,
                 mla_q_gain=jnp.concatenate([dg_qn, dg_qr[:, :MLA_ROPE]], axis=1),
                 mla_k_gain=jnp.concatenate([dg_kn, dg_kr[:, :MLA_ROPE]], axis=1))
    return da, dict(mla_w_in=dw_in, mla_w_q_b=dw_qb_s, mla_w_kv_b=dw_kvb_s), small


def sb_fwd_layer(a, w, p):
    qkv = mm_nn(a, w['in'], BF16, "sb_qkv")
    qn = rms_fwd("sb_qnorm", qkv, p['sb_q_gain'], HEAD_DIM, 512, HEAD_DIM, xcol=0, nh=N_HEADS)
    kn = rms_fwd("sb_knorm", qkv, p['sb_k_gain'], HEAD_DIM, 512, HEAD_DIM, xcol=N_HEADS, nh=N_HEADS)
    o, carries = sb_fwd("sb_attn_fwd", qn, kn, qkv, HEAD_DIM ** -0.5, 2 * N_HEADS)
    return o, dict(a=a, qkv=qkv, qn=qn, kn=kn, carries=carries)


def sb_bwd_layer(do, s, w, p):
    dqn, dkn, dv = sb_bwd("sb_attn_bwd", s['qn'], s['kn'], s['qkv'], do, s['carries'], HEAD_DIM ** -0.5, 2 * N_HEADS)
    dq, dgq = rms_bwd("sb_qnorm_bwd", s['qkv'], p['sb_q_gain'], dqn, HEAD_DIM, 512, HEAD_DIM, xcol=0, nh=N_HEADS,
                      out_dtype=BF16)
    dk, dgk = rms_bwd("sb_knorm_bwd", s['qkv'], p['sb_k_gain'], dkn, HEAD_DIM, 512, HEAD_DIM, xcol=N_HEADS, nh=N_HEADS,
                      out_dtype=BF16)
    dqkv = jnp.concatenate([dq, dk, dv.astype(BF16)], axis=1)
    da = mm_nt(dqkv, w['in'], F32, "sb_da")
    dw = mm_tn(s['a'], dqkv, N_DEV, BF16, "sb_dw_in")
    return da, dict(sb_w_in=dw), dict(sb_q_gain=dgq, sb_k_gain=dgk)


def sgu_fwd(a, w, p):
    pre = mm_nn(a, w['in'], BF16, "sgu_in")
    u, vn = sgu_pre_fwd("sgu_pre_fwd", pre, w['v_gain'], 256)
    ws = p['sgu_w_s'][0]
    bs3 = p['sgu_b_s'][0][:, :, None]
    prod = sgu_mix_fwd("sgu_mix_fwd", vn, u, ws, bs3)
    return prod, dict(a=a, pre=pre, u=u, vn=vn, ws=ws, bs3=bs3)


def sgu_bwd(dprod, s, w, p):
    du, dvn, dws, dbs3 = sgu_mix_bwd("sgu_mix_bwd", s['vn'], s['u'], s['ws'], s['bs3'], dprod)
    dpre, dgv = sgu_pre_bwd("sgu_pre_bwd", s['pre'], w['v_gain'], du, dvn, 128)
    da = mm_nt(dpre, w['in'], F32, "sgu_da")
    dw = mm_tn(s['a'], dpre, N_DEV, BF16, "sgu_dw_in")
    return da, dict(sgu_w_in=dw), dict(sgu_v_gain=dgv, sgu_w_s=dws[None], sgu_b_s=dbs3[None, :, :, 0])


MIXERS = [("fox", fox_fwd, fox_bwd), ("mla", mla_fwd, mla_bwd), ("sb", sb_fwd_layer, sb_bwd_layer),
          ("sgu", sgu_fwd, sgu_bwd)]


def _pack_rows(arrays, row_mult=256):
    flat = jnp.concatenate([a.reshape(-1).astype(F32) for a in arrays])
    per = row_mult * HEAD_DIM
    total = -(-flat.shape[0] // per) * per
    return jnp.pad(flat, (0, total - flat.shape[0])).reshape(total // HEAD_DIM, HEAD_DIM)


def _unpack_rows(packed, shapes):
    flat = packed.reshape(-1)
    out, off = [], 0
    for shp in shapes:
        n = math.prod(shp)
        out.append(flat[off:off + n].reshape(shp))
        off += n
    return out


def _mixer_shards(i, p):
    name = MIXERS[i][0]
    bf = lambda a: a.astype(BF16)
    items = {'out': bf(p[name + '_w_out'][0])}
    if name == "fox":
        items['in'] = bf(p['fox_w_in'][0])
        items['small'] = _pack_rows([p[n] for n in SMALL_SHARDED], 8)
    elif name == "mla":
        items.update({'in': bf(p['mla_w_in'][0]), 'q_b': bf(p['mla_w_q_b'][0]), 'kv_b': bf(p['mla_w_kv_b'][0])})
    else:
        items['in'] = bf(p[name + '_w_in'][0])
    return items


def _ffn_shards(i, p):
    return {'up': p['ffn_w_up'][i].astype(BF16), 'down': p['ffn_w_down'][i].astype(BF16)}


def _assemble_ffn(got):
    return {'up': got['up'], 'down': got['down'].reshape(1, D_FF, -1)}


def _assemble_mixer(i, got, p):
    name = MIXERS[i][0]
    w = {'out': got['out'].reshape(1, -1, got['out'].shape[-1])} if 'out' in got else {}
    small = None
    if name == "fox":
        if 'in' in got:
            full = got['in'].transpose(1, 0, 2).reshape(got['in'].shape[1], -1)
            w['qkv'] = full[None, :, :3 * N_HEADS * HEAD_DIM]
            w['f'] = _pad_lanes(full[:, 3 * N_HEADS * HEAD_DIM:])[None]
        if 'small' in got:
            small_shapes = [p[n].shape for n in SMALL_SHARDED]
            parts = [_unpack_rows(got['small'][d], small_shapes) for d in range(N_DEV)]
            small = {n: jnp.concatenate([parts[d][k] for d in range(N_DEV)], axis=ax)
                     for k, (n, ax) in enumerate(SMALL_SHARDED.items())}
    elif name == "mla":
        w['in'] = _pad_lanes(got['in'].reshape(-1, 1088), 1152)[None]
        hp = 2
        qb = got['q_b'].reshape(N_DEV, 512, hp, 192).transpose(1, 0, 2, 3)
        nope = qb[..., :HEAD_DIM].reshape(512, -1)
        rope = _pad_lanes(qb[..., HEAD_DIM:]).reshape(512, -1)
        w['q_b'] = jnp.concatenate([nope, rope], axis=1)[None]
        kvb = got['kv_b'].reshape(N_DEV, 512, hp, 256).transpose(1, 0, 2, 3)
        w['kv_b'] = jnp.concatenate([kvb[..., :HEAD_DIM].reshape(512, -1), kvb[..., HEAD_DIM:].reshape(512, -1)], axis=1)[None]
    else:
        w['in'] = got['in']
    return w, small


def _train_step(p):
    x, target = p['x'][0], p['loss_target'][0]
    p = dict(p, positions=p['positions'][0])
    xi, yi, ci = _place()
    my_slot = _slot(xi, yi, ci)

    shards0 = _mixer_shards(0, p)
    got0 = {'in': all_gather("gather_first", [shards0.pop('in')])[0]}
    pending, order_token = {}, jnp.zeros((1, 1), F32)
    for i in range(DEPTH):
        for kind, shards in (("mix", _mixer_shards(i, p) if i else shards0), ("ffn", _ffn_shards(i, p))):
            handle, token = exchange_start(f"xstart_ag_{kind}{i}", "gather_chipwise", list(shards.values()))
            pending[kind, i] = (handle, list(shards))
            order_token = order_token + token[0:1, 0:1]

    def pass_on(kind, i, after):
        handle, keys = pending[kind, i]
        handle, token = exchange_start(f"xstart_pass_{kind}{i}", "pass_on", exchange_wait(handle, after))
        pending[kind, i] = (handle, keys)
        return token[0:1, 0:1]

    def gathered(kind, i, after):
        handle, keys = pending[kind, i]
        return dict(zip(keys, exchange_wait(handle, after)))

    h = x
    saved, weights = [], []
    small_full = None
    for i in range(DEPTH):
        if i == 0:
            w, _ = _assemble_mixer(0, got0, p)
            gain = p['mix_norm'][0:1] + order_token
        else:
            w, _ = _assemble_mixer(i, gathered("mix", i, h), p)
            gain = p['mix_norm'][i:i + 1] + pass_on("ffn", i, h)
            w.update(q_a_gain=small_full['mla_q_a_gain'], kv_a_gain=small_full['mla_kv_a_gain'],
                     v_gain=small_full['sgu_v_gain'])
        a = rms_fwd(f"mix_norm_{i}", h, gain, h.shape[1], 512, h.shape[1])
        mixed, s_mix = MIXERS[i][1](a, w, p)
        if i == 0:
            pass_on("mix", 0, mixed)
            pass_on("ffn", 0, mixed)
            rest, small_full = _assemble_mixer(0, gathered("mix", 0, mixed), p)
            w.update(rest)
            conv_w = small_full['ffn_conv_w']
        w.update(_assemble_ffn(gathered("ffn", i, mixed)))
        weights.append(w)
        h1 = mm_nn(mixed, w['out'], F32, f"mix_out_{i}", add=h)
        b = rms_fwd(f"ffn_norm_{i}", h1, p['ffn_norm'][i:i + 1], h.shape[1], 512, h.shape[1])
        up = mm_nn(b, w['up'], BF16, f"ffn_up_{i}")
        conv_b = p['ffn_conv_b'][i:i + 1]
        if i + 1 < DEPTH:
            conv_b = conv_b + pass_on("mix", i + 1, up)
        act = conv_fwd(f"ffn_conv_{i}", up, conv_w[i], conv_b)
        h2 = mm_nn(act, w['down'], F32, f"ffn_down_{i}", add=h1)
        saved.append(dict(h=h, mixed=mixed, s_mix=s_mix, h1=h1, b=b, up=up, act=act))
        h = h2

    dh16, dh, sq = loss_head("loss_head", h, target)
    loss = lax.psum(0.5 * jnp.sum(sq) / h.shape[1], ("x", "y", "c"))

    scatters = []
    layered = ('mix_norm', 'ffn_norm', 'ffn_conv_w', 'ffn_conv_b')
    early_names = [n for n in SMALL if not n.startswith('fox_')]
    late_names = [n for n in SMALL if n in layered or n.startswith('fox_')]
    small_g = {n: [None] * p[n].shape[0] for n in layered}
    early_token = jnp.zeros((1, 1), F32)
    for i in reversed(range(DEPTH)):
        w, s = weights[i], saved[i]
        name = MIXERS[i][0]
        dact = mm_nt(dh16, w['down'], BF16, f"ffn_dact_{i}")
        dw_down = mm_tn(s['act'], dh16, 1, BF16, f"ffn_dw_down_{i}").reshape(N_DEV, D_FF // N_DEV, -1)
        dup, dcwg, dcwv, dcbg, dcbv = conv_bwd(f"ffn_conv_bwd_{i}", s['up'], conv_w[i],
                                               p['ffn_conv_b'][i:i + 1] + early_token, dact)
        dw_up = mm_tn(s['b'], dup, N_DEV, BF16, f"ffn_dw_up_{i}")
        handle, token = exchange_start(f"xstart_rs_ffn{i}", "scatter", [dw_up, dw_down])
        scatters.append((handle, ['ffn_w_up', 'ffn_w_down']))
        db = mm_nt(dup, w['up'], F32, f"ffn_db_{i}")
        dh1_16, dh1, dg_ffn = rms_bwd(f"ffn_norm_bwd_{i}", s['h1'], p['ffn_norm'][i:i + 1] + token[0:1, 0:1], db,
                                      h.shape[1], 256, h.shape[1], resid=dh, also_bf16=True)
        dw_out = mm_tn(s['mixed'], dh1_16, 1, BF16, f"mix_dw_out_{i}").reshape(N_DEV, -1, h.shape[1])
        if i == 0:
            handle, token = exchange_start("xstart_rs_out0", "scatter", [dw_out])
            scatters.append((handle, [name + '_w_out']))
            dh1_16 = dh1_16 + token[0:1, 0:1].astype(BF16)
        dmix = mm_nt(dh1_16, w['out'], BF16, f"mix_dout_{i}")
        da, big_i, small_i = MIXERS[i][2](dmix, s['s_mix'], w, p)
        if i > 0:
            big_i[name + '_w_out'] = dw_out
        handle, token = exchange_start(f"xstart_rs_mix{i}", "scatter", list(big_i.values()))
        scatters.append((handle, list(big_i)))
        dh16, dh, dg_mix = rms_bwd(f"mix_norm_bwd_{i}", s['h'], p['mix_norm'][i:i + 1] + token[0:1, 0:1], da, h.shape[1],
                                   256, h.shape[1], resid=dh1, also_bf16=True)
        for k, v in small_i.items():
            small_g[k] = v
        small_g['mix_norm'][i], small_g['ffn_norm'][i] = dg_mix, dg_ffn
        small_g['ffn_conv_w'][i] = jnp.concatenate([dcwg, dcwv], axis=1)[None]
        small_g['ffn_conv_b'][i] = jnp.concatenate([dcbg, dcbv], axis=1)
        if i == 1:
            early = [jnp.concatenate(small_g[n][1:], axis=0) if n in layered else small_g[n] for n in early_names]
            early_handle, token = exchange_start("xstart_small_early", "gather", [_pack_rows(early)])
            early_token = token[0:1, 0:1]
    late = [small_g[n][0] if n in layered else small_g[n] for n in late_names]
    grad_x = dh[None]

    small_handle, small_token = exchange_start("xstart_small", "gather", [_pack_rows(late)])
    big = {}
    for handle, names in scatters[:-1]:
        for n, landed in zip(names, exchange_wait(handle, small_token)):
            big.setdefault(n, []).insert(0, landed)
    grads, deltas, new_m, new_v = {}, {}, {}, {}
    last_handle, last_names = scatters[-1]
    for n in BIG:
        if n not in last_names:
            grads[n], deltas[n], new_m[n], new_v[n] = adam_sum("adam_" + n, big[n], p[n], p['m_' + n], p['v_' + n])
    for n, landed in zip(last_names, exchange_wait(last_handle, new_v['ffn_w_up'])):
        grads[n], deltas[n], new_m[n], new_v[n] = adam_sum("adam_" + n, [landed], p[n], p['m_' + n], p['v_' + n])
    early_sum = _unpack_rows(sum_parts("sum_small_early", exchange_wait(early_handle, small_token)[0]),
                             [tuple(a.shape) for a in early])
    late_sum = _unpack_rows(sum_parts("sum_small_late", exchange_wait(small_handle, new_v[last_names[0]])[0]),
                            [tuple(a.shape) for a in late])
    summed = dict(zip(early_names, early_sum))
    for n, g in zip(late_names, late_sum):
        summed[n] = jnp.concatenate([g, summed[n]], axis=0) if n in layered else g
    mine = []
    for n in SMALL:
        g = summed[n]
        if n in SMALL_SHARDED:
            ax = SMALL_SHARDED[n]
            g = lax.dynamic_slice_in_dim(g, my_slot * p[n].shape[ax], p[n].shape[ax], axis=ax)
        mine.append(g)
    shapes = [p[n].shape for n in SMALL]
    packed = [_pack_rows(arrs) for arrs in ([p[n] for n in SMALL], mine, [p['m_' + n] for n in SMALL], [p['v_' + n] for n in SMALL])]
    d_s, m_s, v_s = adam_flat("adam_small", *packed)
    for n, g, d, m, v in zip(SMALL, mine, _unpack_rows(d_s, shapes), _unpack_rows(m_s, shapes), _unpack_rows(v_s, shapes)):
        grads[n], deltas[n], new_m[n], new_v[n] = g, d, m, v

    return (loss, grad_x, *[grads[n] for n in WEIGHTS], *[deltas[n] for n in WEIGHTS], *[new_m[n] for n in WEIGHTS],
            *[new_v[n] for n in WEIGHTS])


def kernel(x, positions, mix_norm, ffn_norm, fox_w_in, fox_b_f, fox_q_gain, fox_k_gain, fox_w_out, mla_w_in, mla_q_a_gain, mla_kv_a_gain, mla_w_q_b, mla_w_kv_b, mla_q_gain, mla_k_gain, mla_w_out, sb_w_in, sb_q_gain, sb_k_gain, sb_w_out, sgu_w_in, sgu_v_gain, sgu_w_s, sgu_b_s, sgu_w_out, ffn_w_up, ffn_conv_w, ffn_conv_b, ffn_w_down, loss_target, m_mix_norm, m_ffn_norm, m_fox_w_in, m_fox_b_f, m_fox_q_gain, m_fox_k_gain, m_fox_w_out, m_mla_w_in, m_mla_q_a_gain, m_mla_kv_a_gain, m_mla_w_q_b, m_mla_w_kv_b, m_mla_q_gain, m_mla_k_gain, m_mla_w_out, m_sb_w_in, m_sb_q_gain, m_sb_k_gain, m_sb_w_out, m_sgu_w_in, m_sgu_v_gain, m_sgu_w_s, m_sgu_b_s, m_sgu_w_out, m_ffn_w_up, m_ffn_conv_w, m_ffn_conv_b, m_ffn_w_down, v_mix_norm, v_ffn_norm, v_fox_w_in, v_fox_b_f, v_fox_q_gain, v_fox_k_gain, v_fox_w_out, v_mla_w_in, v_mla_q_a_gain, v_mla_kv_a_gain, v_mla_w_q_b, v_mla_w_kv_b, v_mla_q_gain, v_mla_k_gain, v_mla_w_out, v_sb_w_in, v_sb_q_gain, v_sb_k_gain, v_sb_w_out, v_sgu_w_in, v_sgu_v_gain, v_sgu_w_s, v_sgu_b_s, v_sgu_w_out, v_ffn_w_up, v_ffn_conv_w, v_ffn_conv_b, v_ffn_w_down):
    args = locals()
    names = ['x', 'positions'] + WEIGHTS + ['loss_target'] + ['m_' + n for n in WEIGHTS] + ['v_' + n for n in WEIGHTS]
    return _train_step({n: args[n] for n in names})
```

```python
import functools
import math

import jax
import jax.numpy as jnp
from jax import lax
from jax.experimental import pallas as pl
from jax.experimental.pallas import tpu as pltpu

F32 = jnp.float32
BF16 = jnp.bfloat16
MESH = pl.DeviceIdType.MESH

N_DEV = 8
N_HEADS = 16
HEAD_DIM = 128
EPS = 1e-6
DEPTH = 4
D_FF = 5632
MLA_ROPE = 64
ROPE_THETA = 10000.0
VMEM_LIMIT = 48 * 1024 * 1024

ADAM_LR, ADAM_B1, ADAM_B2, ADAM_EPS, ADAM_WD, ADAM_STEP = 0.001, 0.9, 0.999, 1e-08, 0.01, 10

WEIGHTS = ['mix_norm', 'ffn_norm', 'fox_w_in', 'fox_b_f', 'fox_q_gain', 'fox_k_gain', 'fox_w_out', 'mla_w_in',
           'mla_q_a_gain', 'mla_kv_a_gain', 'mla_w_q_b', 'mla_w_kv_b', 'mla_q_gain', 'mla_k_gain', 'mla_w_out',
           'sb_w_in', 'sb_q_gain', 'sb_k_gain', 'sb_w_out', 'sgu_w_in', 'sgu_v_gain', 'sgu_w_s', 'sgu_b_s',
           'sgu_w_out', 'ffn_w_up', 'ffn_conv_w', 'ffn_conv_b', 'ffn_w_down']
BIG = ['fox_w_in', 'fox_w_out', 'mla_w_in', 'mla_w_q_b', 'mla_w_kv_b', 'mla_w_out', 'sb_w_in', 'sb_w_out',
       'sgu_w_in', 'sgu_w_out', 'ffn_w_up', 'ffn_w_down']
SMALL = [w for w in WEIGHTS if w not in BIG]
SMALL_SHARDED = {'mla_q_a_gain': 1, 'mla_kv_a_gain': 1, 'sgu_v_gain': 1, 'ffn_conv_w': 2}


def _cparams(n_grid):
    return pltpu.CompilerParams(dimension_semantics=("arbitrary",) * n_grid, vmem_limit_bytes=VMEM_LIMIT)


def _pick(n, cap):
    best = None
    t = 128
    while t <= min(n, cap):
        if n % t == 0:
            best = t
        t += 128
    return best if best is not None else n


def _mm_call(name, a, b, out_shape, a_spec, b_spec, o_spec, grid, dims, acc_shape, add=None):
    nk = grid[2]

    def body(*refs):
        a_ref, b_ref = refs[:2]
        add_ref = refs[2] if add is not None else None
        o_ref = refs[3] if add is not None else refs[2]
        prod = lax.dot_general(a_ref[...].astype(BF16), b_ref[...].astype(BF16), (dims, ((), ())),
                               preferred_element_type=F32)

        def finish(r):
            if add_ref is not None:
                r = r + add_ref[...].astype(F32)
            o_ref[...] = r.astype(o_ref.dtype)

        if nk == 1:
            finish(prod)
            return
        acc = refs[-1]
        k = pl.program_id(2)

        @pl.when(k == 0)
        def _():
            acc[...] = prod

        @pl.when(k > 0)
        def _():
            acc[...] += prod

        @pl.when(k == nk - 1)
        def _():
            finish(acc[...])

    ins = [a, b] + ([] if add is None else [add])
    in_specs = [a_spec, b_spec] + ([] if add is None else [o_spec])
    return pl.pallas_call(body, grid=grid, in_specs=in_specs, out_specs=o_spec, out_shape=out_shape,
                          scratch_shapes=[] if nk == 1 else [pltpu.VMEM(acc_shape, F32)], name=name,
                          compiler_params=_cparams(3))(*ins)


def mm_nn(a, b3, out_dtype, name, add=None, joff=0, nj=None):
    m, kk = a.shape
    _, kb, n = b3.shape
    assert kb == kk
    nj = b3.shape[0] - joff if nj is None else nj
    tn, tk = _pick(n, 1536), _pick(kk, 2048)
    tm = _pick(m, 512 if a.dtype == F32 and tk > 1024 else 1024)
    nb = n // tn
    return _mm_call(
        name, a, b3, jax.ShapeDtypeStruct((m, nj * n), out_dtype),
        pl.BlockSpec((tm, tk), lambda i, c, k: (i, k)),
        pl.BlockSpec((None, tk, tn), lambda i, c, k: (joff + c // nb, k, c % nb)),
        pl.BlockSpec((tm, tn), lambda i, c, k: (i, c)),
        (m // tm, nj * nb, kk // tk), ((1,), (0,)), (tm, tn), add=add)


def _grouped_cols(x, tile, rows_tile, rows_arg):
    pick = {'i': lambda i, o, c: (i, c), 'k': lambda o, c, k: (k, c)}[rows_arg]
    if x.ndim == 2:
        return pl.BlockSpec((rows_tile, tile), pick)
    per_group = x.shape[2] // tile
    assert x.shape[2] % tile == 0

    def index(*g):
        r, c = pick(*g)
        return c // per_group, r, c % per_group

    return pl.BlockSpec((None, rows_tile, tile), index)


def mm_nt(a, b3, out_dtype, name, add=None, joff=0, nj=None):
    m, na = (a.shape[1], a.shape[0] * a.shape[2]) if a.ndim == 3 else a.shape
    _, ko, n = b3.shape
    nj = b3.shape[0] - joff if nj is None else nj
    assert na == nj * n
    to, tn = _pick(ko, 1024), _pick(n, 2048)
    tm = _pick(m, 512 if a.dtype == F32 and tn > 1024 else 1024)
    nb = n // tn
    return _mm_call(
        name, a, b3, jax.ShapeDtypeStruct((m, ko), out_dtype),
        _grouped_cols(a, tn, tm, 'i'),
        pl.BlockSpec((None, to, tn), lambda i, o, c: (joff + c // nb, o, c % nb)),
        pl.BlockSpec((tm, to), lambda i, o, c: (i, o)),
        (m // tm, ko // to, nj * nb), ((1,), (1,)), (tm, to), add=add)


def mm_tn(a, b, nj, out_dtype, name):
    s, ko = a.shape
    sb, nb_tot = (b.shape[1], b.shape[0] * b.shape[2]) if b.ndim == 3 else b.shape
    assert sb == s and nb_tot % nj == 0
    n = nb_tot // nj
    to, tn = _pick(ko, 1024), _pick(n, 1536)
    ts = _pick(s, 1024 if F32 in (a.dtype, b.dtype) else 2048)
    nb = n // tn
    return _mm_call(
        name, a, b, jax.ShapeDtypeStruct((nj, ko, n), out_dtype),
        pl.BlockSpec((ts, to), lambda o, c, k: (k, o)),
        _grouped_cols(b, tn, ts, 'k'),
        pl.BlockSpec((None, to, tn), lambda o, c, k: (c // nb, o, c % nb)),
        (ko // to, nj * nb, s // ts), ((0,), (0,)), (to, tn))


def blockk(name, fn, grid, ins, outs):
    n_in = len(ins)
    accs = [o[2] for o in outs]

    def body(*refs):
        vals = fn(*[r[...] for r in refs[:n_in]])
        if not isinstance(vals, (tuple, list)):
            vals = (vals,)
        i, j = pl.program_id(0), pl.program_id(1)
        for r, v, acc in zip(refs[n_in:], vals, accs):
            if acc is None:
                r[...] = v.astype(r.dtype)
            else:
                first = (j == 0) if acc == 'inner' else jnp.logical_and(i == 0, j == 0)

                @pl.when(first)
                def _(r=r, v=v):
                    r[...] = v.astype(r.dtype)

                @pl.when(jnp.logical_not(first))
                def _(r=r, v=v):
                    r[...] += v.astype(r.dtype)

    res = pl.pallas_call(body, grid=grid, in_specs=[s for _, s in ins], out_specs=[o[1] for o in outs],
                         out_shape=[o[0] for o in outs], name=name, compiler_params=_cparams(2))(*[a for a, _ in ins])
    return res


def _sds(shape, dtype):
    return jax.ShapeDtypeStruct(tuple(shape), dtype)


def _rows(tr, w, col=0):
    if col == 'j':
        return pl.BlockSpec((tr, w), lambda i, j: (i, j))
    if callable(col):
        return pl.BlockSpec((tr, w), lambda i, j: (i, col(j)))
    return pl.BlockSpec((tr, w), lambda i, j: (i, col))


def _whole(shape):
    nd = len(shape)
    return pl.BlockSpec(tuple(shape), lambda i, j: (0,) * nd)


def _rms(x, g, n):
    ms = jnp.sum(x * x, axis=-1, keepdims=True) * (1.0 / n)
    return x * lax.rsqrt(ms + EPS) * g


def _sig(x):
    return 1.0 / (1.0 + jnp.exp(-x))


def _gelu(x):
    return 0.5 * x * (1.0 + jnp.tanh(math.sqrt(2.0 / math.pi) * (x + 0.044715 * (x * x * x))))


def _lane_iota(shape):
    return lax.broadcasted_iota(jnp.int32, shape, len(shape) - 1)


def _rope(x, cos, sin):
    lane = _lane_iota(x.shape)
    half = MLA_ROPE // 2
    swapped = jnp.where(lane < half, pltpu.roll(x, HEAD_DIM - half, 1), pltpu.roll(x, half, 1))
    sign = jnp.where(lane < half, -1.0, 1.0)
    return x * cos + swapped * (sin * sign)


def _rope_t(dy, cos, sin):
    lane = _lane_iota(dy.shape)
    half = MLA_ROPE // 2
    t = dy * sin
    swapped = jnp.where(lane < half, pltpu.roll(t, HEAD_DIM - half, 1), pltpu.roll(t, half, 1))
    sign = jnp.where(lane < half, 1.0, -1.0)
    return dy * cos + swapped * sign


def rms_fwd(name, x, gain, n, tr, width, xcol=0, nh=1, out_dtype=BF16, out_cols=None):
    r = x.shape[0]
    tr = r if nh > 1 else tr
    out_cols = width * nh if out_cols is None else out_cols
    xspec = _rows(tr, width, (lambda j: xcol + j) if nh > 1 else xcol)
    ospec = _rows(tr, width, 'j' if nh > 1 else 0)
    return blockk(name, lambda xb, g: _rms(xb.astype(F32), g, n), (r // tr, nh),
                  [(x, xspec), (gain, _whole(gain.shape))], [(_sds((r, out_cols), out_dtype), ospec, None)])[0]


def rms_bwd(name, x, gain, dy, n, tr, width, xcol=0, nh=1, dycol=0, resid=None, out_dtype=F32, also_bf16=False):
    r = x.shape[0]
    tr = r if nh > 1 else tr
    xspec = _rows(tr, width, (lambda j: xcol + j) if nh > 1 else xcol)
    dyspec = _rows(tr, width, (lambda j: dycol + j) if nh > 1 else dycol)
    ospec = _rows(tr, width, 'j' if nh > 1 else 0)

    def fn(xb, g, dyb, *rest):
        _, vjp = jax.vjp(lambda a, b: _rms(a, b, n), xb.astype(F32), g)
        dx, dg = vjp(dyb.astype(F32))
        if rest:
            dx = dx + rest[0].astype(F32)
        return ((dx,) if also_bf16 else ()) + (dx, dg)

    ins = [(x, xspec), (gain, _whole(gain.shape)), (dy, dyspec)]
    if resid is not None:
        ins.append((resid, ospec))
    outs = [(_sds((r, width * nh), out_dtype), ospec, None), (_sds(gain.shape, F32), _whole(gain.shape), 'all')]
    if also_bf16:
        outs.insert(0, (_sds((r, width * nh), BF16), ospec, None))
    return blockk(name, fn, (r // tr, nh), ins, outs)


def _nt(a, b):
    return lax.dot_general(a, b, (((1,), (1,)), ((), ())), preferred_element_type=F32)


def _tn(a, b):
    return lax.dot_general(a, b, (((0,), (0,)), ((), ())), preferred_element_type=F32)


def _nn(a, b):
    return lax.dot_general(a, b, (((1,), (0,)), ((), ())), preferred_element_type=F32)


ATTN_BLOCK = 512
ATTN_HEADS = 4
ATTN_GRID_HEADS = N_HEADS // ATTN_HEADS


def _head_cols(off):
    assert off % ATTN_HEADS == 0
    return off // ATTN_HEADS


def _hd(ref, hh):
    return ref[:, hh * HEAD_DIM:(hh + 1) * HEAD_DIM]


def _attn_specs(tq, qoff, koff, voff, extra, bias, q2off):
    w = HEAD_DIM * ATTN_HEADS
    qc, kc, vc, q2c = (_head_cols(o) for o in (qoff, koff, voff, q2off))
    specs = [pl.BlockSpec((tq, w), lambda h, i, j: (i, qc + h)),
             pl.BlockSpec((tq, w), lambda h, i, j: (jnp.minimum(i, j), kc + h)),
             pl.BlockSpec((tq, w), lambda h, i, j: (jnp.minimum(i, j), vc + h))]
    if extra:
        specs += [pl.BlockSpec((tq, w), lambda h, i, j: (i, q2c + h)),
                  pl.BlockSpec((tq, HEAD_DIM), lambda h, i, j: (jnp.minimum(i, j), 0))]
    if bias:
        specs += [pl.BlockSpec((ATTN_HEADS, tq, 1), lambda h, i, j: (h, i, 0)),
                  pl.BlockSpec((ATTN_HEADS, 1, tq), lambda h, i, j: (h, 0, jnp.minimum(i, j)))]
    return specs


def _scores(q, k, q2, k2, cc, cr, scale, tq, diagonal):
    s = _nt(q.astype(BF16), k.astype(BF16))
    if q2 is not None:
        s = s + _nt(q2.astype(BF16), k2.astype(BF16))
    s = s * scale
    if cc is not None:
        s = s + (cc - cr)
    if not diagonal:
        return s, None
    return s, lax.broadcasted_iota(jnp.int32, (tq, tq), 1) <= lax.broadcasted_iota(jnp.int32, (tq, tq), 0)


def _on_blocks(qi, kj, step):
    @pl.when(kj < qi)
    def _():
        step(False)

    @pl.when(kj == qi)
    def _():
        step(True)


def attn_fwd(name, q, k, v, scale, *, qoff=0, koff=0, voff=0, q2=None, k2=None, q2off=0, cum=None, tq=ATTN_BLOCK,
             exact_o=False):
    s_len = q.shape[0]
    nq = s_len // tq
    extra, bias = q2 is not None, cum is not None
    n_in = 3 + 2 * extra + 2 * bias

    def body(*refs):
        q_ref, k_ref, v_ref = refs[:3]
        p = 3
        q2_ref = k2_ref = cc_ref = cr_ref = None
        if extra:
            q2_ref, k2_ref = refs[p:p + 2]
            p += 2
        if bias:
            cc_ref, cr_ref = refs[p:p + 2]
            p += 2
        o_ref, lse_ref, m_s, l_s, acc_s = refs[p:]
        qi, kj = pl.program_id(1), pl.program_id(2)

        @pl.when(kj == 0)
        def _():
            m_s[...] = jnp.full_like(m_s, -jnp.inf)
            l_s[...] = jnp.zeros_like(l_s)
            acc_s[...] = jnp.zeros_like(acc_s)

        def step(diagonal):
            for hh in range(ATTN_HEADS):
                s, allowed = _scores(_hd(q_ref, hh), _hd(k_ref, hh), _hd(q2_ref, hh) if extra else None,
                                     k2_ref[...] if extra else None, cc_ref[hh] if bias else None,
                                     cr_ref[hh] if bias else None, scale, tq, diagonal)
                if diagonal:
                    s = jnp.where(allowed, s, -jnp.inf)
                m_old = m_s[hh]
                m_new = jnp.maximum(m_old, jnp.max(s, axis=-1, keepdims=True))
                alpha = jnp.exp(m_old - m_new)
                pr = jnp.exp(s - m_new)
                l_s[hh] = alpha * l_s[hh] + jnp.sum(pr, axis=-1, keepdims=True)
                vb = _hd(v_ref, hh).astype(BF16)
                pv = _nn(pr.astype(BF16), vb)
                if exact_o:
                    pv = pv + _nn((pr - pr.astype(BF16).astype(F32)).astype(BF16), vb)
                acc_s[hh] = alpha * acc_s[hh] + pv
                m_s[hh] = m_new

        _on_blocks(qi, kj, step)

        @pl.when(kj == qi)
        def _():
            for hh in range(ATTN_HEADS):
                o_ref[:, hh * HEAD_DIM:(hh + 1) * HEAD_DIM] = (acc_s[hh] / l_s[hh]).astype(o_ref.dtype)
                lse_ref[hh] = m_s[hh] + jnp.log(l_s[hh])

    ins = [q, k, v] + ([q2, k2] if extra else []) + (list(cum) if bias else [])
    d, w = HEAD_DIM, HEAD_DIM * ATTN_HEADS
    return pl.pallas_call(
        body, grid=(ATTN_GRID_HEADS, nq, nq), in_specs=_attn_specs(tq, qoff, koff, voff, extra, bias, q2off),
        out_specs=[pl.BlockSpec((tq, w), lambda h, i, j: (i, h)),
                   pl.BlockSpec((ATTN_HEADS, tq, 1), lambda h, i, j: (h, i, 0))],
        out_shape=[_sds((s_len, N_HEADS * d), F32 if exact_o else BF16), _sds((N_HEADS, s_len, 1), F32)],
        scratch_shapes=[pltpu.VMEM((ATTN_HEADS, tq, 1), F32), pltpu.VMEM((ATTN_HEADS, tq, 1), F32),
                        pltpu.VMEM((ATTN_HEADS, tq, d), F32)],
        name=name, compiler_params=_cparams(3))(*ins)


def attn_bwd(name, q, k, v, o, do, lse, scale, *, qoff=0, koff=0, voff=0, q2=None, k2=None, q2off=0, cum=None,
             tq=ATTN_BLOCK):
    s_len = q.shape[0]
    nq = s_len // tq
    extra, bias = q2 is not None, cum is not None
    d = HEAD_DIM
    n_in = 6 + 2 * extra + 2 * bias

    def body(*refs):
        q_ref, k_ref, v_ref = refs[:3]
        p = 3
        q2_ref = k2_ref = cc_ref = cr_ref = None
        if extra:
            q2_ref, k2_ref = refs[p:p + 2]
            p += 2
        if bias:
            cc_ref, cr_ref = refs[p:p + 2]
            p += 2
        o_ref, do_ref, lse_ref = refs[p:p + 3]
        p += 3
        dq_ref, dk_ref, dv_ref = refs[p:p + 3]
        p += 3
        dq2_ref = dk2_ref = dcs_ref = None
        if extra:
            dq2_ref, dk2_ref = refs[p:p + 2]
            p += 2
        if bias:
            dcs_ref = refs[p]
            p += 1
        dq_s, delta_s = refs[p:p + 2]
        dq2_s = refs[p + 2] if extra else None
        h, qi, kj = pl.program_id(0), pl.program_id(1), pl.program_id(2)

        @pl.when(jnp.logical_and(qi == 0, kj == 0))
        def _():
            dk_ref[...] = jnp.zeros_like(dk_ref)
            dv_ref[...] = jnp.zeros_like(dv_ref)

        if extra:
            @pl.when(jnp.logical_and(h == 0, jnp.logical_and(qi == 0, kj == 0)))
            def _():
                dk2_ref[...] = jnp.zeros_like(dk2_ref)

        @pl.when(kj == 0)
        def _():
            dq_s[...] = jnp.zeros_like(dq_s)
            if extra:
                dq2_s[...] = jnp.zeros_like(dq2_s)
            for hh in range(ATTN_HEADS):
                delta_s[hh] = jnp.sum(_hd(do_ref, hh).astype(F32) * _hd(o_ref, hh).astype(F32), axis=-1, keepdims=True)

        if bias:
            @pl.when(kj > qi)
            def _():
                dcs_ref[...] = jnp.zeros_like(dcs_ref)

        def step(diagonal):
            ks = pl.ds(pl.multiple_of(kj * tq, tq), tq)
            for hh in range(ATTN_HEADS):
                cols = slice(hh * HEAD_DIM, (hh + 1) * HEAD_DIM)
                qh, kh = _hd(q_ref, hh).astype(BF16), _hd(k_ref, hh).astype(BF16)
                q2h = _hd(q2_ref, hh).astype(BF16) if extra else None
                k2h = k2_ref[...].astype(BF16) if extra else None
                s, allowed = _scores(qh, kh, q2h, k2h, cc_ref[hh] if bias else None, cr_ref[hh] if bias else None,
                                     scale, tq, diagonal)
                pr = jnp.exp(s - lse_ref[hh])
                if diagonal:
                    pr = jnp.where(allowed, pr, 0.0)
                dob = _hd(do_ref, hh).astype(BF16)
                dp = _nt(dob, _hd(v_ref, hh).astype(BF16))
                ds = pr * (dp - delta_s[hh])
                dsb = (ds * scale).astype(BF16)
                dq_s[hh] += _nn(dsb, kh)
                dk_ref[ks, cols] += _tn(dsb, qh)
                dv_ref[ks, cols] += _tn(pr.astype(BF16), dob)
                if extra:
                    dq2_s[hh] += _nn(dsb, k2h)
                    dk2_ref[ks, :] += _tn(dsb, q2h)
                if bias:
                    dcs_ref[hh] = jnp.sum(ds, axis=0, keepdims=True)

        _on_blocks(qi, kj, step)

        @pl.when(kj == qi)
        def _():
            for hh in range(ATTN_HEADS):
                cols = slice(hh * HEAD_DIM, (hh + 1) * HEAD_DIM)
                dq_ref[:, cols] = dq_s[hh]
                if extra:
                    dq2_ref[:, cols] = dq2_s[hh]

    w = HEAD_DIM * ATTN_HEADS
    ins = [q, k, v] + ([q2, k2] if extra else []) + (list(cum) if bias else []) + [o, do, lse]
    in_specs = _attn_specs(tq, qoff, koff, voff, extra, bias, q2off) + [
        pl.BlockSpec((tq, w), lambda h, i, j: (i, h)), pl.BlockSpec((tq, w), lambda h, i, j: (i, h)),
        pl.BlockSpec((ATTN_HEADS, tq, 1), lambda h, i, j: (h, i, 0))]
    full = _sds((s_len, N_HEADS * d), F32)
    out_shape = [full, full, full]
    out_specs = [pl.BlockSpec((tq, w), lambda h, i, j: (i, h)), pl.BlockSpec((s_len, w), lambda h, i, j: (0, h)),
                 pl.BlockSpec((s_len, w), lambda h, i, j: (0, h))]
    scratch = [pltpu.VMEM((ATTN_HEADS, tq, d), F32), pltpu.VMEM((ATTN_HEADS, tq, 1), F32)]
    if extra:
        out_shape += [full, _sds((s_len, d), F32)]
        out_specs += [pl.BlockSpec((tq, w), lambda h, i, j: (i, h)), pl.BlockSpec((s_len, d), lambda h, i, j: (0, 0))]
        scratch.append(pltpu.VMEM((ATTN_HEADS, tq, d), F32))
    if bias:
        out_shape.append(_sds((N_HEADS, nq, 1, s_len), F32))
        out_specs.append(pl.BlockSpec((ATTN_HEADS, None, 1, tq), lambda h, i, j: (h, i, 0, j)))
    return pl.pallas_call(body, grid=(ATTN_GRID_HEADS, nq, nq), in_specs=in_specs, out_specs=out_specs,
                          out_shape=out_shape, scratch_shapes=scratch, name=name, compiler_params=_cparams(3))(*ins)


def _sb_terms(q, k, scale, tq, diagonal):
    z = _nt(q.astype(BF16), k.astype(BF16)) * scale
    lg = jnp.log(1.0 + jnp.exp(-jnp.abs(z)))
    log_keep = -(jnp.maximum(z, 0.0) + lg)
    log_beta = jnp.minimum(z, 0.0) - lg
    if not diagonal:
        return None, log_keep, log_beta
    strict = lax.broadcasted_iota(jnp.int32, (tq, tq), 1) < lax.broadcasted_iota(jnp.int32, (tq, tq), 0)
    return strict, jnp.where(strict, log_keep, 0.0), log_beta


def _tri(tq, pred):
    a = lax.broadcasted_iota(jnp.int32, (tq, tq), 0)
    b = lax.broadcasted_iota(jnp.int32, (tq, tq), 1)
    return jnp.where(pred(a, b), 1.0, 0.0).astype(BF16)


SUM_CHUNK = 256


def _lane_sums(x, later):
    n = x.shape[1]
    chunk = min(SUM_CHUNK, n)
    tri = _tri(chunk, (lambda m, j: m > j) if later else (lambda m, j: m < j))
    order = range(n // chunk - 1, -1, -1) if later else range(n // chunk)
    pieces, carry = [None] * (n // chunk), None
    for cidx in order:
        xc = x[:, cidx * chunk:(cidx + 1) * chunk]
        hi = xc.astype(BF16)
        local = _nn(hi, tri) + _nn((xc - hi.astype(F32)).astype(BF16), tri)
        total = jnp.sum(xc, axis=-1, keepdims=True)
        pieces[cidx] = local if carry is None else local + carry
        carry = total if carry is None else carry + total
    return jnp.concatenate(pieces, axis=1), carry


def sb_fwd(name, qn, kn, qkv, scale, voff, tq=ATTN_BLOCK):
    s_len = qn.shape[0]
    nq = s_len // tq
    d = HEAD_DIM

    def body(q_ref, k_ref, v_ref, o_ref, car_ref, ca_s, acc_s):
        qi, kj = pl.program_id(1), pl.program_id(2)

        @pl.when(kj == 0)
        def _():
            ca_s[...] = jnp.zeros_like(ca_s)
            acc_s[...] = jnp.zeros_like(acc_s)

        def step(diagonal):
            for hh in range(ATTN_HEADS):
                strict, log_keep, log_beta = _sb_terms(_hd(q_ref, hh), _hd(k_ref, hh), scale, tq, diagonal)
                ca = ca_s[hh]
                car_ref[hh] = ca
                after, total = _lane_sums(log_keep, True)
                a = jnp.exp(log_beta + (after + ca))
                if diagonal:
                    a = jnp.where(strict, a, 0.0)
                acc_s[hh] += _nn(a.astype(BF16), _hd(v_ref, hh).astype(BF16))
                ca_s[hh] = ca + total

        @pl.when(kj == 0)
        def _():
            step(True)

        @pl.when(jnp.logical_and(kj > 0, kj <= qi))
        def _():
            step(False)

        @pl.when(kj == qi)
        def _():
            for hh in range(ATTN_HEADS):
                o_ref[:, hh * HEAD_DIM:(hh + 1) * HEAD_DIM] = acc_s[hh].astype(o_ref.dtype)

    kblk = lambda i, j: jnp.maximum(i - j, 0)
    w, vc = HEAD_DIM * ATTN_HEADS, _head_cols(voff)
    return pl.pallas_call(
        body, grid=(ATTN_GRID_HEADS, nq, nq),
        in_specs=[pl.BlockSpec((tq, w), lambda h, i, j: (i, h)), pl.BlockSpec((tq, w), lambda h, i, j: (kblk(i, j), h)),
                  pl.BlockSpec((tq, w), lambda h, i, j: (kblk(i, j), vc + h))],
        out_specs=[pl.BlockSpec((tq, w), lambda h, i, j: (i, h)),
                   pl.BlockSpec((ATTN_HEADS, None, tq, 1), lambda h, i, j: (h, kblk(i, j), i, 0))],
        out_shape=[_sds((s_len, N_HEADS * d), BF16), _sds((N_HEADS, nq, s_len, 1), F32)],
        scratch_shapes=[pltpu.VMEM((ATTN_HEADS, tq, 1), F32), pltpu.VMEM((ATTN_HEADS, tq, d), F32)],
        name=name, compiler_params=_cparams(3))(qn, kn, qkv)


def sb_bwd(name, qn, kn, qkv, do, carries, scale, voff, tq=ATTN_BLOCK):
    s_len = qn.shape[0]
    nq = s_len // tq
    d = HEAD_DIM

    def body(q_ref, k_ref, v_ref, do_ref, car_ref, dq_ref, dk_ref, dv_ref, dq_s, cg_s):
        qi, kj = pl.program_id(1), pl.program_id(2)

        @pl.when(jnp.logical_and(qi == 0, kj == 0))
        def _():
            dk_ref[...] = jnp.zeros_like(dk_ref)
            dv_ref[...] = jnp.zeros_like(dv_ref)

        @pl.when(kj == 0)
        def _():
            dq_s[...] = jnp.zeros_like(dq_s)
            cg_s[...] = jnp.zeros_like(cg_s)

        def step(diagonal):
            ks = pl.ds(pl.multiple_of(kj * tq, tq), tq)
            for hh in range(ATTN_HEADS):
                cols = slice(hh * HEAD_DIM, (hh + 1) * HEAD_DIM)
                qh, kh = _hd(q_ref, hh).astype(BF16), _hd(k_ref, hh).astype(BF16)
                strict, log_keep, log_beta = _sb_terms(qh, kh, scale, tq, diagonal)
                after, _ = _lane_sums(log_keep, True)
                a = jnp.exp(log_beta + (after + car_ref[hh]))
                if diagonal:
                    a = jnp.where(strict, a, 0.0)
                dob = _hd(do_ref, hh).astype(BF16)
                g = a * _nt(dob, _hd(v_ref, hh).astype(BF16))
                cg = cg_s[hh]
                before, total = _lane_sums(g, False)
                big_g = before + cg
                cg_s[hh] = cg + total
                beta = jnp.exp(log_beta)
                dz = g * (1.0 - beta) - big_g * beta
                if diagonal:
                    dz = jnp.where(strict, dz, 0.0)
                dzb = (dz * scale).astype(BF16)
                dq_s[hh] += _nn(dzb, kh)
                dk_ref[ks, cols] += _tn(dzb, qh)
                dv_ref[ks, cols] += _tn(a.astype(BF16), dob)

        _on_blocks(qi, kj, step)

        @pl.when(kj == qi)
        def _():
            for hh in range(ATTN_HEADS):
                dq_ref[:, hh * HEAD_DIM:(hh + 1) * HEAD_DIM] = dq_s[hh]

    kblk = lambda i, j: jnp.minimum(i, j)
    full = _sds((s_len, N_HEADS * d), F32)
    w, vc = HEAD_DIM * ATTN_HEADS, _head_cols(voff)
    return pl.pallas_call(
        body, grid=(ATTN_GRID_HEADS, nq, nq),
        in_specs=[pl.BlockSpec((tq, w), lambda h, i, j: (i, h)), pl.BlockSpec((tq, w), lambda h, i, j: (kblk(i, j), h)),
                  pl.BlockSpec((tq, w), lambda h, i, j: (kblk(i, j), vc + h)),
                  pl.BlockSpec((tq, w), lambda h, i, j: (i, h)),
                  pl.BlockSpec((ATTN_HEADS, None, tq, 1), lambda h, i, j: (h, kblk(i, j), i, 0))],
        out_specs=[pl.BlockSpec((tq, w), lambda h, i, j: (i, h)), pl.BlockSpec((s_len, w), lambda h, i, j: (0, h)),
                   pl.BlockSpec((s_len, w), lambda h, i, j: (0, h))],
        out_shape=[full, full, full],
        scratch_shapes=[pltpu.VMEM((ATTN_HEADS, tq, d), F32), pltpu.VMEM((ATTN_HEADS, tq, 1), F32)],
        name=name, compiler_params=_cparams(3))(qn, kn, qkv, do, carries)


def _cumsum_rows(x, reverse):
    n = x.shape[0] // HEAD_DIM
    tri = _tri(HEAD_DIM, (lambda a, b: b >= a) if reverse else (lambda a, b: b <= a))
    pieces = [None] * n
    carry = jnp.zeros((1, HEAD_DIM), F32)
    order = range(n - 1, -1, -1) if reverse else range(n)
    for blk in order:
        xb = x[blk * HEAD_DIM:(blk + 1) * HEAD_DIM, :]
        x1 = xb.astype(BF16)
        r1 = xb - x1.astype(F32)
        x2 = r1.astype(BF16)
        x3 = (r1 - x2.astype(F32)).astype(BF16)
        c = _nn(tri, x1) + _nn(tri, x2) + _nn(tri, x3) + carry
        pieces[blk] = c
        carry = c[0:1, :] if reverse else c[HEAD_DIM - 1:HEAD_DIM, :]
    return jnp.concatenate(pieces, axis=0)


def _log_sigmoid(x):
    return jnp.minimum(x, 0.0) - jnp.log(1.0 + jnp.exp(-jnp.abs(x)))


def fox_gate_fwd(name, fl, bf):
    return blockk(name, lambda f, b: _cumsum_rows(_log_sigmoid(f + b), False), (1, 1),
                  [(fl, _whole(fl.shape)), (bf, _whole(bf.shape))], [(_sds(fl.shape, F32), _whole(fl.shape), None)])[0]


def fox_gate_bwd(name, fl, bf, dcum):
    def fn(f, b, dc):
        dlogf = _cumsum_rows(dc, True)
        dfl = dlogf * _sig(-(f + b))
        return dfl, jnp.sum(dfl, axis=0, keepdims=True)

    return blockk(name, fn, (1, 1), [(fl, _whole(fl.shape)), (bf, _whole(bf.shape)), (dcum, _whole(dcum.shape))],
                  [(_sds(fl.shape, F32), _whole(fl.shape), None), (_sds(bf.shape, F32), _whole(bf.shape), None)])


def rope_fwd(name, x, gain, cos, sin, tr, xcol, nh):
    r = x.shape[0]
    tr = r if nh > 1 else tr
    xspec = _rows(tr, HEAD_DIM, lambda j: xcol + j)
    tspec = _rows(tr, HEAD_DIM, 0)
    return blockk(name, lambda xb, g, c, s: _rope(_rms(xb.astype(F32), g, MLA_ROPE), c, s), (r // tr, nh),
                  [(x, xspec), (gain, _whole(gain.shape)), (cos, tspec), (sin, tspec)],
                  [(_sds((r, HEAD_DIM * nh), BF16), _rows(tr, HEAD_DIM, 'j'), None)])[0]


def rope_bwd(name, x, gain, cos, sin, dy, tr, xcol, nh):
    r = x.shape[0]
    tr = r if nh > 1 else tr
    xspec = _rows(tr, HEAD_DIM, lambda j: xcol + j)
    tspec = _rows(tr, HEAD_DIM, 0)
    ospec = _rows(tr, HEAD_DIM, 'j')

    def fn(xb, g, c, s, dyb):
        _, vjp = jax.vjp(lambda a, b: _rms(a, b, MLA_ROPE), xb.astype(F32), g)
        return vjp(_rope_t(dyb.astype(F32), c, s))

    return blockk(name, fn, (r // tr, nh),
                  [(x, xspec), (gain, _whole(gain.shape)), (cos, tspec), (sin, tspec), (dy, ospec)],
                  [(_sds((r, HEAD_DIM * nh), F32), ospec, None), (_sds(gain.shape, F32), _whole(gain.shape), 'all')])


def sgu_pre_fwd(name, pre, gain, tr):
    s_len, w2 = pre.shape
    w = w2 // 2
    return blockk(name, lambda pu, pv, g: (_gelu(pu.astype(F32)), _rms(_gelu(pv.astype(F32)), g, w)), (s_len // tr, 1),
                  [(pre, _rows(tr, w, 0)), (pre, _rows(tr, w, 1)), (gain, _whole(gain.shape))],
                  [(_sds((s_len, w), BF16), _rows(tr, w, 0), None), (_sds((s_len, w), BF16), _rows(tr, w, 0), None)])


def sgu_pre_bwd(name, pre, gain, du, dvn, tr):
    s_len, w2 = pre.shape
    w = w2 // 2

    def fn(pu, pv, g, dub, dvb):
        _, vjp_u = jax.vjp(_gelu, pu.astype(F32))
        _, vjp_v = jax.vjp(lambda a, b: _rms(_gelu(a), b, w), pv.astype(F32), g)
        dpv, dg = vjp_v(dvb.astype(F32))
        return jnp.stack([vjp_u(dub.astype(F32))[0], dpv]), dg

    spec = _rows(tr, w, 0)
    return blockk(name, fn, (s_len // tr, 1),
                  [(pre, spec), (pre, _rows(tr, w, 1)), (gain, _whole(gain.shape)), (du, spec), (dvn, spec)],
                  [(_sds((2, s_len, w), BF16), pl.BlockSpec((2, tr, w), lambda i, j: (0, i, 0)), None),
                   (_sds(gain.shape, F32), _whole(gain.shape), 'all')])


def _ws_masked(ws):
    t = ws.shape[0]
    a = lax.broadcasted_iota(jnp.int32, (t, t), 0)
    b = lax.broadcasted_iota(jnp.int32, (t, t), 1)
    return jnp.where(b <= a, ws, 0.0)


def sgu_mix_fwd(name, vn, u, ws, bs3):
    s_len, w = vn.shape
    t = ws.shape[1]

    n_groups = w // t

    def fn(vb, ub, wsb, bsb):
        pieces = []
        for g in range(n_groups):
            cols = slice(g * t, (g + 1) * t)
            mixed = _nn(_ws_masked(wsb[g]).astype(BF16), vb[:, cols].astype(BF16)) + bsb[g]
            pieces.append(ub[:, cols].astype(F32) * mixed)
        return jnp.concatenate(pieces, axis=1)

    blk = pl.BlockSpec((t, w), lambda i, j: (i, 0))
    return blockk(name, fn, (s_len // t, 1), [(vn, blk), (u, blk), (ws, _whole(ws.shape)), (bs3, _whole(bs3.shape))],
                  [(_sds((s_len, w), BF16), blk, None)])[0]


def sgu_mix_bwd(name, vn, u, ws, bs3, dprod):
    s_len, w = vn.shape
    t = ws.shape[1]

    n_groups = w // t

    def fn(vb, ub, wsb, bsb, dpb):
        dus, dvns, dwss, dbss = [], [], [], []
        for g in range(n_groups):
            cols = slice(g * t, (g + 1) * t)
            wm = _ws_masked(wsb[g]).astype(BF16)
            vb16 = vb[:, cols].astype(BF16)
            mixed = _nn(wm, vb16) + bsb[g]
            dp = dpb[:, cols].astype(F32)
            dus.append(dp * mixed)
            dm = dp * ub[:, cols].astype(F32)
            dmb = dm.astype(BF16)
            dvns.append(_tn(wm, dmb))
            dwss.append(_ws_masked(_nt(dmb, vb16)))
            dbss.append(jnp.sum(dm, axis=-1, keepdims=True))
        return jnp.concatenate(dus, axis=1), jnp.concatenate(dvns, axis=1), jnp.stack(dwss), jnp.stack(dbss)

    blk = pl.BlockSpec((t, w), lambda i, j: (j, 0))
    return blockk(name, fn, (1, s_len // t),
                  [(vn, blk), (u, blk), (ws, _whole(ws.shape)), (bs3, _whole(bs3.shape)), (dprod, blk)],
                  [(_sds((s_len, w), BF16), blk, None), (_sds((s_len, w), BF16), blk, None),
                   (_sds(ws.shape, F32), _whole(ws.shape), 'all'), (_sds(bs3.shape, F32), _whole(bs3.shape), 'all')])


def _shift_down(x, k):
    row = lax.broadcasted_iota(jnp.int32, x.shape, 0)
    return jnp.where(row >= k, pltpu.roll(x, k, 0), 0.0)


def _shift_up(x, k):
    n = x.shape[0]
    row = lax.broadcasted_iota(jnp.int32, x.shape, 0)
    return jnp.where(row < n - k, pltpu.roll(x, n - k, 0), 0.0)


def _conv(up, cw, cb):
    return cb + cw[0:1, :] * _shift_down(up, 2) + cw[1:2, :] * _shift_down(up, 1) + cw[2:3, :] * up


def _conv_specs(s_len, tc, nf):
    g = pl.BlockSpec((s_len, tc), lambda i, j: (0, j))
    v = pl.BlockSpec((s_len, tc), lambda i, j: (0, j + nf))
    wg = pl.BlockSpec((3, tc), lambda i, j: (0, j))
    wv = pl.BlockSpec((3, tc), lambda i, j: (0, j + nf))
    bg = pl.BlockSpec((1, tc), lambda i, j: (0, j))
    bv = pl.BlockSpec((1, tc), lambda i, j: (0, j + nf))
    return g, v, wg, wv, bg, bv


def conv_fwd(name, up, cw, cb, tc=256):
    s_len, f2 = up.shape
    f = f2 // 2
    nf = f // tc
    g, v, wg, wv, bg, bv = _conv_specs(s_len, tc, nf)

    def fn(ug, uv, cwg, cwv, cbg, cbv):
        yg = _conv(ug.astype(F32), cwg, cbg)
        yv = _conv(uv.astype(F32), cwv, cbv)
        return yg * _sig(yg) * yv, jnp.stack([yg, yv])

    both = pl.BlockSpec((2, s_len, tc), lambda i, j: (0, 0, j))
    return blockk(name, fn, (1, nf), [(up, g), (up, v), (cw, wg), (cw, wv), (cb, bg), (cb, bv)],
                  [(_sds((s_len, f), BF16), g, None), (_sds((2, s_len, f), BF16), both, None)])


def conv_bwd(name, up, y, cw, dact, tc=128):
    s_len, f2 = up.shape
    f = f2 // 2
    nf = f // tc
    g, v, wg, wv, bg, bv = _conv_specs(s_len, tc, nf)

    def half(dy, u, cwh):
        d1 = _shift_up(dy, 1)
        d2 = _shift_up(d1, 1)
        dup = cwh[2:3, :] * dy + cwh[1:2, :] * d1 + cwh[0:1, :] * d2
        dcw = jnp.concatenate([jnp.sum(d2 * u, axis=0, keepdims=True), jnp.sum(d1 * u, axis=0, keepdims=True),
                               jnp.sum(dy * u, axis=0, keepdims=True)], axis=0)
        return dup, dcw, jnp.sum(dy, axis=0, keepdims=True)

    def fn(ug, uv, yb, cwg, cwv, da):
        ug, uv, da = ug.astype(F32), uv.astype(F32), da.astype(F32)
        yg, yv = yb[0].astype(F32), yb[1].astype(F32)
        sg = _sig(yg)
        silu = yg * sg
        dyv = da * silu
        dyg = da * yv * (sg + silu * (1.0 - sg))
        dug, dcwg, dcbg = half(dyg, ug, cwg)
        duv, dcwv, dcbv = half(dyv, uv, cwv)
        return jnp.stack([dug, duv]), dcwg, dcwv, dcbg, dcbv

    both = pl.BlockSpec((2, s_len, tc), lambda i, j: (0, 0, j))
    return blockk(name, fn, (1, nf), [(up, g), (up, v), (y, both), (cw, wg), (cw, wv), (dact, g)],
                  [(_sds((2, s_len, f), BF16), both, None),
                   (_sds((3, f), F32), wg, None), (_sds((3, f), F32), wg, None),
                   (_sds((1, f), F32), bg, None), (_sds((1, f), F32), bg, None)])


def loss_head(name, y, target, tr=256):
    s_len, d = y.shape

    def fn(yb, tb):
        e = yb - tb
        dy = e * (1.0 / d)
        return dy, dy, jnp.sum(e * e, axis=0, keepdims=True)

    spec = _rows(tr, d, 0)
    return blockk(name, fn, (s_len // tr, 1), [(y, spec), (target, spec)],
                  [(_sds((s_len, d), BF16), spec, None), (_sds((s_len, d), F32), spec, None),
                   (_sds((1, d), F32), _whole((1, d)), 'all')])


def _adam(w, g, m, v):
    m = ADAM_B1 * m + (1.0 - ADAM_B1) * g
    v = ADAM_B2 * v + (1.0 - ADAM_B2) * (g * g)
    m_hat = m / (1.0 - ADAM_B1 ** ADAM_STEP)
    v_hat = v / (1.0 - ADAM_B2 ** ADAM_STEP)
    delta = -ADAM_LR * (m_hat / (jnp.sqrt(v_hat) + ADAM_EPS) + ADAM_WD * w)
    return delta, m, v


def adam_sum(name, parts, w, m, v):
    n_layers, r, c = w.shape
    n_parts = parts[0].shape[0]
    assert len(parts) == n_layers
    tr = r
    for cand in (512, 256, 128, 64, 32, 16):
        if r % cand == 0 and cand * c * 4 <= 1024 * 1024:
            tr = cand
            break

    def body(*refs):
        part_refs = refs[:n_layers]
        w_ref, m_ref, v_ref, g_out, d_out, m_out, v_out = refs[n_layers:]
        layer = pl.program_id(0)
        for ll in range(n_layers):
            @pl.when(layer == ll)
            def _(ll=ll):
                g = part_refs[ll][0].astype(F32)
                for k in range(1, n_parts):
                    g = g + part_refs[ll][k].astype(F32)
                delta, m_new, v_new = _adam(w_ref[...], g, m_ref[...], v_ref[...])
                g_out[...] = g
                d_out[...] = delta
                m_out[...] = m_new
                v_out[...] = v_new

    spec = pl.BlockSpec((None, tr, c), lambda l, i: (l, i, 0))
    part_specs = [pl.BlockSpec((n_parts, tr, c), lambda l, i, ll=ll: (0, jnp.where(l == ll, i, 0), 0))
                  for ll in range(n_layers)]
    out = _sds((n_layers, r, c), F32)
    return pl.pallas_call(body, grid=(n_layers, r // tr), in_specs=part_specs + [spec] * 3, out_specs=[spec] * 4,
                          out_shape=[out] * 4, name=name, compiler_params=_cparams(2))(*parts, w, m, v)


def sum_parts(name, parts, tr=256):
    n_parts, r, c = parts.shape

    def fn(pb):
        g = pb[0].astype(F32)
        for k in range(1, n_parts):
            g = g + pb[k].astype(F32)
        return g

    return blockk(name, fn, (r // tr, 1), [(parts, pl.BlockSpec((n_parts, tr, c), lambda i, j: (0, i, 0)))],
                  [(_sds((r, c), F32), _rows(tr, c, 0), None)])[0]


def adam_flat(name, w, g, m, v, tr=256):
    r, c = w.shape
    spec = _rows(tr, c, 0)
    return blockk(name, lambda wb, gb, mb, vb: _adam(wb, gb, mb, vb), (r // tr, 1),
                  [(w, spec), (g, spec), (m, spec), (v, spec)], [(_sds((r, c), F32), spec, None)] * 3)


_ANY = pl.BlockSpec(memory_space=pl.ANY)


def _place():
    return lax.axis_index("x"), lax.axis_index("y"), lax.axis_index("c")


def _slot(px, py, pc):
    return 4 * px + 2 * py + pc


def all_gather(name, items):
    n = len(items)

    def body(*refs):
        xs, outs = refs[:n], refs[n:2 * n]
        send_sems, recv_sems, local_sems = refs[2 * n:]
        x, y, c = _place()
        me, sibling = (x, y, c), (x, y, 1 - c)
        chips = [(1 - x, y), (x, 1 - y), (1 - x, 1 - y)]

        def copy(t, k, block, to, src=None):
            dst = outs[t].at[_slot(*block)]
            return pltpu.make_async_remote_copy(src_ref=dst if src is None else src, dst_ref=dst,
                                                send_sem=send_sems.at[7 * t + k], recv_sem=recv_sems.at[7 * t + k],
                                                device_id=to, device_id_type=MESH)

        mine = [pltpu.make_async_copy(xs[t], outs[t].at[_slot(*me)], local_sems.at[t]) for t in range(n)]
        for cp in mine:
            cp.start()
        started = []
        for t in range(n):
            started.append(copy(t, 0, me, sibling, src=xs[t]))
            started += [copy(t, 1 + j, me, (*chip, c), src=xs[t]) for j, chip in enumerate(chips)]
        for cp in started:
            cp.start()
        for j, chip in enumerate(chips):
            for t in range(n):
                copy(t, 1 + j, (*chip, c), me).wait_recv()
                passed = copy(t, 4 + j, (*chip, c), sibling)
                passed.start()
                started.append(passed)
        for t in range(n):
            copy(t, 0, sibling, me).wait_recv()
            for j, chip in enumerate(chips):
                copy(t, 4 + j, (*chip, 1 - c), me).wait_recv()
        for cp in started:
            cp.wait_send()
        for cp in mine:
            cp.wait()

    return pl.pallas_call(
        body, in_specs=[_ANY] * n, out_specs=[_ANY] * n,
        out_shape=[_sds((N_DEV,) + a.shape, a.dtype) for a in items],
        scratch_shapes=[pltpu.SemaphoreType.DMA((7 * n,)), pltpu.SemaphoreType.DMA((7 * n,)), pltpu.SemaphoreType.DMA((n,))],
        name=name)(*items)


_HBM = pl.BlockSpec(memory_space=pltpu.HBM)
_SEM = pl.BlockSpec(memory_space=pltpu.SEMAPHORE)
_EFFECT = pltpu.SideEffectType.DATAFLOW_SIDE_EFFECTING


def _peer(k, x, y, c):
    return ((1 - x) if k & 4 else x, (1 - y) if k & 2 else y, (1 - c) if k & 1 else c)


PEERS_ALL = (1, 2, 3, 4, 5, 6, 7)
PEERS_CHIPWISE = (1, 2, 4, 6)


def _exchange_copies(mode, refs, send_sems, recv_sems, landing):
    x, y, c = _place()
    my_slot = _slot(x, y, c)
    copies = []
    if mode == "pass_on":
        for t, land in enumerate(refs):
            for j, chip in enumerate([(1 - x, y), (x, 1 - y), (1 - x, 1 - y)]):
                slot = _slot(*chip, (1 - c) if landing else c)
                copies.append(pltpu.make_async_remote_copy(
                    src_ref=land.at[slot], dst_ref=land.at[slot], send_sem=send_sems.at[3 * t + j],
                    recv_sem=recv_sems.at[3 * t + j], device_id=(x, y, 1 - c), device_id_type=MESH))
        return copies
    n = len(refs) // 2
    for t, (src, land) in enumerate(zip(refs[:n], refs[n:])):
        for k in (PEERS_CHIPWISE if mode == "gather_chipwise" else PEERS_ALL):
            peer = _peer(k, x, y, c)
            copies.append(pltpu.make_async_remote_copy(
                src_ref=src.at[_slot(*peer)] if mode == "scatter" else src,
                dst_ref=land.at[_slot(*peer) if landing else my_slot],
                send_sem=send_sems.at[7 * t + k - 1], recv_sem=recv_sems.at[7 * t + k - 1],
                device_id=peer, device_id_type=MESH))
    return copies


def exchange_start(name, mode, arrays):
    n = len(arrays)
    passing = mode == "pass_on"
    n_sems = (3 if passing else 7) * n
    lands = [] if passing else [pltpu.HBM(a.shape if mode == "scatter" else (N_DEV,) + a.shape, a.dtype) for a in arrays]

    def body(*refs):
        srcs, outs = refs[:n], refs[n + 2:]
        bufs = list(srcs) if passing else list(srcs) + list(outs[n:2 * n])
        for send in _exchange_copies(mode, bufs, refs[n], refs[n + 1], False):
            send.start()
        refs[-1][...] = jnp.zeros_like(refs[-1])

    arrays = [pltpu.with_memory_space_constraint(a, pltpu.HBM) for a in arrays]
    res = pl.pallas_call(
        body, name=name, in_specs=[_HBM] * n,
        out_specs=(_SEM, _SEM) + (_HBM,) * (n + len(lands)) + (pl.BlockSpec(memory_space=pltpu.VMEM),),
        out_shape=(pltpu.SemaphoreType.DMA((n_sems,)), pltpu.SemaphoreType.DMA((n_sems,)))
        + tuple(pltpu.HBM(a.shape, a.dtype) for a in arrays) + tuple(lands) + (_sds((8, HEAD_DIM), F32),),
        input_output_aliases={i: 2 + i for i in range(n)},
        compiler_params=pltpu.CompilerParams(has_side_effects=_EFFECT))(*arrays)
    return dict(name=name, mode=mode, sems=res[:2], bufs=res[2:-1]), res[-1]


def exchange_wait(handle, after):
    mode, bufs = handle['mode'], handle['bufs']
    nb = len(bufs)

    def body(*refs):
        for landed in _exchange_copies(mode, refs[:nb], refs[nb], refs[nb + 1], True):
            landed.wait_send()
            landed.wait_recv()

    res = pl.pallas_call(
        body, name=handle['name'].replace("start", "wait"), in_specs=[_HBM] * nb + [_SEM, _SEM, _ANY],
        out_specs=(_HBM,) * nb, out_shape=tuple(pltpu.HBM(a.shape, a.dtype) for a in bufs),
        input_output_aliases={i: i for i in range(nb)},
        compiler_params=pltpu.CompilerParams(has_side_effects=_EFFECT))(*bufs, *handle['sems'], after)
    if mode == "pass_on":
        return list(res)
    my_slot = _slot(*_place())
    filled = []
    for src, land in zip(res[:nb // 2], res[nb // 2:]):
        own = lax.dynamic_index_in_dim(src, my_slot, 0, keepdims=True) if mode == "scatter" else src[None]
        filled.append(lax.dynamic_update_slice(land, own, (my_slot,) + (0,) * (land.ndim - 1)))
    return filled


def _pad_lanes(a, width=HEAD_DIM):
    return jnp.pad(a, [(0, 0)] * (a.ndim - 1) + [(0, width - a.shape[-1])])


def _slabs(full, n):
    k = full.shape[0]
    return full.reshape(k, N_DEV, n).transpose(1, 0, 2)


def fox_fwd(a, w, p):
    qkv = mm_nn(a, w['qkv'], BF16, "fox_qkv")
    fl = mm_nn(a, w['f'], F32, "fox_flogit")
    bf = _pad_lanes(p['fox_b_f'])
    cum = fox_gate_fwd("fox_gate_fwd", fl, bf)
    cum_t = cum[:, :N_HEADS].T
    cums = (cum_t[:, :, None], cum_t[:, None, :])
    qn = rms_fwd("fox_qnorm", qkv, p['fox_q_gain'], HEAD_DIM, 512, HEAD_DIM, xcol=0, nh=N_HEADS)
    kn = rms_fwd("fox_knorm", qkv, p['fox_k_gain'], HEAD_DIM, 512, HEAD_DIM, xcol=N_HEADS, nh=N_HEADS)
    o, lse = attn_fwd("fox_attn_fwd", qn, kn, qkv, HEAD_DIM ** -0.5, voff=2 * N_HEADS, cum=cums, exact_o=True)
    return o, dict(a=a, qkv=qkv, fl=fl, bf=bf, cums=cums, qn=qn, kn=kn, o=o, lse=lse)


def fox_bwd(do, s, w, p):
    dqn, dkn, dv, dcs = attn_bwd("fox_attn_bwd", s['qn'], s['kn'], s['qkv'], s['o'], do, s['lse'], HEAD_DIM ** -0.5,
                                 voff=2 * N_HEADS, cum=s['cums'])
    dq, dgq = rms_bwd("fox_qnorm_bwd", s['qkv'], p['fox_q_gain'], dqn, HEAD_DIM, 512, HEAD_DIM, xcol=0, nh=N_HEADS,
                      out_dtype=BF16)
    dk, dgk = rms_bwd("fox_knorm_bwd", s['qkv'], p['fox_k_gain'], dkn, HEAD_DIM, 512, HEAD_DIM, xcol=N_HEADS,
                      nh=N_HEADS, out_dtype=BF16)
    dqkv = jnp.concatenate([dq, dk, dv.astype(BF16)], axis=1)
    dcum = _pad_lanes(-jnp.sum(dcs[:, :, 0, :], axis=1).T)
    dfl, dbf = fox_gate_bwd("fox_gate_bwd", s['fl'], s['bf'], dcum)
    da = mm_nt(dqkv, w['qkv'], F32, "fox_da_qkv")
    da = mm_nt(dfl, w['f'], F32, "fox_da_f", add=da)
    dw_qkv = mm_tn(s['a'], dqkv, 1, BF16, "fox_dw_qkv")[0]
    dw_f = mm_tn(s['a'], dfl, 1, BF16, "fox_dw_f")[0][:, :N_HEADS]
    dw = _slabs(jnp.concatenate([dw_qkv, dw_f], axis=1), 770)
    return da, dict(fox_w_in=dw), dict(fox_b_f=dbf[:, :N_HEADS], fox_q_gain=dgq, fox_k_gain=dgk)


def _rope_tables(positions):
    inv_freq = ROPE_THETA ** (-jnp.arange(0, MLA_ROPE, 2, dtype=F32) / MLA_ROPE)
    ang = positions.astype(F32)[:, None] * inv_freq
    cos, sin = jnp.cos(ang), jnp.sin(ang)
    return _pad_lanes(jnp.concatenate([cos, cos], axis=1)), _pad_lanes(jnp.concatenate([sin, sin], axis=1))


def mla_fwd(a, w, p):
    qg, kg = p['mla_q_gain'], p['mla_k_gain']
    gains = dict(qn=qg[:, :HEAD_DIM], qr=_pad_lanes(qg[:, HEAD_DIM:]), kn=kg[:, :HEAD_DIM], kr=_pad_lanes(kg[:, HEAD_DIM:]))
    cos, sin = _rope_tables(p['positions'])
    ccr = mm_nn(a, w['in'], F32, "mla_in")
    cqn = rms_fwd("mla_cq_norm", ccr, w['q_a_gain'], 512, 256, 512, xcol=0)
    ckvn = rms_fwd("mla_ckv_norm", ccr, w['kv_a_gain'], 512, 256, 512, xcol=1)
    qf = mm_nn(cqn, w['q_b'], F32, "mla_q_b")
    kvf = mm_nn(ckvn, w['kv_b'], F32, "mla_kv_b")
    q_nope = rms_fwd("mla_qnope_norm", qf, gains['qn'], HEAD_DIM, 512, HEAD_DIM, xcol=0, nh=N_HEADS)
    k_nope = rms_fwd("mla_knope_norm", kvf, gains['kn'], HEAD_DIM, 512, HEAD_DIM, xcol=0, nh=N_HEADS)
    q_rope = rope_fwd("mla_qrope", qf, gains['qr'], cos, sin, 512, N_HEADS, N_HEADS)
    k_rope = rope_fwd("mla_krope", ccr, gains['kr'], cos, sin, 512, 8, 1)
    scale = (HEAD_DIM + MLA_ROPE) ** -0.5
    o, lse = attn_fwd("mla_attn_fwd", q_nope, k_nope, kvf, scale, voff=N_HEADS, q2=q_rope, k2=k_rope)
    return o, dict(a=a, gains=gains, cos=cos, sin=sin, ccr=ccr, cqn=cqn, ckvn=ckvn, qf=qf, kvf=kvf, q_nope=q_nope,
                   k_nope=k_nope, q_rope=q_rope, k_rope=k_rope, o=o, lse=lse, scale=scale)


def mla_bwd(do, s, w, p):
    g = s['gains']
    dqn, dkn, dv, dq2, dk2 = attn_bwd("mla_attn_bwd", s['q_nope'], s['k_nope'], s['kvf'], s['o'], do, s['lse'], s['scale'],
                                      voff=N_HEADS, q2=s['q_rope'], k2=s['k_rope'])
    dqf_n, dg_qn = rms_bwd("mla_qnope_bwd", s['qf'], g['qn'], dqn, HEAD_DIM, 512, HEAD_DIM, xcol=0, nh=N_HEADS)
    dkf, dg_kn = rms_bwd("mla_knope_bwd", s['kvf'], g['kn'], dkn, HEAD_DIM, 512, HEAD_DIM, xcol=0, nh=N_HEADS)
    dqf_r, dg_qr = rope_bwd("mla_qrope_bwd", s['qf'], g['qr'], s['cos'], s['sin'], dq2, 512, N_HEADS, N_HEADS)
    dkr, dg_kr = rope_bwd("mla_krope_bwd", s['ccr'], g['kr'], s['cos'], s['sin'], dk2, 512, 8, 1)
    dqf = jnp.concatenate([dqf_n, dqf_r], axis=1)
    dkvf = jnp.concatenate([dkf, dv], axis=1)
    dcqn = mm_nt(dqf, w['q_b'], F32, "mla_dcq")
    dckvn = mm_nt(dkvf, w['kv_b'], F32, "mla_dckv")
    dw_qb = mm_tn(s['cqn'], dqf, 1, BF16, "mla_dw_qb")[0]
    dw_kvb = mm_tn(s['ckvn'], dkvf, 1, BF16, "mla_dw_kvb")[0]
    dcq, dg_qa = rms_bwd("mla_cq_bwd", s['ccr'], w['q_a_gain'], dcqn, 512, 256, 512, xcol=0)
    dckv, dg_kva = rms_bwd("mla_ckv_bwd", s['ccr'], w['kv_a_gain'], dckvn, 512, 256, 512, xcol=1)
    dccr = jnp.concatenate([dcq, dckv, dkr], axis=1)
    da = mm_nt(dccr, w['in'], F32, "mla_da")
    dw_in = mm_tn(s['a'], dccr, 1, BF16, "mla_dw_in")[0][:, :1088].reshape(N_DEV, 256, 1088)
    hp = 2
    nope = dw_qb[:, :2048].reshape(512, N_DEV, hp, HEAD_DIM)
    rope = dw_qb[:, 2048:].reshape(512, N_DEV, hp, HEAD_DIM)[..., :MLA_ROPE]
    dw_qb_s = jnp.concatenate([nope, rope], axis=-1).transpose(1, 0, 2, 3).reshape(N_DEV, 512, hp * 192)
    kk = dw_kvb[:, :2048].reshape(512, N_DEV, hp, HEAD_DIM)
    vv = dw_kvb[:, 2048:].reshape(512, N_DEV, hp, HEAD_DIM)
    dw_kvb_s = jnp.concatenate([kk, vv], axis=-1).transpose(1, 0, 2, 3).reshape(N_DEV, 512, hp * 256)
    small = dict(mla_q_a_gain=dg_qa, mla_kv_a_gain=dg_kva,
                 mla_q_gain=jnp.concatenate([dg_qn, dg_qr[:, :MLA_ROPE]], axis=1),
                 mla_k_gain=jnp.concatenate([dg_kn, dg_kr[:, :MLA_ROPE]], axis=1))
    return da, dict(mla_w_in=dw_in, mla_w_q_b=dw_qb_s, mla_w_kv_b=dw_kvb_s), small


def sb_fwd_layer(a, w, p):
    qkv = mm_nn(a, w['in'], BF16, "sb_qkv")
    qn = rms_fwd("sb_qnorm", qkv, p['sb_q_gain'], HEAD_DIM, 512, HEAD_DIM, xcol=0, nh=N_HEADS)
    kn = rms_fwd("sb_knorm", qkv, p['sb_k_gain'], HEAD_DIM, 512, HEAD_DIM, xcol=N_HEADS, nh=N_HEADS)
    o, carries = sb_fwd("sb_attn_fwd", qn, kn, qkv, HEAD_DIM ** -0.5, 2 * N_HEADS)
    return o, dict(a=a, qkv=qkv, qn=qn, kn=kn, carries=carries)


def sb_bwd_layer(do, s, w, p):
    dqn, dkn, dv = sb_bwd("sb_attn_bwd", s['qn'], s['kn'], s['qkv'], do, s['carries'], HEAD_DIM ** -0.5, 2 * N_HEADS)
    dq, dgq = rms_bwd("sb_qnorm_bwd", s['qkv'], p['sb_q_gain'], dqn, HEAD_DIM, 512, HEAD_DIM, xcol=0, nh=N_HEADS,
                      out_dtype=BF16)
    dk, dgk = rms_bwd("sb_knorm_bwd", s['qkv'], p['sb_k_gain'], dkn, HEAD_DIM, 512, HEAD_DIM, xcol=N_HEADS, nh=N_HEADS,
                      out_dtype=BF16)
    dqkv = jnp.concatenate([dq, dk, dv.astype(BF16)], axis=1)
    da = mm_nt(dqkv, w['in'], F32, "sb_da")
    dw = mm_tn(s['a'], dqkv, N_DEV, BF16, "sb_dw_in")
    return da, dict(sb_w_in=dw), dict(sb_q_gain=dgq, sb_k_gain=dgk)


def sgu_fwd(a, w, p):
    pre = mm_nn(a, w['in'], BF16, "sgu_in")
    u, vn = sgu_pre_fwd("sgu_pre_fwd", pre, w['v_gain'], 256)
    ws = p['sgu_w_s'][0]
    bs3 = p['sgu_b_s'][0][:, :, None]
    prod = sgu_mix_fwd("sgu_mix_fwd", vn, u, ws, bs3)
    return prod, dict(a=a, pre=pre, u=u, vn=vn, ws=ws, bs3=bs3)


def sgu_bwd(dprod, s, w, p):
    du, dvn, dws, dbs3 = sgu_mix_bwd("sgu_mix_bwd", s['vn'], s['u'], s['ws'], s['bs3'], dprod)
    dpre, dgv = sgu_pre_bwd("sgu_pre_bwd", s['pre'], w['v_gain'], du, dvn, 128)
    da = mm_nt(dpre, w['in'], F32, "sgu_da")
    dw = mm_tn(s['a'], dpre, N_DEV, BF16, "sgu_dw_in")
    return da, dict(sgu_w_in=dw), dict(sgu_v_gain=dgv, sgu_w_s=dws[None], sgu_b_s=dbs3[None, :, :, 0])


MIXERS = [("fox", fox_fwd, fox_bwd), ("mla", mla_fwd, mla_bwd), ("sb", sb_fwd_layer, sb_bwd_layer),
          ("sgu", sgu_fwd, sgu_bwd)]


def _pack_rows(arrays, row_mult=256):
    flat = jnp.concatenate([a.reshape(-1).astype(F32) for a in arrays])
    per = row_mult * HEAD_DIM
    total = -(-flat.shape[0] // per) * per
    return jnp.pad(flat, (0, total - flat.shape[0])).reshape(total // HEAD_DIM, HEAD_DIM)


def _unpack_rows(packed, shapes):
    flat = packed.reshape(-1)
    out, off = [], 0
    for shp in shapes:
        n = math.prod(shp)
        out.append(flat[off:off + n].reshape(shp))
        off += n
    return out


def _mixer_shards(i, p):
    name = MIXERS[i][0]
    bf = lambda a: a.astype(BF16)
    items = {'out': bf(p[name + '_w_out'][0])}
    if name == "fox":
        items['in'] = bf(p['fox_w_in'][0])
        items['small'] = _pack_rows([p[n] for n in SMALL_SHARDED], 8)
    elif name == "mla":
        items.update({'in': bf(p['mla_w_in'][0]), 'q_b': bf(p['mla_w_q_b'][0]), 'kv_b': bf(p['mla_w_kv_b'][0])})
    else:
        items['in'] = bf(p[name + '_w_in'][0])
    return items


def _ffn_shards(i, p):
    return {'up': p['ffn_w_up'][i].astype(BF16), 'down': p['ffn_w_down'][i].astype(BF16)}


def _assemble_ffn(got):
    return {'up': got['up'], 'down': got['down'].reshape(1, D_FF, -1)}


def _assemble_mixer(i, got, p):
    name = MIXERS[i][0]
    w = {'out': got['out'].reshape(1, -1, got['out'].shape[-1])} if 'out' in got else {}
    small = None
    if name == "fox":
        if 'in' in got:
            full = got['in'].transpose(1, 0, 2).reshape(got['in'].shape[1], -1)
            w['qkv'] = full[None, :, :3 * N_HEADS * HEAD_DIM]
            w['f'] = _pad_lanes(full[:, 3 * N_HEADS * HEAD_DIM:])[None]
        if 'small' in got:
            small_shapes = [p[n].shape for n in SMALL_SHARDED]
            parts = [_unpack_rows(got['small'][d], small_shapes) for d in range(N_DEV)]
            small = {n: jnp.concatenate([parts[d][k] for d in range(N_DEV)], axis=ax)
                     for k, (n, ax) in enumerate(SMALL_SHARDED.items())}
    elif name == "mla":
        w['in'] = _pad_lanes(got['in'].reshape(-1, 1088), 1152)[None]
        hp = 2
        qb = got['q_b'].reshape(N_DEV, 512, hp, 192).transpose(1, 0, 2, 3)
        nope = qb[..., :HEAD_DIM].reshape(512, -1)
        rope = _pad_lanes(qb[..., HEAD_DIM:]).reshape(512, -1)
        w['q_b'] = jnp.concatenate([nope, rope], axis=1)[None]
        kvb = got['kv_b'].reshape(N_DEV, 512, hp, 256).transpose(1, 0, 2, 3)
        w['kv_b'] = jnp.concatenate([kvb[..., :HEAD_DIM].reshape(512, -1), kvb[..., HEAD_DIM:].reshape(512, -1)], axis=1)[None]
    else:
        w['in'] = got['in']
    return w, small


def _train_step(p):
    x, target = p['x'][0], p['loss_target'][0]
    p = dict(p, positions=p['positions'][0])
    xi, yi, ci = _place()
    my_slot = _slot(xi, yi, ci)

    shards0 = _mixer_shards(0, p)
    got0 = {'in': all_gather("gather_first", [shards0.pop('in')])[0]}
    pending, order_token = {}, jnp.zeros((1, 1), F32)
    for i in range(DEPTH):
        for kind, shards in (("mix", _mixer_shards(i, p) if i else shards0), ("ffn", _ffn_shards(i, p))):
            handle, token = exchange_start(f"xstart_ag_{kind}{i}", "gather_chipwise", list(shards.values()))
            pending[kind, i] = (handle, list(shards))
            order_token = order_token + token[0:1, 0:1]

    def pass_on(kind, i, after):
        handle, keys = pending[kind, i]
        handle, token = exchange_start(f"xstart_pass_{kind}{i}", "pass_on", exchange_wait(handle, after))
        pending[kind, i] = (handle, keys)
        return token[0:1, 0:1]

    def gathered(kind, i, after):
        handle, keys = pending[kind, i]
        return dict(zip(keys, exchange_wait(handle, after)))

    h = x
    saved, weights = [], []
    small_full = None
    for i in range(DEPTH):
        if i == 0:
            w, _ = _assemble_mixer(0, got0, p)
            gain = p['mix_norm'][0:1] + order_token
        else:
            w, _ = _assemble_mixer(i, gathered("mix", i, h), p)
            gain = p['mix_norm'][i:i + 1] + pass_on("ffn", i, h)
            w.update(q_a_gain=small_full['mla_q_a_gain'], kv_a_gain=small_full['mla_kv_a_gain'],
                     v_gain=small_full['sgu_v_gain'])
        a = rms_fwd(f"mix_norm_{i}", h, gain, h.shape[1], 512, h.shape[1])
        mixed, s_mix = MIXERS[i][1](a, w, p)
        if i == 0:
            pass_on("mix", 0, mixed)
            pass_on("ffn", 0, mixed)
            rest, small_full = _assemble_mixer(0, gathered("mix", 0, mixed), p)
            w.update(rest)
            conv_w = small_full['ffn_conv_w']
        w.update(_assemble_ffn(gathered("ffn", i, mixed)))
        weights.append(w)
        h1 = mm_nn(mixed, w['out'], F32, f"mix_out_{i}", add=h)
        b = rms_fwd(f"ffn_norm_{i}", h1, p['ffn_norm'][i:i + 1], h.shape[1], 512, h.shape[1])
        up = mm_nn(b, w['up'], BF16, f"ffn_up_{i}")
        conv_b = p['ffn_conv_b'][i:i + 1]
        if i + 1 < DEPTH:
            conv_b = conv_b + pass_on("mix", i + 1, up)
        act, conv_y = conv_fwd(f"ffn_conv_{i}", up, conv_w[i], conv_b)
        h2 = mm_nn(act, w['down'], F32, f"ffn_down_{i}", add=h1)
        saved.append(dict(h=h, mixed=mixed, s_mix=s_mix, h1=h1, b=b, up=up, act=act, conv_y=conv_y))
        h = h2

    dh16, dh, sq = loss_head("loss_head", h, target)
    loss = lax.psum(0.5 * jnp.sum(sq) / h.shape[1], ("x", "y", "c"))

    scatters = []
    layered = ('mix_norm', 'ffn_norm', 'ffn_conv_w', 'ffn_conv_b')
    early_names = [n for n in SMALL if not n.startswith('fox_')]
    late_names = [n for n in SMALL if n in layered or n.startswith('fox_')]
    small_g = {n: [None] * p[n].shape[0] for n in layered}
    early_token = jnp.zeros((1, 1), F32)
    for i in reversed(range(DEPTH)):
        w, s = weights[i], saved[i]
        name = MIXERS[i][0]
        dact = mm_nt(dh16, w['down'], BF16, f"ffn_dact_{i}")
        dw_down = mm_tn(s['act'], dh16, 1, BF16, f"ffn_dw_down_{i}").reshape(N_DEV, D_FF // N_DEV, -1)
        dup, dcwg, dcwv, dcbg, dcbv = conv_bwd(f"ffn_conv_bwd_{i}", s['up'], s['conv_y'], conv_w[i] + early_token, dact)
        dw_up = mm_tn(s['b'], dup, N_DEV, BF16, f"ffn_dw_up_{i}")
        handle, token = exchange_start(f"xstart_rs_ffn{i}", "scatter", [dw_up, dw_down])
        scatters.append((handle, ['ffn_w_up', 'ffn_w_down']))
        db = mm_nt(dup, w['up'], F32, f"ffn_db_{i}")
        dh1_16, dh1, dg_ffn = rms_bwd(f"ffn_norm_bwd_{i}", s['h1'], p['ffn_norm'][i:i + 1] + token[0:1, 0:1], db,
                                      h.shape[1], 256, h.shape[1], resid=dh, also_bf16=True)
        dw_out = mm_tn(s['mixed'], dh1_16, 1, BF16, f"mix_dw_out_{i}").reshape(N_DEV, -1, h.shape[1])
        if i == 0:
            handle, token = exchange_start("xstart_rs_out0", "scatter", [dw_out])
            scatters.append((handle, [name + '_w_out']))
            dh1_16 = dh1_16 + token[0:1, 0:1].astype(BF16)
        dmix = mm_nt(dh1_16, w['out'], BF16, f"mix_dout_{i}")
        da, big_i, small_i = MIXERS[i][2](dmix, s['s_mix'], w, p)
        if i > 0:
            big_i[name + '_w_out'] = dw_out
        handle, token = exchange_start(f"xstart_rs_mix{i}", "scatter", list(big_i.values()))
        scatters.append((handle, list(big_i)))
        dh16, dh, dg_mix = rms_bwd(f"mix_norm_bwd_{i}", s['h'], p['mix_norm'][i:i + 1] + token[0:1, 0:1], da, h.shape[1],
                                   256, h.shape[1], resid=dh1, also_bf16=True)
        for k, v in small_i.items():
            small_g[k] = v
        small_g['mix_norm'][i], small_g['ffn_norm'][i] = dg_mix, dg_ffn
        small_g['ffn_conv_w'][i] = jnp.concatenate([dcwg, dcwv], axis=1)[None]
        small_g['ffn_conv_b'][i] = jnp.concatenate([dcbg, dcbv], axis=1)
        if i == 1:
            early = [jnp.concatenate(small_g[n][1:], axis=0) if n in layered else small_g[n] for n in early_names]
            early_handle, token = exchange_start("xstart_small_early", "gather", [_pack_rows(early)])
            early_token = token[0:1, 0:1]
    late = [small_g[n][0] if n in layered else small_g[n] for n in late_names]
    grad_x = dh[None]

    small_handle, small_token = exchange_start("xstart_small", "gather", [_pack_rows(late)])
    big = {}
    for handle, names in scatters[:-1]:
        for n, landed in zip(names, exchange_wait(handle, small_token)):
            big.setdefault(n, []).insert(0, landed)
    grads, deltas, new_m, new_v = {}, {}, {}, {}
    last_handle, last_names = scatters[-1]
    for n in BIG:
        if n not in last_names:
            grads[n], deltas[n], new_m[n], new_v[n] = adam_sum("adam_" + n, big[n], p[n], p['m_' + n], p['v_' + n])
    for n, landed in zip(last_names, exchange_wait(last_handle, new_v['ffn_w_up'])):
        grads[n], deltas[n], new_m[n], new_v[n] = adam_sum("adam_" + n, [landed], p[n], p['m_' + n], p['v_' + n])
    early_sum = _unpack_rows(sum_parts("sum_small_early", exchange_wait(early_handle, small_token)[0]),
                             [tuple(a.shape) for a in early])
    late_sum = _unpack_rows(sum_parts("sum_small_late", exchange_wait(small_handle, new_v[last_names[0]])[0]),
                            [tuple(a.shape) for a in late])
    summed = dict(zip(early_names, early_sum))
    for n, g in zip(late_names, late_sum):
        summed[n] = jnp.concatenate([g, summed[n]], axis=0) if n in layered else g
    mine = []
    for n in SMALL:
        g = summed[n]
        if n in SMALL_SHARDED:
            ax = SMALL_SHARDED[n]
            g = lax.dynamic_slice_in_dim(g, my_slot * p[n].shape[ax], p[n].shape[ax], axis=ax)
        mine.append(g)
    shapes = [p[n].shape for n in SMALL]
    packed = [_pack_rows(arrs) for arrs in ([p[n] for n in SMALL], mine, [p['m_' + n] for n in SMALL], [p['v_' + n] for n in SMALL])]
    d_s, m_s, v_s = adam_flat("adam_small", *packed)
    for n, g, d, m, v in zip(SMALL, mine, _unpack_rows(d_s, shapes), _unpack_rows(m_s, shapes), _unpack_rows(v_s, shapes)):
        grads[n], deltas[n], new_m[n], new_v[n] = g, d, m, v

    return (loss, grad_x, *[grads[n] for n in WEIGHTS], *[deltas[n] for n in WEIGHTS], *[new_m[n] for n in WEIGHTS],
            *[new_v[n] for n in WEIGHTS])


def kernel(x, positions, mix_norm, ffn_norm, fox_w_in, fox_b_f, fox_q_gain, fox_k_gain, fox_w_out, mla_w_in, mla_q_a_gain, mla_kv_a_gain, mla_w_q_b, mla_w_kv_b, mla_q_gain, mla_k_gain, mla_w_out, sb_w_in, sb_q_gain, sb_k_gain, sb_w_out, sgu_w_in, sgu_v_gain, sgu_w_s, sgu_b_s, sgu_w_out, ffn_w_up, ffn_conv_w, ffn_conv_b, ffn_w_down, loss_target, m_mix_norm, m_ffn_norm, m_fox_w_in, m_fox_b_f, m_fox_q_gain, m_fox_k_gain, m_fox_w_out, m_mla_w_in, m_mla_q_a_gain, m_mla_kv_a_gain, m_mla_w_q_b, m_mla_w_kv_b, m_mla_q_gain, m_mla_k_gain, m_mla_w_out, m_sb_w_in, m_sb_q_gain, m_sb_k_gain, m_sb_w_out, m_sgu_w_in, m_sgu_v_gain, m_sgu_w_s, m_sgu_b_s, m_sgu_w_out, m_ffn_w_up, m_ffn_conv_w, m_ffn_conv_b, m_ffn_w_down, v_mix_norm, v_ffn_norm, v_fox_w_in, v_fox_b_f, v_fox_q_gain, v_fox_k_gain, v_fox_w_out, v_mla_w_in, v_mla_q_a_gain, v_mla_kv_a_gain, v_mla_w_q_b, v_mla_w_kv_b, v_mla_q_gain, v_mla_k_gain, v_mla_w_out, v_sb_w_in, v_sb_q_gain, v_sb_k_gain, v_sb_w_out, v_sgu_w_in, v_sgu_v_gain, v_sgu_w_s, v_sgu_b_s, v_sgu_w_out, v_ffn_w_up, v_ffn_conv_w, v_ffn_conv_b, v_ffn_w_down):
    args = locals()
    names = ['x', 'positions'] + WEIGHTS + ['loss_target'] + ['m_' + n for n in WEIGHTS] + ['v_' + n for n in WEIGHTS]
    return _train_step({n: args[n] for n in names})
```
